```python
import math
import jax, jax.numpy as jnp
from jax import lax
import numpy as np

D_MODEL = 1024
BATCH = 8
SEQ = 2048
DEPTH = 2

M_HEADS = 4
M_HEAD_DIM = 64
M_WIDTH = M_HEADS * M_HEAD_DIM
A_HEADS = 8
A_HEAD_DIM = 64
A_WIDTH = A_HEADS * A_HEAD_DIM
S_GROUPS = 16
S_GROUP_DIM = 16
S_WIDTH = S_GROUPS * S_GROUP_DIM
S_STATE = 64
D_MIX = M_WIDTH + A_WIDTH + S_WIDTH

CONV_K = 4
M_CHUNK = 64
MOBA_BLOCK = 256
MOBA_TOPK = 3
MOBA_QCHUNK = 32
ROPE_THETA = 500000.0
ROPE_DIM = A_HEAD_DIM // 4

N_GROUPS = 4
EXPERTS_PER_GROUP = 8
N_EXPERTS = N_GROUPS * EXPERTS_PER_GROUP
EXPERT_TOPK = 2
D_EXPERT = 512
DISPATCH_BLOCK = 128
EPS = 1e-6

COL_SIZES = [M_WIDTH, M_WIDTH, M_WIDTH, M_WIDTH, M_HEADS, M_HEADS,
             A_WIDTH, A_WIDTH, A_WIDTH,
             S_WIDTH]
D_IN = int(sum(COL_SIZES))
COL_SPLITS = [int(c) for c in np.cumsum(COL_SIZES)[:-1]]

kernel_name = "hymba_style_mlstm_moba_s5_hmoe"

F32 = jnp.float32


def rmsnorm(x, g):
    xf = x.astype(F32)
    y = xf * lax.rsqrt(jnp.mean(xf * xf, axis=-1, keepdims=True) + EPS)
    return (y * g.astype(F32)).astype(x.dtype)


def causal_dwconv(x, w, b):
    kw, s = w.shape[0], x.shape[1]
    xp = jnp.pad(x, ((0, 0), (kw - 1, 0), (0, 0)))
    y = b
    for j in range(kw):
        y = y + xp[:, j:j + s] * w[j]
    return y


def partial_rope(x, pos):
    half = ROPE_DIM // 2
    inv = ROPE_THETA ** (-jnp.arange(half, dtype=F32) * 2.0 / ROPE_DIM)
    ang = pos[:, None] * inv[None, :]
    cos, sin = jnp.cos(ang)[:, None, :], jnp.sin(ang)[:, None, :]
    x1 = x[..., :half].astype(F32)
    x2 = x[..., half:ROPE_DIM].astype(F32)
    rot = jnp.concatenate([x1 * cos - x2 * sin, x2 * cos + x1 * sin], axis=-1).astype(x.dtype)
    return jnp.concatenate([rot, x[..., ROPE_DIM:]], axis=-1)


def mlstm_heads(q, k, v, o_pre, i_pre, f_pre, out_g):
    bsz, s, _ = q.shape
    nc = s // M_CHUNK
    dh = M_HEAD_DIM

    def to_chunks(t):
        return t.astype(F32).reshape(bsz, nc, M_CHUNK, M_HEADS, dh).transpose(1, 0, 3, 2, 4)

    def gate_chunks(g):
        return g.astype(F32).reshape(bsz, nc, M_CHUNK, M_HEADS).transpose(1, 0, 3, 2)

    qc, kc, vc = to_chunks(q), to_chunks(k) * (dh ** -0.5), to_chunks(v)
    ic = gate_chunks(i_pre)
    lfc = jax.nn.log_sigmoid(gate_chunks(f_pre))
    causal = jnp.tril(jnp.ones((M_CHUNK, M_CHUNK), dtype=bool))

    def step(carry, inp):
        C, n, m = carry
        qb, kb, vb, ib, lfb = inp
        b = jnp.cumsum(lfb, axis=-1)
        a = b + m[..., None]
        W = b[..., :, None] - b[..., None, :] + ib[..., None, :]
        W = jnp.where(causal, W, -jnp.inf)
        m_t = jnp.maximum(a, jnp.max(W, axis=-1))
        inter = jnp.exp(a - m_t)
        Sqk = jnp.einsum('bhtd,bhsd->bhts', qb, kb) * jnp.exp(W - m_t[..., None])
        num = inter[..., None] * jnp.einsum('bhtd,bhde->bhte', qb, C) + jnp.einsum('bhts,bhse->bhte', Sqk, vb)
        den = inter * jnp.einsum('bhtd,bhd->bht', qb, n) + jnp.sum(Sqk, axis=-1)
        h = num / jnp.maximum(jnp.abs(den), jnp.exp(-m_t))[..., None]
        bL = b[..., -1]
        g_s = bL[..., None] - b + ib
        m_new = jnp.maximum(bL + m, jnp.max(g_s, axis=-1))
        decay = jnp.exp(bL + m - m_new)
        wts = jnp.exp(g_s - m_new[..., None])
        C_new = decay[..., None, None] * C + jnp.einsum('bhs,bhsd,bhse->bhde', wts, kb, vb)
        n_new = decay[..., None] * n + jnp.einsum('bhs,bhsd->bhd', wts, kb)
        return (C_new, n_new, m_new), h

    init = (jnp.zeros((bsz, M_HEADS, dh, dh), F32), jnp.zeros((bsz, M_HEADS, dh), F32),
            jnp.zeros((bsz, M_HEADS), F32))
    _, hs = lax.scan(step, init, (qc, kc, vc, ic, lfc))
    h = hs.transpose(1, 0, 3, 2, 4).reshape(bsz, s, M_HEADS, dh)
    h = rmsnorm(h, out_g).reshape(bsz, s, M_WIDTH)
    return (h * jax.nn.sigmoid(o_pre.astype(F32))).astype(q.dtype)


def moba_heads(q, k, v, q_g, k_g, out_g):
    bsz, s, _ = q.shape
    dh = A_HEAD_DIM
    pos = jnp.arange(s, dtype=F32)
    heads = lambda t: t.reshape(bsz, s, A_HEADS, dh)
    q = partial_rope(rmsnorm(heads(q), q_g), pos).transpose(0, 2, 1, 3)
    k = partial_rope(rmsnorm(heads(k), k_g), pos).transpose(0, 2, 1, 3)
    v = heads(v).transpose(0, 2, 1, 3)
    nb = -(-s // MOBA_BLOCK)
    pad = nb * MOBA_BLOCK - s
    padk = lambda t: jnp.pad(t, ((0, 0), (0, 0), (0, pad), (0, 0))).reshape(bsz, A_HEADS, nb, MOBA_BLOCK, dh)
    kb, vb = padk(k), padk(v)
    q_blk = jnp.arange(s) // MOBA_BLOCK
    gate = jnp.einsum('bhsd,bhnd->bhsn', q, jnp.mean(kb, axis=3)).astype(F32)
    past = jnp.arange(nb)[None, :] < q_blk[:, None]
    gate = jnp.where(past, gate, -jnp.inf)
    n_sel = min(MOBA_TOPK, nb)
    _, sel = lax.top_k(gate, n_sel)
    sel_ok = sel < q_blk[:, None]
    nq = s // MOBA_QCHUNK

    def chunk(t):
        return jnp.moveaxis(t.reshape(t.shape[0], t.shape[1], nq, MOBA_QCHUNK, *t.shape[3:]), 2, 0)

    bi = jnp.arange(bsz)[:, None, None, None]
    hi = jnp.arange(A_HEADS)[None, :, None, None]
    scale = dh ** -0.5

    def attend(args):
        qc, sc, okc, c = args
        start = c * MOBA_QCHUNK
        own = start // MOBA_BLOCK
        ko = lax.dynamic_index_in_dim(kb, own, axis=2, keepdims=False)
        vo = lax.dynamic_index_in_dim(vb, own, axis=2, keepdims=False)
        qpos = start + jnp.arange(MOBA_QCHUNK)
        kpos = own * MOBA_BLOCK + jnp.arange(MOBA_BLOCK)
        s_own = jnp.einsum('bhqd,bhkd->bhqk', qc, ko).astype(F32) * scale
        s_own = jnp.where(kpos[None, :] <= qpos[:, None], s_own, -jnp.inf)
        ks = kb[bi, hi, sc]
        vs = vb[bi, hi, sc]
        s_sel = jnp.einsum('bhqd,bhqnkd->bhqnk', qc, ks).astype(F32) * scale
        s_sel = jnp.where(okc[..., None], s_sel, -jnp.inf)
        logits = jnp.concatenate([s_sel.reshape(bsz, A_HEADS, MOBA_QCHUNK, n_sel * MOBA_BLOCK), s_own], axis=-1)
        p = jax.nn.softmax(logits, axis=-1).astype(v.dtype)
        p_sel = p[..., :n_sel * MOBA_BLOCK].reshape(bsz, A_HEADS, MOBA_QCHUNK, n_sel, MOBA_BLOCK)
        p_own = p[..., n_sel * MOBA_BLOCK:]
        return jnp.einsum('bhqnk,bhqnkd->bhqd', p_sel, vs) + jnp.einsum('bhqk,bhkd->bhqd', p_own, vo)

    out = lax.map(attend, (chunk(q), chunk(sel), chunk(sel_ok), jnp.arange(nq)))
    out = out.transpose(1, 0, 3, 2, 4).reshape(bsz, s, A_HEADS, dh)
    return rmsnorm(out, out_g).reshape(bsz, s, A_WIDTH)


def s5_groups(u, a_re, a_im, b_re, b_im, c_re, c_im, d, log_dt, glu_w, glu_b, out_g):
    bsz, s, _ = u.shape
    ug = u.astype(F32).reshape(bsz, s, S_GROUPS, S_GROUP_DIM)
    lr, li = a_re.astype(F32), a_im.astype(F32)
    dt = jnp.exp(log_dt.astype(F32))[:, None]
    mag = jnp.exp(lr * dt)
    abar_re, abar_im = mag * jnp.cos(li * dt), mag * jnp.sin(li * dt)
    zr, zi = abar_re - 1.0, abar_im
    den = lr * lr + li * li
    fr, fi = (zr * lr + zi * li) / den, (zi * lr - zr * li) / den
    br, bim = b_re.astype(F32), b_im.astype(F32)
    bbar_re = fr[..., None] * br - fi[..., None] * bim
    bbar_im = fr[..., None] * bim + fi[..., None] * br
    bu_re = jnp.einsum('bsgh,gph->bsgp', ug, bbar_re)
    bu_im = jnp.einsum('bsgh,gph->bsgp', ug, bbar_im)
    ar = jnp.broadcast_to(abar_re, bu_re.shape)
    ai = jnp.broadcast_to(abar_im, bu_re.shape)

    def combine(e1, e2):
        a1r, a1i, b1r, b1i = e1
        a2r, a2i, b2r, b2i = e2
        return (a2r * a1r - a2i * a1i, a2r * a1i + a2i * a1r,
                a2r * b1r - a2i * b1i + b2r, a2r * b1i + a2i * b1r + b2i)

    _, _, xr, xi = lax.associative_scan(combine, (ar, ai, bu_re, bu_im), axis=1)
    y = (jnp.einsum('bsgp,ghp->bsgh', xr, c_re.astype(F32))
         - jnp.einsum('bsgp,ghp->bsgh', xi, c_im.astype(F32))
         + d.astype(F32) * ug).reshape(bsz, s, S_WIDTH)
    z = jax.nn.gelu(y)
    y = z * jax.nn.sigmoid(z @ glu_w.astype(F32) + glu_b.astype(F32))
    y = rmsnorm(y.reshape(bsz, s, S_GROUPS, S_GROUP_DIM), out_g).reshape(bsz, s, S_WIDTH)
    return y.astype(u.dtype)


def hierarchical_moe(x, w_group, b_group, w_expert, b_expert, w_gate, w_up, w_down):
    bsz, s, dm = x.shape
    T = bsz * s
    xf = x.reshape(T, dm)
    gl = (xf @ w_group).astype(F32) + b_group.astype(F32)
    pg = jax.nn.softmax(gl, axis=-1)
    g_sel = jnp.argmax(gl, axis=-1)
    el = ((xf @ w_expert).astype(F32) + b_expert.astype(F32)).reshape(T, N_GROUPS, EXPERTS_PER_GROUP)
    el_g = jnp.take_along_axis(el, g_sel[:, None, None], axis=1)[:, 0]
    top_v, top_i = lax.top_k(el_g, EXPERT_TOPK)
    wts = jax.nn.softmax(top_v, axis=-1) * jnp.take_along_axis(pg, g_sel[:, None], axis=1)
    eid = g_sel[:, None] * EXPERTS_PER_GROUP + top_i
    A = T * EXPERT_TOPK
    e_flat = eid.reshape(-1)
    tok_flat = jnp.repeat(jnp.arange(T, dtype=jnp.int32), EXPERT_TOPK)
    w_flat = wts.reshape(-1)
    order = jnp.argsort(e_flat)
    e_s, tok_s, w_s = e_flat[order], tok_flat[order], w_flat[order]
    counts = jnp.bincount(e_flat, length=N_EXPERTS)
    starts = jnp.cumsum(counts) - counts
    padded = (counts + DISPATCH_BLOCK - 1) // DISPATCH_BLOCK * DISPATCH_BLOCK
    pends = jnp.cumsum(padded)
    pstarts = pends - padded
    dest = pstarts[e_s] + (jnp.arange(A) - starts[e_s])
    R = (A + N_EXPERTS * (DISPATCH_BLOCK - 1) + DISPATCH_BLOCK - 1) // DISPATCH_BLOCK * DISPATCH_BLOCK
    nblk = R // DISPATCH_BLOCK
    row_tok = jnp.full((R,), T, dtype=jnp.int32).at[dest].set(tok_s)
    row_w = jnp.zeros((R,), F32).at[dest].set(w_s)
    blk_exp = jnp.minimum(jnp.searchsorted(pends, jnp.arange(nblk) * DISPATCH_BLOCK, side='right'),
                          N_EXPERTS - 1)
    x_pad = jnp.concatenate([xf, jnp.zeros((1, dm), xf.dtype)], axis=0)
    xs = x_pad[row_tok].reshape(nblk, DISPATCH_BLOCK, dm)

    def expert_block(args):
        xb, e = args
        h = jax.nn.silu(xb @ w_gate[e]) * (xb @ w_up[e])
        return h @ w_down[e]

    ys = lax.map(expert_block, (xs, blk_exp)).reshape(R, dm)
    out = jax.ops.segment_sum(ys * row_w[:, None].astype(ys.dtype), row_tok, num_segments=T + 1)[:T]
    return out.reshape(bsz, s, dm)


def setup_inputs(seed: int = 0) -> dict:
    key = jax.random.key(seed)
    ks = iter(jax.random.split(key, 64))
    L = DEPTH

    def nrm(shape, scale):
        return jax.random.normal(next(ks), shape, F32) * scale

    def gain(shape):
        return 1.0 + nrm(shape, 0.02)

    x = nrm((BATCH, SEQ, D_MODEL), 1.0)
    s_a_re = -0.5 + nrm((L, S_GROUPS, S_STATE), 0.01)
    s_a_im = math.pi * jnp.arange(S_STATE, dtype=F32)[None, None, :] + nrm((L, S_GROUPS, S_STATE), 0.01)
    log_dt = jnp.log(0.001) + jax.random.uniform(next(ks), (L, S_GROUPS), F32) * (jnp.log(0.1) - jnp.log(0.001))
    cplx_b = S_GROUP_DIM ** -0.5 / math.sqrt(2.0)
    cplx_c = S_STATE ** -0.5 / math.sqrt(2.0)
    return {
        "x": x,
        "norm1_g": gain((L, D_MODEL)),
        "w_in": nrm((L, D_MODEL, D_IN), D_MODEL ** -0.5),
        "m_bias_i": nrm((L, M_HEADS), 0.1),
        "m_bias_f": jnp.linspace(3.0, 6.0, M_HEADS)[None, :] + nrm((L, M_HEADS), 0.1),
        "m_conv_w": nrm((L, CONV_K, 2 * M_WIDTH), CONV_K ** -0.5),
        "m_conv_b": nrm((L, 2 * M_WIDTH), 0.02),
        "m_out_g": gain((L, M_HEADS, M_HEAD_DIM)),
        "a_q_g": gain((L, A_HEAD_DIM)),
        "a_k_g": gain((L, A_HEAD_DIM)),
        "a_out_g": gain((L, A_HEADS, A_HEAD_DIM)),
        "s_a_re": s_a_re,
        "s_a_im": s_a_im,
        "s_b_re": nrm((L, S_GROUPS, S_STATE, S_GROUP_DIM), cplx_b),
        "s_b_im": nrm((L, S_GROUPS, S_STATE, S_GROUP_DIM), cplx_b),
        "s_c_re": nrm((L, S_GROUPS, S_GROUP_DIM, S_STATE), cplx_c),
        "s_c_im": nrm((L, S_GROUPS, S_GROUP_DIM, S_STATE), cplx_c),
        "s_d": nrm((L, S_GROUPS, S_GROUP_DIM), 1.0),
        "s_log_dt": log_dt,
        "s_glu_w": nrm((L, S_WIDTH, S_WIDTH), S_WIDTH ** -0.5),
        "s_glu_b": nrm((L, S_WIDTH), 0.02),
        "s_out_g": gain((L, S_GROUPS, S_GROUP_DIM)),
        "w_out": nrm((L, D_MIX, D_MODEL), D_MIX ** -0.5),
        "norm2_g": gain((L, D_MODEL)),
        "r_group_w": nrm((L, D_MODEL, N_GROUPS), D_MODEL ** -0.5),
        "r_group_b": nrm((L, N_GROUPS), 0.01),
        "r_expert_w": nrm((L, D_MODEL, N_EXPERTS), D_MODEL ** -0.5),
        "r_expert_b": nrm((L, N_EXPERTS), 0.01),
        "e_w_gate": nrm((L, N_EXPERTS, D_MODEL, D_EXPERT), D_MODEL ** -0.5),
        "e_w_up": nrm((L, N_EXPERTS, D_MODEL, D_EXPERT), D_MODEL ** -0.5),
        "e_w_down": nrm((L, N_EXPERTS, D_EXPERT, D_MODEL), D_EXPERT ** -0.5),
    }


def reference(x, norm1_g, w_in, m_bias_i, m_bias_f, m_conv_w, m_conv_b, m_out_g,
              a_q_g, a_k_g, a_out_g, s_a_re, s_a_im, s_b_re, s_b_im, s_c_re, s_c_im,
              s_d, s_log_dt, s_glu_w, s_glu_b, s_out_g, w_out, norm2_g,
              r_group_w, r_group_b, r_expert_w, r_expert_b, e_w_gate, e_w_up, e_w_down):
    for l in range(DEPTH):
        h = rmsnorm(x, norm1_g[l])
        proj = h @ w_in[l]
        mq, mk, mv, mo, mi, mf, aq, ak, av, su = jnp.split(proj, COL_SPLITS, axis=-1)
        qk = jax.nn.silu(causal_dwconv(jnp.concatenate([mq, mk], axis=-1), m_conv_w[l], m_conv_b[l]))
        mq, mk = qk[..., :M_WIDTH], qk[..., M_WIDTH:]
        y_m = mlstm_heads(mq, mk, mv, mo, mi + m_bias_i[l], mf + m_bias_f[l], m_out_g[l])
        y_a = moba_heads(aq, ak, av, a_q_g[l], a_k_g[l], a_out_g[l])
        y_s = s5_groups(su, s_a_re[l], s_a_im[l], s_b_re[l], s_b_im[l], s_c_re[l], s_c_im[l],
                        s_d[l].reshape(S_GROUPS, S_GROUP_DIM), s_log_dt[l], s_glu_w[l], s_glu_b[l], s_out_g[l])
        mix = jnp.concatenate([y_m.astype(x.dtype), y_a.astype(x.dtype), y_s.astype(x.dtype)], axis=-1)
        x = x + mix @ w_out[l]
        x = x + hierarchical_moe(rmsnorm(x, norm2_g[l]), r_group_w[l], r_group_b[l], r_expert_w[l],
                                 r_expert_b[l], e_w_gate[l], e_w_up[l], e_w_down[l])
    return x
```

```python
import functools
import math

import numpy as np
import jax
import jax.numpy as jnp
from jax import lax
from jax.experimental import pallas as pl
from jax.experimental.pallas import tpu as pltpu

F32 = jnp.float32
BF16 = jnp.bfloat16

D_MODEL = 1024
BATCH = 8
SEQ = 2048
DEPTH = 2
TOKENS = BATCH * SEQ

M_HEADS = 4
M_HEAD_DIM = 64
M_WIDTH = M_HEADS * M_HEAD_DIM
A_HEADS = 8
A_HEAD_DIM = 64
A_WIDTH = A_HEADS * A_HEAD_DIM
S_GROUPS = 16
S_GROUP_DIM = 16
S_WIDTH = S_GROUPS * S_GROUP_DIM
S_STATE = 64
S_FLAT = S_GROUPS * S_STATE
D_MIX = M_WIDTH + A_WIDTH + S_WIDTH

CONV_K = 4
MOBA_BLOCK = 256
MOBA_NB = SEQ // MOBA_BLOCK
MOBA_TOPK = 3
ROPE_THETA = 500000.0
ROPE_DIM = A_HEAD_DIM // 4
ROPE_HALF = ROPE_DIM // 2

N_GROUPS = 4
EXPERTS_PER_GROUP = 8
N_EXPERTS = N_GROUPS * EXPERTS_PER_GROUP
EXPERT_TOPK = 2
D_EXPERT = 512
DISPATCH_BLOCK = 128
EPS = 1e-6

LANES = 128
HEAD_PAIR = 2 * A_HEAD_DIM

W_MAIN = 3 * A_WIDTH + 4 * M_WIDTH + S_WIDTH
TM_PROJ = 512
M_CHUNK = 256
M_NCHUNK = SEQ // M_CHUNK
S5_STEPS = 128
VMEM_LIMIT = 56 * 1024 * 1024

HIGHEST = lax.Precision.HIGHEST


def _dot(a, b, precision=None):
    return jnp.dot(a, b, preferred_element_type=F32, precision=precision)


def _dot_nt(a, b, precision=None):
    return lax.dot_general(a, b, (((1,), (1,)), ((), ())), preferred_element_type=F32, precision=precision)


def _dot_tn(a, b, precision=None):
    return lax.dot_general(a, b, (((0,), (0,)), ((), ())), preferred_element_type=F32, precision=precision)


def _log_sigmoid(x):
    return jnp.minimum(x, 0.0) - jnp.log1p(jnp.exp(-jnp.abs(x)))


def _sigmoid(x):
    return 1.0 / (1.0 + jnp.exp(-x))


def _inproj_kernel(x_ref, g_ref, w_ref, wg_ref, wgt_ref,
                   aq_ref, ak_ref, av_ref, mqk_ref, mv_ref, mo_ref, su_ref, gates_ref, gatest_ref):
    x = x_ref[...]
    h = x * lax.rsqrt(jnp.mean(x * x, axis=-1, keepdims=True) + EPS) * g_ref[...]
    hb = h.astype(BF16)
    off = 0
    for ref, width in ((aq_ref, A_WIDTH), (ak_ref, A_WIDTH), (av_ref, A_WIDTH), (mqk_ref, 2 * M_WIDTH),
                       (mv_ref, M_WIDTH), (mo_ref, M_WIDTH), (su_ref, S_WIDTH)):
        ref[...] = _dot(hb, w_ref[:, off:off + width]).astype(ref.dtype)
        off += width
    gates_ref[...] = _dot(h, wg_ref[...], precision=HIGHEST)
    gt = _dot_nt(wgt_ref[...], h, precision=HIGHEST)
    for c in range(TM_PROJ // M_CHUNK):
        gatest_ref[c] = gt[:, c * M_CHUNK:(c + 1) * M_CHUNK]


def _inproj(x, g, w_main, w_gate, w_gate_t):
    n_s = SEQ // TM_PROJ
    row = lambda b, s: (b * n_s + s, 0)
    const = lambda b, s: (0, 0)
    out_shapes = (
        jax.ShapeDtypeStruct((TOKENS, A_WIDTH), BF16),
        jax.ShapeDtypeStruct((TOKENS, A_WIDTH), BF16),
        jax.ShapeDtypeStruct((TOKENS, A_WIDTH), BF16),
        jax.ShapeDtypeStruct((TOKENS, 2 * M_WIDTH), BF16),
        jax.ShapeDtypeStruct((TOKENS, M_WIDTH), BF16),
        jax.ShapeDtypeStruct((TOKENS, M_WIDTH), BF16),
        jax.ShapeDtypeStruct((SEQ, BATCH * S_WIDTH), BF16),
        jax.ShapeDtypeStruct((TOKENS, LANES), F32),
        jax.ShapeDtypeStruct((TOKENS // M_CHUNK, 8, M_CHUNK), F32),
    )
    out_specs = (
        pl.BlockSpec((TM_PROJ, A_WIDTH), row),
        pl.BlockSpec((TM_PROJ, A_WIDTH), row),
        pl.BlockSpec((TM_PROJ, A_WIDTH), row),
        pl.BlockSpec((TM_PROJ, 2 * M_WIDTH), row),
        pl.BlockSpec((TM_PROJ, M_WIDTH), row),
        pl.BlockSpec((TM_PROJ, M_WIDTH), row),
        pl.BlockSpec((TM_PROJ, S_WIDTH), lambda b, s: (s, b)),
        pl.BlockSpec((TM_PROJ, LANES), row),
        pl.BlockSpec((TM_PROJ // M_CHUNK, 8, M_CHUNK), lambda b, s: (b * n_s + s, 0, 0)),
    )
    return pl.pallas_call(
        _inproj_kernel,
        grid=(BATCH, n_s),
        in_specs=[
            pl.BlockSpec((TM_PROJ, D_MODEL), row),
            pl.BlockSpec((1, D_MODEL), const),
            pl.BlockSpec((D_MODEL, W_MAIN), const),
            pl.BlockSpec((D_MODEL, LANES), const),
            pl.BlockSpec((8, D_MODEL), const),
        ],
        out_specs=out_specs,
        out_shape=out_shapes,
        compiler_params=pltpu.CompilerParams(
            dimension_semantics=("arbitrary", "arbitrary"), vmem_limit_bytes=VMEM_LIMIT),
        name="inproj",
    )(x, g, w_main, w_gate, w_gate_t)


def _mlstm_kernel(mqk_ref, mv_ref, mo_ref, gates_ref, gatest_ref, cw_ref, cb_ref, brow_ref, bcol_ref, og_ref,
                  y_ref, qk_s):
    L = M_CHUNK
    dh = M_HEAD_DIM
    w = cw_ref[...]
    cb = cb_ref[...]
    lane_q = lax.broadcasted_iota(jnp.int32, (1, 2 * M_WIDTH), 1) < M_WIDTH
    kscale = jnp.where(lane_q, 1.0, dh ** -0.5).astype(F32)
    for c in range(M_NCHUNK):
        halo = 16 if c > 0 else 0
        xt = mqk_ref[c * L - halo:(c + 1) * L, :].astype(F32)
        rows = lax.broadcasted_iota(jnp.int32, xt.shape, 0)
        y = cb + w[CONV_K - 1:CONV_K, :] * xt
        for k in range(1, CONV_K):
            sh = pltpu.roll(xt, k, 0)
            if c == 0:
                sh = jnp.where(rows >= k, sh, 0.0)
            y = y + w[CONV_K - 1 - k:CONV_K - k, :] * sh
        y = y[halo:, :]
        y = y * _sigmoid(y) * kscale
        qk_s[c * L:(c + 1) * L, :] = y.astype(BF16)

    r_i = lax.broadcasted_iota(jnp.int32, (L, L), 0)
    c_i = lax.broadcasted_iota(jnp.int32, (L, L), 1)
    causal = c_i <= r_i
    tri = causal.astype(F32)
    tri_u = (r_i <= c_i).astype(F32)
    pad_lane = lax.broadcasted_iota(jnp.int32, (L, LANES - dh), 1)
    ones_pad = jnp.where(pad_lane == 0, 1.0, 0.0).astype(BF16)
    og = og_ref[...]

    def chunk(c, carry):
        r0 = pl.multiple_of(c * L, L)
        gcol = gates_ref[pl.ds(r0, L), :] + brow_ref[...]
        grow = gatest_ref[c] + bcol_ref[...]
        cs_col = _dot(tri, _log_sigmoid(gcol), precision=HIGHEST)
        cs_row = _dot(_log_sigmoid(grow), tri_u, precision=HIGHEST)
        qk = qk_s[pl.ds(r0, L), :]
        vv = mv_ref[pl.ds(r0, L), :]
        new_carry = []
        hs = []
        for h in range(M_HEADS):
            cst, m = carry[h]
            q = qk[:, h * dh:(h + 1) * dh]
            k = qk[:, M_WIDTH + h * dh:M_WIDTH + (h + 1) * dh]
            v_aug = jnp.concatenate([vv[:, h * dh:(h + 1) * dh], ones_pad], axis=1)
            bc = cs_col[:, M_HEADS + h:M_HEADS + h + 1]
            br = cs_row[M_HEADS + h:M_HEADS + h + 1, :]
            ic = gcol[:, h:h + 1]
            ir = grow[h:h + 1, :]
            a = bc + m
            wmat = jnp.where(causal, bc - br + ir, -jnp.inf)
            mt = jnp.maximum(a, jnp.max(wmat, axis=1, keepdims=True))
            inter = jnp.exp(a - mt)
            sqk = _dot_nt(q, k) * jnp.exp(wmat - mt)
            nd = inter * _dot(q, cst.astype(BF16)) + _dot(sqk.astype(BF16), v_aug)
            num = nd[:, :dh]
            den = nd[:, dh:dh + 1]
            hh = num / jnp.maximum(jnp.abs(den), jnp.exp(-mt))
            hh = hh * lax.rsqrt(jnp.mean(hh * hh, axis=-1, keepdims=True) + EPS) * og[:, h * dh:(h + 1) * dh]
            hs.append(hh)
            bl = bc[L - 1:L, :]
            gs = bl - bc + ic
            m_new = jnp.maximum(bl + m, jnp.max(gs, axis=0, keepdims=True))
            decay = jnp.exp(bl + m - m_new)
            wts = jnp.exp(gs - m_new)
            kw = (k.astype(F32) * wts).astype(BF16)
            new_carry.append((decay * cst + _dot_tn(kw, v_aug), m_new))
        hcat = jnp.concatenate(hs, axis=1)
        o = mo_ref[pl.ds(r0, L), :].astype(F32)
        y_ref[pl.ds(r0, L), :] = (hcat * _sigmoid(o)).astype(y_ref.dtype)
        return tuple(new_carry)

    init = tuple((jnp.zeros((dh, LANES), F32), jnp.zeros((1, 1), F32)) for _ in range(M_HEADS))
    lax.fori_loop(0, M_NCHUNK, chunk, init)


def _mlstm(mqk, mv, mo, gates, gatest, conv_w, conv_b, bias_row, bias_col, out_g):
    per_b = lambda b: (b, 0)
    const = lambda b: (0, 0)
    return pl.pallas_call(
        _mlstm_kernel,
        grid=(BATCH,),
        in_specs=[
            pl.BlockSpec((SEQ, 2 * M_WIDTH), per_b),
            pl.BlockSpec((SEQ, M_WIDTH), per_b),
            pl.BlockSpec((SEQ, M_WIDTH), per_b),
            pl.BlockSpec((SEQ, LANES), per_b),
            pl.BlockSpec((M_NCHUNK, 8, M_CHUNK), lambda b: (b, 0, 0)),
            pl.BlockSpec((CONV_K, 2 * M_WIDTH), const),
            pl.BlockSpec((1, 2 * M_WIDTH), const),
            pl.BlockSpec((1, LANES), const),
            pl.BlockSpec((8, 1), const),
            pl.BlockSpec((1, M_WIDTH), const),
        ],
        out_specs=pl.BlockSpec((SEQ, M_WIDTH), per_b),
        out_shape=jax.ShapeDtypeStruct((TOKENS, M_WIDTH), BF16),
        scratch_shapes=[pltpu.VMEM((SEQ, 2 * M_WIDTH), BF16)],
        compiler_params=pltpu.CompilerParams(
            dimension_semantics=("arbitrary",), vmem_limit_bytes=VMEM_LIMIT),
        name="mlstm",
    )(mqk, mv, mo, gates, gatest, conv_w, conv_b, bias_row, bias_col, out_g)


def _qk_norm_rope(x, gain, cos, sin):
    lane = lax.broadcasted_iota(jnp.int32, x.shape, 1)
    first = lane < A_HEAD_DIM
    sq = x * x
    tot = jnp.sum(sq, axis=-1, keepdims=True)
    s0 = jnp.sum(jnp.where(first, sq, 0.0), axis=-1, keepdims=True)
    ms = jnp.where(first, s0, tot - s0) * (1.0 / A_HEAD_DIM)
    xn = x * lax.rsqrt(ms + EPS) * gain
    in_low = (lane % A_HEAD_DIM) < ROPE_HALF
    swapped = jnp.where(in_low, pltpu.roll(xn, LANES - ROPE_HALF, 1), pltpu.roll(xn, ROPE_HALF, 1))
    return xn * cos + swapped * sin


def _moba_kernel(aq_ref, ak_ref, av_ref, cosq_ref, sinq_ref, cos_ref, sin_ref, qg_ref, kg_ref, og_ref,
                 y_ref, kr_s, vt_s, km_s, sel_s):
    i = pl.program_id(2)
    blk = MOBA_BLOCK
    dh = A_HEAD_DIM

    @pl.when(i == 0)
    def _prep():
        kr = _qk_norm_rope(ak_ref[...].astype(F32), kg_ref[...], cos_ref[...], sin_ref[...])
        kr_s[...] = kr.astype(BF16)
        km_s[...] = jnp.mean(kr.reshape(MOBA_NB, blk, HEAD_PAIR), axis=1)
        for j in range(MOBA_NB):
            vt_s[j] = av_ref[j * blk:(j + 1) * blk, :].astype(F32).T.astype(BF16)

    qr = _qk_norm_rope(aq_ref[...].astype(F32), qg_ref[...], cosq_ref[...], sinq_ref[...])
    qt = qr.T
    km = km_s[...]
    km_lane = lax.broadcasted_iota(jnp.int32, km.shape, 1)
    qt_row = lax.broadcasted_iota(jnp.int32, qt.shape, 0)
    blk_id = lax.broadcasted_iota(jnp.int32, (MOBA_NB, blk), 0)
    past = blk_id < i
    k_idx = lax.broadcasted_iota(jnp.int32, (blk, blk), 0)
    q_idx = lax.broadcasted_iota(jnp.int32, (blk, blk), 1)
    og = og_ref[...]

    outs = []
    for h in range(2):
        head_lane = (km_lane < dh) if h == 0 else (km_lane >= dh)
        head_row = (qt_row < dh) if h == 0 else (qt_row >= dh)
        gate = _dot(jnp.where(head_lane, km, 0.0), qt, precision=HIGHEST)
        gate = jnp.where(past, gate, -jnp.inf)
        rank = jnp.zeros((MOBA_NB, blk), F32)
        for m_blk in range(MOBA_NB):
            gm = gate[m_blk:m_blk + 1, :]
            ahead = (gm > gate) | ((gm == gate) & (m_blk < blk_id))
            rank = rank + jnp.where(ahead, 1.0, 0.0)
        sel = past & (rank < float(MOBA_TOPK))
        sel_s[h] = jnp.where(sel, 0.0, -jnp.inf)

        qtm = (jnp.where(head_row, qt, 0.0) * (dh ** -0.5)).astype(BF16)
        r0 = pl.multiple_of(i * blk, blk)
        st = _dot(kr_s[pl.ds(r0, blk), :], qtm)
        st = jnp.where(k_idx <= q_idx, st, -jnp.inf)
        m0 = jnp.max(st, axis=0, keepdims=True)
        p = jnp.exp(st - m0)
        l0 = jnp.sum(p, axis=0, keepdims=True)
        acc0 = _dot(vt_s[i, h * dh:(h + 1) * dh, :], p.astype(BF16))

        def past_block(j, carry, h=h, qtm=qtm):
            m_run, l_run, acc = carry
            rj = pl.multiple_of(j * blk, blk)
            s = _dot(kr_s[pl.ds(rj, blk), :], qtm) + sel_s[h, pl.ds(j, 1), :]
            m_new = jnp.maximum(m_run, jnp.max(s, axis=0, keepdims=True))
            alpha = jnp.exp(m_run - m_new)
            pj = jnp.exp(s - m_new)
            l_new = alpha * l_run + jnp.sum(pj, axis=0, keepdims=True)
            acc_new = alpha * acc + _dot(vt_s[j, h * dh:(h + 1) * dh, :], pj.astype(BF16))
            return m_new, l_new, acc_new

        _, l_fin, acc = lax.fori_loop(0, i, past_block, (m0, l0, acc0))
        o = acc / l_fin
        o = o * lax.rsqrt(jnp.mean(o * o, axis=0, keepdims=True) + EPS) * og[h * dh:(h + 1) * dh, :]
        outs.append(o)
    y_ref[...] = jnp.concatenate(outs, axis=0).T.astype(y_ref.dtype)


def _moba(aq, ak, av, cos_t, sin_t, q_gain, k_gain, out_gain_col):
    n_pairs = A_HEADS // 2
    nb = MOBA_NB
    const = lambda b, p, i: (0, 0)
    return pl.pallas_call(
        _moba_kernel,
        grid=(BATCH, n_pairs, nb),
        in_specs=[
            pl.BlockSpec((MOBA_BLOCK, HEAD_PAIR), lambda b, p, i: (b * nb + i, p)),
            pl.BlockSpec((SEQ, HEAD_PAIR), lambda b, p, i: (b, p)),
            pl.BlockSpec((SEQ, HEAD_PAIR), lambda b, p, i: (b, p)),
            pl.BlockSpec((MOBA_BLOCK, HEAD_PAIR), lambda b, p, i: (i, 0)),
            pl.BlockSpec((MOBA_BLOCK, HEAD_PAIR), lambda b, p, i: (i, 0)),
            pl.BlockSpec((SEQ, HEAD_PAIR), const),
            pl.BlockSpec((SEQ, HEAD_PAIR), const),
            pl.BlockSpec((1, HEAD_PAIR), const),
            pl.BlockSpec((1, HEAD_PAIR), const),
            pl.BlockSpec((HEAD_PAIR, 1), lambda b, p, i: (p, 0)),
        ],
        out_specs=pl.BlockSpec((MOBA_BLOCK, HEAD_PAIR), lambda b, p, i: (b * nb + i, p)),
        out_shape=jax.ShapeDtypeStruct((TOKENS, A_WIDTH), BF16),
        scratch_shapes=[
            pltpu.VMEM((SEQ, HEAD_PAIR), BF16),
            pltpu.VMEM((nb, HEAD_PAIR, MOBA_BLOCK), BF16),
            pltpu.VMEM((nb, HEAD_PAIR), F32),
            pltpu.VMEM((2, nb, MOBA_BLOCK), F32),
        ],
        compiler_params=pltpu.CompilerParams(
            dimension_semantics=("arbitrary", "arbitrary", "arbitrary"), vmem_limit_bytes=VMEM_LIMIT),
        name="moba",
    )(aq, ak, av, cos_t, sin_t, cos_t, sin_t, q_gain, k_gain, out_gain_col)


def _s5_kernel(u_ref, abar_ref, bre_ref, bim_ref, cre_ref, cim_ref, d_ref, gw_ref, gb_ref, gmat_ref, og_ref,
               y_ref, xr_s, xi_s, st_s):
    step = pl.program_id(0)
    rows = S5_STEPS * BATCH
    tile = 256
    n_tiles = rows // tile

    @pl.when(step == 0)
    def _init():
        st_s[...] = jnp.zeros_like(st_s)

    for r in range(n_tiles):
        u = u_ref[r * tile:(r + 1) * tile, :]
        t0 = r * tile // BATCH
        xr_s[t0:t0 + tile // BATCH] = _dot(u, bre_ref[...]).reshape(tile // BATCH, BATCH, S_FLAT)
        xi_s[t0:t0 + tile // BATCH] = _dot(u, bim_ref[...]).reshape(tile // BATCH, BATCH, S_FLAT)

    ar = jnp.broadcast_to(abar_ref[0:1, :], (BATCH, S_FLAT))
    ai = jnp.broadcast_to(abar_ref[1:2, :], (BATCH, S_FLAT))

    def scan_step(t, carry):
        sr, si = carry
        nr = ar * sr - ai * si + xr_s[t]
        ni = ar * si + ai * sr + xi_s[t]
        xr_s[t] = nr
        xi_s[t] = ni
        return nr, ni

    sr, si = lax.fori_loop(0, S5_STEPS, scan_step, (st_s[0], st_s[1]), unroll=4)
    st_s[0] = sr
    st_s[1] = si

    for r in range(n_tiles):
        t0 = r * tile // BATCH
        xr = xr_s[t0:t0 + tile // BATCH].reshape(tile, S_FLAT).astype(BF16)
        xi = xi_s[t0:t0 + tile // BATCH].reshape(tile, S_FLAT).astype(BF16)
        u = u_ref[r * tile:(r + 1) * tile, :].astype(F32)
        y = _dot(xr, cre_ref[...]) - _dot(xi, cim_ref[...]) + d_ref[...] * u
        z = 0.5 * y * (1.0 + jnp.tanh(math.sqrt(2.0 / math.pi) * (y + 0.044715 * (y * y * y))))
        y2 = z * _sigmoid(_dot(z.astype(BF16), gw_ref[...]) + gb_ref[...])
        ms = _dot(y2 * y2, gmat_ref[...], precision=HIGHEST)
        y_ref[r * tile:(r + 1) * tile, :] = (y2 * lax.rsqrt(ms + EPS) * og_ref[...]).astype(y_ref.dtype)


def _s5(u_sb, abar, bre, bim, cre, cim, d_row, glu_w, glu_b, gmat, out_g):
    rows = S5_STEPS * BATCH
    const = lambda s: (0, 0)
    return pl.pallas_call(
        _s5_kernel,
        grid=(SEQ // S5_STEPS,),
        in_specs=[
            pl.BlockSpec((rows, S_WIDTH), lambda s: (s, 0)),
            pl.BlockSpec((2, S_FLAT), const),
            pl.BlockSpec((S_WIDTH, S_FLAT), const),
            pl.BlockSpec((S_WIDTH, S_FLAT), const),
            pl.BlockSpec((S_FLAT, S_WIDTH), const),
            pl.BlockSpec((S_FLAT, S_WIDTH), const),
            pl.BlockSpec((1, S_WIDTH), const),
            pl.BlockSpec((S_WIDTH, S_WIDTH), const),
            pl.BlockSpec((1, S_WIDTH), const),
            pl.BlockSpec((S_WIDTH, S_WIDTH), const),
            pl.BlockSpec((1, S_WIDTH), const),
        ],
        out_specs=pl.BlockSpec((rows, S_WIDTH), lambda s: (s, 0)),
        out_shape=jax.ShapeDtypeStruct((SEQ * BATCH, S_WIDTH), BF16),
        scratch_shapes=[
            pltpu.VMEM((S5_STEPS, BATCH, S_FLAT), F32),
            pltpu.VMEM((S5_STEPS, BATCH, S_FLAT), F32),
            pltpu.VMEM((2, BATCH, S_FLAT), F32),
        ],
        compiler_params=pltpu.CompilerParams(
            dimension_semantics=("arbitrary",), vmem_limit_bytes=VMEM_LIMIT),
        name="s5",
    )(u_sb, abar, bre, bim, cre, cim, d_row, glu_w, glu_b, gmat, out_g)


def _outproj_kernel(x_ref, ym_ref, ya_ref, ys_ref, w_ref, g_ref, wr_ref, br_ref,
                    x1_ref, xn_ref, logit_ref):
    x1 = x_ref[...]
    x1 = x1 + _dot(ym_ref[...], w_ref[0:M_WIDTH, :])
    x1 = x1 + _dot(ya_ref[...], w_ref[M_WIDTH:M_WIDTH + A_WIDTH, :])
    x1 = x1 + _dot(ys_ref[...], w_ref[M_WIDTH + A_WIDTH:, :])
    x1_ref[...] = x1
    xn = x1 * lax.rsqrt(jnp.mean(x1 * x1, axis=-1, keepdims=True) + EPS) * g_ref[...]
    xn_ref[...] = xn.astype(xn_ref.dtype)
    logit_ref[...] = _dot(xn, wr_ref[...], precision=HIGHEST) + br_ref[...]


def _outproj(x, y_m, y_a, y_s_sb, w_out, g2, w_router, b_router):
    n_s = SEQ // TM_PROJ
    row = lambda b, s: (b * n_s + s, 0)
    const = lambda b, s: (0, 0)
    return pl.pallas_call(
        _outproj_kernel,
        grid=(BATCH, n_s),
        in_specs=[
            pl.BlockSpec((TM_PROJ, D_MODEL), row),
            pl.BlockSpec((TM_PROJ, M_WIDTH), row),
            pl.BlockSpec((TM_PROJ, A_WIDTH), row),
            pl.BlockSpec((TM_PROJ, S_WIDTH), lambda b, s: (s, b)),
            pl.BlockSpec((D_MIX, D_MODEL), const),
            pl.BlockSpec((1, D_MODEL), const),
            pl.BlockSpec((D_MODEL, LANES), const),
            pl.BlockSpec((1, LANES), const),
        ],
        out_specs=(
            pl.BlockSpec((TM_PROJ, D_MODEL), row),
            pl.BlockSpec((TM_PROJ, D_MODEL), row),
            pl.BlockSpec((TM_PROJ, LANES), row),
        ),
        out_shape=(
            jax.ShapeDtypeStruct((TOKENS, D_MODEL), F32),
            jax.ShapeDtypeStruct((TOKENS, D_MODEL), BF16),
            jax.ShapeDtypeStruct((TOKENS, LANES), F32),
        ),
        compiler_params=pltpu.CompilerParams(
            dimension_semantics=("arbitrary", "arbitrary"), vmem_limit_bytes=VMEM_LIMIT),
        name="outproj",
    )(x, y_m, y_a, y_s_sb, w_out, g2, w_router, b_router)


def _ffn_kernel(be_ref, xs_ref, roww_ref, wg_ref, wu_ref, wd_ref, ys_ref, wg_s, wu_s, wd_s):
    i = pl.program_id(0)
    prev = be_ref[jnp.maximum(i - 1, 0)]

    @pl.when((i == 0) | (be_ref[i] != prev))
    def _load_expert():
        wg_s[...] = wg_ref[...].astype(BF16)
        wu_s[...] = wu_ref[...].astype(BF16)
        wd_s[...] = wd_ref[...].astype(BF16)

    xb = xs_ref[...]
    g = _dot(xb, wg_s[...])
    u = _dot(xb, wu_s[...])
    hmid = (g * _sigmoid(g) * u).astype(BF16)
    ys_ref[...] = _dot(hmid, wd_s[...]) * roww_ref[...]


def _expert_ffn(blk_exp, xs, row_w, w_gate, w_up, w_down):
    n_rows = xs.shape[0]
    nblk = n_rows // DISPATCH_BLOCK
    grid_spec = pltpu.PrefetchScalarGridSpec(
        num_scalar_prefetch=1,
        grid=(nblk,),
        in_specs=[
            pl.BlockSpec((DISPATCH_BLOCK, D_MODEL), lambda i, be: (i, 0)),
            pl.BlockSpec((DISPATCH_BLOCK, 1), lambda i, be: (i, 0)),
            pl.BlockSpec((None, D_MODEL, D_EXPERT), lambda i, be: (be[i], 0, 0)),
            pl.BlockSpec((None, D_MODEL, D_EXPERT), lambda i, be: (be[i], 0, 0)),
            pl.BlockSpec((None, D_EXPERT, D_MODEL), lambda i, be: (be[i], 0, 0)),
        ],
        out_specs=pl.BlockSpec((DISPATCH_BLOCK, D_MODEL), lambda i, be: (i, 0)),
        scratch_shapes=[
            pltpu.VMEM((D_MODEL, D_EXPERT), BF16),
            pltpu.VMEM((D_MODEL, D_EXPERT), BF16),
            pltpu.VMEM((D_EXPERT, D_MODEL), BF16),
        ],
    )
    return pl.pallas_call(
        _ffn_kernel,
        grid_spec=grid_spec,
        out_shape=jax.ShapeDtypeStruct((n_rows, D_MODEL), F32),
        compiler_params=pltpu.CompilerParams(
            dimension_semantics=("arbitrary",), vmem_limit_bytes=VMEM_LIMIT),
        name="expert_ffn",
    )(blk_exp, xs, row_w, w_gate, w_up, w_down)


def _rope_tables():
    inv = ROPE_THETA ** (-np.arange(ROPE_HALF, dtype=np.float64) * 2.0 / ROPE_DIM)
    ang = np.arange(SEQ, dtype=np.float64)[:, None] * inv[None, :]
    cos = np.ones((SEQ, A_HEAD_DIM))
    sin = np.zeros((SEQ, A_HEAD_DIM))
    cos[:, :ROPE_HALF] = np.cos(ang)
    cos[:, ROPE_HALF:ROPE_DIM] = np.cos(ang)
    sin[:, :ROPE_HALF] = -np.sin(ang)
    sin[:, ROPE_HALF:ROPE_DIM] = np.sin(ang)
    return (jnp.asarray(np.tile(cos, (1, 2)), F32), jnp.asarray(np.tile(sin, (1, 2)), F32))


def _split_w_in(w_in):
    sizes = [M_WIDTH, M_WIDTH, M_WIDTH, M_WIDTH, M_HEADS, M_HEADS, A_WIDTH, A_WIDTH, A_WIDTH, S_WIDTH]
    offs = np.cumsum([0] + sizes)
    mq, mk, mv, mo, mi, mf, aq, ak, av, su = (w_in[:, offs[n]:offs[n + 1]] for n in range(len(sizes)))
    w_main = jnp.concatenate([aq, ak, av, mq, mk, mv, mo, su], axis=1).astype(BF16)
    gate = jnp.concatenate([mi, mf], axis=1)
    w_gate = jnp.pad(gate, ((0, 0), (0, LANES - 2 * M_HEADS)))
    return w_main, w_gate, gate.T


def _s5_params(a_re, a_im, b_re, b_im, c_re, c_im, log_dt):
    dt = jnp.exp(log_dt)[:, None]
    mag = jnp.exp(a_re * dt)
    abar_re, abar_im = mag * jnp.cos(a_im * dt), mag * jnp.sin(a_im * dt)
    zr, zi = abar_re - 1.0, abar_im
    den = a_re * a_re + a_im * a_im
    fr, fi = (zr * a_re + zi * a_im) / den, (zi * a_re - zr * a_im) / den
    bbar_re = fr[..., None] * b_re - fi[..., None] * b_im
    bbar_im = fr[..., None] * b_im + fi[..., None] * b_re
    eye = jnp.eye(S_GROUPS, dtype=F32)
    dense_b = lambda t: jnp.einsum('gph,gk->ghkp', t, eye).reshape(S_WIDTH, S_FLAT).astype(BF16)
    dense_c = lambda t: jnp.einsum('ghp,gk->gpkh', t, eye).reshape(S_FLAT, S_WIDTH).astype(BF16)
    abar = jnp.stack([abar_re.reshape(S_FLAT), abar_im.reshape(S_FLAT)])
    return abar, dense_b(bbar_re), dense_b(bbar_im), dense_c(c_re), dense_c(c_im)


def _route(logits):
    T = TOKENS
    gl = logits[:, :N_GROUPS]
    pg = jax.nn.softmax(gl, axis=-1)
    g_sel = jnp.argmax(gl, axis=-1)
    el = logits[:, N_GROUPS:N_GROUPS + N_EXPERTS].reshape(T, N_GROUPS, EXPERTS_PER_GROUP)
    el_g = jnp.take_along_axis(el, g_sel[:, None, None], axis=1)[:, 0]
    top_v, top_i = lax.top_k(el_g, EXPERT_TOPK)
    wts = jax.nn.softmax(top_v, axis=-1) * jnp.take_along_axis(pg, g_sel[:, None], axis=1)
    eid = g_sel[:, None] * EXPERTS_PER_GROUP + top_i
    A = T * EXPERT_TOPK
    e_flat = eid.reshape(-1).astype(jnp.int32)
    tok_flat = jnp.repeat(jnp.arange(T, dtype=jnp.int32), EXPERT_TOPK)
    w_flat = wts.reshape(-1)
    order = jnp.argsort(e_flat)
    e_s, tok_s, w_s = e_flat[order], tok_flat[order], w_flat[order]
    counts = jnp.bincount(e_flat, length=N_EXPERTS)
    starts = jnp.cumsum(counts) - counts
    padded = (counts + DISPATCH_BLOCK - 1) // DISPATCH_BLOCK * DISPATCH_BLOCK
    pends = jnp.cumsum(padded)
    pstarts = pends - padded
    dest = pstarts[e_s] + (jnp.arange(A) - starts[e_s])
    R = (A + N_EXPERTS * (DISPATCH_BLOCK - 1) + DISPATCH_BLOCK - 1) // DISPATCH_BLOCK * DISPATCH_BLOCK
    nblk = R // DISPATCH_BLOCK
    row_tok = jnp.full((R,), T, dtype=jnp.int32).at[dest].set(tok_s)
    row_w = jnp.zeros((R,), F32).at[dest].set(w_s)
    blk_exp = jnp.minimum(jnp.searchsorted(pends, jnp.arange(nblk) * DISPATCH_BLOCK, side='right'),
                          N_EXPERTS - 1).astype(jnp.int32)
    return row_tok, row_w, blk_exp


def kernel(x, norm1_g, w_in, m_bias_i, m_bias_f, m_conv_w, m_conv_b, m_out_g, a_q_g, a_k_g, a_out_g, s_a_re, s_a_im, s_b_re, s_b_im, s_c_re, s_c_im, s_d, s_log_dt, s_glu_w, s_glu_b, s_out_g, w_out, norm2_g, r_group_w, r_group_b, r_expert_w, r_expert_b, e_w_gate, e_w_up, e_w_down):
    cos_t, sin_t = _rope_tables()
    gmat = jnp.asarray(np.kron(np.eye(S_GROUPS), np.full((S_GROUP_DIM, S_GROUP_DIM), 1.0 / S_GROUP_DIM)), F32)
    xf = x.reshape(TOKENS, D_MODEL)
    for l in range(DEPTH):
        w_main, w_gate, w_gate_t = _split_w_in(w_in[l])
        aq, ak, av, mqk, mv, mo, su, gates, gatest = _inproj(
            xf, norm1_g[l].reshape(1, D_MODEL), w_main, w_gate, w_gate_t)

        bias = jnp.concatenate([m_bias_i[l], m_bias_f[l]])
        bias_row = jnp.pad(bias, (0, LANES - 2 * M_HEADS)).reshape(1, LANES)
        y_m = _mlstm(mqk, mv, mo, gates, gatest, m_conv_w[l], m_conv_b[l].reshape(1, 2 * M_WIDTH),
                     bias_row, bias.reshape(2 * M_HEADS, 1), m_out_g[l].reshape(1, M_WIDTH))

        y_a = _moba(aq, ak, av, cos_t, sin_t,
                    jnp.tile(a_q_g[l], 2).reshape(1, HEAD_PAIR), jnp.tile(a_k_g[l], 2).reshape(1, HEAD_PAIR),
                    a_out_g[l].reshape(A_WIDTH, 1))

        abar, bre, bim, cre, cim = _s5_params(s_a_re[l], s_a_im[l], s_b_re[l], s_b_im[l],
                                              s_c_re[l], s_c_im[l], s_log_dt[l])
        y_s = _s5(su.reshape(SEQ * BATCH, S_WIDTH), abar, bre, bim, cre, cim,
                  s_d[l].reshape(1, S_WIDTH), s_glu_w[l].astype(BF16), s_glu_b[l].reshape(1, S_WIDTH),
                  gmat, s_out_g[l].reshape(1, S_WIDTH))

        w_router = jnp.pad(jnp.concatenate([r_group_w[l], r_expert_w[l]], axis=1),
                           ((0, 0), (0, LANES - N_GROUPS - N_EXPERTS)))
        b_router = jnp.pad(jnp.concatenate([r_group_b[l], r_expert_b[l]]),
                           (0, LANES - N_GROUPS - N_EXPERTS)).reshape(1, LANES)
        x1, xn, logits = _outproj(xf, y_m, y_a, y_s.reshape(SEQ, BATCH * S_WIDTH), w_out[l].astype(BF16),
                                  norm2_g[l].reshape(1, D_MODEL), w_router, b_router)

        row_tok, row_w, blk_exp = _route(logits)
        xn_pad = jnp.concatenate([xn, jnp.zeros((1, D_MODEL), xn.dtype)], axis=0)
        ys = _expert_ffn(blk_exp, xn_pad[row_tok], row_w.reshape(-1, 1), e_w_gate[l], e_w_up[l], e_w_down[l])
        moe = jax.ops.segment_sum(ys, row_tok, num_segments=TOKENS + 1)[:TOKENS]
        xf = x1 + moe
    return xf.reshape(BATCH, SEQ, D_MODEL)
```

```python
import functools
import math

import numpy as np
import jax
import jax.numpy as jnp
from jax import lax
from jax.experimental import pallas as pl
from jax.experimental.pallas import tpu as pltpu

F32 = jnp.float32
BF16 = jnp.bfloat16

D_MODEL = 1024
BATCH = 8
SEQ = 2048
DEPTH = 2
TOKENS = BATCH * SEQ

M_HEADS = 4
M_HEAD_DIM = 64
M_WIDTH = M_HEADS * M_HEAD_DIM
A_HEADS = 8
A_HEAD_DIM = 64
A_WIDTH = A_HEADS * A_HEAD_DIM
S_GROUPS = 16
S_GROUP_DIM = 16
S_WIDTH = S_GROUPS * S_GROUP_DIM
S_STATE = 64
S_FLAT = S_GROUPS * S_STATE
D_MIX = M_WIDTH + A_WIDTH + S_WIDTH

CONV_K = 4
MOBA_BLOCK = 256
MOBA_NB = SEQ // MOBA_BLOCK
MOBA_TOPK = 3
ROPE_THETA = 500000.0
ROPE_DIM = A_HEAD_DIM // 4
ROPE_HALF = ROPE_DIM // 2

N_GROUPS = 4
EXPERTS_PER_GROUP = 8
N_EXPERTS = N_GROUPS * EXPERTS_PER_GROUP
EXPERT_TOPK = 2
D_EXPERT = 512
DISPATCH_BLOCK = 128
EPS = 1e-6

LANES = 128
HEAD_PAIR = 2 * A_HEAD_DIM

W_MAIN = 3 * A_WIDTH + 4 * M_WIDTH + S_WIDTH
TM_PROJ = 512
M_CHUNK = 256
M_NCHUNK = SEQ // M_CHUNK
S5_STEPS = 128
PLAN_TILE = 1024
DISPATCH_TOKENS = 512
COMBINE_TOKENS = 256
VMEM_LIMIT = 56 * 1024 * 1024

HIGHEST = lax.Precision.HIGHEST


def _dot(a, b, precision=None):
    return jnp.dot(a, b, preferred_element_type=F32, precision=precision)


def _dot_nt(a, b, precision=None):
    return lax.dot_general(a, b, (((1,), (1,)), ((), ())), preferred_element_type=F32, precision=precision)


def _dot_tn(a, b, precision=None):
    return lax.dot_general(a, b, (((0,), (0,)), ((), ())), preferred_element_type=F32, precision=precision)


def _log_sigmoid(x):
    return jnp.minimum(x, 0.0) - jnp.log1p(jnp.exp(-jnp.abs(x)))


def _sigmoid(x):
    return 1.0 / (1.0 + jnp.exp(-x))


def _inproj_kernel(x_ref, g_ref, w_ref, wg_ref, wgt_ref,
                   aq_ref, ak_ref, av_ref, mqk_ref, mv_ref, mo_ref, su_ref, gates_ref, gatest_ref):
    x = x_ref[...]
    h = x * lax.rsqrt(jnp.mean(x * x, axis=-1, keepdims=True) + EPS) * g_ref[...]
    hb = h.astype(BF16)
    off = 0
    for ref, width in ((aq_ref, A_WIDTH), (ak_ref, A_WIDTH), (av_ref, A_WIDTH), (mqk_ref, 2 * M_WIDTH),
                       (mv_ref, M_WIDTH), (mo_ref, M_WIDTH), (su_ref, S_WIDTH)):
        ref[...] = _dot(hb, w_ref[:, off:off + width]).astype(ref.dtype)
        off += width
    gates_ref[...] = _dot(h, wg_ref[...], precision=HIGHEST)
    gt = _dot_nt(wgt_ref[...], h, precision=HIGHEST)
    for c in range(TM_PROJ // M_CHUNK):
        gatest_ref[c] = gt[:, c * M_CHUNK:(c + 1) * M_CHUNK]


def _inproj(x, g, w_main, w_gate, w_gate_t):
    n_s = SEQ // TM_PROJ
    row = lambda b, s: (b * n_s + s, 0)
    const = lambda b, s: (0, 0)
    out_shapes = (
        jax.ShapeDtypeStruct((TOKENS, A_WIDTH), BF16),
        jax.ShapeDtypeStruct((TOKENS, A_WIDTH), BF16),
        jax.ShapeDtypeStruct((TOKENS, A_WIDTH), BF16),
        jax.ShapeDtypeStruct((TOKENS, 2 * M_WIDTH), BF16),
        jax.ShapeDtypeStruct((TOKENS, M_WIDTH), BF16),
        jax.ShapeDtypeStruct((TOKENS, M_WIDTH), BF16),
        jax.ShapeDtypeStruct((SEQ, BATCH * S_WIDTH), BF16),
        jax.ShapeDtypeStruct((TOKENS, LANES), F32),
        jax.ShapeDtypeStruct((TOKENS // M_CHUNK, 8, M_CHUNK), F32),
    )
    out_specs = (
        pl.BlockSpec((TM_PROJ, A_WIDTH), row),
        pl.BlockSpec((TM_PROJ, A_WIDTH), row),
        pl.BlockSpec((TM_PROJ, A_WIDTH), row),
        pl.BlockSpec((TM_PROJ, 2 * M_WIDTH), row),
        pl.BlockSpec((TM_PROJ, M_WIDTH), row),
        pl.BlockSpec((TM_PROJ, M_WIDTH), row),
        pl.BlockSpec((TM_PROJ, S_WIDTH), lambda b, s: (s, b)),
        pl.BlockSpec((TM_PROJ, LANES), row),
        pl.BlockSpec((TM_PROJ // M_CHUNK, 8, M_CHUNK), lambda b, s: (b * n_s + s, 0, 0)),
    )
    return pl.pallas_call(
        _inproj_kernel,
        grid=(BATCH, n_s),
        in_specs=[
            pl.BlockSpec((TM_PROJ, D_MODEL), row),
            pl.BlockSpec((1, D_MODEL), const),
            pl.BlockSpec((D_MODEL, W_MAIN), const),
            pl.BlockSpec((D_MODEL, LANES), const),
            pl.BlockSpec((8, D_MODEL), const),
        ],
        out_specs=out_specs,
        out_shape=out_shapes,
        compiler_params=pltpu.CompilerParams(
            dimension_semantics=("arbitrary", "arbitrary"), vmem_limit_bytes=VMEM_LIMIT),
        name="inproj",
    )(x, g, w_main, w_gate, w_gate_t)


def _mlstm_kernel(mqk_ref, mv_ref, mo_ref, gates_ref, gatest_ref, cw_ref, cb_ref, brow_ref, bcol_ref, og_ref,
                  y_ref, qk_s):
    L = M_CHUNK
    dh = M_HEAD_DIM
    w = cw_ref[...]
    cb = cb_ref[...]
    lane_q = lax.broadcasted_iota(jnp.int32, (1, 2 * M_WIDTH), 1) < M_WIDTH
    kscale = jnp.where(lane_q, 1.0, dh ** -0.5).astype(F32)
    for c in range(M_NCHUNK):
        halo = 16 if c > 0 else 0
        xt = mqk_ref[c * L - halo:(c + 1) * L, :].astype(F32)
        rows = lax.broadcasted_iota(jnp.int32, xt.shape, 0)
        y = cb + w[CONV_K - 1:CONV_K, :] * xt
        for k in range(1, CONV_K):
            sh = pltpu.roll(xt, k, 0)
            if c == 0:
                sh = jnp.where(rows >= k, sh, 0.0)
            y = y + w[CONV_K - 1 - k:CONV_K - k, :] * sh
        y = y[halo:, :]
        y = y * _sigmoid(y) * kscale
        qk_s[c * L:(c + 1) * L, :] = y.astype(BF16)

    r_i = lax.broadcasted_iota(jnp.int32, (L, L), 0)
    c_i = lax.broadcasted_iota(jnp.int32, (L, L), 1)
    causal = c_i <= r_i
    tri = causal.astype(F32)
    tri_u = (r_i <= c_i).astype(F32)
    pad_lane = lax.broadcasted_iota(jnp.int32, (L, LANES - dh), 1)
    ones_pad = jnp.where(pad_lane == 0, 1.0, 0.0).astype(BF16)
    og = og_ref[...]

    def chunk(c, carry):
        r0 = pl.multiple_of(c * L, L)
        gcol = gates_ref[pl.ds(r0, L), :] + brow_ref[...]
        grow = gatest_ref[c] + bcol_ref[...]
        cs_col = _dot(tri, _log_sigmoid(gcol), precision=HIGHEST)
        cs_row = _dot(_log_sigmoid(grow), tri_u, precision=HIGHEST)
        qk = qk_s[pl.ds(r0, L), :]
        vv = mv_ref[pl.ds(r0, L), :]
        new_carry = []
        hs = []
        for h in range(M_HEADS):
            cst, m = carry[h]
            q = qk[:, h * dh:(h + 1) * dh]
            k = qk[:, M_WIDTH + h * dh:M_WIDTH + (h + 1) * dh]
            v_aug = jnp.concatenate([vv[:, h * dh:(h + 1) * dh], ones_pad], axis=1)
            bc = cs_col[:, M_HEADS + h:M_HEADS + h + 1]
            br = cs_row[M_HEADS + h:M_HEADS + h + 1, :]
            ic = gcol[:, h:h + 1]
            ir = grow[h:h + 1, :]
            a = bc + m
            wmat = jnp.where(causal, bc - br + ir, -jnp.inf)
            mt = jnp.maximum(a, jnp.max(wmat, axis=1, keepdims=True))
            inter = jnp.exp(a - mt)
            sqk = _dot_nt(q, k) * jnp.exp(wmat - mt)
            nd = inter * _dot(q, cst.astype(BF16)) + _dot(sqk.astype(BF16), v_aug)
            num = nd[:, :dh]
            den = nd[:, dh:dh + 1]
            hh = num / jnp.maximum(jnp.abs(den), jnp.exp(-mt))
            hh = hh * lax.rsqrt(jnp.mean(hh * hh, axis=-1, keepdims=True) + EPS) * og[:, h * dh:(h + 1) * dh]
            hs.append(hh)
            bl = bc[L - 1:L, :]
            gs = bl - bc + ic
            m_new = jnp.maximum(bl + m, jnp.max(gs, axis=0, keepdims=True))
            decay = jnp.exp(bl + m - m_new)
            wts = jnp.exp(gs - m_new)
            kw = (k.astype(F32) * wts).astype(BF16)
            new_carry.append((decay * cst + _dot_tn(kw, v_aug), m_new))
        hcat = jnp.concatenate(hs, axis=1)
        o = mo_ref[pl.ds(r0, L), :].astype(F32)
        y_ref[pl.ds(r0, L), :] = (hcat * _sigmoid(o)).astype(y_ref.dtype)
        return tuple(new_carry)

    init = tuple((jnp.zeros((dh, LANES), F32), jnp.zeros((1, 1), F32)) for _ in range(M_HEADS))
    lax.fori_loop(0, M_NCHUNK, chunk, init)


def _mlstm(mqk, mv, mo, gates, gatest, conv_w, conv_b, bias_row, bias_col, out_g):
    per_b = lambda b: (b, 0)
    const = lambda b: (0, 0)
    return pl.pallas_call(
        _mlstm_kernel,
        grid=(BATCH,),
        in_specs=[
            pl.BlockSpec((SEQ, 2 * M_WIDTH), per_b),
            pl.BlockSpec((SEQ, M_WIDTH), per_b),
            pl.BlockSpec((SEQ, M_WIDTH), per_b),
            pl.BlockSpec((SEQ, LANES), per_b),
            pl.BlockSpec((M_NCHUNK, 8, M_CHUNK), lambda b: (b, 0, 0)),
            pl.BlockSpec((CONV_K, 2 * M_WIDTH), const),
            pl.BlockSpec((1, 2 * M_WIDTH), const),
            pl.BlockSpec((1, LANES), const),
            pl.BlockSpec((8, 1), const),
            pl.BlockSpec((1, M_WIDTH), const),
        ],
        out_specs=pl.BlockSpec((SEQ, M_WIDTH), per_b),
        out_shape=jax.ShapeDtypeStruct((TOKENS, M_WIDTH), BF16),
        scratch_shapes=[pltpu.VMEM((SEQ, 2 * M_WIDTH), BF16)],
        compiler_params=pltpu.CompilerParams(
            dimension_semantics=("arbitrary",), vmem_limit_bytes=VMEM_LIMIT),
        name="mlstm",
    )(mqk, mv, mo, gates, gatest, conv_w, conv_b, bias_row, bias_col, out_g)


def _qk_norm_rope(x, gain, cos, sin):
    lane = lax.broadcasted_iota(jnp.int32, x.shape, 1)
    first = lane < A_HEAD_DIM
    sq = x * x
    tot = jnp.sum(sq, axis=-1, keepdims=True)
    s0 = jnp.sum(jnp.where(first, sq, 0.0), axis=-1, keepdims=True)
    ms = jnp.where(first, s0, tot - s0) * (1.0 / A_HEAD_DIM)
    xn = x * lax.rsqrt(ms + EPS) * gain
    in_low = (lane % A_HEAD_DIM) < ROPE_HALF
    swapped = jnp.where(in_low, pltpu.roll(xn, LANES - ROPE_HALF, 1), pltpu.roll(xn, ROPE_HALF, 1))
    return xn * cos + swapped * sin


def _moba_kernel(aq_ref, ak_ref, av_ref, cosq_ref, sinq_ref, cos_ref, sin_ref, qg_ref, kg_ref, og_ref,
                 y_ref, kr_s, vt_s, km_s, sel_s):
    i = pl.program_id(2)
    blk = MOBA_BLOCK
    dh = A_HEAD_DIM

    @pl.when(i == 0)
    def _prep():
        kr = _qk_norm_rope(ak_ref[...].astype(F32), kg_ref[...], cos_ref[...], sin_ref[...])
        kr_s[...] = kr.astype(BF16)
        km_s[...] = jnp.mean(kr.reshape(MOBA_NB, blk, HEAD_PAIR), axis=1)
        for j in range(MOBA_NB):
            vt_s[j] = av_ref[j * blk:(j + 1) * blk, :].astype(F32).T.astype(BF16)

    qr = _qk_norm_rope(aq_ref[...].astype(F32), qg_ref[...], cosq_ref[...], sinq_ref[...])
    qt = qr.T
    km = km_s[...]
    km_lane = lax.broadcasted_iota(jnp.int32, km.shape, 1)
    qt_row = lax.broadcasted_iota(jnp.int32, qt.shape, 0)
    blk_id = lax.broadcasted_iota(jnp.int32, (MOBA_NB, blk), 0)
    past = blk_id < i
    k_idx = lax.broadcasted_iota(jnp.int32, (blk, blk), 0)
    q_idx = lax.broadcasted_iota(jnp.int32, (blk, blk), 1)
    og = og_ref[...]

    outs = []
    for h in range(2):
        head_lane = (km_lane < dh) if h == 0 else (km_lane >= dh)
        head_row = (qt_row < dh) if h == 0 else (qt_row >= dh)
        gate = _dot(jnp.where(head_lane, km, 0.0), qt, precision=HIGHEST)
        gate = jnp.where(past, gate, -jnp.inf)
        rank = jnp.zeros((MOBA_NB, blk), F32)
        for m_blk in range(MOBA_NB):
            gm = gate[m_blk:m_blk + 1, :]
            ahead = (gm > gate) | ((gm == gate) & (m_blk < blk_id))
            rank = rank + jnp.where(ahead, 1.0, 0.0)
        sel = past & (rank < float(MOBA_TOPK))
        sel_s[h] = jnp.where(sel, 0.0, -jnp.inf)

        qtm = (jnp.where(head_row, qt, 0.0) * (dh ** -0.5)).astype(BF16)
        r0 = pl.multiple_of(i * blk, blk)
        st = _dot(kr_s[pl.ds(r0, blk), :], qtm)
        st = jnp.where(k_idx <= q_idx, st, -jnp.inf)
        m0 = jnp.max(st, axis=0, keepdims=True)
        p = jnp.exp(st - m0)
        l0 = jnp.sum(p, axis=0, keepdims=True)
        acc0 = _dot(vt_s[i, h * dh:(h + 1) * dh, :], p.astype(BF16))

        def past_block(j, carry, h=h, qtm=qtm):
            m_run, l_run, acc = carry
            rj = pl.multiple_of(j * blk, blk)
            s = _dot(kr_s[pl.ds(rj, blk), :], qtm) + sel_s[h, pl.ds(j, 1), :]
            m_new = jnp.maximum(m_run, jnp.max(s, axis=0, keepdims=True))
            alpha = jnp.exp(m_run - m_new)
            pj = jnp.exp(s - m_new)
            l_new = alpha * l_run + jnp.sum(pj, axis=0, keepdims=True)
            acc_new = alpha * acc + _dot(vt_s[j, h * dh:(h + 1) * dh, :], pj.astype(BF16))
            return m_new, l_new, acc_new

        _, l_fin, acc = lax.fori_loop(0, i, past_block, (m0, l0, acc0))
        o = acc / l_fin
        o = o * lax.rsqrt(jnp.mean(o * o, axis=0, keepdims=True) + EPS) * og[h * dh:(h + 1) * dh, :]
        outs.append(o)
    y_ref[...] = jnp.concatenate(outs, axis=0).T.astype(y_ref.dtype)


def _moba(aq, ak, av, cos_t, sin_t, q_gain, k_gain, out_gain_col):
    n_pairs = A_HEADS // 2
    nb = MOBA_NB
    const = lambda b, p, i: (0, 0)
    return pl.pallas_call(
        _moba_kernel,
        grid=(BATCH, n_pairs, nb),
        in_specs=[
            pl.BlockSpec((MOBA_BLOCK, HEAD_PAIR), lambda b, p, i: (b * nb + i, p)),
            pl.BlockSpec((SEQ, HEAD_PAIR), lambda b, p, i: (b, p)),
            pl.BlockSpec((SEQ, HEAD_PAIR), lambda b, p, i: (b, p)),
            pl.BlockSpec((MOBA_BLOCK, HEAD_PAIR), lambda b, p, i: (i, 0)),
            pl.BlockSpec((MOBA_BLOCK, HEAD_PAIR), lambda b, p, i: (i, 0)),
            pl.BlockSpec((SEQ, HEAD_PAIR), const),
            pl.BlockSpec((SEQ, HEAD_PAIR), const),
            pl.BlockSpec((1, HEAD_PAIR), const),
            pl.BlockSpec((1, HEAD_PAIR), const),
            pl.BlockSpec((HEAD_PAIR, 1), lambda b, p, i: (p, 0)),
        ],
        out_specs=pl.BlockSpec((MOBA_BLOCK, HEAD_PAIR), lambda b, p, i: (b * nb + i, p)),
        out_shape=jax.ShapeDtypeStruct((TOKENS, A_WIDTH), BF16),
        scratch_shapes=[
            pltpu.VMEM((SEQ, HEAD_PAIR), BF16),
            pltpu.VMEM((nb, HEAD_PAIR, MOBA_BLOCK), BF16),
            pltpu.VMEM((nb, HEAD_PAIR), F32),
            pltpu.VMEM((2, nb, MOBA_BLOCK), F32),
        ],
        compiler_params=pltpu.CompilerParams(
            dimension_semantics=("arbitrary", "arbitrary", "arbitrary"), vmem_limit_bytes=VMEM_LIMIT),
        name="moba",
    )(aq, ak, av, cos_t, sin_t, cos_t, sin_t, q_gain, k_gain, out_gain_col)


def _s5_kernel(u_ref, abar_ref, bre_ref, bim_ref, cre_ref, cim_ref, d_ref, gw_ref, gb_ref, gmat_ref, og_ref,
               y_ref, xr_s, xi_s, st_s):
    step = pl.program_id(0)
    rows = S5_STEPS * BATCH
    tile = 256
    n_tiles = rows // tile

    @pl.when(step == 0)
    def _init():
        st_s[...] = jnp.zeros_like(st_s)

    for r in range(n_tiles):
        u = u_ref[r * tile:(r + 1) * tile, :]
        t0 = r * tile // BATCH
        xr_s[t0:t0 + tile // BATCH] = _dot(u, bre_ref[...]).reshape(tile // BATCH, BATCH, S_FLAT)
        xi_s[t0:t0 + tile // BATCH] = _dot(u, bim_ref[...]).reshape(tile // BATCH, BATCH, S_FLAT)

    ar = jnp.broadcast_to(abar_ref[0:1, :], (BATCH, S_FLAT))
    ai = jnp.broadcast_to(abar_ref[1:2, :], (BATCH, S_FLAT))

    def scan_step(t, carry):
        sr, si = carry
        nr = ar * sr - ai * si + xr_s[t]
        ni = ar * si + ai * sr + xi_s[t]
        xr_s[t] = nr
        xi_s[t] = ni
        return nr, ni

    sr, si = lax.fori_loop(0, S5_STEPS, scan_step, (st_s[0], st_s[1]), unroll=4)
    st_s[0] = sr
    st_s[1] = si

    for r in range(n_tiles):
        t0 = r * tile // BATCH
        xr = xr_s[t0:t0 + tile // BATCH].reshape(tile, S_FLAT).astype(BF16)
        xi = xi_s[t0:t0 + tile // BATCH].reshape(tile, S_FLAT).astype(BF16)
        u = u_ref[r * tile:(r + 1) * tile, :].astype(F32)
        y = _dot(xr, cre_ref[...]) - _dot(xi, cim_ref[...]) + d_ref[...] * u
        z = 0.5 * y * (1.0 + jnp.tanh(math.sqrt(2.0 / math.pi) * (y + 0.044715 * (y * y * y))))
        y2 = z * _sigmoid(_dot(z.astype(BF16), gw_ref[...]) + gb_ref[...])
        ms = _dot(y2 * y2, gmat_ref[...], precision=HIGHEST)
        y_ref[r * tile:(r + 1) * tile, :] = (y2 * lax.rsqrt(ms + EPS) * og_ref[...]).astype(y_ref.dtype)


def _s5(u_sb, abar, bre, bim, cre, cim, d_row, glu_w, glu_b, gmat, out_g):
    rows = S5_STEPS * BATCH
    const = lambda s: (0, 0)
    return pl.pallas_call(
        _s5_kernel,
        grid=(SEQ // S5_STEPS,),
        in_specs=[
            pl.BlockSpec((rows, S_WIDTH), lambda s: (s, 0)),
            pl.BlockSpec((2, S_FLAT), const),
            pl.BlockSpec((S_WIDTH, S_FLAT), const),
            pl.BlockSpec((S_WIDTH, S_FLAT), const),
            pl.BlockSpec((S_FLAT, S_WIDTH), const),
            pl.BlockSpec((S_FLAT, S_WIDTH), const),
            pl.BlockSpec((1, S_WIDTH), const),
            pl.BlockSpec((S_WIDTH, S_WIDTH), const),
            pl.BlockSpec((1, S_WIDTH), const),
            pl.BlockSpec((S_WIDTH, S_WIDTH), const),
            pl.BlockSpec((1, S_WIDTH), const),
        ],
        out_specs=pl.BlockSpec((rows, S_WIDTH), lambda s: (s, 0)),
        out_shape=jax.ShapeDtypeStruct((SEQ * BATCH, S_WIDTH), BF16),
        scratch_shapes=[
            pltpu.VMEM((S5_STEPS, BATCH, S_FLAT), F32),
            pltpu.VMEM((S5_STEPS, BATCH, S_FLAT), F32),
            pltpu.VMEM((2, BATCH, S_FLAT), F32),
        ],
        compiler_params=pltpu.CompilerParams(
            dimension_semantics=("arbitrary",), vmem_limit_bytes=VMEM_LIMIT),
        name="s5",
    )(u_sb, abar, bre, bim, cre, cim, d_row, glu_w, glu_b, gmat, out_g)


def _outproj_kernel(x_ref, ym_ref, ya_ref, ys_ref, w_ref, g_ref, wr_ref, br_ref,
                    x1_ref, xn_ref, route_ref):
    x1 = x_ref[...]
    x1 = x1 + _dot(ym_ref[...], w_ref[0:M_WIDTH, :])
    x1 = x1 + _dot(ya_ref[...], w_ref[M_WIDTH:M_WIDTH + A_WIDTH, :])
    x1 = x1 + _dot(ys_ref[...], w_ref[M_WIDTH + A_WIDTH:, :])
    x1_ref[...] = x1
    xn = x1 * lax.rsqrt(jnp.mean(x1 * x1, axis=-1, keepdims=True) + EPS) * g_ref[...]
    xn_ref[...] = xn.astype(xn_ref.dtype)
    logits = _dot(xn, wr_ref[...], precision=HIGHEST) + br_ref[...]

    lane = lax.broadcasted_iota(jnp.int32, logits.shape, 1).astype(F32)
    is_grp = lane < N_GROUPS
    neg = -jnp.inf
    gl = jnp.where(is_grp, logits, neg)
    gmax = jnp.max(gl, axis=-1, keepdims=True)
    gsum = jnp.sum(jnp.where(is_grp, jnp.exp(logits - gmax), 0.0), axis=-1, keepdims=True)
    g_sel = jnp.min(jnp.where(gl == gmax, lane, float(LANES)), axis=-1, keepdims=True)
    lo = N_GROUPS + EXPERTS_PER_GROUP * g_sel
    el = jnp.where((lane >= lo) & (lane < lo + EXPERTS_PER_GROUP), logits, neg)
    t1 = jnp.max(el, axis=-1, keepdims=True)
    i1 = jnp.min(jnp.where(el == t1, lane, float(LANES)), axis=-1, keepdims=True)
    el2 = jnp.where(lane == i1, neg, el)
    t2 = jnp.max(el2, axis=-1, keepdims=True)
    i2 = jnp.min(jnp.where(el2 == t2, lane, float(LANES)), axis=-1, keepdims=True)
    e21 = jnp.exp(t2 - t1)
    w1 = 1.0 / ((1.0 + e21) * gsum)
    w2 = e21 / ((1.0 + e21) * gsum)
    route = jnp.where(lane == 0.0, i1 - N_GROUPS,
                      jnp.where(lane == 1.0, i2 - N_GROUPS,
                                jnp.where(lane == 2.0, w1, jnp.where(lane == 3.0, w2, 0.0))))
    route_ref[...] = route


def _outproj(x, y_m, y_a, y_s_sb, w_out, g2, w_router, b_router):
    n_s = SEQ // TM_PROJ
    row = lambda b, s: (b * n_s + s, 0)
    const = lambda b, s: (0, 0)
    return pl.pallas_call(
        _outproj_kernel,
        grid=(BATCH, n_s),
        in_specs=[
            pl.BlockSpec((TM_PROJ, D_MODEL), row),
            pl.BlockSpec((TM_PROJ, M_WIDTH), row),
            pl.BlockSpec((TM_PROJ, A_WIDTH), row),
            pl.BlockSpec((TM_PROJ, S_WIDTH), lambda b, s: (s, b)),
            pl.BlockSpec((D_MIX, D_MODEL), const),
            pl.BlockSpec((1, D_MODEL), const),
            pl.BlockSpec((D_MODEL, LANES), const),
            pl.BlockSpec((1, LANES), const),
        ],
        out_specs=(
            pl.BlockSpec((TM_PROJ, D_MODEL), row),
            pl.BlockSpec((TM_PROJ, D_MODEL), row),
            pl.BlockSpec((TM_PROJ, LANES), row),
        ),
        out_shape=(
            jax.ShapeDtypeStruct((TOKENS, D_MODEL), F32),
            jax.ShapeDtypeStruct((TOKENS, D_MODEL), F32),
            jax.ShapeDtypeStruct((TOKENS, LANES), F32),
        ),
        compiler_params=pltpu.CompilerParams(
            dimension_semantics=("arbitrary", "arbitrary"), vmem_limit_bytes=VMEM_LIMIT),
        name="outproj",
    )(x, y_m, y_a, y_s_sb, w_out, g2, w_router, b_router)


def _ffn_kernel(be_ref, xs_ref, wg_ref, wu_ref, wd_ref, ys_ref, wg_s, wu_s, wd_s):
    i = pl.program_id(0)
    prev = be_ref[jnp.maximum(i - 1, 0)]

    @pl.when((i == 0) | (be_ref[i] != prev))
    def _load_expert():
        wg_s[...] = wg_ref[...].astype(BF16)
        wu_s[...] = wu_ref[...].astype(BF16)
        wd_s[...] = wd_ref[...].astype(BF16)

    xb = xs_ref[...].astype(BF16)
    g = _dot(xb, wg_s[...])
    u = _dot(xb, wu_s[...])
    hmid = (g * _sigmoid(g) * u).astype(BF16)
    ys_ref[...] = _dot(hmid, wd_s[...])


def _expert_ffn(blk_exp, xs, w_gate, w_up, w_down):
    n_rows = xs.shape[0]
    nblk = n_rows // DISPATCH_BLOCK
    grid_spec = pltpu.PrefetchScalarGridSpec(
        num_scalar_prefetch=1,
        grid=(nblk,),
        in_specs=[
            pl.BlockSpec((DISPATCH_BLOCK, D_MODEL), lambda i, be: (i, 0)),
            pl.BlockSpec((None, D_MODEL, D_EXPERT), lambda i, be: (be[i], 0, 0)),
            pl.BlockSpec((None, D_MODEL, D_EXPERT), lambda i, be: (be[i], 0, 0)),
            pl.BlockSpec((None, D_EXPERT, D_MODEL), lambda i, be: (be[i], 0, 0)),
        ],
        out_specs=pl.BlockSpec((DISPATCH_BLOCK, D_MODEL), lambda i, be: (i, 0)),
        scratch_shapes=[
            pltpu.VMEM((D_MODEL, D_EXPERT), BF16),
            pltpu.VMEM((D_MODEL, D_EXPERT), BF16),
            pltpu.VMEM((D_EXPERT, D_MODEL), BF16),
        ],
    )
    return pl.pallas_call(
        _ffn_kernel,
        grid_spec=grid_spec,
        out_shape=jax.ShapeDtypeStruct((n_rows, D_MODEL), F32),
        compiler_params=pltpu.CompilerParams(
            dimension_semantics=("arbitrary",), vmem_limit_bytes=VMEM_LIMIT),
        name="expert_ffn",
    )(blk_exp, xs, w_gate, w_up, w_down)


def _plan_kernel(e_ref, ps_ref, dest_ref, run_s, tri_s):
    i = pl.program_id(0)

    @pl.when(i == 0)
    def _init():
        run_s[...] = ps_ref[...]
        s_i = lax.broadcasted_iota(jnp.int32, (PLAN_TILE, PLAN_TILE), 0)
        t_i = lax.broadcasted_iota(jnp.int32, (PLAN_TILE, PLAN_TILE), 1)
        tri_s[...] = (s_i < t_i).astype(BF16)

    expert = lax.broadcasted_iota(jnp.int32, (N_EXPERTS, PLAN_TILE), 0)
    onehot = jnp.where(e_ref[...] == expert, 1.0, 0.0)
    before = _dot(onehot.astype(BF16), tri_s[...])
    dest = jnp.sum(onehot * (before + run_s[:, 0:1]), axis=0, keepdims=True)
    dest_ref[...] = dest.astype(jnp.int32)
    run_s[...] = run_s[...] + jnp.sum(onehot, axis=1, keepdims=True)


def _plan(e_rows, pstart_col):
    n_tiles = e_rows.shape[0]
    return pl.pallas_call(
        _plan_kernel,
        grid=(n_tiles,),
        in_specs=[
            pl.BlockSpec((None, 1, PLAN_TILE), lambda i: (i, 0, 0)),
            pl.BlockSpec((N_EXPERTS, LANES), lambda i: (0, 0)),
        ],
        out_specs=pl.BlockSpec((None, 1, PLAN_TILE), lambda i: (i, 0, 0)),
        out_shape=jax.ShapeDtypeStruct((n_tiles, 1, PLAN_TILE), jnp.int32),
        scratch_shapes=[pltpu.VMEM((N_EXPERTS, LANES), F32), pltpu.VMEM((PLAN_TILE, PLAN_TILE), BF16)],
        compiler_params=pltpu.CompilerParams(
            dimension_semantics=("arbitrary",), vmem_limit_bytes=VMEM_LIMIT),
        name="moe_plan",
    )(e_rows, pstart_col)


def _dispatch_kernel(dest_ref, xn_hbm, xs_init_hbm, xs_hbm, sem):
    del xs_init_hbm
    base = pl.program_id(0) * DISPATCH_TOKENS

    def row_copy(a):
        tok = base + lax.shift_right_logical(a, 1)
        return pltpu.make_async_copy(xn_hbm.at[pl.ds(tok, 1), :], xs_hbm.at[pl.ds(dest_ref[0, a], 1), :], sem)

    def start(a, carry):
        row_copy(a).start()
        return carry

    def wait(a, carry):
        row_copy(a).wait()
        return carry

    lax.fori_loop(0, EXPERT_TOPK * DISPATCH_TOKENS, start, 0)
    lax.fori_loop(0, EXPERT_TOPK * DISPATCH_TOKENS, wait, 0)


def _dispatch(dest_tiles, xn, n_rows):
    n_tiles = TOKENS // DISPATCH_TOKENS
    return pl.pallas_call(
        _dispatch_kernel,
        grid=(n_tiles,),
        in_specs=[
            pl.BlockSpec((None, 1, EXPERT_TOPK * DISPATCH_TOKENS), lambda i: (i, 0, 0), memory_space=pltpu.SMEM),
            pl.BlockSpec(memory_space=pl.ANY),
            pl.BlockSpec(memory_space=pl.ANY),
        ],
        out_specs=pl.BlockSpec(memory_space=pl.ANY),
        out_shape=jax.ShapeDtypeStruct((n_rows, D_MODEL), F32),
        scratch_shapes=[pltpu.SemaphoreType.DMA(())],
        input_output_aliases={2: 0},
        compiler_params=pltpu.CompilerParams(
            dimension_semantics=("arbitrary",), vmem_limit_bytes=VMEM_LIMIT),
        name="moe_dispatch",
    )(dest_tiles, xn, jnp.zeros((n_rows, D_MODEL), F32))


def _combine_kernel(dest_ref, dest_next_ref, x1_ref, route_ref, ys_hbm, out_ref, buf, sem):
    i = pl.program_id(0)
    n = pl.num_programs(0)

    def row_copy(dref, a, slot):
        return pltpu.make_async_copy(
            ys_hbm.at[pl.ds(dref[0, a], 1), :],
            buf.at[slot, a & 1, pl.ds(lax.shift_right_logical(a, 1), 1), :],
            sem.at[slot])

    def start_tile(dref, slot):
        def body(a, carry):
            row_copy(dref, a, slot).start()
            return carry
        lax.fori_loop(0, EXPERT_TOPK * COMBINE_TOKENS, body, 0)

    slot = i % 2

    @pl.when(i == 0)
    def _first():
        start_tile(dest_ref, 0)

    @pl.when(i + 1 < n)
    def _prefetch():
        start_tile(dest_next_ref, 1 - slot)

    def wait(a, carry):
        row_copy(dest_ref, a, slot).wait()
        return carry

    lax.fori_loop(0, EXPERT_TOPK * COMBINE_TOKENS, wait, 0)
    w = route_ref[...]
    out_ref[...] = x1_ref[...] + w[:, 2:3] * buf[slot, 0] + w[:, 3:4] * buf[slot, 1]


def _combine(dest_tiles, x1, route, ys):
    n_tiles = TOKENS // COMBINE_TOKENS
    smem_tile = lambda index_map: pl.BlockSpec(
        (None, 1, EXPERT_TOPK * COMBINE_TOKENS), index_map, memory_space=pltpu.SMEM)
    return pl.pallas_call(
        _combine_kernel,
        grid=(n_tiles,),
        in_specs=[
            smem_tile(lambda i: (i, 0, 0)),
            smem_tile(lambda i: (jnp.minimum(i + 1, n_tiles - 1), 0, 0)),
            pl.BlockSpec((COMBINE_TOKENS, D_MODEL), lambda i: (i, 0)),
            pl.BlockSpec((COMBINE_TOKENS, LANES), lambda i: (i, 0)),
            pl.BlockSpec(memory_space=pl.ANY),
        ],
        out_specs=pl.BlockSpec((COMBINE_TOKENS, D_MODEL), lambda i: (i, 0)),
        out_shape=jax.ShapeDtypeStruct((TOKENS, D_MODEL), F32),
        scratch_shapes=[
            pltpu.VMEM((2, EXPERT_TOPK, COMBINE_TOKENS, D_MODEL), F32),
            pltpu.SemaphoreType.DMA((2,)),
        ],
        compiler_params=pltpu.CompilerParams(
            dimension_semantics=("arbitrary",), vmem_limit_bytes=VMEM_LIMIT),
        name="moe_combine",
    )(dest_tiles, dest_tiles, x1, route, ys)


def _rope_tables():
    inv = ROPE_THETA ** (-np.arange(ROPE_HALF, dtype=np.float64) * 2.0 / ROPE_DIM)
    ang = np.arange(SEQ, dtype=np.float64)[:, None] * inv[None, :]
    cos = np.ones((SEQ, A_HEAD_DIM))
    sin = np.zeros((SEQ, A_HEAD_DIM))
    cos[:, :ROPE_HALF] = np.cos(ang)
    cos[:, ROPE_HALF:ROPE_DIM] = np.cos(ang)
    sin[:, :ROPE_HALF] = -np.sin(ang)
    sin[:, ROPE_HALF:ROPE_DIM] = np.sin(ang)
    return (jnp.asarray(np.tile(cos, (1, 2)), F32), jnp.asarray(np.tile(sin, (1, 2)), F32))


def _split_w_in(w_in):
    sizes = [M_WIDTH, M_WIDTH, M_WIDTH, M_WIDTH, M_HEADS, M_HEADS, A_WIDTH, A_WIDTH, A_WIDTH, S_WIDTH]
    offs = np.cumsum([0] + sizes)
    mq, mk, mv, mo, mi, mf, aq, ak, av, su = (w_in[:, offs[n]:offs[n + 1]] for n in range(len(sizes)))
    w_main = jnp.concatenate([aq, ak, av, mq, mk, mv, mo, su], axis=1).astype(BF16)
    gate = jnp.concatenate([mi, mf], axis=1)
    w_gate = jnp.pad(gate, ((0, 0), (0, LANES - 2 * M_HEADS)))
    return w_main, w_gate, gate.T


def _s5_params(a_re, a_im, b_re, b_im, c_re, c_im, log_dt):
    dt = jnp.exp(log_dt)[:, None]
    mag = jnp.exp(a_re * dt)
    abar_re, abar_im = mag * jnp.cos(a_im * dt), mag * jnp.sin(a_im * dt)
    zr, zi = abar_re - 1.0, abar_im
    den = a_re * a_re + a_im * a_im
    fr, fi = (zr * a_re + zi * a_im) / den, (zi * a_re - zr * a_im) / den
    bbar_re = fr[..., None] * b_re - fi[..., None] * b_im
    bbar_im = fr[..., None] * b_im + fi[..., None] * b_re
    eye = jnp.eye(S_GROUPS, dtype=F32)
    dense_b = lambda t: jnp.einsum('gph,gk->ghkp', t, eye).reshape(S_WIDTH, S_FLAT).astype(BF16)
    dense_c = lambda t: jnp.einsum('ghp,gk->gpkh', t, eye).reshape(S_FLAT, S_WIDTH).astype(BF16)
    abar = jnp.stack([abar_re.reshape(S_FLAT), abar_im.reshape(S_FLAT)])
    return abar, dense_b(bbar_re), dense_b(bbar_im), dense_c(c_re), dense_c(c_im)


def _moe(x1, xn, route, w_gate, w_up, w_down):
    n_assign = TOKENS * EXPERT_TOPK
    e_flat = route[:, :EXPERT_TOPK].astype(jnp.int32).reshape(n_assign)
    experts = jnp.arange(N_EXPERTS, dtype=jnp.int32)
    counts = jnp.sum((e_flat[:, None] == experts[None, :]).astype(jnp.int32), axis=0)
    padded = (counts + DISPATCH_BLOCK - 1) // DISPATCH_BLOCK * DISPATCH_BLOCK
    pends = jnp.cumsum(padded)
    pstarts = pends - padded
    n_rows = (n_assign + N_EXPERTS * (DISPATCH_BLOCK - 1) + DISPATCH_BLOCK - 1) // DISPATCH_BLOCK * DISPATCH_BLOCK
    nblk = n_rows // DISPATCH_BLOCK
    blk_start = jnp.arange(nblk, dtype=jnp.int32) * DISPATCH_BLOCK
    blk_exp = jnp.minimum(jnp.sum((pends[None, :] <= blk_start[:, None]).astype(jnp.int32), axis=1),
                          N_EXPERTS - 1).astype(jnp.int32)
    pstart_col = jnp.broadcast_to(pstarts.astype(F32)[:, None], (N_EXPERTS, LANES))
    dest = _plan(e_flat.reshape(n_assign // PLAN_TILE, 1, PLAN_TILE), pstart_col)
    xs = _dispatch(dest.reshape(TOKENS // DISPATCH_TOKENS, 1, EXPERT_TOPK * DISPATCH_TOKENS), xn, n_rows)
    ys = _expert_ffn(blk_exp, xs, w_gate, w_up, w_down)
    return _combine(dest.reshape(TOKENS // COMBINE_TOKENS, 1, EXPERT_TOPK * COMBINE_TOKENS), x1, route, ys)


def kernel(x, norm1_g, w_in, m_bias_i, m_bias_f, m_conv_w, m_conv_b, m_out_g, a_q_g, a_k_g, a_out_g, s_a_re, s_a_im, s_b_re, s_b_im, s_c_re, s_c_im, s_d, s_log_dt, s_glu_w, s_glu_b, s_out_g, w_out, norm2_g, r_group_w, r_group_b, r_expert_w, r_expert_b, e_w_gate, e_w_up, e_w_down):
    cos_t, sin_t = _rope_tables()
    gmat = jnp.asarray(np.kron(np.eye(S_GROUPS), np.full((S_GROUP_DIM, S_GROUP_DIM), 1.0 / S_GROUP_DIM)), F32)
    xf = x.reshape(TOKENS, D_MODEL)
    for l in range(DEPTH):
        w_main, w_gate, w_gate_t = _split_w_in(w_in[l])
        aq, ak, av, mqk, mv, mo, su, gates, gatest = _inproj(
            xf, norm1_g[l].reshape(1, D_MODEL), w_main, w_gate, w_gate_t)

        bias = jnp.concatenate([m_bias_i[l], m_bias_f[l]])
        bias_row = jnp.pad(bias, (0, LANES - 2 * M_HEADS)).reshape(1, LANES)
        y_m = _mlstm(mqk, mv, mo, gates, gatest, m_conv_w[l], m_conv_b[l].reshape(1, 2 * M_WIDTH),
                     bias_row, bias.reshape(2 * M_HEADS, 1), m_out_g[l].reshape(1, M_WIDTH))

        y_a = _moba(aq, ak, av, cos_t, sin_t,
                    jnp.tile(a_q_g[l], 2).reshape(1, HEAD_PAIR), jnp.tile(a_k_g[l], 2).reshape(1, HEAD_PAIR),
                    a_out_g[l].reshape(A_WIDTH, 1))

        abar, bre, bim, cre, cim = _s5_params(s_a_re[l], s_a_im[l], s_b_re[l], s_b_im[l],
                                              s_c_re[l], s_c_im[l], s_log_dt[l])
        y_s = _s5(su.reshape(SEQ * BATCH, S_WIDTH), abar, bre, bim, cre, cim,
                  s_d[l].reshape(1, S_WIDTH), s_glu_w[l].astype(BF16), s_glu_b[l].reshape(1, S_WIDTH),
                  gmat, s_out_g[l].reshape(1, S_WIDTH))

        w_router = jnp.pad(jnp.concatenate([r_group_w[l], r_expert_w[l]], axis=1),
                           ((0, 0), (0, LANES - N_GROUPS - N_EXPERTS)))
        b_router = jnp.pad(jnp.concatenate([r_group_b[l], r_expert_b[l]]),
                           (0, LANES - N_GROUPS - N_EXPERTS)).reshape(1, LANES)
        x1, xn, route = _outproj(xf, y_m, y_a, y_s.reshape(SEQ, BATCH * S_WIDTH), w_out[l].astype(BF16),
                                 norm2_g[l].reshape(1, D_MODEL), w_router, b_router)
        xf = _moe(x1, xn, route, e_w_gate[l], e_w_up[l], e_w_down[l])
    return xf.reshape(BATCH, SEQ, D_MODEL)
```

```python
import functools
import math

import numpy as np
import jax
import jax.numpy as jnp
from jax import lax
from jax.experimental import pallas as pl
from jax.experimental.pallas import tpu as pltpu

F32 = jnp.float32
BF16 = jnp.bfloat16

D_MODEL = 1024
BATCH = 8
SEQ = 2048
DEPTH = 2
TOKENS = BATCH * SEQ

M_HEADS = 4
M_HEAD_DIM = 64
M_WIDTH = M_HEADS * M_HEAD_DIM
A_HEADS = 8
A_HEAD_DIM = 64
A_WIDTH = A_HEADS * A_HEAD_DIM
S_GROUPS = 16
S_GROUP_DIM = 16
S_WIDTH = S_GROUPS * S_GROUP_DIM
S_STATE = 64
S_FLAT = S_GROUPS * S_STATE
D_MIX = M_WIDTH + A_WIDTH + S_WIDTH

CONV_K = 4
MOBA_BLOCK = 256
MOBA_NB = SEQ // MOBA_BLOCK
MOBA_TOPK = 3
ROPE_THETA = 500000.0
ROPE_DIM = A_HEAD_DIM // 4
ROPE_HALF = ROPE_DIM // 2

N_GROUPS = 4
EXPERTS_PER_GROUP = 8
N_EXPERTS = N_GROUPS * EXPERTS_PER_GROUP
EXPERT_TOPK = 2
D_EXPERT = 512
DISPATCH_BLOCK = 128
EPS = 1e-6

LANES = 128
ROW_TILES = D_MODEL // LANES
HEAD_PAIR = 2 * A_HEAD_DIM

W_MAIN = 3 * A_WIDTH + 4 * M_WIDTH + S_WIDTH
TM_PROJ = 512
M_CHUNK = 256
M_NCHUNK = SEQ // M_CHUNK
S5_STEPS = 128
PLAN_TILE = 1024
DISPATCH_TOKENS = 512
COMBINE_TOKENS = 256
VMEM_LIMIT = 56 * 1024 * 1024

HIGHEST = lax.Precision.HIGHEST


def _dot(a, b, precision=None):
    return jnp.dot(a, b, preferred_element_type=F32, precision=precision)


def _dot_nt(a, b, precision=None):
    return lax.dot_general(a, b, (((1,), (1,)), ((), ())), preferred_element_type=F32, precision=precision)


def _dot_tn(a, b, precision=None):
    return lax.dot_general(a, b, (((0,), (0,)), ((), ())), preferred_element_type=F32, precision=precision)


def _log_sigmoid(x):
    return jnp.minimum(x, 0.0) - jnp.log1p(jnp.exp(-jnp.abs(x)))


def _sigmoid(x):
    return 1.0 / (1.0 + jnp.exp(-x))


def _inproj_kernel(x_ref, g_ref, w_ref, wg_ref, wgt_ref,
                   aq_ref, ak_ref, av_ref, mqk_ref, mv_ref, mo_ref, su_ref, gates_ref, gatest_ref):
    x = x_ref[...]
    h = x * lax.rsqrt(jnp.mean(x * x, axis=-1, keepdims=True) + EPS) * g_ref[...]
    hb = h.astype(BF16)
    off = 0
    for ref, width in ((aq_ref, A_WIDTH), (ak_ref, A_WIDTH), (av_ref, A_WIDTH), (mqk_ref, 2 * M_WIDTH),
                       (mv_ref, M_WIDTH), (mo_ref, M_WIDTH), (su_ref, S_WIDTH)):
        ref[...] = _dot(hb, w_ref[:, off:off + width]).astype(ref.dtype)
        off += width
    gates_ref[...] = _dot(h, wg_ref[...], precision=HIGHEST)
    gt = _dot_nt(wgt_ref[...], h, precision=HIGHEST)
    for c in range(TM_PROJ // M_CHUNK):
        gatest_ref[c] = gt[:, c * M_CHUNK:(c + 1) * M_CHUNK]


def _inproj(x, g, w_main, w_gate, w_gate_t):
    n_s = SEQ // TM_PROJ
    row = lambda b, s: (b * n_s + s, 0)
    const = lambda b, s: (0, 0)
    out_shapes = (
        jax.ShapeDtypeStruct((TOKENS, A_WIDTH), BF16),
        jax.ShapeDtypeStruct((TOKENS, A_WIDTH), BF16),
        jax.ShapeDtypeStruct((TOKENS, A_WIDTH), BF16),
        jax.ShapeDtypeStruct((TOKENS, 2 * M_WIDTH), BF16),
        jax.ShapeDtypeStruct((TOKENS, M_WIDTH), BF16),
        jax.ShapeDtypeStruct((TOKENS, M_WIDTH), BF16),
        jax.ShapeDtypeStruct((SEQ, BATCH * S_WIDTH), BF16),
        jax.ShapeDtypeStruct((TOKENS, LANES), F32),
        jax.ShapeDtypeStruct((TOKENS // M_CHUNK, 8, M_CHUNK), F32),
    )
    out_specs = (
        pl.BlockSpec((TM_PROJ, A_WIDTH), row),
        pl.BlockSpec((TM_PROJ, A_WIDTH), row),
        pl.BlockSpec((TM_PROJ, A_WIDTH), row),
        pl.BlockSpec((TM_PROJ, 2 * M_WIDTH), row),
        pl.BlockSpec((TM_PROJ, M_WIDTH), row),
        pl.BlockSpec((TM_PROJ, M_WIDTH), row),
        pl.BlockSpec((TM_PROJ, S_WIDTH), lambda b, s: (s, b)),
        pl.BlockSpec((TM_PROJ, LANES), row),
        pl.BlockSpec((TM_PROJ // M_CHUNK, 8, M_CHUNK), lambda b, s: (b * n_s + s, 0, 0)),
    )
    return pl.pallas_call(
        _inproj_kernel,
        grid=(BATCH, n_s),
        in_specs=[
            pl.BlockSpec((TM_PROJ, D_MODEL), row),
            pl.BlockSpec((1, D_MODEL), const),
            pl.BlockSpec((D_MODEL, W_MAIN), const),
            pl.BlockSpec((D_MODEL, LANES), const),
            pl.BlockSpec((8, D_MODEL), const),
        ],
        out_specs=out_specs,
        out_shape=out_shapes,
        compiler_params=pltpu.CompilerParams(
            dimension_semantics=("arbitrary", "arbitrary"), vmem_limit_bytes=VMEM_LIMIT),
        name="inproj",
    )(x, g, w_main, w_gate, w_gate_t)


def _mlstm_kernel(mqk_ref, mv_ref, mo_ref, gates_ref, gatest_ref, cw_ref, cb_ref, brow_ref, bcol_ref, og_ref,
                  y_ref, qk_s):
    L = M_CHUNK
    dh = M_HEAD_DIM
    w = cw_ref[...]
    cb = cb_ref[...]
    lane_q = lax.broadcasted_iota(jnp.int32, (1, 2 * M_WIDTH), 1) < M_WIDTH
    kscale = jnp.where(lane_q, 1.0, dh ** -0.5).astype(F32)
    for c in range(M_NCHUNK):
        halo = 16 if c > 0 else 0
        xt = mqk_ref[c * L - halo:(c + 1) * L, :].astype(F32)
        rows = lax.broadcasted_iota(jnp.int32, xt.shape, 0)
        y = cb + w[CONV_K - 1:CONV_K, :] * xt
        for k in range(1, CONV_K):
            sh = pltpu.roll(xt, k, 0)
            if c == 0:
                sh = jnp.where(rows >= k, sh, 0.0)
            y = y + w[CONV_K - 1 - k:CONV_K - k, :] * sh
        y = y[halo:, :]
        y = y * _sigmoid(y) * kscale
        qk_s[c * L:(c + 1) * L, :] = y.astype(BF16)

    r_i = lax.broadcasted_iota(jnp.int32, (L, L), 0)
    c_i = lax.broadcasted_iota(jnp.int32, (L, L), 1)
    causal = c_i <= r_i
    tri = causal.astype(F32)
    tri_u = (r_i <= c_i).astype(F32)
    pad_lane = lax.broadcasted_iota(jnp.int32, (L, LANES - dh), 1)
    ones_pad = jnp.where(pad_lane == 0, 1.0, 0.0).astype(BF16)
    og = og_ref[...]

    def chunk(c, carry):
        r0 = pl.multiple_of(c * L, L)
        gcol = gates_ref[pl.ds(r0, L), :] + brow_ref[...]
        grow = gatest_ref[c] + bcol_ref[...]
        cs_col = _dot(tri, _log_sigmoid(gcol), precision=HIGHEST)
        cs_row = _dot(_log_sigmoid(grow), tri_u, precision=HIGHEST)
        qk = qk_s[pl.ds(r0, L), :]
        vv = mv_ref[pl.ds(r0, L), :]
        new_carry = []
        hs = []
        for h in range(M_HEADS):
            cst, m = carry[h]
            q = qk[:, h * dh:(h + 1) * dh]
            k = qk[:, M_WIDTH + h * dh:M_WIDTH + (h + 1) * dh]
            v_aug = jnp.concatenate([vv[:, h * dh:(h + 1) * dh], ones_pad], axis=1)
            bc = cs_col[:, M_HEADS + h:M_HEADS + h + 1]
            br = cs_row[M_HEADS + h:M_HEADS + h + 1, :]
            ic = gcol[:, h:h + 1]
            ir = grow[h:h + 1, :]
            a = bc + m
            wmat = jnp.where(causal, bc - br + ir, -jnp.inf)
            mt = jnp.maximum(a, jnp.max(wmat, axis=1, keepdims=True))
            inter = jnp.exp(a - mt)
            sqk = _dot_nt(q, k) * jnp.exp(wmat - mt)
            nd = inter * _dot(q, cst.astype(BF16)) + _dot(sqk.astype(BF16), v_aug)
            num = nd[:, :dh]
            den = nd[:, dh:dh + 1]
            hh = num / jnp.maximum(jnp.abs(den), jnp.exp(-mt))
            hh = hh * lax.rsqrt(jnp.mean(hh * hh, axis=-1, keepdims=True) + EPS) * og[:, h * dh:(h + 1) * dh]
            hs.append(hh)
            bl = bc[L - 1:L, :]
            gs = bl - bc + ic
            m_new = jnp.maximum(bl + m, jnp.max(gs, axis=0, keepdims=True))
            decay = jnp.exp(bl + m - m_new)
            wts = jnp.exp(gs - m_new)
            kw = (k.astype(F32) * wts).astype(BF16)
            new_carry.append((decay * cst + _dot_tn(kw, v_aug), m_new))
        hcat = jnp.concatenate(hs, axis=1)
        o = mo_ref[pl.ds(r0, L), :].astype(F32)
        y_ref[pl.ds(r0, L), :] = (hcat * _sigmoid(o)).astype(y_ref.dtype)
        return tuple(new_carry)

    init = tuple((jnp.zeros((dh, LANES), F32), jnp.zeros((1, 1), F32)) for _ in range(M_HEADS))
    lax.fori_loop(0, M_NCHUNK, chunk, init)


def _mlstm(mqk, mv, mo, gates, gatest, conv_w, conv_b, bias_row, bias_col, out_g):
    per_b = lambda b: (b, 0)
    const = lambda b: (0, 0)
    return pl.pallas_call(
        _mlstm_kernel,
        grid=(BATCH,),
        in_specs=[
            pl.BlockSpec((SEQ, 2 * M_WIDTH), per_b),
            pl.BlockSpec((SEQ, M_WIDTH), per_b),
            pl.BlockSpec((SEQ, M_WIDTH), per_b),
            pl.BlockSpec((SEQ, LANES), per_b),
            pl.BlockSpec((M_NCHUNK, 8, M_CHUNK), lambda b: (b, 0, 0)),
            pl.BlockSpec((CONV_K, 2 * M_WIDTH), const),
            pl.BlockSpec((1, 2 * M_WIDTH), const),
            pl.BlockSpec((1, LANES), const),
            pl.BlockSpec((8, 1), const),
            pl.BlockSpec((1, M_WIDTH), const),
        ],
        out_specs=pl.BlockSpec((SEQ, M_WIDTH), per_b),
        out_shape=jax.ShapeDtypeStruct((TOKENS, M_WIDTH), BF16),
        scratch_shapes=[pltpu.VMEM((SEQ, 2 * M_WIDTH), BF16)],
        compiler_params=pltpu.CompilerParams(
            dimension_semantics=("arbitrary",), vmem_limit_bytes=VMEM_LIMIT),
        name="mlstm",
    )(mqk, mv, mo, gates, gatest, conv_w, conv_b, bias_row, bias_col, out_g)


def _qk_norm_rope(x, gain, cos, sin):
    lane = lax.broadcasted_iota(jnp.int32, x.shape, 1)
    first = lane < A_HEAD_DIM
    sq = x * x
    tot = jnp.sum(sq, axis=-1, keepdims=True)
    s0 = jnp.sum(jnp.where(first, sq, 0.0), axis=-1, keepdims=True)
    ms = jnp.where(first, s0, tot - s0) * (1.0 / A_HEAD_DIM)
    xn = x * lax.rsqrt(ms + EPS) * gain
    in_low = (lane % A_HEAD_DIM) < ROPE_HALF
    swapped = jnp.where(in_low, pltpu.roll(xn, LANES - ROPE_HALF, 1), pltpu.roll(xn, ROPE_HALF, 1))
    return xn * cos + swapped * sin


def _moba_kernel(aq_ref, ak_ref, av_ref, cosq_ref, sinq_ref, cos_ref, sin_ref, qg_ref, kg_ref, og_ref,
                 y_ref, kr_s, vt_s, km_s, sel_s):
    i = pl.program_id(2)
    blk = MOBA_BLOCK
    dh = A_HEAD_DIM

    @pl.when(i == 0)
    def _prep():
        kr = _qk_norm_rope(ak_ref[...].astype(F32), kg_ref[...], cos_ref[...], sin_ref[...])
        kr_s[...] = kr.astype(BF16)
        km_s[...] = jnp.mean(kr.reshape(MOBA_NB, blk, HEAD_PAIR), axis=1)
        for j in range(MOBA_NB):
            vt_s[j] = av_ref[j * blk:(j + 1) * blk, :].astype(F32).T.astype(BF16)

    qr = _qk_norm_rope(aq_ref[...].astype(F32), qg_ref[...], cosq_ref[...], sinq_ref[...])
    qt = qr.T
    km = km_s[...]
    km_lane = lax.broadcasted_iota(jnp.int32, km.shape, 1)
    qt_row = lax.broadcasted_iota(jnp.int32, qt.shape, 0)
    blk_id = lax.broadcasted_iota(jnp.int32, (MOBA_NB, blk), 0)
    past = blk_id < i
    k_idx = lax.broadcasted_iota(jnp.int32, (blk, blk), 0)
    q_idx = lax.broadcasted_iota(jnp.int32, (blk, blk), 1)
    og = og_ref[...]

    outs = []
    for h in range(2):
        head_lane = (km_lane < dh) if h == 0 else (km_lane >= dh)
        head_row = (qt_row < dh) if h == 0 else (qt_row >= dh)
        gate = _dot(jnp.where(head_lane, km, 0.0), qt, precision=HIGHEST)
        gate = jnp.where(past, gate, -jnp.inf)
        rank = jnp.zeros((MOBA_NB, blk), F32)
        for m_blk in range(MOBA_NB):
            gm = gate[m_blk:m_blk + 1, :]
            ahead = (gm > gate) | ((gm == gate) & (m_blk < blk_id))
            rank = rank + jnp.where(ahead, 1.0, 0.0)
        sel = past & (rank < float(MOBA_TOPK))
        sel_s[h] = jnp.where(sel, 0.0, -jnp.inf)

        qtm = (jnp.where(head_row, qt, 0.0) * (dh ** -0.5)).astype(BF16)
        r0 = pl.multiple_of(i * blk, blk)
        st = _dot(kr_s[pl.ds(r0, blk), :], qtm)
        st = jnp.where(k_idx <= q_idx, st, -jnp.inf)
        m0 = jnp.max(st, axis=0, keepdims=True)
        p = jnp.exp(st - m0)
        l0 = jnp.sum(p, axis=0, keepdims=True)
        acc0 = _dot(vt_s[i, h * dh:(h + 1) * dh, :], p.astype(BF16))

        def past_block(j, carry, h=h, qtm=qtm):
            m_run, l_run, acc = carry
            rj = pl.multiple_of(j * blk, blk)
            s = _dot(kr_s[pl.ds(rj, blk), :], qtm) + sel_s[h, pl.ds(j, 1), :]
            m_new = jnp.maximum(m_run, jnp.max(s, axis=0, keepdims=True))
            alpha = jnp.exp(m_run - m_new)
            pj = jnp.exp(s - m_new)
            l_new = alpha * l_run + jnp.sum(pj, axis=0, keepdims=True)
            acc_new = alpha * acc + _dot(vt_s[j, h * dh:(h + 1) * dh, :], pj.astype(BF16))
            return m_new, l_new, acc_new

        _, l_fin, acc = lax.fori_loop(0, i, past_block, (m0, l0, acc0))
        o = acc / l_fin
        o = o * lax.rsqrt(jnp.mean(o * o, axis=0, keepdims=True) + EPS) * og[h * dh:(h + 1) * dh, :]
        outs.append(o)
    y_ref[...] = jnp.concatenate(outs, axis=0).T.astype(y_ref.dtype)


def _moba(aq, ak, av, cos_t, sin_t, q_gain, k_gain, out_gain_col):
    n_pairs = A_HEADS // 2
    nb = MOBA_NB
    const = lambda b, p, i: (0, 0)
    return pl.pallas_call(
        _moba_kernel,
        grid=(BATCH, n_pairs, nb),
        in_specs=[
            pl.BlockSpec((MOBA_BLOCK, HEAD_PAIR), lambda b, p, i: (b * nb + i, p)),
            pl.BlockSpec((SEQ, HEAD_PAIR), lambda b, p, i: (b, p)),
            pl.BlockSpec((SEQ, HEAD_PAIR), lambda b, p, i: (b, p)),
            pl.BlockSpec((MOBA_BLOCK, HEAD_PAIR), lambda b, p, i: (i, 0)),
            pl.BlockSpec((MOBA_BLOCK, HEAD_PAIR), lambda b, p, i: (i, 0)),
            pl.BlockSpec((SEQ, HEAD_PAIR), const),
            pl.BlockSpec((SEQ, HEAD_PAIR), const),
            pl.BlockSpec((1, HEAD_PAIR), const),
            pl.BlockSpec((1, HEAD_PAIR), const),
            pl.BlockSpec((HEAD_PAIR, 1), lambda b, p, i: (p, 0)),
        ],
        out_specs=pl.BlockSpec((MOBA_BLOCK, HEAD_PAIR), lambda b, p, i: (b * nb + i, p)),
        out_shape=jax.ShapeDtypeStruct((TOKENS, A_WIDTH), BF16),
        scratch_shapes=[
            pltpu.VMEM((SEQ, HEAD_PAIR), BF16),
            pltpu.VMEM((nb, HEAD_PAIR, MOBA_BLOCK), BF16),
            pltpu.VMEM((nb, HEAD_PAIR), F32),
            pltpu.VMEM((2, nb, MOBA_BLOCK), F32),
        ],
        compiler_params=pltpu.CompilerParams(
            dimension_semantics=("arbitrary", "arbitrary", "arbitrary"), vmem_limit_bytes=VMEM_LIMIT),
        name="moba",
    )(aq, ak, av, cos_t, sin_t, cos_t, sin_t, q_gain, k_gain, out_gain_col)


def _s5_kernel(u_ref, abar_ref, bre_ref, bim_ref, cre_ref, cim_ref, d_ref, gw_ref, gb_ref, gmat_ref, og_ref,
               y_ref, xr_s, xi_s, st_s):
    step = pl.program_id(0)
    rows = S5_STEPS * BATCH
    tile = 256
    n_tiles = rows // tile

    @pl.when(step == 0)
    def _init():
        st_s[...] = jnp.zeros_like(st_s)

    for r in range(n_tiles):
        u = u_ref[r * tile:(r + 1) * tile, :]
        t0 = r * tile // BATCH
        xr_s[t0:t0 + tile // BATCH] = _dot(u, bre_ref[...]).reshape(tile // BATCH, BATCH, S_FLAT)
        xi_s[t0:t0 + tile // BATCH] = _dot(u, bim_ref[...]).reshape(tile // BATCH, BATCH, S_FLAT)

    ar = jnp.broadcast_to(abar_ref[0:1, :], (BATCH, S_FLAT))
    ai = jnp.broadcast_to(abar_ref[1:2, :], (BATCH, S_FLAT))

    def scan_step(t, carry):
        sr, si = carry
        nr = ar * sr - ai * si + xr_s[t]
        ni = ar * si + ai * sr + xi_s[t]
        xr_s[t] = nr
        xi_s[t] = ni
        return nr, ni

    sr, si = lax.fori_loop(0, S5_STEPS, scan_step, (st_s[0], st_s[1]), unroll=4)
    st_s[0] = sr
    st_s[1] = si

    for r in range(n_tiles):
        t0 = r * tile // BATCH
        xr = xr_s[t0:t0 + tile // BATCH].reshape(tile, S_FLAT).astype(BF16)
        xi = xi_s[t0:t0 + tile // BATCH].reshape(tile, S_FLAT).astype(BF16)
        u = u_ref[r * tile:(r + 1) * tile, :].astype(F32)
        y = _dot(xr, cre_ref[...]) - _dot(xi, cim_ref[...]) + d_ref[...] * u
        z = 0.5 * y * (1.0 + jnp.tanh(math.sqrt(2.0 / math.pi) * (y + 0.044715 * (y * y * y))))
        y2 = z * _sigmoid(_dot(z.astype(BF16), gw_ref[...]) + gb_ref[...])
        ms = _dot(y2 * y2, gmat_ref[...], precision=HIGHEST)
        y_ref[r * tile:(r + 1) * tile, :] = (y2 * lax.rsqrt(ms + EPS) * og_ref[...]).astype(y_ref.dtype)


def _s5(u_sb, abar, bre, bim, cre, cim, d_row, glu_w, glu_b, gmat, out_g):
    rows = S5_STEPS * BATCH
    const = lambda s: (0, 0)
    return pl.pallas_call(
        _s5_kernel,
        grid=(SEQ // S5_STEPS,),
        in_specs=[
            pl.BlockSpec((rows, S_WIDTH), lambda s: (s, 0)),
            pl.BlockSpec((2, S_FLAT), const),
            pl.BlockSpec((S_WIDTH, S_FLAT), const),
            pl.BlockSpec((S_WIDTH, S_FLAT), const),
            pl.BlockSpec((S_FLAT, S_WIDTH), const),
            pl.BlockSpec((S_FLAT, S_WIDTH), const),
            pl.BlockSpec((1, S_WIDTH), const),
            pl.BlockSpec((S_WIDTH, S_WIDTH), const),
            pl.BlockSpec((1, S_WIDTH), const),
            pl.BlockSpec((S_WIDTH, S_WIDTH), const),
            pl.BlockSpec((1, S_WIDTH), const),
        ],
        out_specs=pl.BlockSpec((rows, S_WIDTH), lambda s: (s, 0)),
        out_shape=jax.ShapeDtypeStruct((SEQ * BATCH, S_WIDTH), BF16),
        scratch_shapes=[
            pltpu.VMEM((S5_STEPS, BATCH, S_FLAT), F32),
            pltpu.VMEM((S5_STEPS, BATCH, S_FLAT), F32),
            pltpu.VMEM((2, BATCH, S_FLAT), F32),
        ],
        compiler_params=pltpu.CompilerParams(
            dimension_semantics=("arbitrary",), vmem_limit_bytes=VMEM_LIMIT),
        name="s5",
    )(u_sb, abar, bre, bim, cre, cim, d_row, glu_w, glu_b, gmat, out_g)


def _outproj_kernel(x_ref, ym_ref, ya_ref, ys_ref, w_ref, g_ref, wr_ref, br_ref,
                    x1_ref, xn_ref, route_ref):
    x1 = x_ref[...]
    x1 = x1 + _dot(ym_ref[...], w_ref[0:M_WIDTH, :])
    x1 = x1 + _dot(ya_ref[...], w_ref[M_WIDTH:M_WIDTH + A_WIDTH, :])
    x1 = x1 + _dot(ys_ref[...], w_ref[M_WIDTH + A_WIDTH:, :])
    x1_ref[...] = x1
    xn = x1 * lax.rsqrt(jnp.mean(x1 * x1, axis=-1, keepdims=True) + EPS) * g_ref[...]
    for j in range(ROW_TILES):
        xn_ref[:, j, :] = xn[:, j * LANES:(j + 1) * LANES]
    logits = _dot(xn, wr_ref[...], precision=HIGHEST) + br_ref[...]

    lane = lax.broadcasted_iota(jnp.int32, logits.shape, 1).astype(F32)
    is_grp = lane < N_GROUPS
    neg = -jnp.inf
    gl = jnp.where(is_grp, logits, neg)
    gmax = jnp.max(gl, axis=-1, keepdims=True)
    gsum = jnp.sum(jnp.where(is_grp, jnp.exp(logits - gmax), 0.0), axis=-1, keepdims=True)
    g_sel = jnp.min(jnp.where(gl == gmax, lane, float(LANES)), axis=-1, keepdims=True)
    lo = N_GROUPS + EXPERTS_PER_GROUP * g_sel
    el = jnp.where((lane >= lo) & (lane < lo + EXPERTS_PER_GROUP), logits, neg)
    t1 = jnp.max(el, axis=-1, keepdims=True)
    i1 = jnp.min(jnp.where(el == t1, lane, float(LANES)), axis=-1, keepdims=True)
    el2 = jnp.where(lane == i1, neg, el)
    t2 = jnp.max(el2, axis=-1, keepdims=True)
    i2 = jnp.min(jnp.where(el2 == t2, lane, float(LANES)), axis=-1, keepdims=True)
    e21 = jnp.exp(t2 - t1)
    w1 = 1.0 / ((1.0 + e21) * gsum)
    w2 = e21 / ((1.0 + e21) * gsum)
    route = jnp.where(lane == 0.0, i1 - N_GROUPS,
                      jnp.where(lane == 1.0, i2 - N_GROUPS,
                                jnp.where(lane == 2.0, w1, jnp.where(lane == 3.0, w2, 0.0))))
    route_ref[...] = route


def _outproj(x, y_m, y_a, y_s_sb, w_out, g2, w_router, b_router):
    n_s = SEQ // TM_PROJ
    row = lambda b, s: (b * n_s + s, 0)
    const = lambda b, s: (0, 0)
    return pl.pallas_call(
        _outproj_kernel,
        grid=(BATCH, n_s),
        in_specs=[
            pl.BlockSpec((TM_PROJ, D_MODEL), row),
            pl.BlockSpec((TM_PROJ, M_WIDTH), row),
            pl.BlockSpec((TM_PROJ, A_WIDTH), row),
            pl.BlockSpec((TM_PROJ, S_WIDTH), lambda b, s: (s, b)),
            pl.BlockSpec((D_MIX, D_MODEL), const),
            pl.BlockSpec((1, D_MODEL), const),
            pl.BlockSpec((D_MODEL, LANES), const),
            pl.BlockSpec((1, LANES), const),
        ],
        out_specs=(
            pl.BlockSpec((TM_PROJ, D_MODEL), row),
            pl.BlockSpec((TM_PROJ, ROW_TILES, LANES), lambda b, s: (b * n_s + s, 0, 0)),
            pl.BlockSpec((TM_PROJ, LANES), row),
        ),
        out_shape=(
            jax.ShapeDtypeStruct((TOKENS, D_MODEL), F32),
            jax.ShapeDtypeStruct((TOKENS, ROW_TILES, LANES), F32),
            jax.ShapeDtypeStruct((TOKENS, LANES), F32),
        ),
        compiler_params=pltpu.CompilerParams(
            dimension_semantics=("arbitrary", "arbitrary"), vmem_limit_bytes=VMEM_LIMIT),
        name="outproj",
    )(x, y_m, y_a, y_s_sb, w_out, g2, w_router, b_router)


def _ffn_kernel(be_ref, xs_ref, wg_ref, wu_ref, wd_ref, ys_ref, wg_s, wu_s, wd_s):
    i = pl.program_id(0)
    prev = be_ref[jnp.maximum(i - 1, 0)]

    @pl.when((i == 0) | (be_ref[i] != prev))
    def _load_expert():
        wg_s[...] = wg_ref[...].astype(BF16)
        wu_s[...] = wu_ref[...].astype(BF16)
        wd_s[...] = wd_ref[...].astype(BF16)

    xb = jnp.concatenate([xs_ref[:, j, :] for j in range(ROW_TILES)], axis=1).astype(BF16)
    g = _dot(xb, wg_s[...])
    u = _dot(xb, wu_s[...])
    hmid = (g * _sigmoid(g) * u).astype(BF16)
    y = _dot(hmid, wd_s[...])
    for j in range(ROW_TILES):
        ys_ref[:, j, :] = y[:, j * LANES:(j + 1) * LANES]


def _expert_ffn(blk_exp, xs, w_gate, w_up, w_down):
    n_rows = xs.shape[0]
    nblk = n_rows // DISPATCH_BLOCK
    grid_spec = pltpu.PrefetchScalarGridSpec(
        num_scalar_prefetch=1,
        grid=(nblk,),
        in_specs=[
            pl.BlockSpec((DISPATCH_BLOCK, ROW_TILES, LANES), lambda i, be: (i, 0, 0)),
            pl.BlockSpec((None, D_MODEL, D_EXPERT), lambda i, be: (be[i], 0, 0)),
            pl.BlockSpec((None, D_MODEL, D_EXPERT), lambda i, be: (be[i], 0, 0)),
            pl.BlockSpec((None, D_EXPERT, D_MODEL), lambda i, be: (be[i], 0, 0)),
        ],
        out_specs=pl.BlockSpec((DISPATCH_BLOCK, ROW_TILES, LANES), lambda i, be: (i, 0, 0)),
        scratch_shapes=[
            pltpu.VMEM((D_MODEL, D_EXPERT), BF16),
            pltpu.VMEM((D_MODEL, D_EXPERT), BF16),
            pltpu.VMEM((D_EXPERT, D_MODEL), BF16),
        ],
    )
    return pl.pallas_call(
        _ffn_kernel,
        grid_spec=grid_spec,
        out_shape=jax.ShapeDtypeStruct((n_rows, ROW_TILES, LANES), F32),
        compiler_params=pltpu.CompilerParams(
            dimension_semantics=("arbitrary",), vmem_limit_bytes=VMEM_LIMIT),
        name="expert_ffn",
    )(blk_exp, xs, w_gate, w_up, w_down)


def _plan_kernel(e_ref, ps_ref, dest_ref, run_s, tri_s):
    i = pl.program_id(0)

    @pl.when(i == 0)
    def _init():
        run_s[...] = ps_ref[...]
        s_i = lax.broadcasted_iota(jnp.int32, (PLAN_TILE, PLAN_TILE), 0)
        t_i = lax.broadcasted_iota(jnp.int32, (PLAN_TILE, PLAN_TILE), 1)
        tri_s[...] = (s_i < t_i).astype(BF16)

    expert = lax.broadcasted_iota(jnp.int32, (N_EXPERTS, PLAN_TILE), 0)
    onehot = jnp.where(e_ref[...] == expert, 1.0, 0.0)
    before = _dot(onehot.astype(BF16), tri_s[...])
    dest = jnp.sum(onehot * (before + run_s[:, 0:1]), axis=0, keepdims=True)
    dest_ref[...] = dest.astype(jnp.int32)
    run_s[...] = run_s[...] + jnp.sum(onehot, axis=1, keepdims=True)


def _plan(e_rows, pstart_col):
    n_tiles = e_rows.shape[0]
    return pl.pallas_call(
        _plan_kernel,
        grid=(n_tiles,),
        in_specs=[
            pl.BlockSpec((None, 1, PLAN_TILE), lambda i: (i, 0, 0)),
            pl.BlockSpec((N_EXPERTS, LANES), lambda i: (0, 0)),
        ],
        out_specs=pl.BlockSpec((None, 1, PLAN_TILE), lambda i: (i, 0, 0)),
        out_shape=jax.ShapeDtypeStruct((n_tiles, 1, PLAN_TILE), jnp.int32),
        scratch_shapes=[pltpu.VMEM((N_EXPERTS, LANES), F32), pltpu.VMEM((PLAN_TILE, PLAN_TILE), BF16)],
        compiler_params=pltpu.CompilerParams(
            dimension_semantics=("arbitrary",), vmem_limit_bytes=VMEM_LIMIT),
        name="moe_plan",
    )(e_rows, pstart_col)


def _dispatch_kernel(d0_ref, d1_ref, xn_ref, xs_init_hbm, xs_hbm, sem):
    del xs_init_hbm

    def row_copy(dref, t):
        return pltpu.make_async_copy(xn_ref.at[pl.ds(t, 1)], xs_hbm.at[pl.ds(dref[0, t], 1)], sem)

    def start(t, carry):
        row_copy(d0_ref, t).start()
        row_copy(d1_ref, t).start()
        return carry

    lax.fori_loop(0, DISPATCH_TOKENS, start, 0, unroll=8)
    for _ in range(EXPERT_TOPK):
        pltpu.make_async_copy(xn_ref, xs_hbm.at[pl.ds(0, DISPATCH_TOKENS)], sem).wait()


def _dispatch(dest, xn3, n_rows):
    n_tiles = TOKENS // DISPATCH_TOKENS
    dest4 = dest.reshape(EXPERT_TOPK, n_tiles, 1, DISPATCH_TOKENS)
    smem_tile = lambda k: pl.BlockSpec((None, None, 1, DISPATCH_TOKENS), lambda i: (k, i, 0, 0),
                                       memory_space=pltpu.SMEM)
    return pl.pallas_call(
        _dispatch_kernel,
        grid=(n_tiles,),
        in_specs=[
            smem_tile(0),
            smem_tile(1),
            pl.BlockSpec((DISPATCH_TOKENS, ROW_TILES, LANES), lambda i: (i, 0, 0)),
            pl.BlockSpec(memory_space=pl.ANY),
        ],
        out_specs=pl.BlockSpec(memory_space=pl.ANY),
        out_shape=jax.ShapeDtypeStruct((n_rows, ROW_TILES, LANES), F32),
        scratch_shapes=[pltpu.SemaphoreType.DMA(())],
        input_output_aliases={3: 0},
        compiler_params=pltpu.CompilerParams(
            dimension_semantics=("arbitrary",), vmem_limit_bytes=VMEM_LIMIT),
        name="moe_dispatch",
    )(dest4, dest4, xn3, jnp.zeros((n_rows, ROW_TILES, LANES), F32))


def _combine_kernel(d0_ref, d1_ref, d0_next_ref, d1_next_ref, x1_ref, route_ref, ys_hbm, out_ref, buf, sem):
    i = pl.program_id(0)
    n = pl.num_programs(0)

    def start_tile(drefs, slot):
        def body(t, carry):
            for k in range(EXPERT_TOPK):
                pltpu.make_async_copy(ys_hbm.at[pl.ds(drefs[k][0, t], 1)], buf.at[slot, k, pl.ds(t, 1)],
                                      sem.at[slot]).start()
            return carry
        lax.fori_loop(0, COMBINE_TOKENS, body, 0, unroll=8)

    slot = i % 2

    @pl.when(i == 0)
    def _first():
        start_tile((d0_ref, d1_ref), 0)

    @pl.when(i + 1 < n)
    def _prefetch():
        start_tile((d0_next_ref, d1_next_ref), 1 - slot)

    for k in range(EXPERT_TOPK):
        pltpu.make_async_copy(ys_hbm.at[pl.ds(0, COMBINE_TOKENS)], buf.at[slot, k], sem.at[slot]).wait()
    w = route_ref[...]
    w1 = w[:, 2:3]
    w2 = w[:, 3:4]
    for j in range(ROW_TILES):
        cols = slice(j * LANES, (j + 1) * LANES)
        out_ref[:, cols] = x1_ref[:, cols] + w1 * buf[slot, 0, :, j, :] + w2 * buf[slot, 1, :, j, :]


def _combine(dest, x1, route, ys3):
    n_tiles = TOKENS // COMBINE_TOKENS
    dest4 = dest.reshape(EXPERT_TOPK, n_tiles, 1, COMBINE_TOKENS)
    smem_tile = lambda k, nxt: pl.BlockSpec(
        (None, None, 1, COMBINE_TOKENS), lambda i: (k, jnp.minimum(i + nxt, n_tiles - 1), 0, 0),
        memory_space=pltpu.SMEM)
    return pl.pallas_call(
        _combine_kernel,
        grid=(n_tiles,),
        in_specs=[
            smem_tile(0, 0), smem_tile(1, 0), smem_tile(0, 1), smem_tile(1, 1),
            pl.BlockSpec((COMBINE_TOKENS, D_MODEL), lambda i: (i, 0)),
            pl.BlockSpec((COMBINE_TOKENS, LANES), lambda i: (i, 0)),
            pl.BlockSpec(memory_space=pl.ANY),
        ],
        out_specs=pl.BlockSpec((COMBINE_TOKENS, D_MODEL), lambda i: (i, 0)),
        out_shape=jax.ShapeDtypeStruct((TOKENS, D_MODEL), F32),
        scratch_shapes=[
            pltpu.VMEM((2, EXPERT_TOPK, COMBINE_TOKENS, ROW_TILES, LANES), F32),
            pltpu.SemaphoreType.DMA((2,)),
        ],
        compiler_params=pltpu.CompilerParams(
            dimension_semantics=("arbitrary",), vmem_limit_bytes=VMEM_LIMIT),
        name="moe_combine",
    )(dest4, dest4, dest4, dest4, x1, route, ys3)


def _rope_tables():
    inv = ROPE_THETA ** (-np.arange(ROPE_HALF, dtype=np.float64) * 2.0 / ROPE_DIM)
    ang = np.arange(SEQ, dtype=np.float64)[:, None] * inv[None, :]
    cos = np.ones((SEQ, A_HEAD_DIM))
    sin = np.zeros((SEQ, A_HEAD_DIM))
    cos[:, :ROPE_HALF] = np.cos(ang)
    cos[:, ROPE_HALF:ROPE_DIM] = np.cos(ang)
    sin[:, :ROPE_HALF] = -np.sin(ang)
    sin[:, ROPE_HALF:ROPE_DIM] = np.sin(ang)
    return (jnp.asarray(np.tile(cos, (1, 2)), F32), jnp.asarray(np.tile(sin, (1, 2)), F32))


def _split_w_in(w_in):
    sizes = [M_WIDTH, M_WIDTH, M_WIDTH, M_WIDTH, M_HEADS, M_HEADS, A_WIDTH, A_WIDTH, A_WIDTH, S_WIDTH]
    offs = np.cumsum([0] + sizes)
    mq, mk, mv, mo, mi, mf, aq, ak, av, su = (w_in[:, offs[n]:offs[n + 1]] for n in range(len(sizes)))
    w_main = jnp.concatenate([aq, ak, av, mq, mk, mv, mo, su], axis=1).astype(BF16)
    gate = jnp.concatenate([mi, mf], axis=1)
    w_gate = jnp.pad(gate, ((0, 0), (0, LANES - 2 * M_HEADS)))
    return w_main, w_gate, gate.T


def _s5_params(a_re, a_im, b_re, b_im, c_re, c_im, log_dt):
    dt = jnp.exp(log_dt)[:, None]
    mag = jnp.exp(a_re * dt)
    abar_re, abar_im = mag * jnp.cos(a_im * dt), mag * jnp.sin(a_im * dt)
    zr, zi = abar_re - 1.0, abar_im
    den = a_re * a_re + a_im * a_im
    fr, fi = (zr * a_re + zi * a_im) / den, (zi * a_re - zr * a_im) / den
    bbar_re = fr[..., None] * b_re - fi[..., None] * b_im
    bbar_im = fr[..., None] * b_im + fi[..., None] * b_re
    eye = jnp.eye(S_GROUPS, dtype=F32)
    dense_b = lambda t: jnp.einsum('gph,gk->ghkp', t, eye).reshape(S_WIDTH, S_FLAT).astype(BF16)
    dense_c = lambda t: jnp.einsum('ghp,gk->gpkh', t, eye).reshape(S_FLAT, S_WIDTH).astype(BF16)
    abar = jnp.stack([abar_re.reshape(S_FLAT), abar_im.reshape(S_FLAT)])
    return abar, dense_b(bbar_re), dense_b(bbar_im), dense_c(c_re), dense_c(c_im)


def _moe(x1, xn, route, w_gate, w_up, w_down):
    n_assign = TOKENS * EXPERT_TOPK
    e_flat = route[:, :EXPERT_TOPK].astype(jnp.int32).T.reshape(n_assign)
    experts = jnp.arange(N_EXPERTS, dtype=jnp.int32)
    counts = jnp.sum((e_flat[:, None] == experts[None, :]).astype(jnp.int32), axis=0)
    padded = (counts + DISPATCH_BLOCK - 1) // DISPATCH_BLOCK * DISPATCH_BLOCK
    pends = jnp.cumsum(padded)
    pstarts = pends - padded
    n_rows = (n_assign + N_EXPERTS * (DISPATCH_BLOCK - 1) + DISPATCH_BLOCK - 1) // DISPATCH_BLOCK * DISPATCH_BLOCK
    nblk = n_rows // DISPATCH_BLOCK
    blk_start = jnp.arange(nblk, dtype=jnp.int32) * DISPATCH_BLOCK
    blk_exp = jnp.minimum(jnp.sum((pends[None, :] <= blk_start[:, None]).astype(jnp.int32), axis=1),
                          N_EXPERTS - 1).astype(jnp.int32)
    pstart_col = jnp.broadcast_to(pstarts.astype(F32)[:, None], (N_EXPERTS, LANES))
    dest = _plan(e_flat.reshape(n_assign // PLAN_TILE, 1, PLAN_TILE), pstart_col)
    xs = _dispatch(dest, xn, n_rows)
    ys = _expert_ffn(blk_exp, xs, w_gate, w_up, w_down)
    return _combine(dest, x1, route, ys)


def kernel(x, norm1_g, w_in, m_bias_i, m_bias_f, m_conv_w, m_conv_b, m_out_g, a_q_g, a_k_g, a_out_g, s_a_re, s_a_im, s_b_re, s_b_im, s_c_re, s_c_im, s_d, s_log_dt, s_glu_w, s_glu_b, s_out_g, w_out, norm2_g, r_group_w, r_group_b, r_expert_w, r_expert_b, e_w_gate, e_w_up, e_w_down):
    cos_t, sin_t = _rope_tables()
    gmat = jnp.asarray(np.kron(np.eye(S_GROUPS), np.full((S_GROUP_DIM, S_GROUP_DIM), 1.0 / S_GROUP_DIM)), F32)
    xf = x.reshape(TOKENS, D_MODEL)
    for l in range(DEPTH):
        w_main, w_gate, w_gate_t = _split_w_in(w_in[l])
        aq, ak, av, mqk, mv, mo, su, gates, gatest = _inproj(
            xf, norm1_g[l].reshape(1, D_MODEL), w_main, w_gate, w_gate_t)

        bias = jnp.concatenate([m_bias_i[l], m_bias_f[l]])
        bias_row = jnp.pad(bias, (0, LANES - 2 * M_HEADS)).reshape(1, LANES)
        y_m = _mlstm(mqk, mv, mo, gates, gatest, m_conv_w[l], m_conv_b[l].reshape(1, 2 * M_WIDTH),
                     bias_row, bias.reshape(2 * M_HEADS, 1), m_out_g[l].reshape(1, M_WIDTH))

        y_a = _moba(aq, ak, av, cos_t, sin_t,
                    jnp.tile(a_q_g[l], 2).reshape(1, HEAD_PAIR), jnp.tile(a_k_g[l], 2).reshape(1, HEAD_PAIR),
                    a_out_g[l].reshape(A_WIDTH, 1))

        abar, bre, bim, cre, cim = _s5_params(s_a_re[l], s_a_im[l], s_b_re[l], s_b_im[l],
                                              s_c_re[l], s_c_im[l], s_log_dt[l])
        y_s = _s5(su.reshape(SEQ * BATCH, S_WIDTH), abar, bre, bim, cre, cim,
                  s_d[l].reshape(1, S_WIDTH), s_glu_w[l].astype(BF16), s_glu_b[l].reshape(1, S_WIDTH),
                  gmat, s_out_g[l].reshape(1, S_WIDTH))

        w_router = jnp.pad(jnp.concatenate([r_group_w[l], r_expert_w[l]], axis=1),
                           ((0, 0), (0, LANES - N_GROUPS - N_EXPERTS)))
        b_router = jnp.pad(jnp.concatenate([r_group_b[l], r_expert_b[l]]),
                           (0, LANES - N_GROUPS - N_EXPERTS)).reshape(1, LANES)
        x1, xn, route = _outproj(xf, y_m, y_a, y_s.reshape(SEQ, BATCH * S_WIDTH), w_out[l].astype(BF16),
                                 norm2_g[l].reshape(1, D_MODEL), w_router, b_router)
        xf = _moe(x1, xn, route, e_w_gate[l], e_w_up[l], e_w_down[l])
    return xf.reshape(BATCH, SEQ, D_MODEL)
```

```python
import functools
import math

import numpy as np
import jax
import jax.numpy as jnp
from jax import lax
from jax.experimental import pallas as pl
from jax.experimental.pallas import tpu as pltpu

F32 = jnp.float32
BF16 = jnp.bfloat16

D_MODEL = 1024
BATCH = 8
SEQ = 2048
DEPTH = 2
TOKENS = BATCH * SEQ

M_HEADS = 4
M_HEAD_DIM = 64
M_WIDTH = M_HEADS * M_HEAD_DIM
A_HEADS = 8
A_HEAD_DIM = 64
A_WIDTH = A_HEADS * A_HEAD_DIM
S_GROUPS = 16
S_GROUP_DIM = 16
S_WIDTH = S_GROUPS * S_GROUP_DIM
S_STATE = 64
S_FLAT = S_GROUPS * S_STATE
D_MIX = M_WIDTH + A_WIDTH + S_WIDTH

CONV_K = 4
MOBA_BLOCK = 256
MOBA_NB = SEQ // MOBA_BLOCK
MOBA_TOPK = 3
ROPE_THETA = 500000.0
ROPE_DIM = A_HEAD_DIM // 4
ROPE_HALF = ROPE_DIM // 2

N_GROUPS = 4
EXPERTS_PER_GROUP = 8
N_EXPERTS = N_GROUPS * EXPERTS_PER_GROUP
EXPERT_TOPK = 2
D_EXPERT = 512
DISPATCH_BLOCK = 128
EPS = 1e-6

LANES = 128
ROW_TILES = D_MODEL // LANES
HEAD_PAIR = 2 * A_HEAD_DIM

W_MAIN = 3 * A_WIDTH + 4 * M_WIDTH + S_WIDTH
TM_PROJ = 512
M_CHUNK = 256
M_NCHUNK = SEQ // M_CHUNK
S5_STEPS = 128
FFN_BLOCK = 256
PLAN_TILE = 1024
DISPATCH_TOKENS = 512
COMBINE_TOKENS = 256
VMEM_LIMIT = 56 * 1024 * 1024

HIGHEST = lax.Precision.HIGHEST


def _dot(a, b, precision=None):
    return jnp.dot(a, b, preferred_element_type=F32, precision=precision)


def _dot_nt(a, b, precision=None):
    return lax.dot_general(a, b, (((1,), (1,)), ((), ())), preferred_element_type=F32, precision=precision)


def _dot_tn(a, b, precision=None):
    return lax.dot_general(a, b, (((0,), (0,)), ((), ())), preferred_element_type=F32, precision=precision)


def _split_dot(a, w_hi, w_lo):
    a_hi = a.astype(BF16)
    a_lo = (a - a_hi.astype(F32)).astype(BF16)
    return _dot(a_hi, w_hi) + _dot(a_lo, w_hi) + _dot(a_hi, w_lo)


def _hi_lo(w):
    w_hi = w.astype(BF16)
    return jnp.stack([w_hi, (w - w_hi.astype(F32)).astype(BF16)])


def _log_sigmoid(x):
    return jnp.minimum(x, 0.0) - jnp.log1p(jnp.exp(-jnp.abs(x)))


def _sigmoid(x):
    return 1.0 / (1.0 + jnp.exp(-x))


def _inproj_kernel(x_ref, g_ref, w_ref, wg_ref,
                   aq_ref, ak_ref, av_ref, mqk_ref, mv_ref, mo_ref, su_ref, gates_ref, gatest_ref):
    x = x_ref[...]
    h = x * lax.rsqrt(jnp.mean(x * x, axis=-1, keepdims=True) + EPS) * g_ref[...]
    hb = h.astype(BF16)
    off = 0
    for ref, width in ((aq_ref, A_WIDTH), (ak_ref, A_WIDTH), (av_ref, A_WIDTH), (mqk_ref, 2 * M_WIDTH),
                       (mv_ref, M_WIDTH), (mo_ref, M_WIDTH), (su_ref, S_WIDTH)):
        ref[...] = _dot(hb, w_ref[:, off:off + width]).astype(ref.dtype)
        off += width
    gates = _split_dot(h, wg_ref[0], wg_ref[1])
    gates_ref[...] = gates
    gt = gates.T[:8, :]
    for c in range(TM_PROJ // M_CHUNK):
        gatest_ref[c] = gt[:, c * M_CHUNK:(c + 1) * M_CHUNK]


def _inproj(x, g, w_main, w_gate):
    n_s = SEQ // TM_PROJ
    row = lambda b, s: (b * n_s + s, 0)
    const = lambda b, s: (0, 0)
    out_shapes = (
        jax.ShapeDtypeStruct((TOKENS, A_WIDTH), BF16),
        jax.ShapeDtypeStruct((TOKENS, A_WIDTH), BF16),
        jax.ShapeDtypeStruct((TOKENS, A_WIDTH), BF16),
        jax.ShapeDtypeStruct((TOKENS, 2 * M_WIDTH), BF16),
        jax.ShapeDtypeStruct((TOKENS, M_WIDTH), BF16),
        jax.ShapeDtypeStruct((TOKENS, M_WIDTH), BF16),
        jax.ShapeDtypeStruct((SEQ, BATCH * S_WIDTH), BF16),
        jax.ShapeDtypeStruct((TOKENS, LANES), F32),
        jax.ShapeDtypeStruct((TOKENS // M_CHUNK, 8, M_CHUNK), F32),
    )
    out_specs = (
        pl.BlockSpec((TM_PROJ, A_WIDTH), row),
        pl.BlockSpec((TM_PROJ, A_WIDTH), row),
        pl.BlockSpec((TM_PROJ, A_WIDTH), row),
        pl.BlockSpec((TM_PROJ, 2 * M_WIDTH), row),
        pl.BlockSpec((TM_PROJ, M_WIDTH), row),
        pl.BlockSpec((TM_PROJ, M_WIDTH), row),
        pl.BlockSpec((TM_PROJ, S_WIDTH), lambda b, s: (s, b)),
        pl.BlockSpec((TM_PROJ, LANES), row),
        pl.BlockSpec((TM_PROJ // M_CHUNK, 8, M_CHUNK), lambda b, s: (b * n_s + s, 0, 0)),
    )
    return pl.pallas_call(
        _inproj_kernel,
        grid=(BATCH, n_s),
        in_specs=[
            pl.BlockSpec((TM_PROJ, D_MODEL), row),
            pl.BlockSpec((1, D_MODEL), const),
            pl.BlockSpec((D_MODEL, W_MAIN), const),
            pl.BlockSpec((2, D_MODEL, LANES), lambda b, s: (0, 0, 0)),
        ],
        out_specs=out_specs,
        out_shape=out_shapes,
        compiler_params=pltpu.CompilerParams(
            dimension_semantics=("arbitrary", "arbitrary"), vmem_limit_bytes=VMEM_LIMIT),
        name="inproj",
    )(x, g, w_main, w_gate)


def _mlstm_kernel(mqk_ref, mv_ref, mo_ref, gates_ref, gatest_ref, cw_ref, cb_ref, brow_ref, bcol_ref, og_ref,
                  y_ref, qk_s):
    L = M_CHUNK
    dh = M_HEAD_DIM
    w = cw_ref[...]
    cb = cb_ref[...]
    lane_q = lax.broadcasted_iota(jnp.int32, (1, 2 * M_WIDTH), 1) < M_WIDTH
    kscale = jnp.where(lane_q, 1.0, dh ** -0.5).astype(F32)
    for c in range(M_NCHUNK):
        halo = 16 if c > 0 else 0
        xt = mqk_ref[c * L - halo:(c + 1) * L, :].astype(F32)
        rows = lax.broadcasted_iota(jnp.int32, xt.shape, 0)
        y = cb + w[CONV_K - 1:CONV_K, :] * xt
        for k in range(1, CONV_K):
            sh = pltpu.roll(xt, k, 0)
            if c == 0:
                sh = jnp.where(rows >= k, sh, 0.0)
            y = y + w[CONV_K - 1 - k:CONV_K - k, :] * sh
        y = y[halo:, :]
        y = y * _sigmoid(y) * kscale
        qk_s[c * L:(c + 1) * L, :] = y.astype(BF16)

    r_i = lax.broadcasted_iota(jnp.int32, (L, L), 0)
    c_i = lax.broadcasted_iota(jnp.int32, (L, L), 1)
    causal = c_i <= r_i
    tri = causal.astype(F32)
    tri_u = (r_i <= c_i).astype(F32)
    pad_lane = lax.broadcasted_iota(jnp.int32, (L, LANES - dh), 1)
    ones_pad = jnp.where(pad_lane == 0, 1.0, 0.0).astype(BF16)
    og = og_ref[...]

    def chunk(c, carry):
        r0 = pl.multiple_of(c * L, L)
        gcol = gates_ref[pl.ds(r0, L), :] + brow_ref[...]
        grow = gatest_ref[c] + bcol_ref[...]
        cs_col = _dot(tri, _log_sigmoid(gcol), precision=HIGHEST)
        cs_row = _dot(_log_sigmoid(grow), tri_u, precision=HIGHEST)
        qk = qk_s[pl.ds(r0, L), :]
        vv = mv_ref[pl.ds(r0, L), :]
        new_carry = []
        hs = []
        for h in range(M_HEADS):
            cst, m = carry[h]
            q = qk[:, h * dh:(h + 1) * dh]
            k = qk[:, M_WIDTH + h * dh:M_WIDTH + (h + 1) * dh]
            v_aug = jnp.concatenate([vv[:, h * dh:(h + 1) * dh], ones_pad], axis=1)
            bc = cs_col[:, M_HEADS + h:M_HEADS + h + 1]
            br = cs_row[M_HEADS + h:M_HEADS + h + 1, :]
            ic = gcol[:, h:h + 1]
            ir = grow[h:h + 1, :]
            a = bc + m
            wmat = jnp.where(causal, bc - br + ir, -jnp.inf)
            mt = jnp.maximum(a, jnp.max(wmat, axis=1, keepdims=True))
            inter = jnp.exp(a - mt)
            sqk = _dot_nt(q, k) * jnp.exp(wmat - mt)
            nd = inter * _dot(q, cst.astype(BF16)) + _dot(sqk.astype(BF16), v_aug)
            num = nd[:, :dh]
            den = nd[:, dh:dh + 1]
            hh = num / jnp.maximum(jnp.abs(den), jnp.exp(-mt))
            hh = hh * lax.rsqrt(jnp.mean(hh * hh, axis=-1, keepdims=True) + EPS) * og[:, h * dh:(h + 1) * dh]
            hs.append(hh)
            bl = bc[L - 1:L, :]
            gs = bl - bc + ic
            m_new = jnp.maximum(bl + m, jnp.max(gs, axis=0, keepdims=True))
            decay = jnp.exp(bl + m - m_new)
            wts = jnp.exp(gs - m_new)
            kw = (k.astype(F32) * wts).astype(BF16)
            new_carry.append((decay * cst + _dot_tn(kw, v_aug), m_new))
        hcat = jnp.concatenate(hs, axis=1)
        o = mo_ref[pl.ds(r0, L), :].astype(F32)
        y_ref[pl.ds(r0, L), :] = (hcat * _sigmoid(o)).astype(y_ref.dtype)
        return tuple(new_carry)

    init = tuple((jnp.zeros((dh, LANES), F32), jnp.zeros((1, 1), F32)) for _ in range(M_HEADS))
    lax.fori_loop(0, M_NCHUNK, chunk, init)


def _mlstm(mqk, mv, mo, gates, gatest, conv_w, conv_b, bias_row, bias_col, out_g):
    per_b = lambda b: (b, 0)
    const = lambda b: (0, 0)
    return pl.pallas_call(
        _mlstm_kernel,
        grid=(BATCH,),
        in_specs=[
            pl.BlockSpec((SEQ, 2 * M_WIDTH), per_b),
            pl.BlockSpec((SEQ, M_WIDTH), per_b),
            pl.BlockSpec((SEQ, M_WIDTH), per_b),
            pl.BlockSpec((SEQ, LANES), per_b),
            pl.BlockSpec((M_NCHUNK, 8, M_CHUNK), lambda b: (b, 0, 0)),
            pl.BlockSpec((CONV_K, 2 * M_WIDTH), const),
            pl.BlockSpec((1, 2 * M_WIDTH), const),
            pl.BlockSpec((1, LANES), const),
            pl.BlockSpec((8, 1), const),
            pl.BlockSpec((1, M_WIDTH), const),
        ],
        out_specs=pl.BlockSpec((SEQ, M_WIDTH), per_b),
        out_shape=jax.ShapeDtypeStruct((TOKENS, M_WIDTH), BF16),
        scratch_shapes=[pltpu.VMEM((SEQ, 2 * M_WIDTH), BF16)],
        compiler_params=pltpu.CompilerParams(
            dimension_semantics=("arbitrary",), vmem_limit_bytes=VMEM_LIMIT),
        name="mlstm",
    )(mqk, mv, mo, gates, gatest, conv_w, conv_b, bias_row, bias_col, out_g)


def _qk_norm_rope(x, gain, cos, sin):
    lane = lax.broadcasted_iota(jnp.int32, x.shape, 1)
    first = lane < A_HEAD_DIM
    sq = x * x
    tot = jnp.sum(sq, axis=-1, keepdims=True)
    s0 = jnp.sum(jnp.where(first, sq, 0.0), axis=-1, keepdims=True)
    ms = jnp.where(first, s0, tot - s0) * (1.0 / A_HEAD_DIM)
    xn = x * lax.rsqrt(ms + EPS) * gain
    in_low = (lane % A_HEAD_DIM) < ROPE_HALF
    swapped = jnp.where(in_low, pltpu.roll(xn, LANES - ROPE_HALF, 1), pltpu.roll(xn, ROPE_HALF, 1))
    return xn * cos + swapped * sin


def _moba_kernel(aq_ref, ak_ref, av_ref, cos_ref, sin_ref, qg_ref, kg_ref, og_ref, y_ref, kr_s, s_s):
    blk = MOBA_BLOCK
    dh = A_HEAD_DIM
    kr = _qk_norm_rope(ak_ref[...].astype(F32), kg_ref[...], cos_ref[...], sin_ref[...])
    kr_s[...] = kr.astype(BF16)
    km = jnp.mean(kr.reshape(MOBA_NB, blk, HEAD_PAIR), axis=1)
    km_first = lax.broadcasted_iota(jnp.int32, km.shape, 1) < dh
    km_heads = (jnp.where(km_first, km, 0.0), jnp.where(km_first, 0.0, km))
    first = lax.broadcasted_iota(jnp.int32, (blk, HEAD_PAIR), 1) < dh
    blk_id = lax.broadcasted_iota(jnp.int32, (MOBA_NB, blk), 0)
    eye = (lax.broadcasted_iota(jnp.int32, (MOBA_NB, LANES), 0)
           == lax.broadcasted_iota(jnp.int32, (MOBA_NB, LANES), 1)).astype(F32)
    q_pos = lax.broadcasted_iota(jnp.int32, (2 * blk, blk), 0) % blk
    k_pos = lax.broadcasted_iota(jnp.int32, (2 * blk, blk), 1)
    causal = k_pos <= q_pos
    og = og_ref[...]

    for i in range(MOBA_NB):
        rows = slice(i * blk, (i + 1) * blk)
        qr = _qk_norm_rope(aq_ref[rows, :].astype(F32), qg_ref[...], cos_ref[rows, :], sin_ref[rows, :])
        qs = qr * (dh ** -0.5)
        q2 = jnp.concatenate([jnp.where(first, qs, 0.0), jnp.where(first, 0.0, qs)], axis=0).astype(BF16)

        bias = None
        if i > MOBA_TOPK:
            past = blk_id < i
            cols = []
            for kmh in km_heads:
                gate = jnp.where(past, _dot_nt(kmh, qr, precision=HIGHEST), -jnp.inf)
                rank = jnp.zeros((MOBA_NB, blk), F32)
                for m_blk in range(i):
                    gm = gate[m_blk:m_blk + 1, :]
                    ahead = (gm > gate) | ((gm == gate) & (m_blk < blk_id))
                    rank = rank + jnp.where(ahead, 1.0, 0.0)
                sel = jnp.where(past & (rank < float(MOBA_TOPK)), 1.0, 0.0)
                cols.append(_dot_tn(sel, eye))
            bias = jnp.where(jnp.concatenate(cols, axis=0) > 0.5, 0.0, -jnp.inf)

        m_run = None
        for j in range(i + 1):
            s = _dot_nt(q2, kr_s[j * blk:(j + 1) * blk, :])
            if j == i:
                s = jnp.where(causal, s, -jnp.inf)
            elif bias is not None:
                s = s + bias[:, j:j + 1]
            s_s[:, j * blk:(j + 1) * blk] = s
            part = jnp.maximum(s[:, :LANES], s[:, LANES:])
            m_run = part if m_run is None else jnp.maximum(m_run, part)
        m_fin = jnp.max(m_run, axis=1, keepdims=True)

        l_run = jnp.zeros((2 * blk, LANES), F32)
        acc = jnp.zeros((2 * blk, HEAD_PAIR), F32)
        for j in range(i + 1):
            p = jnp.exp(s_s[:, j * blk:(j + 1) * blk] - m_fin)
            l_run = l_run + (p[:, :LANES] + p[:, LANES:])
            acc = acc + _dot(p.astype(BF16), av_ref[j * blk:(j + 1) * blk, :])
        o2 = acc / jnp.sum(l_run, axis=1, keepdims=True)
        o = jnp.where(first, o2[:blk, :], o2[blk:, :])
        sq = o * o
        tot = jnp.sum(sq, axis=-1, keepdims=True)
        s0 = jnp.sum(jnp.where(first, sq, 0.0), axis=-1, keepdims=True)
        ms = jnp.where(first, s0, tot - s0) * (1.0 / dh)
        y_ref[rows, :] = (o * lax.rsqrt(ms + EPS) * og).astype(y_ref.dtype)


def _moba(aq, ak, av, cos_t, sin_t, q_gain, k_gain, out_gain):
    n_pairs = A_HEADS // 2
    const = lambda b, p: (0, 0)
    pair = lambda b, p: (b, p)
    return pl.pallas_call(
        _moba_kernel,
        grid=(BATCH, n_pairs),
        in_specs=[
            pl.BlockSpec((SEQ, HEAD_PAIR), pair),
            pl.BlockSpec((SEQ, HEAD_PAIR), pair),
            pl.BlockSpec((SEQ, HEAD_PAIR), pair),
            pl.BlockSpec((SEQ, HEAD_PAIR), const),
            pl.BlockSpec((SEQ, HEAD_PAIR), const),
            pl.BlockSpec((1, HEAD_PAIR), const),
            pl.BlockSpec((1, HEAD_PAIR), const),
            pl.BlockSpec((None, 1, HEAD_PAIR), lambda b, p: (p, 0, 0)),
        ],
        out_specs=pl.BlockSpec((SEQ, HEAD_PAIR), pair),
        out_shape=jax.ShapeDtypeStruct((TOKENS, A_WIDTH), BF16),
        scratch_shapes=[
            pltpu.VMEM((SEQ, HEAD_PAIR), BF16),
            pltpu.VMEM((2 * MOBA_BLOCK, SEQ), F32),
        ],
        compiler_params=pltpu.CompilerParams(
            dimension_semantics=("arbitrary", "arbitrary"), vmem_limit_bytes=VMEM_LIMIT),
        name="moba",
    )(aq, ak, av, cos_t, sin_t, q_gain, k_gain, out_gain)


def _s5_kernel(u_ref, abar_ref, bre_ref, bim_ref, cre_ref, cim_ref, d_ref, gw_ref, gb_ref, gmat_ref, og_ref,
               y_ref, xr_s, xi_s, st_s):
    step = pl.program_id(0)
    rows = S5_STEPS * BATCH
    tile = 256
    n_tiles = rows // tile

    @pl.when(step == 0)
    def _init():
        st_s[...] = jnp.zeros_like(st_s)

    for r in range(n_tiles):
        u = u_ref[r * tile:(r + 1) * tile, :]
        t0 = r * tile // BATCH
        xr_s[t0:t0 + tile // BATCH] = _dot(u, bre_ref[...]).reshape(tile // BATCH, BATCH, S_FLAT)
        xi_s[t0:t0 + tile // BATCH] = _dot(u, bim_ref[...]).reshape(tile // BATCH, BATCH, S_FLAT)

    ar = jnp.broadcast_to(abar_ref[0:1, :], (BATCH, S_FLAT))
    ai = jnp.broadcast_to(abar_ref[1:2, :], (BATCH, S_FLAT))

    def scan_step(t, carry):
        sr, si = carry
        nr = ar * sr - ai * si + xr_s[t]
        ni = ar * si + ai * sr + xi_s[t]
        xr_s[t] = nr
        xi_s[t] = ni
        return nr, ni

    sr, si = lax.fori_loop(0, S5_STEPS, scan_step, (st_s[0], st_s[1]), unroll=4)
    st_s[0] = sr
    st_s[1] = si

    for r in range(n_tiles):
        t0 = r * tile // BATCH
        xr = xr_s[t0:t0 + tile // BATCH].reshape(tile, S_FLAT).astype(BF16)
        xi = xi_s[t0:t0 + tile // BATCH].reshape(tile, S_FLAT).astype(BF16)
        u = u_ref[r * tile:(r + 1) * tile, :].astype(F32)
        y = _dot(xr, cre_ref[...]) - _dot(xi, cim_ref[...]) + d_ref[...] * u
        z = 0.5 * y * (1.0 + jnp.tanh(math.sqrt(2.0 / math.pi) * (y + 0.044715 * (y * y * y))))
        y2 = z * _sigmoid(_dot(z.astype(BF16), gw_ref[...]) + gb_ref[...])
        ms = _dot(y2 * y2, gmat_ref[...], precision=HIGHEST)
        y_ref[r * tile:(r + 1) * tile, :] = (y2 * lax.rsqrt(ms + EPS) * og_ref[...]).astype(y_ref.dtype)


def _s5(u_sb, abar, bre, bim, cre, cim, d_row, glu_w, glu_b, gmat, out_g):
    rows = S5_STEPS * BATCH
    const = lambda s: (0, 0)
    return pl.pallas_call(
        _s5_kernel,
        grid=(SEQ // S5_STEPS,),
        in_specs=[
            pl.BlockSpec((rows, S_WIDTH), lambda s: (s, 0)),
            pl.BlockSpec((2, S_FLAT), const),
            pl.BlockSpec((S_WIDTH, S_FLAT), const),
            pl.BlockSpec((S_WIDTH, S_FLAT), const),
            pl.BlockSpec((S_FLAT, S_WIDTH), const),
            pl.BlockSpec((S_FLAT, S_WIDTH), const),
            pl.BlockSpec((1, S_WIDTH), const),
            pl.BlockSpec((S_WIDTH, S_WIDTH), const),
            pl.BlockSpec((1, S_WIDTH), const),
            pl.BlockSpec((S_WIDTH, S_WIDTH), const),
            pl.BlockSpec((1, S_WIDTH), const),
        ],
        out_specs=pl.BlockSpec((rows, S_WIDTH), lambda s: (s, 0)),
        out_shape=jax.ShapeDtypeStruct((SEQ * BATCH, S_WIDTH), BF16),
        scratch_shapes=[
            pltpu.VMEM((S5_STEPS, BATCH, S_FLAT), F32),
            pltpu.VMEM((S5_STEPS, BATCH, S_FLAT), F32),
            pltpu.VMEM((2, BATCH, S_FLAT), F32),
        ],
        compiler_params=pltpu.CompilerParams(
            dimension_semantics=("arbitrary",), vmem_limit_bytes=VMEM_LIMIT),
        name="s5",
    )(u_sb, abar, bre, bim, cre, cim, d_row, glu_w, glu_b, gmat, out_g)


def _outproj_kernel(x_ref, ym_ref, ya_ref, ys_ref, w_ref, g_ref, wr_ref, br_ref,
                    x1_ref, xn_ref, route_ref):
    x1 = x_ref[...]
    x1 = x1 + _dot(ym_ref[...], w_ref[0:M_WIDTH, :])
    x1 = x1 + _dot(ya_ref[...], w_ref[M_WIDTH:M_WIDTH + A_WIDTH, :])
    x1 = x1 + _dot(ys_ref[...], w_ref[M_WIDTH + A_WIDTH:, :])
    x1_ref[...] = x1
    xn = x1 * lax.rsqrt(jnp.mean(x1 * x1, axis=-1, keepdims=True) + EPS) * g_ref[...]
    for j in range(ROW_TILES):
        xn_ref[pl.ds(j, TM_PROJ, stride=ROW_TILES), :] = xn[:, j * LANES:(j + 1) * LANES]
    logits = _split_dot(xn, wr_ref[0], wr_ref[1]) + br_ref[...]

    lane = lax.broadcasted_iota(jnp.int32, logits.shape, 1).astype(F32)
    is_grp = lane < N_GROUPS
    neg = -jnp.inf
    gl = jnp.where(is_grp, logits, neg)
    gmax = jnp.max(gl, axis=-1, keepdims=True)
    gsum = jnp.sum(jnp.where(is_grp, jnp.exp(logits - gmax), 0.0), axis=-1, keepdims=True)
    g_sel = jnp.min(jnp.where(gl == gmax, lane, float(LANES)), axis=-1, keepdims=True)
    lo = N_GROUPS + EXPERTS_PER_GROUP * g_sel
    el = jnp.where((lane >= lo) & (lane < lo + EXPERTS_PER_GROUP), logits, neg)
    t1 = jnp.max(el, axis=-1, keepdims=True)
    i1 = jnp.min(jnp.where(el == t1, lane, float(LANES)), axis=-1, keepdims=True)
    el2 = jnp.where(lane == i1, neg, el)
    t2 = jnp.max(el2, axis=-1, keepdims=True)
    i2 = jnp.min(jnp.where(el2 == t2, lane, float(LANES)), axis=-1, keepdims=True)
    e21 = jnp.exp(t2 - t1)
    w1 = 1.0 / ((1.0 + e21) * gsum)
    w2 = e21 / ((1.0 + e21) * gsum)
    route = jnp.where(lane == 0.0, i1 - N_GROUPS,
                      jnp.where(lane == 1.0, i2 - N_GROUPS,
                                jnp.where(lane == 2.0, w1, jnp.where(lane == 3.0, w2, 0.0))))
    route_ref[...] = route


def _outproj(x, y_m, y_a, y_s_sb, w_out, g2, w_router, b_router):
    n_s = SEQ // TM_PROJ
    row = lambda b, s: (b * n_s + s, 0)
    const = lambda b, s: (0, 0)
    return pl.pallas_call(
        _outproj_kernel,
        grid=(BATCH, n_s),
        in_specs=[
            pl.BlockSpec((TM_PROJ, D_MODEL), row),
            pl.BlockSpec((TM_PROJ, M_WIDTH), row),
            pl.BlockSpec((TM_PROJ, A_WIDTH), row),
            pl.BlockSpec((TM_PROJ, S_WIDTH), lambda b, s: (s, b)),
            pl.BlockSpec((D_MIX, D_MODEL), const),
            pl.BlockSpec((1, D_MODEL), const),
            pl.BlockSpec((2, D_MODEL, LANES), lambda b, s: (0, 0, 0)),
            pl.BlockSpec((1, LANES), const),
        ],
        out_specs=(
            pl.BlockSpec((TM_PROJ, D_MODEL), row),
            pl.BlockSpec((TM_PROJ * ROW_TILES, LANES), row),
            pl.BlockSpec((TM_PROJ, LANES), row),
        ),
        out_shape=(
            jax.ShapeDtypeStruct((TOKENS, D_MODEL), F32),
            jax.ShapeDtypeStruct((TOKENS * ROW_TILES, LANES), F32),
            jax.ShapeDtypeStruct((TOKENS, LANES), F32),
        ),
        compiler_params=pltpu.CompilerParams(
            dimension_semantics=("arbitrary", "arbitrary"), vmem_limit_bytes=VMEM_LIMIT),
        name="outproj",
    )(x, y_m, y_a, y_s_sb, w_out, g2, w_router, b_router)


def _ffn_kernel(be_ref, xs_ref, wg_ref, wu_ref, wd_ref, ys_ref, wg_s, wu_s, wd_s):
    i = pl.program_id(0)
    prev = be_ref[jnp.maximum(i - 1, 0)]

    @pl.when((i == 0) | (be_ref[i] != prev))
    def _load_expert():
        wg_s[...] = wg_ref[...].astype(BF16)
        wu_s[...] = wu_ref[...].astype(BF16)
        wd_s[...] = wd_ref[...].astype(BF16)

    xb = jnp.concatenate([xs_ref[pl.ds(j, FFN_BLOCK, stride=ROW_TILES), :] for j in range(ROW_TILES)],
                         axis=1).astype(BF16)
    g = _dot(xb, wg_s[...])
    u = _dot(xb, wu_s[...])
    hmid = (g * _sigmoid(g) * u).astype(BF16)
    y = _dot(hmid, wd_s[...])
    for j in range(ROW_TILES):
        ys_ref[pl.ds(j, FFN_BLOCK, stride=ROW_TILES), :] = y[:, j * LANES:(j + 1) * LANES]


def _expert_ffn(layer, blk_exp, xs, w_gate, w_up, w_down):
    nblk = xs.shape[0] // (FFN_BLOCK * ROW_TILES)
    rows = pl.BlockSpec((FFN_BLOCK * ROW_TILES, LANES), lambda i, be: (i, 0))
    grid_spec = pltpu.PrefetchScalarGridSpec(
        num_scalar_prefetch=1,
        grid=(nblk,),
        in_specs=[
            rows,
            pl.BlockSpec((None, None, D_MODEL, D_EXPERT), lambda i, be: (layer, be[i], 0, 0)),
            pl.BlockSpec((None, None, D_MODEL, D_EXPERT), lambda i, be: (layer, be[i], 0, 0)),
            pl.BlockSpec((None, None, D_EXPERT, D_MODEL), lambda i, be: (layer, be[i], 0, 0)),
        ],
        out_specs=rows,
        scratch_shapes=[
            pltpu.VMEM((D_MODEL, D_EXPERT), BF16),
            pltpu.VMEM((D_MODEL, D_EXPERT), BF16),
            pltpu.VMEM((D_EXPERT, D_MODEL), BF16),
        ],
    )
    return pl.pallas_call(
        _ffn_kernel,
        grid_spec=grid_spec,
        out_shape=jax.ShapeDtypeStruct(xs.shape, F32),
        compiler_params=pltpu.CompilerParams(
            dimension_semantics=("arbitrary",), vmem_limit_bytes=VMEM_LIMIT),
        name="expert_ffn",
    )(blk_exp, xs, w_gate, w_up, w_down)


def _plan_kernel(e_ref, ps_ref, dest_ref, run_s, tri_s):
    i = pl.program_id(0)

    @pl.when(i == 0)
    def _init():
        run_s[...] = ps_ref[...]
        s_i = lax.broadcasted_iota(jnp.int32, (PLAN_TILE, PLAN_TILE), 0)
        t_i = lax.broadcasted_iota(jnp.int32, (PLAN_TILE, PLAN_TILE), 1)
        tri_s[...] = (s_i < t_i).astype(BF16)

    expert = lax.broadcasted_iota(jnp.int32, (N_EXPERTS, PLAN_TILE), 0)
    onehot = jnp.where(e_ref[...] == expert, 1.0, 0.0)
    before = _dot(onehot.astype(BF16), tri_s[...])
    dest = jnp.sum(onehot * (before + run_s[:, 0:1]), axis=0, keepdims=True)
    dest_ref[...] = dest.astype(jnp.int32)
    run_s[...] = run_s[...] + jnp.sum(onehot, axis=1, keepdims=True)


def _plan(e_rows, pstart_col):
    n_tiles = e_rows.shape[0]
    return pl.pallas_call(
        _plan_kernel,
        grid=(n_tiles,),
        in_specs=[
            pl.BlockSpec((None, 1, PLAN_TILE), lambda i: (i, 0, 0)),
            pl.BlockSpec((N_EXPERTS, LANES), lambda i: (0, 0)),
        ],
        out_specs=pl.BlockSpec((None, 1, PLAN_TILE), lambda i: (i, 0, 0)),
        out_shape=jax.ShapeDtypeStruct((n_tiles, 1, PLAN_TILE), jnp.int32),
        scratch_shapes=[pltpu.VMEM((N_EXPERTS, LANES), F32), pltpu.VMEM((PLAN_TILE, PLAN_TILE), BF16)],
        compiler_params=pltpu.CompilerParams(
            dimension_semantics=("arbitrary",), vmem_limit_bytes=VMEM_LIMIT),
        name="moe_plan",
    )(e_rows, pstart_col)


def _dispatch_kernel(d0_ref, d1_ref, xn_ref, xs_init_hbm, xs_hbm, sem):
    del xs_init_hbm

    def row_copy(dref, t):
        src = xn_ref.at[pl.ds(pl.multiple_of(t * ROW_TILES, ROW_TILES), ROW_TILES), :]
        dst = xs_hbm.at[pl.ds(pl.multiple_of(dref[0, t] * ROW_TILES, ROW_TILES), ROW_TILES), :]
        return pltpu.make_async_copy(src, dst, sem)

    def start(t, carry):
        row_copy(d0_ref, t).start()
        row_copy(d1_ref, t).start()
        return carry

    lax.fori_loop(0, DISPATCH_TOKENS, start, 0, unroll=8)
    for _ in range(EXPERT_TOPK):
        pltpu.make_async_copy(xn_ref, xs_hbm.at[pl.ds(0, DISPATCH_TOKENS * ROW_TILES), :], sem).wait()


def _dispatch(dest, xn, n_rows):
    n_tiles = TOKENS // DISPATCH_TOKENS
    dest4 = dest.reshape(EXPERT_TOPK, n_tiles, 1, DISPATCH_TOKENS)
    smem_tile = lambda k: pl.BlockSpec((None, None, 1, DISPATCH_TOKENS), lambda i: (k, i, 0, 0),
                                       memory_space=pltpu.SMEM)
    return pl.pallas_call(
        _dispatch_kernel,
        grid=(n_tiles,),
        in_specs=[
            smem_tile(0),
            smem_tile(1),
            pl.BlockSpec((DISPATCH_TOKENS * ROW_TILES, LANES), lambda i: (i, 0)),
            pl.BlockSpec(memory_space=pl.ANY),
        ],
        out_specs=pl.BlockSpec(memory_space=pl.ANY),
        out_shape=jax.ShapeDtypeStruct((n_rows * ROW_TILES, LANES), F32),
        scratch_shapes=[pltpu.SemaphoreType.DMA(())],
        input_output_aliases={3: 0},
        compiler_params=pltpu.CompilerParams(
            dimension_semantics=("arbitrary",), vmem_limit_bytes=VMEM_LIMIT),
        name="moe_dispatch",
    )(dest4, dest4, xn, jnp.zeros((n_rows * ROW_TILES, LANES), F32))


def _combine_kernel(d0_ref, d1_ref, d0_next_ref, d1_next_ref, x1_ref, route_ref, ys_hbm, out_ref, buf, sem):
    i = pl.program_id(0)
    n = pl.num_programs(0)

    def start_tile(drefs, slot):
        def body(t, carry):
            for k in range(EXPERT_TOPK):
                src = ys_hbm.at[pl.ds(pl.multiple_of(drefs[k][0, t] * ROW_TILES, ROW_TILES), ROW_TILES), :]
                dst = buf.at[slot, k, pl.ds(pl.multiple_of(t * ROW_TILES, ROW_TILES), ROW_TILES), :]
                pltpu.make_async_copy(src, dst, sem.at[slot]).start()
            return carry
        lax.fori_loop(0, COMBINE_TOKENS, body, 0, unroll=8)

    slot = i % 2

    @pl.when(i == 0)
    def _first():
        start_tile((d0_ref, d1_ref), 0)

    @pl.when(i + 1 < n)
    def _prefetch():
        start_tile((d0_next_ref, d1_next_ref), 1 - slot)

    for k in range(EXPERT_TOPK):
        pltpu.make_async_copy(ys_hbm.at[pl.ds(0, COMBINE_TOKENS * ROW_TILES), :], buf.at[slot, k],
                              sem.at[slot]).wait()
    w = route_ref[...]
    w1 = w[:, 2:3]
    w2 = w[:, 3:4]
    for j in range(ROW_TILES):
        cols = slice(j * LANES, (j + 1) * LANES)
        chunk = pl.ds(j, COMBINE_TOKENS, stride=ROW_TILES)
        out_ref[:, cols] = x1_ref[:, cols] + w1 * buf[slot, 0, chunk, :] + w2 * buf[slot, 1, chunk, :]


def _combine(dest, x1, route, ys3):
    n_tiles = TOKENS // COMBINE_TOKENS
    dest4 = dest.reshape(EXPERT_TOPK, n_tiles, 1, COMBINE_TOKENS)
    smem_tile = lambda k, nxt: pl.BlockSpec(
        (None, None, 1, COMBINE_TOKENS), lambda i: (k, jnp.minimum(i + nxt, n_tiles - 1), 0, 0),
        memory_space=pltpu.SMEM)
    return pl.pallas_call(
        _combine_kernel,
        grid=(n_tiles,),
        in_specs=[
            smem_tile(0, 0), smem_tile(1, 0), smem_tile(0, 1), smem_tile(1, 1),
            pl.BlockSpec((COMBINE_TOKENS, D_MODEL), lambda i: (i, 0)),
            pl.BlockSpec((COMBINE_TOKENS, LANES), lambda i: (i, 0)),
            pl.BlockSpec(memory_space=pl.ANY),
        ],
        out_specs=pl.BlockSpec((COMBINE_TOKENS, D_MODEL), lambda i: (i, 0)),
        out_shape=jax.ShapeDtypeStruct((TOKENS, D_MODEL), F32),
        scratch_shapes=[
            pltpu.VMEM((2, EXPERT_TOPK, COMBINE_TOKENS * ROW_TILES, LANES), F32),
            pltpu.SemaphoreType.DMA((2,)),
        ],
        compiler_params=pltpu.CompilerParams(
            dimension_semantics=("arbitrary",), vmem_limit_bytes=VMEM_LIMIT),
        name="moe_combine",
    )(dest4, dest4, dest4, dest4, x1, route, ys3)


def _rope_tables():
    inv = ROPE_THETA ** (-np.arange(ROPE_HALF, dtype=np.float64) * 2.0 / ROPE_DIM)
    ang = np.arange(SEQ, dtype=np.float64)[:, None] * inv[None, :]
    cos = np.ones((SEQ, A_HEAD_DIM))
    sin = np.zeros((SEQ, A_HEAD_DIM))
    cos[:, :ROPE_HALF] = np.cos(ang)
    cos[:, ROPE_HALF:ROPE_DIM] = np.cos(ang)
    sin[:, :ROPE_HALF] = -np.sin(ang)
    sin[:, ROPE_HALF:ROPE_DIM] = np.sin(ang)
    return (jnp.asarray(np.tile(cos, (1, 2)), F32), jnp.asarray(np.tile(sin, (1, 2)), F32))


def _split_w_in(w_in):
    sizes = [M_WIDTH, M_WIDTH, M_WIDTH, M_WIDTH, M_HEADS, M_HEADS, A_WIDTH, A_WIDTH, A_WIDTH, S_WIDTH]
    offs = np.cumsum([0] + sizes)
    mq, mk, mv, mo, mi, mf, aq, ak, av, su = (w_in[:, offs[n]:offs[n + 1]] for n in range(len(sizes)))
    w_main = jnp.concatenate([aq, ak, av, mq, mk, mv, mo, su], axis=1).astype(BF16)
    gate = jnp.concatenate([mi, mf], axis=1)
    return w_main, _hi_lo(jnp.pad(gate, ((0, 0), (0, LANES - 2 * M_HEADS))))


def _s5_params(a_re, a_im, b_re, b_im, c_re, c_im, log_dt):
    dt = jnp.exp(log_dt)[:, None]
    mag = jnp.exp(a_re * dt)
    abar_re, abar_im = mag * jnp.cos(a_im * dt), mag * jnp.sin(a_im * dt)
    zr, zi = abar_re - 1.0, abar_im
    den = a_re * a_re + a_im * a_im
    fr, fi = (zr * a_re + zi * a_im) / den, (zi * a_re - zr * a_im) / den
    bbar_re = fr[..., None] * b_re - fi[..., None] * b_im
    bbar_im = fr[..., None] * b_im + fi[..., None] * b_re
    eye = jnp.eye(S_GROUPS, dtype=F32)
    dense_b = lambda t: jnp.einsum('gph,gk->ghkp', t, eye).reshape(S_WIDTH, S_FLAT).astype(BF16)
    dense_c = lambda t: jnp.einsum('ghp,gk->gpkh', t, eye).reshape(S_FLAT, S_WIDTH).astype(BF16)
    abar = jnp.stack([abar_re.reshape(S_FLAT), abar_im.reshape(S_FLAT)])
    return abar, dense_b(bbar_re), dense_b(bbar_im), dense_c(c_re), dense_c(c_im)


def _moe(layer, x1, xn, route, w_gate, w_up, w_down):
    n_assign = TOKENS * EXPERT_TOPK
    e_flat = route[:, :EXPERT_TOPK].astype(jnp.int32).T.reshape(n_assign)
    experts = jnp.arange(N_EXPERTS, dtype=jnp.int32)
    counts = jnp.sum((e_flat[:, None] == experts[None, :]).astype(jnp.int32), axis=0)
    padded = (counts + FFN_BLOCK - 1) // FFN_BLOCK * FFN_BLOCK
    pends = jnp.cumsum(padded)
    pstarts = pends - padded
    n_rows = (n_assign + N_EXPERTS * (FFN_BLOCK - 1) + FFN_BLOCK - 1) // FFN_BLOCK * FFN_BLOCK
    nblk = n_rows // FFN_BLOCK
    blk_start = jnp.arange(nblk, dtype=jnp.int32) * FFN_BLOCK
    blk_exp = jnp.minimum(jnp.sum((pends[None, :] <= blk_start[:, None]).astype(jnp.int32), axis=1),
                          N_EXPERTS - 1).astype(jnp.int32)
    pstart_col = jnp.broadcast_to(pstarts.astype(F32)[:, None], (N_EXPERTS, LANES))
    dest = _plan(e_flat.reshape(n_assign // PLAN_TILE, 1, PLAN_TILE), pstart_col)
    xs = _dispatch(dest, xn, n_rows)
    ys = _expert_ffn(layer, blk_exp, xs, w_gate, w_up, w_down)
    return _combine(dest, x1, route, ys)


def kernel(x, norm1_g, w_in, m_bias_i, m_bias_f, m_conv_w, m_conv_b, m_out_g, a_q_g, a_k_g, a_out_g, s_a_re, s_a_im, s_b_re, s_b_im, s_c_re, s_c_im, s_d, s_log_dt, s_glu_w, s_glu_b, s_out_g, w_out, norm2_g, r_group_w, r_group_b, r_expert_w, r_expert_b, e_w_gate, e_w_up, e_w_down):
    cos_t, sin_t = _rope_tables()
    gmat = jnp.asarray(np.kron(np.eye(S_GROUPS), np.full((S_GROUP_DIM, S_GROUP_DIM), 1.0 / S_GROUP_DIM)), F32)
    xf = x.reshape(TOKENS, D_MODEL)
    for l in range(DEPTH):
        w_main, w_gate = _split_w_in(w_in[l])
        aq, ak, av, mqk, mv, mo, su, gates, gatest = _inproj(xf, norm1_g[l].reshape(1, D_MODEL), w_main, w_gate)

        bias = jnp.concatenate([m_bias_i[l], m_bias_f[l]])
        bias_row = jnp.pad(bias, (0, LANES - 2 * M_HEADS)).reshape(1, LANES)
        y_m = _mlstm(mqk, mv, mo, gates, gatest, m_conv_w[l], m_conv_b[l].reshape(1, 2 * M_WIDTH),
                     bias_row, bias.reshape(2 * M_HEADS, 1), m_out_g[l].reshape(1, M_WIDTH))

        y_a = _moba(aq, ak, av, cos_t, sin_t,
                    jnp.tile(a_q_g[l], 2).reshape(1, HEAD_PAIR), jnp.tile(a_k_g[l], 2).reshape(1, HEAD_PAIR),
                    a_out_g[l].reshape(A_HEADS // 2, 1, HEAD_PAIR))

        abar, bre, bim, cre, cim = _s5_params(s_a_re[l], s_a_im[l], s_b_re[l], s_b_im[l],
                                              s_c_re[l], s_c_im[l], s_log_dt[l])
        y_s = _s5(su.reshape(SEQ * BATCH, S_WIDTH), abar, bre, bim, cre, cim,
                  s_d[l].reshape(1, S_WIDTH), s_glu_w[l].astype(BF16), s_glu_b[l].reshape(1, S_WIDTH),
                  gmat, s_out_g[l].reshape(1, S_WIDTH))

        w_router = jnp.pad(jnp.concatenate([r_group_w[l], r_expert_w[l]], axis=1),
                           ((0, 0), (0, LANES - N_GROUPS - N_EXPERTS)))
        b_router = jnp.pad(jnp.concatenate([r_group_b[l], r_expert_b[l]]),
                           (0, LANES - N_GROUPS - N_EXPERTS)).reshape(1, LANES)
        x1, xn, route = _outproj(xf, y_m, y_a, y_s.reshape(SEQ, BATCH * S_WIDTH), w_out[l].astype(BF16),
                                 norm2_g[l].reshape(1, D_MODEL), _hi_lo(w_router), b_router)
        xf = _moe(l, x1, xn, route, e_w_gate, e_w_up, e_w_down)
    return xf.reshape(BATCH, SEQ, D_MODEL)
```

```python
import functools
import math

import numpy as np
import jax
import jax.numpy as jnp
from jax import lax
from jax.experimental import pallas as pl
from jax.experimental.pallas import tpu as pltpu

F32 = jnp.float32
BF16 = jnp.bfloat16

D_MODEL = 1024
BATCH = 8
SEQ = 2048
DEPTH = 2
TOKENS = BATCH * SEQ

M_HEADS = 4
M_HEAD_DIM = 64
M_WIDTH = M_HEADS * M_HEAD_DIM
A_HEADS = 8
A_HEAD_DIM = 64
A_WIDTH = A_HEADS * A_HEAD_DIM
S_GROUPS = 16
S_GROUP_DIM = 16
S_WIDTH = S_GROUPS * S_GROUP_DIM
S_STATE = 64
S_FLAT = S_GROUPS * S_STATE
D_MIX = M_WIDTH + A_WIDTH + S_WIDTH

CONV_K = 4
MOBA_BLOCK = 256
MOBA_NB = SEQ // MOBA_BLOCK
MOBA_TOPK = 3
ROPE_THETA = 500000.0
ROPE_DIM = A_HEAD_DIM // 4
ROPE_HALF = ROPE_DIM // 2

N_GROUPS = 4
EXPERTS_PER_GROUP = 8
N_EXPERTS = N_GROUPS * EXPERTS_PER_GROUP
EXPERT_TOPK = 2
D_EXPERT = 512
DISPATCH_BLOCK = 128
EPS = 1e-6

LANES = 128
ROW_TILES = D_MODEL // LANES
HEAD_PAIR = 2 * A_HEAD_DIM

W_MAIN = 3 * A_WIDTH + 4 * M_WIDTH + S_WIDTH
TM_PROJ = 512
M_CHUNK = 256
M_NCHUNK = SEQ // M_CHUNK
S5_STEPS = 128
FFN_BLOCK = 256
N_FILLS = 2 * N_EXPERTS
PLAN_TILE = 1024
DISPATCH_TOKENS = 512
COMBINE_TOKENS = 256
VMEM_LIMIT = 56 * 1024 * 1024

HIGHEST = lax.Precision.HIGHEST


def _dot(a, b, precision=None):
    return jnp.dot(a, b, preferred_element_type=F32, precision=precision)


def _dot_nt(a, b, precision=None):
    return lax.dot_general(a, b, (((1,), (1,)), ((), ())), preferred_element_type=F32, precision=precision)


def _dot_tn(a, b, precision=None):
    return lax.dot_general(a, b, (((0,), (0,)), ((), ())), preferred_element_type=F32, precision=precision)


def _split_dot(a, w_hi, w_lo):
    a_hi = a.astype(BF16)
    a_lo = (a - a_hi.astype(F32)).astype(BF16)
    return _dot(a_hi, w_hi) + _dot(a_lo, w_hi) + _dot(a_hi, w_lo)


def _hi_lo(w):
    w_hi = w.astype(BF16)
    return jnp.stack([w_hi, (w - w_hi.astype(F32)).astype(BF16)])


def _log_sigmoid(x):
    return jnp.minimum(x, 0.0) - jnp.log1p(jnp.exp(-jnp.abs(x)))


def _sigmoid(x):
    return 1.0 / (1.0 + jnp.exp(-x))


def _inproj_kernel(x_ref, g_ref, w_ref, wg_ref,
                   aq_ref, ak_ref, av_ref, mqk_ref, mv_ref, mo_ref, su_ref, gates_ref, gatest_ref):
    x = x_ref[...]
    h = x * lax.rsqrt(jnp.mean(x * x, axis=-1, keepdims=True) + EPS) * g_ref[...]
    hb = h.astype(BF16)
    off = 0
    for ref, width in ((aq_ref, A_WIDTH), (ak_ref, A_WIDTH), (av_ref, A_WIDTH), (mqk_ref, 2 * M_WIDTH),
                       (mv_ref, M_WIDTH), (mo_ref, M_WIDTH), (su_ref, S_WIDTH)):
        ref[...] = _dot(hb, w_ref[:, off:off + width]).astype(ref.dtype)
        off += width
    gates = _split_dot(h, wg_ref[0], wg_ref[1])
    gates_ref[...] = gates
    gt = gates.T[:8, :]
    for c in range(TM_PROJ // M_CHUNK):
        gatest_ref[c] = gt[:, c * M_CHUNK:(c + 1) * M_CHUNK]


def _inproj(x, g, w_main, w_gate):
    n_s = SEQ // TM_PROJ
    row = lambda b, s: (b * n_s + s, 0)
    const = lambda b, s: (0, 0)
    out_shapes = (
        jax.ShapeDtypeStruct((TOKENS, A_WIDTH), BF16),
        jax.ShapeDtypeStruct((TOKENS, A_WIDTH), BF16),
        jax.ShapeDtypeStruct((TOKENS, A_WIDTH), BF16),
        jax.ShapeDtypeStruct((TOKENS, 2 * M_WIDTH), BF16),
        jax.ShapeDtypeStruct((TOKENS, M_WIDTH), BF16),
        jax.ShapeDtypeStruct((TOKENS, M_WIDTH), BF16),
        jax.ShapeDtypeStruct((SEQ, BATCH * S_WIDTH), BF16),
        jax.ShapeDtypeStruct((TOKENS, LANES), F32),
        jax.ShapeDtypeStruct((TOKENS // M_CHUNK, 8, M_CHUNK), F32),
    )
    out_specs = (
        pl.BlockSpec((TM_PROJ, A_WIDTH), row),
        pl.BlockSpec((TM_PROJ, A_WIDTH), row),
        pl.BlockSpec((TM_PROJ, A_WIDTH), row),
        pl.BlockSpec((TM_PROJ, 2 * M_WIDTH), row),
        pl.BlockSpec((TM_PROJ, M_WIDTH), row),
        pl.BlockSpec((TM_PROJ, M_WIDTH), row),
        pl.BlockSpec((TM_PROJ, S_WIDTH), lambda b, s: (s, b)),
        pl.BlockSpec((TM_PROJ, LANES), row),
        pl.BlockSpec((TM_PROJ // M_CHUNK, 8, M_CHUNK), lambda b, s: (b * n_s + s, 0, 0)),
    )
    return pl.pallas_call(
        _inproj_kernel,
        grid=(BATCH, n_s),
        in_specs=[
            pl.BlockSpec((TM_PROJ, D_MODEL), row),
            pl.BlockSpec((1, D_MODEL), const),
            pl.BlockSpec((D_MODEL, W_MAIN), const),
            pl.BlockSpec((2, D_MODEL, LANES), lambda b, s: (0, 0, 0)),
        ],
        out_specs=out_specs,
        out_shape=out_shapes,
        compiler_params=pltpu.CompilerParams(
            dimension_semantics=("arbitrary", "arbitrary"), vmem_limit_bytes=VMEM_LIMIT),
        name="inproj",
    )(x, g, w_main, w_gate)


def _mlstm_kernel(mqk_ref, mv_ref, mo_ref, gates_ref, gatest_ref, cw_ref, cb_ref, brow_ref, bcol_ref, og_ref,
                  y_ref, qk_s):
    L = M_CHUNK
    dh = M_HEAD_DIM
    w = cw_ref[...]
    cb = cb_ref[...]
    lane_q = lax.broadcasted_iota(jnp.int32, (1, 2 * M_WIDTH), 1) < M_WIDTH
    kscale = jnp.where(lane_q, 1.0, dh ** -0.5).astype(F32)
    for c in range(M_NCHUNK):
        halo = 16 if c > 0 else 0
        xt = mqk_ref[c * L - halo:(c + 1) * L, :].astype(F32)
        rows = lax.broadcasted_iota(jnp.int32, xt.shape, 0)
        y = cb + w[CONV_K - 1:CONV_K, :] * xt
        for k in range(1, CONV_K):
            sh = pltpu.roll(xt, k, 0)
            if c == 0:
                sh = jnp.where(rows >= k, sh, 0.0)
            y = y + w[CONV_K - 1 - k:CONV_K - k, :] * sh
        y = y[halo:, :]
        y = y * _sigmoid(y) * kscale
        qk_s[c * L:(c + 1) * L, :] = y.astype(BF16)

    r_i = lax.broadcasted_iota(jnp.int32, (L, L), 0)
    c_i = lax.broadcasted_iota(jnp.int32, (L, L), 1)
    causal = c_i <= r_i
    tri = causal.astype(F32)
    tri_u = (r_i <= c_i).astype(F32)
    pad_lane = lax.broadcasted_iota(jnp.int32, (L, LANES - dh), 1)
    ones_pad = jnp.where(pad_lane == 0, 1.0, 0.0).astype(BF16)
    og = og_ref[...]
    lane_t = lax.broadcasted_iota(jnp.int32, (8, L), 1)
    eye = (lax.broadcasted_iota(jnp.int32, (8, LANES), 0)
           == lax.broadcasted_iota(jnp.int32, (8, LANES), 1)).astype(F32)
    head_mean = jnp.where(lax.broadcasted_iota(jnp.int32, (M_WIDTH, M_WIDTH), 0) // dh
                          == lax.broadcasted_iota(jnp.int32, (M_WIDTH, M_WIDTH), 1) // dh,
                          1.0 / dh, 0.0).astype(BF16)

    def chunk(c, carry):
        r0 = pl.multiple_of(c * L, L)
        gcol = gates_ref[pl.ds(r0, L), :] + brow_ref[...]
        grow = gatest_ref[c] + bcol_ref[...]
        cs_col = _dot(tri, _log_sigmoid(gcol), precision=HIGHEST)
        cs_row = _dot(_log_sigmoid(grow), tri_u, precision=HIGHEST)
        w_row = grow - pltpu.roll(cs_row, M_HEADS, 0)
        pm = w_row
        shift = 1
        while shift < L:
            pm = jnp.maximum(pm, jnp.where(lane_t >= shift, pltpu.roll(pm, shift, 1), -jnp.inf))
            shift *= 2
        pm_col = _dot_tn(pm, eye, precision=HIGHEST)
        qk = qk_s[pl.ds(r0, L), :]
        vv = mv_ref[pl.ds(r0, L), :]
        new_carry = []
        hs = []
        for h in range(M_HEADS):
            cst, m = carry[h]
            q = qk[:, h * dh:(h + 1) * dh]
            k = qk[:, M_WIDTH + h * dh:M_WIDTH + (h + 1) * dh]
            v_aug = jnp.concatenate([vv[:, h * dh:(h + 1) * dh], ones_pad], axis=1)
            bc = cs_col[:, M_HEADS + h:M_HEADS + h + 1]
            ic = gcol[:, h:h + 1]
            m_row = jnp.maximum(m, pm_col[:, h:h + 1])
            mt = bc + m_row
            inter = jnp.exp(m - m_row)
            sqk = _dot_nt(q, k) * jnp.exp(jnp.where(causal, w_row[h:h + 1, :] - m_row, -jnp.inf))
            nd = inter * _dot(q, cst.astype(BF16)) + _dot(sqk.astype(BF16), v_aug)
            num = nd[:, :dh]
            den = nd[:, dh:dh + 1]
            hs.append(num / jnp.maximum(jnp.abs(den), jnp.exp(-mt)))
            bl = bc[L - 1:L, :]
            gs = bl - bc + ic
            m_new = jnp.maximum(bl + m, jnp.max(gs, axis=0, keepdims=True))
            decay = jnp.exp(bl + m - m_new)
            wts = jnp.exp(gs - m_new)
            kw = (k.astype(F32) * wts).astype(BF16)
            new_carry.append((decay * cst + _dot_tn(kw, v_aug), m_new))
        hcat = jnp.concatenate(hs, axis=1)
        hcat = hcat * lax.rsqrt(_dot((hcat * hcat).astype(BF16), head_mean) + EPS) * og
        o = mo_ref[pl.ds(r0, L), :].astype(F32)
        y_ref[pl.ds(r0, L), :] = (hcat * _sigmoid(o)).astype(y_ref.dtype)
        return tuple(new_carry)

    init = tuple((jnp.zeros((dh, LANES), F32), jnp.zeros((1, 1), F32)) for _ in range(M_HEADS))
    lax.fori_loop(0, M_NCHUNK, chunk, init)


def _mlstm(mqk, mv, mo, gates, gatest, conv_w, conv_b, bias_row, bias_col, out_g):
    per_b = lambda b: (b, 0)
    const = lambda b: (0, 0)
    return pl.pallas_call(
        _mlstm_kernel,
        grid=(BATCH,),
        in_specs=[
            pl.BlockSpec((SEQ, 2 * M_WIDTH), per_b),
            pl.BlockSpec((SEQ, M_WIDTH), per_b),
            pl.BlockSpec((SEQ, M_WIDTH), per_b),
            pl.BlockSpec((SEQ, LANES), per_b),
            pl.BlockSpec((M_NCHUNK, 8, M_CHUNK), lambda b: (b, 0, 0)),
            pl.BlockSpec((CONV_K, 2 * M_WIDTH), const),
            pl.BlockSpec((1, 2 * M_WIDTH), const),
            pl.BlockSpec((1, LANES), const),
            pl.BlockSpec((8, 1), const),
            pl.BlockSpec((1, M_WIDTH), const),
        ],
        out_specs=pl.BlockSpec((SEQ, M_WIDTH), per_b),
        out_shape=jax.ShapeDtypeStruct((TOKENS, M_WIDTH), BF16),
        scratch_shapes=[pltpu.VMEM((SEQ, 2 * M_WIDTH), BF16)],
        compiler_params=pltpu.CompilerParams(
            dimension_semantics=("arbitrary",), vmem_limit_bytes=VMEM_LIMIT),
        name="mlstm",
    )(mqk, mv, mo, gates, gatest, conv_w, conv_b, bias_row, bias_col, out_g)


def _qk_norm_rope(x, gain, cos, sin):
    lane = lax.broadcasted_iota(jnp.int32, x.shape, 1)
    first = lane < A_HEAD_DIM
    sq = x * x
    tot = jnp.sum(sq, axis=-1, keepdims=True)
    s0 = jnp.sum(jnp.where(first, sq, 0.0), axis=-1, keepdims=True)
    ms = jnp.where(first, s0, tot - s0) * (1.0 / A_HEAD_DIM)
    xn = x * lax.rsqrt(ms + EPS) * gain
    in_low = (lane % A_HEAD_DIM) < ROPE_HALF
    swapped = jnp.where(in_low, pltpu.roll(xn, LANES - ROPE_HALF, 1), pltpu.roll(xn, ROPE_HALF, 1))
    return xn * cos + swapped * sin


def _moba_kernel(aq_ref, ak_ref, av_ref, cos_ref, sin_ref, qg_ref, kg_ref, og_ref, y_ref, kr_s, s_s):
    blk = MOBA_BLOCK
    dh = A_HEAD_DIM
    kr = _qk_norm_rope(ak_ref[...].astype(F32), kg_ref[...], cos_ref[...], sin_ref[...])
    kr_s[...] = kr.astype(BF16)
    km = jnp.mean(kr.reshape(MOBA_NB, blk, HEAD_PAIR), axis=1)
    km_first = lax.broadcasted_iota(jnp.int32, km.shape, 1) < dh
    km_heads = (jnp.where(km_first, km, 0.0), jnp.where(km_first, 0.0, km))
    first = lax.broadcasted_iota(jnp.int32, (blk, HEAD_PAIR), 1) < dh
    blk_id = lax.broadcasted_iota(jnp.int32, (MOBA_NB, blk), 0)
    eye = (lax.broadcasted_iota(jnp.int32, (MOBA_NB, LANES), 0)
           == lax.broadcasted_iota(jnp.int32, (MOBA_NB, LANES), 1)).astype(F32)
    q_pos = lax.broadcasted_iota(jnp.int32, (2 * blk, blk), 0) % blk
    k_pos = lax.broadcasted_iota(jnp.int32, (2 * blk, blk), 1)
    causal = k_pos <= q_pos
    og = og_ref[...]

    def scores(i, out):
        rows = slice(i * blk, (i + 1) * blk)
        qr = _qk_norm_rope(aq_ref[rows, :].astype(F32), qg_ref[...], cos_ref[rows, :], sin_ref[rows, :])
        qs = qr * (dh ** -0.5)
        q2 = jnp.concatenate([jnp.where(first, qs, 0.0), jnp.where(first, 0.0, qs)], axis=0).astype(BF16)

        bias = None
        if i > MOBA_TOPK:
            past = blk_id < i
            cols = []
            for kmh in km_heads:
                gate = jnp.where(past, _dot_nt(kmh, qr, precision=HIGHEST), -jnp.inf)
                rank = jnp.zeros((MOBA_NB, blk), F32)
                for m_blk in range(i):
                    gm = gate[m_blk:m_blk + 1, :]
                    ahead = (gm > gate) | ((gm == gate) & (m_blk < blk_id))
                    rank = rank + jnp.where(ahead, 1.0, 0.0)
                sel = jnp.where(past & (rank < float(MOBA_TOPK)), 1.0, 0.0)
                cols.append(_dot_tn(sel, eye))
            bias = jnp.where(jnp.concatenate(cols, axis=0) > 0.5, 0.0, -jnp.inf)

        yield
        m_run = None
        for j in range(i + 1):
            s = _dot_nt(q2, kr_s[j * blk:(j + 1) * blk, :])
            if j == i:
                s = jnp.where(causal, s, -jnp.inf)
            elif bias is not None:
                s = s + bias[:, j:j + 1]
            s_s[i % 2, :, j * blk:(j + 1) * blk] = s
            part = jnp.maximum(s[:, :LANES], s[:, LANES:])
            m_run = part if m_run is None else jnp.maximum(m_run, part)
            yield
        out.append(jnp.max(m_run, axis=1, keepdims=True))

    def attend(i, m_fin):
        rows = slice(i * blk, (i + 1) * blk)
        l_run = jnp.zeros((2 * blk, LANES), F32)
        acc = jnp.zeros((2 * blk, HEAD_PAIR), F32)
        for j in range(i + 1):
            p = jnp.exp(s_s[i % 2, :, j * blk:(j + 1) * blk] - m_fin)
            l_run = l_run + (p[:, :LANES] + p[:, LANES:])
            acc = acc + _dot(p.astype(BF16), av_ref[j * blk:(j + 1) * blk, :])
            yield
        o2 = acc / jnp.sum(l_run, axis=1, keepdims=True)
        o = jnp.where(first, o2[:blk, :], o2[blk:, :])
        sq = o * o
        tot = jnp.sum(sq, axis=-1, keepdims=True)
        s0 = jnp.sum(jnp.where(first, sq, 0.0), axis=-1, keepdims=True)
        ms = jnp.where(first, s0, tot - s0) * (1.0 / dh)
        y_ref[rows, :] = (o * lax.rsqrt(ms + EPS) * og).astype(y_ref.dtype)

    def interleave(*stages):
        stages = list(stages)
        while stages:
            for stage in list(stages):
                try:
                    next(stage)
                except StopIteration:
                    stages.remove(stage)

    m_cur = []
    interleave(scores(0, m_cur))
    for i in range(MOBA_NB):
        m_next = []
        stages = [attend(i, m_cur[0])]
        if i + 1 < MOBA_NB:
            stages.insert(0, scores(i + 1, m_next))
        interleave(*stages)
        m_cur = m_next


def _moba(aq, ak, av, cos_t, sin_t, q_gain, k_gain, out_gain):
    n_pairs = A_HEADS // 2
    const = lambda b, p: (0, 0)
    pair = lambda b, p: (b, p)
    return pl.pallas_call(
        _moba_kernel,
        grid=(BATCH, n_pairs),
        in_specs=[
            pl.BlockSpec((SEQ, HEAD_PAIR), pair),
            pl.BlockSpec((SEQ, HEAD_PAIR), pair),
            pl.BlockSpec((SEQ, HEAD_PAIR), pair),
            pl.BlockSpec((SEQ, HEAD_PAIR), const),
            pl.BlockSpec((SEQ, HEAD_PAIR), const),
            pl.BlockSpec((1, HEAD_PAIR), const),
            pl.BlockSpec((1, HEAD_PAIR), const),
            pl.BlockSpec((None, 1, HEAD_PAIR), lambda b, p: (p, 0, 0)),
        ],
        out_specs=pl.BlockSpec((SEQ, HEAD_PAIR), pair),
        out_shape=jax.ShapeDtypeStruct((TOKENS, A_WIDTH), BF16),
        scratch_shapes=[
            pltpu.VMEM((SEQ, HEAD_PAIR), BF16),
            pltpu.VMEM((2, 2 * MOBA_BLOCK, SEQ), F32),
        ],
        compiler_params=pltpu.CompilerParams(
            dimension_semantics=("arbitrary", "arbitrary"), vmem_limit_bytes=VMEM_LIMIT),
        name="moba",
    )(aq, ak, av, cos_t, sin_t, q_gain, k_gain, out_gain)


def _s5_kernel(u_ref, abar_ref, bre_ref, bim_ref, cre_ref, cim_ref, d_ref, gw_ref, gb_ref, gmat_ref, og_ref,
               y_ref, xr_s, xi_s, st_s):
    step = pl.program_id(0)
    rows = S5_STEPS * BATCH
    tile = 256
    n_tiles = rows // tile

    @pl.when(step == 0)
    def _init():
        st_s[...] = jnp.zeros_like(st_s)

    for r in range(n_tiles):
        u = u_ref[r * tile:(r + 1) * tile, :]
        t0 = r * tile // BATCH
        xr_s[t0:t0 + tile // BATCH] = _dot(u, bre_ref[...]).reshape(tile // BATCH, BATCH, S_FLAT)
        xi_s[t0:t0 + tile // BATCH] = _dot(u, bim_ref[...]).reshape(tile // BATCH, BATCH, S_FLAT)

    ar = jnp.broadcast_to(abar_ref[0:1, :], (BATCH, S_FLAT))
    ai = jnp.broadcast_to(abar_ref[1:2, :], (BATCH, S_FLAT))

    def scan_step(t, carry):
        sr, si = carry
        nr = ar * sr - ai * si + xr_s[t]
        ni = ar * si + ai * sr + xi_s[t]
        xr_s[t] = nr
        xi_s[t] = ni
        return nr, ni

    sr, si = lax.fori_loop(0, S5_STEPS, scan_step, (st_s[0], st_s[1]), unroll=4)
    st_s[0] = sr
    st_s[1] = si

    for r in range(n_tiles):
        t0 = r * tile // BATCH
        xr = xr_s[t0:t0 + tile // BATCH].reshape(tile, S_FLAT).astype(BF16)
        xi = xi_s[t0:t0 + tile // BATCH].reshape(tile, S_FLAT).astype(BF16)
        u = u_ref[r * tile:(r + 1) * tile, :].astype(F32)
        y = _dot(xr, cre_ref[...]) - _dot(xi, cim_ref[...]) + d_ref[...] * u
        z = 0.5 * y * (1.0 + jnp.tanh(math.sqrt(2.0 / math.pi) * (y + 0.044715 * (y * y * y))))
        y2 = z * _sigmoid(_dot(z.astype(BF16), gw_ref[...]) + gb_ref[...])
        ms = _dot(y2 * y2, gmat_ref[...], precision=HIGHEST)
        y_ref[r * tile:(r + 1) * tile, :] = (y2 * lax.rsqrt(ms + EPS) * og_ref[...]).astype(y_ref.dtype)


def _s5(u_sb, abar, bre, bim, cre, cim, d_row, glu_w, glu_b, gmat, out_g):
    rows = S5_STEPS * BATCH
    const = lambda s: (0, 0)
    return pl.pallas_call(
        _s5_kernel,
        grid=(SEQ // S5_STEPS,),
        in_specs=[
            pl.BlockSpec((rows, S_WIDTH), lambda s: (s, 0)),
            pl.BlockSpec((2, S_FLAT), const),
            pl.BlockSpec((S_WIDTH, S_FLAT), const),
            pl.BlockSpec((S_WIDTH, S_FLAT), const),
            pl.BlockSpec((S_FLAT, S_WIDTH), const),
            pl.BlockSpec((S_FLAT, S_WIDTH), const),
            pl.BlockSpec((1, S_WIDTH), const),
            pl.BlockSpec((S_WIDTH, S_WIDTH), const),
            pl.BlockSpec((1, S_WIDTH), const),
            pl.BlockSpec((S_WIDTH, S_WIDTH), const),
            pl.BlockSpec((1, S_WIDTH), const),
        ],
        out_specs=pl.BlockSpec((rows, S_WIDTH), lambda s: (s, 0)),
        out_shape=jax.ShapeDtypeStruct((SEQ * BATCH, S_WIDTH), BF16),
        scratch_shapes=[
            pltpu.VMEM((S5_STEPS, BATCH, S_FLAT), F32),
            pltpu.VMEM((S5_STEPS, BATCH, S_FLAT), F32),
            pltpu.VMEM((2, BATCH, S_FLAT), F32),
        ],
        compiler_params=pltpu.CompilerParams(
            dimension_semantics=("arbitrary",), vmem_limit_bytes=VMEM_LIMIT),
        name="s5",
    )(u_sb, abar, bre, bim, cre, cim, d_row, glu_w, glu_b, gmat, out_g)


def _outproj_kernel(x_ref, ym_ref, ya_ref, ys_ref, w_ref, g_ref, wr_ref, br_ref,
                    x1_ref, xn_ref, route_ref):
    x1 = x_ref[...]
    x1 = x1 + _dot(ym_ref[...], w_ref[0:M_WIDTH, :])
    x1 = x1 + _dot(ya_ref[...], w_ref[M_WIDTH:M_WIDTH + A_WIDTH, :])
    x1 = x1 + _dot(ys_ref[...], w_ref[M_WIDTH + A_WIDTH:, :])
    x1_ref[...] = x1
    xn = x1 * lax.rsqrt(jnp.mean(x1 * x1, axis=-1, keepdims=True) + EPS) * g_ref[...]
    for j in range(ROW_TILES):
        xn_ref[pl.ds(j, TM_PROJ, stride=ROW_TILES), :] = xn[:, j * LANES:(j + 1) * LANES]
    logits = _split_dot(xn, wr_ref[0], wr_ref[1]) + br_ref[...]

    lane = lax.broadcasted_iota(jnp.int32, logits.shape, 1).astype(F32)
    is_grp = lane < N_GROUPS
    neg = -jnp.inf
    gl = jnp.where(is_grp, logits, neg)
    gmax = jnp.max(gl, axis=-1, keepdims=True)
    gsum = jnp.sum(jnp.where(is_grp, jnp.exp(logits - gmax), 0.0), axis=-1, keepdims=True)
    g_sel = jnp.min(jnp.where(gl == gmax, lane, float(LANES)), axis=-1, keepdims=True)
    lo = N_GROUPS + EXPERTS_PER_GROUP * g_sel
    el = jnp.where((lane >= lo) & (lane < lo + EXPERTS_PER_GROUP), logits, neg)
    t1 = jnp.max(el, axis=-1, keepdims=True)
    i1 = jnp.min(jnp.where(el == t1, lane, float(LANES)), axis=-1, keepdims=True)
    el2 = jnp.where(lane == i1, neg, el)
    t2 = jnp.max(el2, axis=-1, keepdims=True)
    i2 = jnp.min(jnp.where(el2 == t2, lane, float(LANES)), axis=-1, keepdims=True)
    e21 = jnp.exp(t2 - t1)
    w1 = 1.0 / ((1.0 + e21) * gsum)
    w2 = e21 / ((1.0 + e21) * gsum)
    route = jnp.where(lane == 0.0, i1 - N_GROUPS,
                      jnp.where(lane == 1.0, i2 - N_GROUPS,
                                jnp.where(lane == 2.0, w1, jnp.where(lane == 3.0, w2, 0.0))))
    route_ref[...] = route


def _outproj(x, y_m, y_a, y_s_sb, w_out, g2, w_router, b_router):
    n_s = SEQ // TM_PROJ
    row = lambda b, s: (b * n_s + s, 0)
    const = lambda b, s: (0, 0)
    return pl.pallas_call(
        _outproj_kernel,
        grid=(BATCH, n_s),
        in_specs=[
            pl.BlockSpec((TM_PROJ, D_MODEL), row),
            pl.BlockSpec((TM_PROJ, M_WIDTH), row),
            pl.BlockSpec((TM_PROJ, A_WIDTH), row),
            pl.BlockSpec((TM_PROJ, S_WIDTH), lambda b, s: (s, b)),
            pl.BlockSpec((D_MIX, D_MODEL), const),
            pl.BlockSpec((1, D_MODEL), const),
            pl.BlockSpec((2, D_MODEL, LANES), lambda b, s: (0, 0, 0)),
            pl.BlockSpec((1, LANES), const),
        ],
        out_specs=(
            pl.BlockSpec((TM_PROJ, D_MODEL), row),
            pl.BlockSpec((TM_PROJ * ROW_TILES, LANES), row),
            pl.BlockSpec((TM_PROJ, LANES), row),
        ),
        out_shape=(
            jax.ShapeDtypeStruct((TOKENS, D_MODEL), F32),
            jax.ShapeDtypeStruct((TOKENS * ROW_TILES, LANES), F32),
            jax.ShapeDtypeStruct((TOKENS, LANES), F32),
        ),
        compiler_params=pltpu.CompilerParams(
            dimension_semantics=("arbitrary", "arbitrary"), vmem_limit_bytes=VMEM_LIMIT),
        name="outproj",
    )(x, y_m, y_a, y_s_sb, w_out, g2, w_router, b_router)


def _ffn_kernel(be_ref, xs_ref, wg_ref, wu_ref, wd_ref, ys_ref, wg_s, wu_s, wd_s):
    i = pl.program_id(0)
    prev = be_ref[jnp.maximum(i - 1, 0)]

    @pl.when((i == 0) | (be_ref[i] != prev))
    def _load_expert():
        wg_s[...] = wg_ref[...].astype(BF16)
        wu_s[...] = wu_ref[...].astype(BF16)
        wd_s[...] = wd_ref[...].astype(BF16)

    xb = jnp.concatenate([xs_ref[pl.ds(j, FFN_BLOCK, stride=ROW_TILES), :] for j in range(ROW_TILES)],
                         axis=1).astype(BF16)
    g = _dot(xb, wg_s[...])
    u = _dot(xb, wu_s[...])
    hmid = (g * _sigmoid(g) * u).astype(BF16)
    y = _dot(hmid, wd_s[...])
    for j in range(ROW_TILES):
        ys_ref[pl.ds(j, FFN_BLOCK, stride=ROW_TILES), :] = y[:, j * LANES:(j + 1) * LANES]


def _expert_ffn(layer, blk_exp, xs, w_gate, w_up, w_down):
    nblk = blk_exp.shape[0]
    rows = pl.BlockSpec((FFN_BLOCK * ROW_TILES, LANES), lambda i, be: (i, 0))
    grid_spec = pltpu.PrefetchScalarGridSpec(
        num_scalar_prefetch=1,
        grid=(nblk,),
        in_specs=[
            rows,
            pl.BlockSpec((None, None, D_MODEL, D_EXPERT), lambda i, be: (layer, be[i], 0, 0)),
            pl.BlockSpec((None, None, D_MODEL, D_EXPERT), lambda i, be: (layer, be[i], 0, 0)),
            pl.BlockSpec((None, None, D_EXPERT, D_MODEL), lambda i, be: (layer, be[i], 0, 0)),
        ],
        out_specs=rows,
        scratch_shapes=[
            pltpu.VMEM((D_MODEL, D_EXPERT), BF16),
            pltpu.VMEM((D_MODEL, D_EXPERT), BF16),
            pltpu.VMEM((D_EXPERT, D_MODEL), BF16),
        ],
    )
    return pl.pallas_call(
        _ffn_kernel,
        grid_spec=grid_spec,
        out_shape=jax.ShapeDtypeStruct((nblk * FFN_BLOCK * ROW_TILES, LANES), F32),
        compiler_params=pltpu.CompilerParams(
            dimension_semantics=("arbitrary",), vmem_limit_bytes=VMEM_LIMIT),
        name="expert_ffn",
    )(blk_exp, xs, w_gate, w_up, w_down)


def _plan_kernel(e_ref, ps_ref, dest_ref, run_s, tri_s):
    i = pl.program_id(0)

    @pl.when(i == 0)
    def _init():
        run_s[...] = ps_ref[...]
        s_i = lax.broadcasted_iota(jnp.int32, (PLAN_TILE, PLAN_TILE), 0)
        t_i = lax.broadcasted_iota(jnp.int32, (PLAN_TILE, PLAN_TILE), 1)
        tri_s[...] = (s_i < t_i).astype(BF16)

    expert = lax.broadcasted_iota(jnp.int32, (N_EXPERTS, PLAN_TILE), 0)
    onehot = jnp.where(e_ref[...] == expert, 1.0, 0.0)
    before = _dot(onehot.astype(BF16), tri_s[...])
    dest = jnp.sum(onehot * (before + run_s[:, 0:1]), axis=0, keepdims=True)
    dest_ref[...] = dest.astype(jnp.int32)
    run_s[...] = run_s[...] + jnp.sum(onehot, axis=1, keepdims=True)


def _plan(e_rows, pstart_col):
    n_tiles = e_rows.shape[0]
    return pl.pallas_call(
        _plan_kernel,
        grid=(n_tiles,),
        in_specs=[
            pl.BlockSpec((None, 1, PLAN_TILE), lambda i: (i, 0, 0)),
            pl.BlockSpec((N_EXPERTS, LANES), lambda i: (0, 0)),
        ],
        out_specs=pl.BlockSpec((None, 1, PLAN_TILE), lambda i: (i, 0, 0)),
        out_shape=jax.ShapeDtypeStruct((n_tiles, 1, PLAN_TILE), jnp.int32),
        scratch_shapes=[pltpu.VMEM((N_EXPERTS, LANES), F32), pltpu.VMEM((PLAN_TILE, PLAN_TILE), BF16)],
        compiler_params=pltpu.CompilerParams(
            dimension_semantics=("arbitrary",), vmem_limit_bytes=VMEM_LIMIT),
        name="moe_plan",
    )(e_rows, pstart_col)


def _dispatch_kernel(fill_ref, d0_ref, d1_ref, xn_ref, xs_hbm, zero_s, sem, fill_sem):
    @pl.when(pl.program_id(0) == 0)
    def _zero_padding():
        zero_s[...] = jnp.zeros_like(zero_s)

        def fill(c):
            start = pl.multiple_of(fill_ref[0, c] * ROW_TILES, ROW_TILES)
            return pltpu.make_async_copy(zero_s, xs_hbm.at[pl.ds(start, FFN_BLOCK * ROW_TILES), :], fill_sem)

        for c in range(N_FILLS):
            pl.when(fill_ref[0, c] >= 0)(lambda c=c: fill(c).start())
        for c in range(N_FILLS):
            pl.when(fill_ref[0, c] >= 0)(lambda c=c: fill(c).wait())

    def row_copy(dref, t):
        src = xn_ref.at[pl.ds(pl.multiple_of(t * ROW_TILES, ROW_TILES), ROW_TILES), :]
        dst = xs_hbm.at[pl.ds(pl.multiple_of(dref[0, t] * ROW_TILES, ROW_TILES), ROW_TILES), :]
        return pltpu.make_async_copy(src, dst, sem)

    def start(t, carry):
        row_copy(d0_ref, t).start()
        row_copy(d1_ref, t).start(priority=1)
        return carry

    lax.fori_loop(0, DISPATCH_TOKENS, start, 0, unroll=8)
    for _ in range(EXPERT_TOPK):
        pltpu.make_async_copy(xn_ref, xs_hbm.at[pl.ds(0, DISPATCH_TOKENS * ROW_TILES), :], sem).wait()


def _dispatch(fill_rows, dest, xn, n_rows):
    n_tiles = TOKENS // DISPATCH_TOKENS
    dest4 = dest.reshape(EXPERT_TOPK, n_tiles, 1, DISPATCH_TOKENS)
    smem_tile = lambda k: pl.BlockSpec((None, None, 1, DISPATCH_TOKENS), lambda i: (k, i, 0, 0),
                                       memory_space=pltpu.SMEM)
    return pl.pallas_call(
        _dispatch_kernel,
        grid=(n_tiles,),
        in_specs=[
            pl.BlockSpec(memory_space=pltpu.SMEM),
            smem_tile(0),
            smem_tile(1),
            pl.BlockSpec((DISPATCH_TOKENS * ROW_TILES, LANES), lambda i: (i, 0)),
        ],
        out_specs=pl.BlockSpec(memory_space=pl.ANY),
        out_shape=jax.ShapeDtypeStruct((n_rows * ROW_TILES, LANES), F32),
        scratch_shapes=[
            pltpu.VMEM((FFN_BLOCK * ROW_TILES, LANES), F32),
            pltpu.SemaphoreType.DMA(()),
            pltpu.SemaphoreType.DMA(()),
        ],
        compiler_params=pltpu.CompilerParams(
            dimension_semantics=("arbitrary",), vmem_limit_bytes=VMEM_LIMIT),
        name="moe_dispatch",
    )(fill_rows, dest4, dest4, xn)


def _combine_kernel(d0_ref, d1_ref, d0_next_ref, d1_next_ref, x1_ref, route_ref, ys_hbm, out_ref, buf, sem):
    i = pl.program_id(0)
    n = pl.num_programs(0)

    def start_tile(drefs, slot):
        def body(t, carry):
            for k in range(EXPERT_TOPK):
                src = ys_hbm.at[pl.ds(pl.multiple_of(drefs[k][0, t] * ROW_TILES, ROW_TILES), ROW_TILES), :]
                dst = buf.at[slot, k, pl.ds(pl.multiple_of(t * ROW_TILES, ROW_TILES), ROW_TILES), :]
                pltpu.make_async_copy(src, dst, sem.at[slot]).start(priority=k)
            return carry
        lax.fori_loop(0, COMBINE_TOKENS, body, 0, unroll=8)

    slot = i % 2

    @pl.when(i == 0)
    def _first():
        start_tile((d0_ref, d1_ref), 0)

    @pl.when(i + 1 < n)
    def _prefetch():
        start_tile((d0_next_ref, d1_next_ref), 1 - slot)

    for k in range(EXPERT_TOPK):
        pltpu.make_async_copy(ys_hbm.at[pl.ds(0, COMBINE_TOKENS * ROW_TILES), :], buf.at[slot, k],
                              sem.at[slot]).wait()
    w = route_ref[...]
    w1 = w[:, 2:3]
    w2 = w[:, 3:4]
    for j in range(ROW_TILES):
        cols = slice(j * LANES, (j + 1) * LANES)
        chunk = pl.ds(j, COMBINE_TOKENS, stride=ROW_TILES)
        out_ref[:, cols] = x1_ref[:, cols] + w1 * buf[slot, 0, chunk, :] + w2 * buf[slot, 1, chunk, :]


def _combine(dest, x1, route, ys3):
    n_tiles = TOKENS // COMBINE_TOKENS
    dest4 = dest.reshape(EXPERT_TOPK, n_tiles, 1, COMBINE_TOKENS)
    smem_tile = lambda k, nxt: pl.BlockSpec(
        (None, None, 1, COMBINE_TOKENS), lambda i: (k, jnp.minimum(i + nxt, n_tiles - 1), 0, 0),
        memory_space=pltpu.SMEM)
    return pl.pallas_call(
        _combine_kernel,
        grid=(n_tiles,),
        in_specs=[
            smem_tile(0, 0), smem_tile(1, 0), smem_tile(0, 1), smem_tile(1, 1),
            pl.BlockSpec((COMBINE_TOKENS, D_MODEL), lambda i: (i, 0)),
            pl.BlockSpec((COMBINE_TOKENS, LANES), lambda i: (i, 0)),
            pl.BlockSpec(memory_space=pl.ANY),
        ],
        out_specs=pl.BlockSpec((COMBINE_TOKENS, D_MODEL), lambda i: (i, 0)),
        out_shape=jax.ShapeDtypeStruct((TOKENS, D_MODEL), F32),
        scratch_shapes=[
            pltpu.VMEM((2, EXPERT_TOPK, COMBINE_TOKENS * ROW_TILES, LANES), F32),
            pltpu.SemaphoreType.DMA((2,)),
        ],
        compiler_params=pltpu.CompilerParams(
            dimension_semantics=("arbitrary",), vmem_limit_bytes=VMEM_LIMIT),
        name="moe_combine",
    )(dest4, dest4, dest4, dest4, x1, route, ys3)


def _rope_tables():
    inv = ROPE_THETA ** (-np.arange(ROPE_HALF, dtype=np.float64) * 2.0 / ROPE_DIM)
    ang = np.arange(SEQ, dtype=np.float64)[:, None] * inv[None, :]
    cos = np.ones((SEQ, A_HEAD_DIM))
    sin = np.zeros((SEQ, A_HEAD_DIM))
    cos[:, :ROPE_HALF] = np.cos(ang)
    cos[:, ROPE_HALF:ROPE_DIM] = np.cos(ang)
    sin[:, :ROPE_HALF] = -np.sin(ang)
    sin[:, ROPE_HALF:ROPE_DIM] = np.sin(ang)
    return (jnp.asarray(np.tile(cos, (1, 2)), F32), jnp.asarray(np.tile(sin, (1, 2)), F32))


def _split_w_in(w_in):
    sizes = [M_WIDTH, M_WIDTH, M_WIDTH, M_WIDTH, M_HEADS, M_HEADS, A_WIDTH, A_WIDTH, A_WIDTH, S_WIDTH]
    offs = np.cumsum([0] + sizes)
    mq, mk, mv, mo, mi, mf, aq, ak, av, su = (w_in[:, offs[n]:offs[n + 1]] for n in range(len(sizes)))
    w_main = jnp.concatenate([aq, ak, av, mq, mk, mv, mo, su], axis=1).astype(BF16)
    gate = jnp.concatenate([mi, mf], axis=1)
    return w_main, _hi_lo(jnp.pad(gate, ((0, 0), (0, LANES - 2 * M_HEADS))))


def _s5_params(a_re, a_im, b_re, b_im, c_re, c_im, log_dt):
    dt = jnp.exp(log_dt)[:, None]
    mag = jnp.exp(a_re * dt)
    abar_re, abar_im = mag * jnp.cos(a_im * dt), mag * jnp.sin(a_im * dt)
    zr, zi = abar_re - 1.0, abar_im
    den = a_re * a_re + a_im * a_im
    fr, fi = (zr * a_re + zi * a_im) / den, (zi * a_re - zr * a_im) / den
    bbar_re = fr[..., None] * b_re - fi[..., None] * b_im
    bbar_im = fr[..., None] * b_im + fi[..., None] * b_re
    eye = jnp.eye(S_GROUPS, dtype=F32)
    dense_b = lambda t: jnp.einsum('gph,gk->ghkp', t, eye).reshape(S_WIDTH, S_FLAT).astype(BF16)
    dense_c = lambda t: jnp.einsum('ghp,gk->gpkh', t, eye).reshape(S_FLAT, S_WIDTH).astype(BF16)
    abar = jnp.stack([abar_re.reshape(S_FLAT), abar_im.reshape(S_FLAT)])
    return abar, dense_b(bbar_re), dense_b(bbar_im), dense_c(c_re), dense_c(c_im)


def _moe(layer, x1, xn, route, w_gate, w_up, w_down):
    n_assign = TOKENS * EXPERT_TOPK
    e_flat = route[:, :EXPERT_TOPK].astype(jnp.int32).T.reshape(n_assign)
    experts = jnp.arange(N_EXPERTS, dtype=jnp.int32)
    counts = jnp.sum((e_flat[:, None] == experts[None, :]).astype(jnp.int32), axis=0)
    padded = (counts + FFN_BLOCK - 1) // FFN_BLOCK * FFN_BLOCK
    pends = jnp.cumsum(padded)
    pstarts = pends - padded
    n_rows = (n_assign + N_EXPERTS * (FFN_BLOCK - 1) + FFN_BLOCK - 1) // FFN_BLOCK * FFN_BLOCK
    nblk = n_rows // FFN_BLOCK
    blk_start = jnp.arange(nblk, dtype=jnp.int32) * FFN_BLOCK
    blk_exp = jnp.minimum(jnp.sum((pends[None, :] <= blk_start[:, None]).astype(jnp.int32), axis=1),
                          N_EXPERTS - 1).astype(jnp.int32)
    pstart_col = jnp.broadcast_to(pstarts.astype(F32)[:, None], (N_EXPERTS, LANES))
    dest = _plan(e_flat.reshape(n_assign // PLAN_TILE, 1, PLAN_TILE), pstart_col)
    tail = pends[-1] + jnp.arange(N_EXPERTS, dtype=jnp.int32) * FFN_BLOCK
    fill_rows = jnp.concatenate([jnp.where(padded > 0, pends - FFN_BLOCK, -1), jnp.where(tail < n_rows, tail, -1)])
    fill_rows = fill_rows.astype(jnp.int32).reshape(1, N_FILLS)
    xs = _dispatch(fill_rows, dest, xn, n_rows)
    ys = _expert_ffn(layer, blk_exp, xs, w_gate, w_up, w_down)
    return _combine(dest, x1, route, ys)


def kernel(x, norm1_g, w_in, m_bias_i, m_bias_f, m_conv_w, m_conv_b, m_out_g, a_q_g, a_k_g, a_out_g, s_a_re, s_a_im, s_b_re, s_b_im, s_c_re, s_c_im, s_d, s_log_dt, s_glu_w, s_glu_b, s_out_g, w_out, norm2_g, r_group_w, r_group_b, r_expert_w, r_expert_b, e_w_gate, e_w_up, e_w_down):
    cos_t, sin_t = _rope_tables()
    gmat = jnp.asarray(np.kron(np.eye(S_GROUPS), np.full((S_GROUP_DIM, S_GROUP_DIM), 1.0 / S_GROUP_DIM)), F32)
    xf = x.reshape(TOKENS, D_MODEL)
    for l in range(DEPTH):
        w_main, w_gate = _split_w_in(w_in[l])
        aq, ak, av, mqk, mv, mo, su, gates, gatest = _inproj(xf, norm1_g[l].reshape(1, D_MODEL), w_main, w_gate)

        bias = jnp.concatenate([m_bias_i[l], m_bias_f[l]])
        bias_row = jnp.pad(bias, (0, LANES - 2 * M_HEADS)).reshape(1, LANES)
        y_m = _mlstm(mqk, mv, mo, gates, gatest, m_conv_w[l], m_conv_b[l].reshape(1, 2 * M_WIDTH),
                     bias_row, bias.reshape(2 * M_HEADS, 1), m_out_g[l].reshape(1, M_WIDTH))

        y_a = _moba(aq, ak, av, cos_t, sin_t,
                    jnp.tile(a_q_g[l], 2).reshape(1, HEAD_PAIR), jnp.tile(a_k_g[l], 2).reshape(1, HEAD_PAIR),
                    a_out_g[l].reshape(A_HEADS // 2, 1, HEAD_PAIR))

        abar, bre, bim, cre, cim = _s5_params(s_a_re[l], s_a_im[l], s_b_re[l], s_b_im[l],
                                              s_c_re[l], s_c_im[l], s_log_dt[l])
        y_s = _s5(su.reshape(SEQ * BATCH, S_WIDTH), abar, bre, bim, cre, cim,
                  s_d[l].reshape(1, S_WIDTH), s_glu_w[l].astype(BF16), s_glu_b[l].reshape(1, S_WIDTH),
                  gmat, s_out_g[l].reshape(1, S_WIDTH))

        w_router = jnp.pad(jnp.concatenate([r_group_w[l], r_expert_w[l]], axis=1),
                           ((0, 0), (0, LANES - N_GROUPS - N_EXPERTS)))
        b_router = jnp.pad(jnp.concatenate([r_group_b[l], r_expert_b[l]]),
                           (0, LANES - N_GROUPS - N_EXPERTS)).reshape(1, LANES)
        x1, xn, route = _outproj(xf, y_m, y_a, y_s.reshape(SEQ, BATCH * S_WIDTH), w_out[l].astype(BF16),
                                 norm2_g[l].reshape(1, D_MODEL), _hi_lo(w_router), b_router)
        xf = _moe(l, x1, xn, route, e_w_gate, e_w_up, e_w_down)
    return xf.reshape(BATCH, SEQ, D_MODEL)
```

```python
import functools
import math

import numpy as np
import jax
import jax.numpy as jnp
from jax import lax
from jax.experimental import pallas as pl
from jax.experimental.pallas import tpu as pltpu

F32 = jnp.float32
BF16 = jnp.bfloat16

D_MODEL = 1024
BATCH = 8
SEQ = 2048
DEPTH = 2
TOKENS = BATCH * SEQ

M_HEADS = 4
M_HEAD_DIM = 64
M_WIDTH = M_HEADS * M_HEAD_DIM
A_HEADS = 8
A_HEAD_DIM = 64
A_WIDTH = A_HEADS * A_HEAD_DIM
S_GROUPS = 16
S_GROUP_DIM = 16
S_WIDTH = S_GROUPS * S_GROUP_DIM
S_STATE = 64
S_FLAT = S_GROUPS * S_STATE
D_MIX = M_WIDTH + A_WIDTH + S_WIDTH

CONV_K = 4
MOBA_BLOCK = 256
MOBA_NB = SEQ // MOBA_BLOCK
MOBA_TOPK = 3
ROPE_THETA = 500000.0
ROPE_DIM = A_HEAD_DIM // 4
ROPE_HALF = ROPE_DIM // 2

N_GROUPS = 4
EXPERTS_PER_GROUP = 8
N_EXPERTS = N_GROUPS * EXPERTS_PER_GROUP
EXPERT_TOPK = 2
D_EXPERT = 512
DISPATCH_BLOCK = 128
EPS = 1e-6

LANES = 128
ROW_TILES = D_MODEL // LANES
HEAD_PAIR = 2 * A_HEAD_DIM

W_MAIN = 3 * A_WIDTH + 4 * M_WIDTH + S_WIDTH
TM_PROJ = 512
M_CHUNK = 256
M_NCHUNK = SEQ // M_CHUNK
S5_STEPS = 128
FFN_BLOCK = 256
N_FILLS = 2 * N_EXPERTS
PLAN_TILE = 1024
DISPATCH_TOKENS = 512
COMBINE_TOKENS = 256
VMEM_LIMIT = 56 * 1024 * 1024

HIGHEST = lax.Precision.HIGHEST


def _dot(a, b, precision=None):
    return jnp.dot(a, b, preferred_element_type=F32, precision=precision)


def _dot_nt(a, b, precision=None):
    return lax.dot_general(a, b, (((1,), (1,)), ((), ())), preferred_element_type=F32, precision=precision)


def _dot_tn(a, b, precision=None):
    return lax.dot_general(a, b, (((0,), (0,)), ((), ())), preferred_element_type=F32, precision=precision)


def _split_dot(a, w_hi, w_lo):
    a_hi = a.astype(BF16)
    a_lo = (a - a_hi.astype(F32)).astype(BF16)
    return _dot(a_hi, w_hi) + _dot(a_lo, w_hi) + _dot(a_hi, w_lo)


def _bf16_terms(x):
    hi = x.astype(BF16)
    rest = x - hi.astype(F32)
    mid = rest.astype(BF16)
    return hi, mid, (rest - mid.astype(F32)).astype(BF16)


def _hi_lo(w):
    w_hi = w.astype(BF16)
    return jnp.stack([w_hi, (w - w_hi.astype(F32)).astype(BF16)])


def _log_sigmoid(x):
    return jnp.minimum(x, 0.0) - jnp.log1p(jnp.exp(-jnp.abs(x)))


def _sigmoid(x):
    return 1.0 / (1.0 + jnp.exp(-x))


def _inproj_kernel(x_ref, g_ref, w_ref, wg_ref,
                   aq_ref, ak_ref, av_ref, mqk_ref, mv_ref, mo_ref, su_ref, gates_ref, gatest_ref):
    x = x_ref[...]
    h = x * lax.rsqrt(jnp.mean(x * x, axis=-1, keepdims=True) + EPS) * g_ref[...]
    hb = h.astype(BF16)
    off = 0
    for ref, width in ((aq_ref, A_WIDTH), (ak_ref, A_WIDTH), (av_ref, A_WIDTH), (mqk_ref, 2 * M_WIDTH),
                       (mv_ref, M_WIDTH), (mo_ref, M_WIDTH), (su_ref, S_WIDTH)):
        ref[...] = _dot(hb, w_ref[:, off:off + width]).astype(ref.dtype)
        off += width
    gates = _split_dot(h, wg_ref[0], wg_ref[1])
    gates_ref[...] = gates
    gt = gates.T[:8, :]
    for c in range(TM_PROJ // M_CHUNK):
        gatest_ref[c] = gt[:, c * M_CHUNK:(c + 1) * M_CHUNK]


def _inproj(x, g, w_main, w_gate):
    n_s = SEQ // TM_PROJ
    row = lambda b, s: (b * n_s + s, 0)
    const = lambda b, s: (0, 0)
    out_shapes = (
        jax.ShapeDtypeStruct((TOKENS, A_WIDTH), BF16),
        jax.ShapeDtypeStruct((TOKENS, A_WIDTH), BF16),
        jax.ShapeDtypeStruct((TOKENS, A_WIDTH), BF16),
        jax.ShapeDtypeStruct((TOKENS, 2 * M_WIDTH), BF16),
        jax.ShapeDtypeStruct((TOKENS, M_WIDTH), BF16),
        jax.ShapeDtypeStruct((TOKENS, M_WIDTH), BF16),
        jax.ShapeDtypeStruct((SEQ, BATCH * S_WIDTH), BF16),
        jax.ShapeDtypeStruct((TOKENS, LANES), F32),
        jax.ShapeDtypeStruct((TOKENS // M_CHUNK, 8, M_CHUNK), F32),
    )
    out_specs = (
        pl.BlockSpec((TM_PROJ, A_WIDTH), row),
        pl.BlockSpec((TM_PROJ, A_WIDTH), row),
        pl.BlockSpec((TM_PROJ, A_WIDTH), row),
        pl.BlockSpec((TM_PROJ, 2 * M_WIDTH), row),
        pl.BlockSpec((TM_PROJ, M_WIDTH), row),
        pl.BlockSpec((TM_PROJ, M_WIDTH), row),
        pl.BlockSpec((TM_PROJ, S_WIDTH), lambda b, s: (s, b)),
        pl.BlockSpec((TM_PROJ, LANES), row),
        pl.BlockSpec((TM_PROJ // M_CHUNK, 8, M_CHUNK), lambda b, s: (b * n_s + s, 0, 0)),
    )
    return pl.pallas_call(
        _inproj_kernel,
        grid=(BATCH, n_s),
        in_specs=[
            pl.BlockSpec((TM_PROJ, D_MODEL), row),
            pl.BlockSpec((1, D_MODEL), const),
            pl.BlockSpec((D_MODEL, W_MAIN), const),
            pl.BlockSpec((2, D_MODEL, LANES), lambda b, s: (0, 0, 0)),
        ],
        out_specs=out_specs,
        out_shape=out_shapes,
        compiler_params=pltpu.CompilerParams(
            dimension_semantics=("arbitrary", "arbitrary"), vmem_limit_bytes=VMEM_LIMIT),
        name="inproj",
    )(x, g, w_main, w_gate)


def _mlstm_kernel(mqk_ref, mv_ref, mo_ref, gates_ref, gatest_ref, cw_ref, cb_ref, brow_ref, bcol_ref, og_ref,
                  y_ref, qk_s, cols_s, wrow_s):
    L = M_CHUNK
    dh = M_HEAD_DIM
    r_i = lax.broadcasted_iota(jnp.int32, (L, L), 0)
    c_i = lax.broadcasted_iota(jnp.int32, (L, L), 1)
    causal = c_i <= r_i
    tri_u = (r_i <= c_i).astype(BF16)
    eye = (lax.broadcasted_iota(jnp.int32, (8, LANES), 0)
           == lax.broadcasted_iota(jnp.int32, (8, LANES), 1)).astype(BF16)
    w = cw_ref[...]
    cb = cb_ref[...]
    lane_q = lax.broadcasted_iota(jnp.int32, (1, 2 * M_WIDTH), 1) < M_WIDTH
    kscale = jnp.where(lane_q, 1.0, dh ** -0.5).astype(F32)
    for c in range(M_NCHUNK):
        halo = 16 if c > 0 else 0
        for strip in range(2 * M_WIDTH // LANES):
            cols = slice(strip * LANES, (strip + 1) * LANES)
            xt = mqk_ref[c * L - halo:(c + 1) * L, cols].astype(F32)
            rows = lax.broadcasted_iota(jnp.int32, xt.shape, 0)
            y = cb[:, cols] + w[CONV_K - 1:CONV_K, cols] * xt
            for k in range(1, CONV_K):
                sh = pltpu.roll(xt, k, 0)
                if c == 0:
                    sh = jnp.where(rows >= k, sh, 0.0)
                y = y + w[CONV_K - 1 - k:CONV_K - k, cols] * sh
            y = y[halo:, :]
            y = y * _sigmoid(y) * kscale[:, cols]
            qk_s[c * L:(c + 1) * L, cols] = y.astype(BF16)

    rows_all = M_NCHUNK * 8
    grow = gatest_ref[...].reshape(rows_all, L) + jnp.tile(bcol_ref[...], (M_NCHUNK, 1))
    cs_row = sum(_dot(term, tri_u) for term in _bf16_terms(_log_sigmoid(grow)))
    w_row = grow - pltpu.roll(cs_row, rows_all - M_HEADS, 0)
    lane_t = lax.broadcasted_iota(jnp.int32, (rows_all, L), 1)
    pm = w_row
    shift = 1
    while shift < L:
        pm = jnp.maximum(pm, jnp.where(lane_t >= shift, pltpu.roll(pm, shift, 1), -jnp.inf))
        shift *= 2
    wrow_s[...] = w_row.reshape(M_NCHUNK, 8, L)
    is_i_row = lax.broadcasted_iota(jnp.int32, (rows_all, L), 0) % 8 < M_HEADS
    terms = _bf16_terms(jnp.where(is_i_row, pm, cs_row))
    for c in range(M_NCHUNK):
        cols_s[c * L:(c + 1) * L, :] = sum(_dot_tn(term[c * 8:(c + 1) * 8, :], eye) for term in terms)

    pad_lane = lax.broadcasted_iota(jnp.int32, (L, LANES - dh), 1)
    ones_pad = jnp.where(pad_lane == 0, 1.0, 0.0).astype(BF16)
    og = og_ref[...]
    head_mean = jnp.where(lax.broadcasted_iota(jnp.int32, (M_WIDTH, M_WIDTH), 0) // dh
                          == lax.broadcasted_iota(jnp.int32, (M_WIDTH, M_WIDTH), 1) // dh,
                          1.0 / dh, 0.0).astype(BF16)

    def chunk(c, carry):
        r0 = pl.multiple_of(c * L, L)
        gcol = gates_ref[pl.ds(r0, L), :] + brow_ref[...]
        cs_col = pm_col = cols_s[pl.ds(r0, L), :]
        w_row = wrow_s[c]
        qk = qk_s[pl.ds(r0, L), :]
        vv = mv_ref[pl.ds(r0, L), :]
        new_carry = []
        hs = []
        for h in range(M_HEADS):
            cst, m = carry[h]
            q = qk[:, h * dh:(h + 1) * dh]
            k = qk[:, M_WIDTH + h * dh:M_WIDTH + (h + 1) * dh]
            v_aug = jnp.concatenate([vv[:, h * dh:(h + 1) * dh], ones_pad], axis=1)
            bc = cs_col[:, M_HEADS + h:M_HEADS + h + 1]
            ic = gcol[:, h:h + 1]
            m_row = jnp.maximum(m, pm_col[:, h:h + 1])
            mt = bc + m_row
            inter = jnp.exp(m - m_row)
            sqk = _dot_nt(q, k) * jnp.exp(jnp.where(causal, w_row[h:h + 1, :] - m_row, -jnp.inf))
            nd = inter * _dot(q, cst.astype(BF16)) + _dot(sqk.astype(BF16), v_aug)
            num = nd[:, :dh]
            den = nd[:, dh:dh + 1]
            hs.append(num / jnp.maximum(jnp.abs(den), jnp.exp(-mt)))
            bl = bc[L - 1:L, :]
            gs = bl - bc + ic
            m_new = jnp.maximum(bl + m, jnp.max(gs, axis=0, keepdims=True))
            decay = jnp.exp(bl + m - m_new)
            wts = jnp.exp(gs - m_new)
            kw = (k.astype(F32) * wts).astype(BF16)
            new_carry.append((decay * cst + _dot_tn(kw, v_aug), m_new))
        hcat = jnp.concatenate(hs, axis=1)
        hcat = hcat * lax.rsqrt(_dot((hcat * hcat).astype(BF16), head_mean) + EPS) * og
        o = mo_ref[pl.ds(r0, L), :].astype(F32)
        y_ref[pl.ds(r0, L), :] = (hcat * _sigmoid(o)).astype(y_ref.dtype)
        return tuple(new_carry)

    init = tuple((jnp.zeros((dh, LANES), F32), jnp.zeros((1, 1), F32)) for _ in range(M_HEADS))
    lax.fori_loop(0, M_NCHUNK, chunk, init, unroll=2)


def _mlstm(mqk, mv, mo, gates, gatest, conv_w, conv_b, bias_row, bias_col, out_g):
    per_b = lambda b: (b, 0)
    const = lambda b: (0, 0)
    return pl.pallas_call(
        _mlstm_kernel,
        grid=(BATCH,),
        in_specs=[
            pl.BlockSpec((SEQ, 2 * M_WIDTH), per_b),
            pl.BlockSpec((SEQ, M_WIDTH), per_b),
            pl.BlockSpec((SEQ, M_WIDTH), per_b),
            pl.BlockSpec((SEQ, LANES), per_b),
            pl.BlockSpec((M_NCHUNK, 8, M_CHUNK), lambda b: (b, 0, 0)),
            pl.BlockSpec((CONV_K, 2 * M_WIDTH), const),
            pl.BlockSpec((1, 2 * M_WIDTH), const),
            pl.BlockSpec((1, LANES), const),
            pl.BlockSpec((8, 1), const),
            pl.BlockSpec((1, M_WIDTH), const),
        ],
        out_specs=pl.BlockSpec((SEQ, M_WIDTH), per_b),
        out_shape=jax.ShapeDtypeStruct((TOKENS, M_WIDTH), BF16),
        scratch_shapes=[
            pltpu.VMEM((SEQ, 2 * M_WIDTH), BF16),
            pltpu.VMEM((SEQ, LANES), F32),
            pltpu.VMEM((M_NCHUNK, 8, M_CHUNK), F32),
        ],
        compiler_params=pltpu.CompilerParams(
            dimension_semantics=("arbitrary",), vmem_limit_bytes=VMEM_LIMIT),
        name="mlstm",
    )(mqk, mv, mo, gates, gatest, conv_w, conv_b, bias_row, bias_col, out_g)


def _qk_norm_rope(x, gain, cos, sin):
    lane = lax.broadcasted_iota(jnp.int32, x.shape, 1)
    first = lane < A_HEAD_DIM
    sq = x * x
    tot = jnp.sum(sq, axis=-1, keepdims=True)
    s0 = jnp.sum(jnp.where(first, sq, 0.0), axis=-1, keepdims=True)
    ms = jnp.where(first, s0, tot - s0) * (1.0 / A_HEAD_DIM)
    xn = x * lax.rsqrt(ms + EPS) * gain
    in_low = (lane % A_HEAD_DIM) < ROPE_HALF
    swapped = jnp.where(in_low, pltpu.roll(xn, LANES - ROPE_HALF, 1), pltpu.roll(xn, ROPE_HALF, 1))
    return xn * cos + swapped * sin


def _moba_kernel(aq_ref, ak_ref, av_ref, cos_ref, sin_ref, qg_ref, kg_ref, og_ref, y_ref, kr_s, s_s):
    blk = MOBA_BLOCK
    dh = A_HEAD_DIM
    kr = _qk_norm_rope(ak_ref[...].astype(F32), kg_ref[...], cos_ref[...], sin_ref[...])
    kr_s[...] = kr.astype(BF16)
    km = jnp.mean(kr.reshape(MOBA_NB, blk, HEAD_PAIR), axis=1)
    km_first = lax.broadcasted_iota(jnp.int32, km.shape, 1) < dh
    km_heads = (jnp.where(km_first, km, 0.0), jnp.where(km_first, 0.0, km))
    first = lax.broadcasted_iota(jnp.int32, (blk, HEAD_PAIR), 1) < dh
    blk_id = lax.broadcasted_iota(jnp.int32, (MOBA_NB, blk), 0)
    eye = (lax.broadcasted_iota(jnp.int32, (MOBA_NB, LANES), 0)
           == lax.broadcasted_iota(jnp.int32, (MOBA_NB, LANES), 1)).astype(F32)
    q_pos = lax.broadcasted_iota(jnp.int32, (2 * blk, blk), 0) % blk
    k_pos = lax.broadcasted_iota(jnp.int32, (2 * blk, blk), 1)
    causal = k_pos <= q_pos
    og = og_ref[...]

    def scores(i, out):
        rows = slice(i * blk, (i + 1) * blk)
        qr = _qk_norm_rope(aq_ref[rows, :].astype(F32), qg_ref[...], cos_ref[rows, :], sin_ref[rows, :])
        qs = qr * (dh ** -0.5 * math.log2(math.e))
        q2 = jnp.concatenate([jnp.where(first, qs, 0.0), jnp.where(first, 0.0, qs)], axis=0).astype(BF16)

        bias = None
        if i > MOBA_TOPK:
            past = blk_id < i
            cols = []
            for kmh in km_heads:
                gate = jnp.where(past, _dot_nt(kmh, qr, precision=HIGHEST), -jnp.inf)
                rank = jnp.zeros((MOBA_NB, blk), F32)
                for m_blk in range(i):
                    gm = gate[m_blk:m_blk + 1, :]
                    ahead = (gm > gate) | ((gm == gate) & (m_blk < blk_id))
                    rank = rank + jnp.where(ahead, 1.0, 0.0)
                sel = jnp.where(past & (rank < float(MOBA_TOPK)), 1.0, 0.0)
                cols.append(_dot_tn(sel, eye))
            bias = jnp.where(jnp.concatenate(cols, axis=0) > 0.5, 0.0, -jnp.inf)

        yield
        m_run = None
        for j in range(i + 1):
            s = _dot_nt(q2, kr_s[j * blk:(j + 1) * blk, :])
            if j == i:
                s = jnp.where(causal, s, -jnp.inf)
            elif bias is not None:
                s = s + bias[:, j:j + 1]
            s_s[i % 2, :, j * blk:(j + 1) * blk] = s
            part = jnp.maximum(s[:, :LANES], s[:, LANES:])
            m_run = part if m_run is None else jnp.maximum(m_run, part)
            yield
        out.append(jnp.max(m_run, axis=1, keepdims=True))

    def attend(i, m_fin):
        rows = slice(i * blk, (i + 1) * blk)
        l_run = jnp.zeros((2 * blk, LANES), F32)
        acc = jnp.zeros((2 * blk, HEAD_PAIR), F32)
        for j in range(i + 1):
            p = jnp.exp2(s_s[i % 2, :, j * blk:(j + 1) * blk] - m_fin)
            l_run = l_run + (p[:, :LANES] + p[:, LANES:])
            acc = acc + _dot(p.astype(BF16), av_ref[j * blk:(j + 1) * blk, :])
            yield
        o2 = acc / jnp.sum(l_run, axis=1, keepdims=True)
        o = jnp.where(first, o2[:blk, :], o2[blk:, :])
        sq = o * o
        tot = jnp.sum(sq, axis=-1, keepdims=True)
        s0 = jnp.sum(jnp.where(first, sq, 0.0), axis=-1, keepdims=True)
        ms = jnp.where(first, s0, tot - s0) * (1.0 / dh)
        y_ref[rows, :] = (o * lax.rsqrt(ms + EPS) * og).astype(y_ref.dtype)

    def interleave(*stages):
        stages = list(stages)
        while stages:
            for stage in list(stages):
                try:
                    next(stage)
                except StopIteration:
                    stages.remove(stage)

    m_cur = []
    interleave(scores(0, m_cur))
    for i in range(MOBA_NB):
        m_next = []
        stages = [attend(i, m_cur[0])]
        if i + 1 < MOBA_NB:
            stages.insert(0, scores(i + 1, m_next))
        interleave(*stages)
        m_cur = m_next


def _moba(aq, ak, av, cos_t, sin_t, q_gain, k_gain, out_gain):
    n_pairs = A_HEADS // 2
    const = lambda b, p: (0, 0)
    pair = lambda b, p: (b, p)
    return pl.pallas_call(
        _moba_kernel,
        grid=(BATCH, n_pairs),
        in_specs=[
            pl.BlockSpec((SEQ, HEAD_PAIR), pair),
            pl.BlockSpec((SEQ, HEAD_PAIR), pair),
            pl.BlockSpec((SEQ, HEAD_PAIR), pair),
            pl.BlockSpec((SEQ, HEAD_PAIR), const),
            pl.BlockSpec((SEQ, HEAD_PAIR), const),
            pl.BlockSpec((1, HEAD_PAIR), const),
            pl.BlockSpec((1, HEAD_PAIR), const),
            pl.BlockSpec((None, 1, HEAD_PAIR), lambda b, p: (p, 0, 0)),
        ],
        out_specs=pl.BlockSpec((SEQ, HEAD_PAIR), pair),
        out_shape=jax.ShapeDtypeStruct((TOKENS, A_WIDTH), BF16),
        scratch_shapes=[
            pltpu.VMEM((SEQ, HEAD_PAIR), BF16),
            pltpu.VMEM((2, 2 * MOBA_BLOCK, SEQ), F32),
        ],
        compiler_params=pltpu.CompilerParams(
            dimension_semantics=("arbitrary", "arbitrary"), vmem_limit_bytes=VMEM_LIMIT),
        name="moba",
    )(aq, ak, av, cos_t, sin_t, q_gain, k_gain, out_gain)


def _s5_kernel(u_ref, abar_ref, bre_ref, bim_ref, cre_ref, cim_ref, d_ref, gw_ref, gb_ref, gmat_ref, og_ref,
               y_ref, xr_s, xi_s, st_s, u_s, y_s):
    step = pl.program_id(0)
    rows = S5_STEPS * BATCH
    tile = 256
    n_tiles = rows // tile
    halves = S_WIDTH // LANES

    @pl.when(step == 0)
    def _init():
        st_s[...] = jnp.zeros_like(st_s)

    for b in range(BATCH):
        for hv in range(halves):
            lanes = slice(b * S_WIDTH + hv * LANES, b * S_WIDTH + (hv + 1) * LANES)
            u_s[hv, pl.ds(b, S5_STEPS, stride=BATCH), :] = u_ref[:, lanes].astype(F32)

    def u_rows(r):
        return jnp.concatenate([u_s[hv, r * tile:(r + 1) * tile, :] for hv in range(halves)], axis=1)

    for r in range(n_tiles):
        u = u_rows(r).astype(BF16)
        t0 = r * tile // BATCH
        xr_s[t0:t0 + tile // BATCH] = _dot(u, bre_ref[...]).reshape(tile // BATCH, BATCH, S_FLAT)
        xi_s[t0:t0 + tile // BATCH] = _dot(u, bim_ref[...]).reshape(tile // BATCH, BATCH, S_FLAT)

    ar = jnp.broadcast_to(abar_ref[0:1, :], (BATCH, S_FLAT))
    ai = jnp.broadcast_to(abar_ref[1:2, :], (BATCH, S_FLAT))

    def scan_step(t, carry):
        sr, si = carry
        nr = ar * sr - ai * si + xr_s[t]
        ni = ar * si + ai * sr + xi_s[t]
        xr_s[t] = nr
        xi_s[t] = ni
        return nr, ni

    sr, si = lax.fori_loop(0, S5_STEPS, scan_step, (st_s[0], st_s[1]), unroll=4)
    st_s[0] = sr
    st_s[1] = si

    for r in range(n_tiles):
        t0 = r * tile // BATCH
        xr = xr_s[t0:t0 + tile // BATCH].reshape(tile, S_FLAT).astype(BF16)
        xi = xi_s[t0:t0 + tile // BATCH].reshape(tile, S_FLAT).astype(BF16)
        y = _dot(xr, cre_ref[...]) - _dot(xi, cim_ref[...]) + d_ref[...] * u_rows(r)
        z = 0.5 * y * (1.0 + jnp.tanh(math.sqrt(2.0 / math.pi) * (y + 0.044715 * (y * y * y))))
        y2 = z * _sigmoid(_dot(z.astype(BF16), gw_ref[...]) + gb_ref[...])
        ms = _dot((y2 * y2).astype(BF16), gmat_ref[...])
        yn = y2 * lax.rsqrt(ms + EPS) * og_ref[...]
        for hv in range(halves):
            y_s[hv, r * tile:(r + 1) * tile, :] = yn[:, hv * LANES:(hv + 1) * LANES]

    for b in range(BATCH):
        for hv in range(halves):
            lanes = slice(b * S_WIDTH + hv * LANES, b * S_WIDTH + (hv + 1) * LANES)
            y_ref[:, lanes] = y_s[hv, pl.ds(b, S5_STEPS, stride=BATCH), :].astype(y_ref.dtype)


def _s5(u_sb, abar, bre, bim, cre, cim, d_row, glu_w, glu_b, gmat, out_g):
    rows = S5_STEPS * BATCH
    const = lambda s: (0, 0)
    return pl.pallas_call(
        _s5_kernel,
        grid=(SEQ // S5_STEPS,),
        in_specs=[
            pl.BlockSpec((S5_STEPS, BATCH * S_WIDTH), lambda s: (s, 0)),
            pl.BlockSpec((2, S_FLAT), const),
            pl.BlockSpec((S_WIDTH, S_FLAT), const),
            pl.BlockSpec((S_WIDTH, S_FLAT), const),
            pl.BlockSpec((S_FLAT, S_WIDTH), const),
            pl.BlockSpec((S_FLAT, S_WIDTH), const),
            pl.BlockSpec((1, S_WIDTH), const),
            pl.BlockSpec((S_WIDTH, S_WIDTH), const),
            pl.BlockSpec((1, S_WIDTH), const),
            pl.BlockSpec((S_WIDTH, S_WIDTH), const),
            pl.BlockSpec((1, S_WIDTH), const),
        ],
        out_specs=pl.BlockSpec((S5_STEPS, BATCH * S_WIDTH), lambda s: (s, 0)),
        out_shape=jax.ShapeDtypeStruct((SEQ, BATCH * S_WIDTH), BF16),
        scratch_shapes=[
            pltpu.VMEM((S5_STEPS, BATCH, S_FLAT), F32),
            pltpu.VMEM((S5_STEPS, BATCH, S_FLAT), F32),
            pltpu.VMEM((2, BATCH, S_FLAT), F32),
            pltpu.VMEM((S_WIDTH // LANES, rows, LANES), F32),
            pltpu.VMEM((S_WIDTH // LANES, rows, LANES), F32),
        ],
        compiler_params=pltpu.CompilerParams(
            dimension_semantics=("arbitrary",), vmem_limit_bytes=VMEM_LIMIT),
        name="s5",
    )(u_sb, abar, bre, bim, cre, cim, d_row, glu_w, glu_b, gmat, out_g)


def _outproj_kernel(x_ref, ym_ref, ya_ref, ys_ref, w_ref, g_ref, wr_ref, br_ref,
                    x1_ref, xn_ref, route_ref):
    x1 = x_ref[...]
    x1 = x1 + _dot(ym_ref[...], w_ref[0:M_WIDTH, :])
    x1 = x1 + _dot(ya_ref[...], w_ref[M_WIDTH:M_WIDTH + A_WIDTH, :])
    x1 = x1 + _dot(ys_ref[...], w_ref[M_WIDTH + A_WIDTH:, :])
    x1_ref[...] = x1
    xn = x1 * lax.rsqrt(jnp.mean(x1 * x1, axis=-1, keepdims=True) + EPS) * g_ref[...]
    for j in range(ROW_TILES):
        xn_ref[pl.ds(j, TM_PROJ, stride=ROW_TILES), :] = xn[:, j * LANES:(j + 1) * LANES]
    logits = _split_dot(xn, wr_ref[0], wr_ref[1]) + br_ref[...]

    lane = lax.broadcasted_iota(jnp.int32, logits.shape, 1).astype(F32)
    is_grp = lane < N_GROUPS
    neg = -jnp.inf
    gl = jnp.where(is_grp, logits, neg)
    gmax = jnp.max(gl, axis=-1, keepdims=True)
    gsum = jnp.sum(jnp.where(is_grp, jnp.exp(logits - gmax), 0.0), axis=-1, keepdims=True)
    g_sel = jnp.min(jnp.where(gl == gmax, lane, float(LANES)), axis=-1, keepdims=True)
    lo = N_GROUPS + EXPERTS_PER_GROUP * g_sel
    el = jnp.where((lane >= lo) & (lane < lo + EXPERTS_PER_GROUP), logits, neg)
    t1 = jnp.max(el, axis=-1, keepdims=True)
    i1 = jnp.min(jnp.where(el == t1, lane, float(LANES)), axis=-1, keepdims=True)
    el2 = jnp.where(lane == i1, neg, el)
    t2 = jnp.max(el2, axis=-1, keepdims=True)
    i2 = jnp.min(jnp.where(el2 == t2, lane, float(LANES)), axis=-1, keepdims=True)
    e21 = jnp.exp(t2 - t1)
    w1 = 1.0 / ((1.0 + e21) * gsum)
    w2 = e21 / ((1.0 + e21) * gsum)
    route = jnp.where(lane == 0.0, i1 - N_GROUPS,
                      jnp.where(lane == 1.0, i2 - N_GROUPS,
                                jnp.where(lane == 2.0, w1, jnp.where(lane == 3.0, w2, 0.0))))
    route_ref[...] = route


def _outproj(x, y_m, y_a, y_s_sb, w_out, g2, w_router, b_router):
    n_s = SEQ // TM_PROJ
    row = lambda b, s: (b * n_s + s, 0)
    const = lambda b, s: (0, 0)
    return pl.pallas_call(
        _outproj_kernel,
        grid=(BATCH, n_s),
        in_specs=[
            pl.BlockSpec((TM_PROJ, D_MODEL), row),
            pl.BlockSpec((TM_PROJ, M_WIDTH), row),
            pl.BlockSpec((TM_PROJ, A_WIDTH), row),
            pl.BlockSpec((TM_PROJ, S_WIDTH), lambda b, s: (s, b)),
            pl.BlockSpec((D_MIX, D_MODEL), const),
            pl.BlockSpec((1, D_MODEL), const),
            pl.BlockSpec((2, D_MODEL, LANES), lambda b, s: (0, 0, 0)),
            pl.BlockSpec((1, LANES), const),
        ],
        out_specs=(
            pl.BlockSpec((TM_PROJ, D_MODEL), row),
            pl.BlockSpec((TM_PROJ * ROW_TILES, LANES), row),
            pl.BlockSpec((TM_PROJ, LANES), row),
        ),
        out_shape=(
            jax.ShapeDtypeStruct((TOKENS, D_MODEL), F32),
            jax.ShapeDtypeStruct((TOKENS * ROW_TILES, LANES), F32),
            jax.ShapeDtypeStruct((TOKENS, LANES), F32),
        ),
        compiler_params=pltpu.CompilerParams(
            dimension_semantics=("arbitrary", "arbitrary"), vmem_limit_bytes=VMEM_LIMIT),
        name="outproj",
    )(x, y_m, y_a, y_s_sb, w_out, g2, w_router, b_router)


def _ffn_kernel(be_ref, xs_ref, wg_ref, wu_ref, wd_ref, ys_ref, wg_s, wu_s, wd_s):
    i = pl.program_id(0)
    prev = be_ref[jnp.maximum(i - 1, 0)]

    @pl.when((i == 0) | (be_ref[i] != prev))
    def _load_expert():
        wg_s[...] = wg_ref[...].astype(BF16)
        wu_s[...] = wu_ref[...].astype(BF16)
        wd_s[...] = wd_ref[...].astype(BF16)

    xb = jnp.concatenate([xs_ref[pl.ds(j, FFN_BLOCK, stride=ROW_TILES), :] for j in range(ROW_TILES)],
                         axis=1).astype(BF16)
    g = _dot(xb, wg_s[...])
    u = _dot(xb, wu_s[...])
    hmid = (g * _sigmoid(g) * u).astype(BF16)
    y = _dot(hmid, wd_s[...])
    for j in range(ROW_TILES):
        ys_ref[pl.ds(j, FFN_BLOCK, stride=ROW_TILES), :] = y[:, j * LANES:(j + 1) * LANES]


def _expert_ffn(layer, blk_exp, xs, w_gate, w_up, w_down):
    nblk = blk_exp.shape[0]
    rows = pl.BlockSpec((FFN_BLOCK * ROW_TILES, LANES), lambda i, be: (i, 0))
    grid_spec = pltpu.PrefetchScalarGridSpec(
        num_scalar_prefetch=1,
        grid=(nblk,),
        in_specs=[
            rows,
            pl.BlockSpec((None, None, D_MODEL, D_EXPERT), lambda i, be: (layer, be[i], 0, 0)),
            pl.BlockSpec((None, None, D_MODEL, D_EXPERT), lambda i, be: (layer, be[i], 0, 0)),
            pl.BlockSpec((None, None, D_EXPERT, D_MODEL), lambda i, be: (layer, be[i], 0, 0)),
        ],
        out_specs=rows,
        scratch_shapes=[
            pltpu.VMEM((D_MODEL, D_EXPERT), BF16),
            pltpu.VMEM((D_MODEL, D_EXPERT), BF16),
            pltpu.VMEM((D_EXPERT, D_MODEL), BF16),
        ],
    )
    return pl.pallas_call(
        _ffn_kernel,
        grid_spec=grid_spec,
        out_shape=jax.ShapeDtypeStruct((nblk * FFN_BLOCK * ROW_TILES, LANES), F32),
        compiler_params=pltpu.CompilerParams(
            dimension_semantics=("arbitrary",), vmem_limit_bytes=VMEM_LIMIT),
        name="expert_ffn",
    )(blk_exp, xs, w_gate, w_up, w_down)


def _plan_kernel(e_ref, ps_ref, dest_ref, run_s, tri_s):
    i = pl.program_id(0)

    @pl.when(i == 0)
    def _init():
        run_s[...] = ps_ref[...]
        s_i = lax.broadcasted_iota(jnp.int32, (PLAN_TILE, PLAN_TILE), 0)
        t_i = lax.broadcasted_iota(jnp.int32, (PLAN_TILE, PLAN_TILE), 1)
        tri_s[...] = (s_i < t_i).astype(BF16)

    expert = lax.broadcasted_iota(jnp.int32, (N_EXPERTS, PLAN_TILE), 0)
    onehot = jnp.where(e_ref[...] == expert, 1.0, 0.0)
    before = _dot(onehot.astype(BF16), tri_s[...])
    dest = jnp.sum(onehot * (before + run_s[:, 0:1]), axis=0, keepdims=True)
    dest_ref[...] = dest.astype(jnp.int32)
    run_s[...] = run_s[...] + jnp.sum(onehot, axis=1, keepdims=True)


def _plan(e_rows, pstart_col):
    n_tiles = e_rows.shape[0]
    return pl.pallas_call(
        _plan_kernel,
        grid=(n_tiles,),
        in_specs=[
            pl.BlockSpec((None, 1, PLAN_TILE), lambda i: (i, 0, 0)),
            pl.BlockSpec((N_EXPERTS, LANES), lambda i: (0, 0)),
        ],
        out_specs=pl.BlockSpec((None, 1, PLAN_TILE), lambda i: (i, 0, 0)),
        out_shape=jax.ShapeDtypeStruct((n_tiles, 1, PLAN_TILE), jnp.int32),
        scratch_shapes=[pltpu.VMEM((N_EXPERTS, LANES), F32), pltpu.VMEM((PLAN_TILE, PLAN_TILE), BF16)],
        compiler_params=pltpu.CompilerParams(
            dimension_semantics=("arbitrary",), vmem_limit_bytes=VMEM_LIMIT),
        name="moe_plan",
    )(e_rows, pstart_col)


def _dispatch_kernel(fill_ref, d0_ref, d1_ref, xn_ref, xs_hbm, zero_s, sem, fill_sem):
    @pl.when(pl.program_id(0) == 0)
    def _zero_padding():
        zero_s[...] = jnp.zeros_like(zero_s)

        def fill(c):
            start = pl.multiple_of(fill_ref[0, c] * ROW_TILES, ROW_TILES)
            return pltpu.make_async_copy(zero_s, xs_hbm.at[pl.ds(start, FFN_BLOCK * ROW_TILES), :], fill_sem)

        for c in range(N_FILLS):
            pl.when(fill_ref[0, c] >= 0)(lambda c=c: fill(c).start())
        for c in range(N_FILLS):
            pl.when(fill_ref[0, c] >= 0)(lambda c=c: fill(c).wait())

    def row_copy(dref, t):
        src = xn_ref.at[pl.ds(pl.multiple_of(t * ROW_TILES, ROW_TILES), ROW_TILES), :]
        dst = xs_hbm.at[pl.ds(pl.multiple_of(dref[0, t] * ROW_TILES, ROW_TILES), ROW_TILES), :]
        return pltpu.make_async_copy(src, dst, sem)

    def start(t, carry):
        row_copy(d0_ref, t).start()
        row_copy(d1_ref, t).start(priority=1)
        return carry

    lax.fori_loop(0, DISPATCH_TOKENS, start, 0, unroll=8)
    for _ in range(EXPERT_TOPK):
        pltpu.make_async_copy(xn_ref, xs_hbm.at[pl.ds(0, DISPATCH_TOKENS * ROW_TILES), :], sem).wait()


def _dispatch(fill_rows, dest, xn, n_rows):
    n_tiles = TOKENS // DISPATCH_TOKENS
    dest4 = dest.reshape(EXPERT_TOPK, n_tiles, 1, DISPATCH_TOKENS)
    smem_tile = lambda k: pl.BlockSpec((None, None, 1, DISPATCH_TOKENS), lambda i: (k, i, 0, 0),
                                       memory_space=pltpu.SMEM)
    return pl.pallas_call(
        _dispatch_kernel,
        grid=(n_tiles,),
        in_specs=[
            pl.BlockSpec(memory_space=pltpu.SMEM),
            smem_tile(0),
            smem_tile(1),
            pl.BlockSpec((DISPATCH_TOKENS * ROW_TILES, LANES), lambda i: (i, 0)),
        ],
        out_specs=pl.BlockSpec(memory_space=pl.ANY),
        out_shape=jax.ShapeDtypeStruct((n_rows * ROW_TILES, LANES), F32),
        scratch_shapes=[
            pltpu.VMEM((FFN_BLOCK * ROW_TILES, LANES), F32),
            pltpu.SemaphoreType.DMA(()),
            pltpu.SemaphoreType.DMA(()),
        ],
        compiler_params=pltpu.CompilerParams(
            dimension_semantics=("arbitrary",), vmem_limit_bytes=VMEM_LIMIT),
        name="moe_dispatch",
    )(fill_rows, dest4, dest4, xn)


def _combine_kernel(d0_ref, d1_ref, d0_next_ref, d1_next_ref, x1_ref, route_ref, ys_hbm, out_ref, buf, sem):
    i = pl.program_id(0)
    n = pl.num_programs(0)

    def start_tile(drefs, slot):
        def body(t, carry):
            for k in range(EXPERT_TOPK):
                src = ys_hbm.at[pl.ds(pl.multiple_of(drefs[k][0, t] * ROW_TILES, ROW_TILES), ROW_TILES), :]
                dst = buf.at[slot, k, pl.ds(pl.multiple_of(t * ROW_TILES, ROW_TILES), ROW_TILES), :]
                pltpu.make_async_copy(src, dst, sem.at[slot]).start(priority=k)
            return carry
        lax.fori_loop(0, COMBINE_TOKENS, body, 0, unroll=8)

    slot = i % 2

    @pl.when(i == 0)
    def _first():
        start_tile((d0_ref, d1_ref), 0)

    @pl.when(i + 1 < n)
    def _prefetch():
        start_tile((d0_next_ref, d1_next_ref), 1 - slot)

    for k in range(EXPERT_TOPK):
        pltpu.make_async_copy(ys_hbm.at[pl.ds(0, COMBINE_TOKENS * ROW_TILES), :], buf.at[slot, k],
                              sem.at[slot]).wait()
    w = route_ref[...]
    w1 = w[:, 2:3]
    w2 = w[:, 3:4]
    for j in range(ROW_TILES):
        cols = slice(j * LANES, (j + 1) * LANES)
        chunk = pl.ds(j, COMBINE_TOKENS, stride=ROW_TILES)
        out_ref[:, cols] = x1_ref[:, cols] + w1 * buf[slot, 0, chunk, :] + w2 * buf[slot, 1, chunk, :]


def _combine(dest, x1, route, ys3):
    n_tiles = TOKENS // COMBINE_TOKENS
    dest4 = dest.reshape(EXPERT_TOPK, n_tiles, 1, COMBINE_TOKENS)
    smem_tile = lambda k, nxt: pl.BlockSpec(
        (None, None, 1, COMBINE_TOKENS), lambda i: (k, jnp.minimum(i + nxt, n_tiles - 1), 0, 0),
        memory_space=pltpu.SMEM)
    return pl.pallas_call(
        _combine_kernel,
        grid=(n_tiles,),
        in_specs=[
            smem_tile(0, 0), smem_tile(1, 0), smem_tile(0, 1), smem_tile(1, 1),
            pl.BlockSpec((COMBINE_TOKENS, D_MODEL), lambda i: (i, 0)),
            pl.BlockSpec((COMBINE_TOKENS, LANES), lambda i: (i, 0)),
            pl.BlockSpec(memory_space=pl.ANY),
        ],
        out_specs=pl.BlockSpec((COMBINE_TOKENS, D_MODEL), lambda i: (i, 0)),
        out_shape=jax.ShapeDtypeStruct((TOKENS, D_MODEL), F32),
        scratch_shapes=[
            pltpu.VMEM((2, EXPERT_TOPK, COMBINE_TOKENS * ROW_TILES, LANES), F32),
            pltpu.SemaphoreType.DMA((2,)),
        ],
        compiler_params=pltpu.CompilerParams(
            dimension_semantics=("arbitrary",), vmem_limit_bytes=VMEM_LIMIT),
        name="moe_combine",
    )(dest4, dest4, dest4, dest4, x1, route, ys3)


def _rope_tables():
    inv = ROPE_THETA ** (-np.arange(ROPE_HALF, dtype=np.float64) * 2.0 / ROPE_DIM)
    ang = np.arange(SEQ, dtype=np.float64)[:, None] * inv[None, :]
    cos = np.ones((SEQ, A_HEAD_DIM))
    sin = np.zeros((SEQ, A_HEAD_DIM))
    cos[:, :ROPE_HALF] = np.cos(ang)
    cos[:, ROPE_HALF:ROPE_DIM] = np.cos(ang)
    sin[:, :ROPE_HALF] = -np.sin(ang)
    sin[:, ROPE_HALF:ROPE_DIM] = np.sin(ang)
    return (jnp.asarray(np.tile(cos, (1, 2)), F32), jnp.asarray(np.tile(sin, (1, 2)), F32))


def _split_w_in(w_in):
    sizes = [M_WIDTH, M_WIDTH, M_WIDTH, M_WIDTH, M_HEADS, M_HEADS, A_WIDTH, A_WIDTH, A_WIDTH, S_WIDTH]
    offs = np.cumsum([0] + sizes)
    mq, mk, mv, mo, mi, mf, aq, ak, av, su = (w_in[:, offs[n]:offs[n + 1]] for n in range(len(sizes)))
    w_main = jnp.concatenate([aq, ak, av, mq, mk, mv, mo, su], axis=1).astype(BF16)
    gate = jnp.concatenate([mi, mf], axis=1)
    return w_main, _hi_lo(jnp.pad(gate, ((0, 0), (0, LANES - 2 * M_HEADS))))


def _s5_params(a_re, a_im, b_re, b_im, c_re, c_im, log_dt):
    dt = jnp.exp(log_dt)[:, None]
    mag = jnp.exp(a_re * dt)
    abar_re, abar_im = mag * jnp.cos(a_im * dt), mag * jnp.sin(a_im * dt)
    zr, zi = abar_re - 1.0, abar_im
    den = a_re * a_re + a_im * a_im
    fr, fi = (zr * a_re + zi * a_im) / den, (zi * a_re - zr * a_im) / den
    bbar_re = fr[..., None] * b_re - fi[..., None] * b_im
    bbar_im = fr[..., None] * b_im + fi[..., None] * b_re
    eye = jnp.eye(S_GROUPS, dtype=F32)
    dense_b = lambda t: jnp.einsum('gph,gk->ghkp', t, eye).reshape(S_WIDTH, S_FLAT).astype(BF16)
    dense_c = lambda t: jnp.einsum('ghp,gk->gpkh', t, eye).reshape(S_FLAT, S_WIDTH).astype(BF16)
    abar = jnp.stack([abar_re.reshape(S_FLAT), abar_im.reshape(S_FLAT)])
    return abar, dense_b(bbar_re), dense_b(bbar_im), dense_c(c_re), dense_c(c_im)


def _moe(layer, x1, xn, route, w_gate, w_up, w_down):
    n_assign = TOKENS * EXPERT_TOPK
    e_flat = route[:, :EXPERT_TOPK].astype(jnp.int32).T.reshape(n_assign)
    experts = jnp.arange(N_EXPERTS, dtype=jnp.int32)
    counts = jnp.sum((e_flat[:, None] == experts[None, :]).astype(jnp.int32), axis=0)
    padded = (counts + FFN_BLOCK - 1) // FFN_BLOCK * FFN_BLOCK
    pends = jnp.cumsum(padded)
    pstarts = pends - padded
    n_rows = (n_assign + N_EXPERTS * (FFN_BLOCK - 1) + FFN_BLOCK - 1) // FFN_BLOCK * FFN_BLOCK
    nblk = n_rows // FFN_BLOCK
    blk_start = jnp.arange(nblk, dtype=jnp.int32) * FFN_BLOCK
    blk_exp = jnp.minimum(jnp.sum((pends[None, :] <= blk_start[:, None]).astype(jnp.int32), axis=1),
                          N_EXPERTS - 1).astype(jnp.int32)
    pstart_col = jnp.broadcast_to(pstarts.astype(F32)[:, None], (N_EXPERTS, LANES))
    dest = _plan(e_flat.reshape(n_assign // PLAN_TILE, 1, PLAN_TILE), pstart_col)
    tail = pends[-1] + jnp.arange(N_EXPERTS, dtype=jnp.int32) * FFN_BLOCK
    fill_rows = jnp.concatenate([jnp.where(padded > 0, pends - FFN_BLOCK, -1), jnp.where(tail < n_rows, tail, -1)])
    fill_rows = fill_rows.astype(jnp.int32).reshape(1, N_FILLS)
    xs = _dispatch(fill_rows, dest, xn, n_rows)
    ys = _expert_ffn(layer, blk_exp, xs, w_gate, w_up, w_down)
    return _combine(dest, x1, route, ys)


def kernel(x, norm1_g, w_in, m_bias_i, m_bias_f, m_conv_w, m_conv_b, m_out_g, a_q_g, a_k_g, a_out_g, s_a_re, s_a_im, s_b_re, s_b_im, s_c_re, s_c_im, s_d, s_log_dt, s_glu_w, s_glu_b, s_out_g, w_out, norm2_g, r_group_w, r_group_b, r_expert_w, r_expert_b, e_w_gate, e_w_up, e_w_down):
    cos_t, sin_t = _rope_tables()
    gmat = jnp.asarray(np.kron(np.eye(S_GROUPS), np.full((S_GROUP_DIM, S_GROUP_DIM), 1.0 / S_GROUP_DIM)), BF16)
    xf = x.reshape(TOKENS, D_MODEL)
    for l in range(DEPTH):
        w_main, w_gate = _split_w_in(w_in[l])
        aq, ak, av, mqk, mv, mo, su, gates, gatest = _inproj(xf, norm1_g[l].reshape(1, D_MODEL), w_main, w_gate)

        bias = jnp.concatenate([m_bias_i[l], m_bias_f[l]])
        bias_row = jnp.pad(bias, (0, LANES - 2 * M_HEADS)).reshape(1, LANES)
        y_m = _mlstm(mqk, mv, mo, gates, gatest, m_conv_w[l], m_conv_b[l].reshape(1, 2 * M_WIDTH),
                     bias_row, bias.reshape(2 * M_HEADS, 1), m_out_g[l].reshape(1, M_WIDTH))

        y_a = _moba(aq, ak, av, cos_t, sin_t,
                    jnp.tile(a_q_g[l], 2).reshape(1, HEAD_PAIR), jnp.tile(a_k_g[l], 2).reshape(1, HEAD_PAIR),
                    a_out_g[l].reshape(A_HEADS // 2, 1, HEAD_PAIR))

        abar, bre, bim, cre, cim = _s5_params(s_a_re[l], s_a_im[l], s_b_re[l], s_b_im[l],
                                              s_c_re[l], s_c_im[l], s_log_dt[l])
        y_s = _s5(su, abar, bre, bim, cre, cim,
                  s_d[l].reshape(1, S_WIDTH), s_glu_w[l].astype(BF16), s_glu_b[l].reshape(1, S_WIDTH),
                  gmat, s_out_g[l].reshape(1, S_WIDTH))

        w_router = jnp.pad(jnp.concatenate([r_group_w[l], r_expert_w[l]], axis=1),
                           ((0, 0), (0, LANES - N_GROUPS - N_EXPERTS)))
        b_router = jnp.pad(jnp.concatenate([r_group_b[l], r_expert_b[l]]),
                           (0, LANES - N_GROUPS - N_EXPERTS)).reshape(1, LANES)
        x1, xn, route = _outproj(xf, y_m, y_a, y_s, w_out[l].astype(BF16),
                                 norm2_g[l].reshape(1, D_MODEL), _hi_lo(w_router), b_router)
        xf = _moe(l, x1, xn, route, e_w_gate, e_w_up, e_w_down)
    return xf.reshape(BATCH, SEQ, D_MODEL)
```

```python
import functools
import math

import numpy as np
import jax
import jax.numpy as jnp
from jax import lax
from jax.experimental import pallas as pl
from jax.experimental.pallas import tpu as pltpu

F32 = jnp.float32
BF16 = jnp.bfloat16

D_MODEL = 1024
BATCH = 8
SEQ = 2048
DEPTH = 2
TOKENS = BATCH * SEQ

M_HEADS = 4
M_HEAD_DIM = 64
M_WIDTH = M_HEADS * M_HEAD_DIM
A_HEADS = 8
A_HEAD_DIM = 64
A_WIDTH = A_HEADS * A_HEAD_DIM
S_GROUPS = 16
S_GROUP_DIM = 16
S_WIDTH = S_GROUPS * S_GROUP_DIM
S_STATE = 64
S_FLAT = S_GROUPS * S_STATE
D_MIX = M_WIDTH + A_WIDTH + S_WIDTH

CONV_K = 4
MOBA_BLOCK = 256
MOBA_NB = SEQ // MOBA_BLOCK
MOBA_TOPK = 3
ROPE_THETA = 500000.0
ROPE_DIM = A_HEAD_DIM // 4
ROPE_HALF = ROPE_DIM // 2

N_GROUPS = 4
EXPERTS_PER_GROUP = 8
N_EXPERTS = N_GROUPS * EXPERTS_PER_GROUP
EXPERT_TOPK = 2
D_EXPERT = 512
DISPATCH_BLOCK = 128
EPS = 1e-6

LANES = 128
ROW_TILES = D_MODEL // (2 * LANES)
U32 = jnp.uint32
HEAD_PAIR = 2 * A_HEAD_DIM

W_MAIN = 3 * A_WIDTH + 4 * M_WIDTH + S_WIDTH
TM_PROJ = 512
M_CHUNK = 256
M_NCHUNK = SEQ // M_CHUNK
S5_STEPS = 128
FFN_BLOCK = 256
N_FILLS = 2 * N_EXPERTS
PLAN_TILE = 1024
DISPATCH_TOKENS = 512
COMBINE_TOKENS = 256
VMEM_LIMIT = 56 * 1024 * 1024

HIGHEST = lax.Precision.HIGHEST


def _dot(a, b, precision=None):
    return jnp.dot(a, b, preferred_element_type=F32, precision=precision)


def _dot_nt(a, b, precision=None):
    return lax.dot_general(a, b, (((1,), (1,)), ((), ())), preferred_element_type=F32, precision=precision)


def _dot_tn(a, b, precision=None):
    return lax.dot_general(a, b, (((0,), (0,)), ((), ())), preferred_element_type=F32, precision=precision)


def _split_dot(a, w_hi, w_lo):
    a_hi = a.astype(BF16)
    a_lo = (a - a_hi.astype(F32)).astype(BF16)
    return _dot(a_hi, w_hi) + _dot(a_lo, w_hi) + _dot(a_hi, w_lo)


def _bf16_terms(x):
    hi = x.astype(BF16)
    rest = x - hi.astype(F32)
    mid = rest.astype(BF16)
    return hi, mid, (rest - mid.astype(F32)).astype(BF16)


def _pack_rows(x):
    half = x.shape[1] // 2
    lo = pltpu.bitcast(x[:, :half].astype(BF16).astype(F32), U32)
    hi = pltpu.bitcast(x[:, half:].astype(BF16).astype(F32), U32)
    return (hi & U32(0xFFFF0000)) | (lo >> U32(16))


def _unpack_lo(w):
    return pltpu.bitcast(w << U32(16), F32)


def _unpack_hi(w):
    return pltpu.bitcast(w & U32(0xFFFF0000), F32)


def _hi_lo(w):
    w_hi = w.astype(BF16)
    return jnp.stack([w_hi, (w - w_hi.astype(F32)).astype(BF16)])


def _log_sigmoid(x):
    return jnp.minimum(x, 0.0) - jnp.log1p(jnp.exp(-jnp.abs(x)))


def _sigmoid(x):
    return 1.0 / (1.0 + jnp.exp(-x))


def _inproj_kernel(x_ref, g_ref, w_ref, wg_ref,
                   aq_ref, ak_ref, av_ref, mqk_ref, mv_ref, mo_ref, su_ref, gates_ref, gatest_ref):
    x = x_ref[...]
    h = x * lax.rsqrt(jnp.mean(x * x, axis=-1, keepdims=True) + EPS) * g_ref[...]
    hb = h.astype(BF16)
    off = 0
    for ref, width in ((aq_ref, A_WIDTH), (ak_ref, A_WIDTH), (av_ref, A_WIDTH), (mqk_ref, 2 * M_WIDTH),
                       (mv_ref, M_WIDTH), (mo_ref, M_WIDTH), (su_ref, S_WIDTH)):
        ref[...] = _dot(hb, w_ref[:, off:off + width]).astype(ref.dtype)
        off += width
    gates = _split_dot(h, wg_ref[0], wg_ref[1])
    gates_ref[...] = gates
    gt = gates.T[:8, :]
    for c in range(TM_PROJ // M_CHUNK):
        gatest_ref[c] = gt[:, c * M_CHUNK:(c + 1) * M_CHUNK]


def _inproj(x, g, w_main, w_gate):
    n_s = SEQ // TM_PROJ
    row = lambda b, s: (b * n_s + s, 0)
    const = lambda b, s: (0, 0)
    out_shapes = (
        jax.ShapeDtypeStruct((TOKENS, A_WIDTH), BF16),
        jax.ShapeDtypeStruct((TOKENS, A_WIDTH), BF16),
        jax.ShapeDtypeStruct((TOKENS, A_WIDTH), BF16),
        jax.ShapeDtypeStruct((TOKENS, 2 * M_WIDTH), BF16),
        jax.ShapeDtypeStruct((TOKENS, M_WIDTH), BF16),
        jax.ShapeDtypeStruct((TOKENS, M_WIDTH), BF16),
        jax.ShapeDtypeStruct((SEQ, BATCH * S_WIDTH), BF16),
        jax.ShapeDtypeStruct((TOKENS, LANES), F32),
        jax.ShapeDtypeStruct((TOKENS // M_CHUNK, 8, M_CHUNK), F32),
    )
    out_specs = (
        pl.BlockSpec((TM_PROJ, A_WIDTH), row),
        pl.BlockSpec((TM_PROJ, A_WIDTH), row),
        pl.BlockSpec((TM_PROJ, A_WIDTH), row),
        pl.BlockSpec((TM_PROJ, 2 * M_WIDTH), row),
        pl.BlockSpec((TM_PROJ, M_WIDTH), row),
        pl.BlockSpec((TM_PROJ, M_WIDTH), row),
        pl.BlockSpec((TM_PROJ, S_WIDTH), lambda b, s: (s, b)),
        pl.BlockSpec((TM_PROJ, LANES), row),
        pl.BlockSpec((TM_PROJ // M_CHUNK, 8, M_CHUNK), lambda b, s: (b * n_s + s, 0, 0)),
    )
    return pl.pallas_call(
        _inproj_kernel,
        grid=(BATCH, n_s),
        in_specs=[
            pl.BlockSpec((TM_PROJ, D_MODEL), row),
            pl.BlockSpec((1, D_MODEL), const),
            pl.BlockSpec((D_MODEL, W_MAIN), const),
            pl.BlockSpec((2, D_MODEL, LANES), lambda b, s: (0, 0, 0)),
        ],
        out_specs=out_specs,
        out_shape=out_shapes,
        compiler_params=pltpu.CompilerParams(
            dimension_semantics=("arbitrary", "arbitrary"), vmem_limit_bytes=VMEM_LIMIT),
        name="inproj",
    )(x, g, w_main, w_gate)


def _mlstm_kernel(mqk_ref, mv_ref, mo_ref, gates_ref, gatest_ref, cw_ref, cb_ref, brow_ref, bcol_ref, og_ref,
                  y_ref, qk_s, cols_s, wrow_s):
    L = M_CHUNK
    dh = M_HEAD_DIM
    r_i = lax.broadcasted_iota(jnp.int32, (L, L), 0)
    c_i = lax.broadcasted_iota(jnp.int32, (L, L), 1)
    causal = c_i <= r_i
    tri_u = (r_i <= c_i).astype(BF16)
    eye = (lax.broadcasted_iota(jnp.int32, (8, LANES), 0)
           == lax.broadcasted_iota(jnp.int32, (8, LANES), 1)).astype(BF16)
    w = cw_ref[...]
    cb = cb_ref[...]
    lane_q = lax.broadcasted_iota(jnp.int32, (1, 2 * M_WIDTH), 1) < M_WIDTH
    kscale = jnp.where(lane_q, 1.0, dh ** -0.5).astype(F32)
    for c in range(M_NCHUNK):
        halo = 16 if c > 0 else 0
        for strip in range(2 * M_WIDTH // LANES):
            cols = slice(strip * LANES, (strip + 1) * LANES)
            xt = mqk_ref[c * L - halo:(c + 1) * L, cols].astype(F32)
            rows = lax.broadcasted_iota(jnp.int32, xt.shape, 0)
            y = cb[:, cols] + w[CONV_K - 1:CONV_K, cols] * xt
            for k in range(1, CONV_K):
                sh = pltpu.roll(xt, k, 0)
                if c == 0:
                    sh = jnp.where(rows >= k, sh, 0.0)
                y = y + w[CONV_K - 1 - k:CONV_K - k, cols] * sh
            y = y[halo:, :]
            y = y * _sigmoid(y) * kscale[:, cols]
            qk_s[c * L:(c + 1) * L, cols] = y.astype(BF16)

    rows_all = M_NCHUNK * 8
    grow = gatest_ref[...].reshape(rows_all, L) + jnp.tile(bcol_ref[...], (M_NCHUNK, 1))
    cs_row = sum(_dot(term, tri_u) for term in _bf16_terms(_log_sigmoid(grow)))
    w_row = grow - pltpu.roll(cs_row, rows_all - M_HEADS, 0)
    lane_t = lax.broadcasted_iota(jnp.int32, (rows_all, L), 1)
    pm = w_row
    shift = 1
    while shift < L:
        pm = jnp.maximum(pm, jnp.where(lane_t >= shift, pltpu.roll(pm, shift, 1), -jnp.inf))
        shift *= 2
    wrow_s[...] = w_row.reshape(M_NCHUNK, 8, L)
    is_i_row = lax.broadcasted_iota(jnp.int32, (rows_all, L), 0) % 8 < M_HEADS
    terms = _bf16_terms(jnp.where(is_i_row, pm, cs_row))
    for c in range(M_NCHUNK):
        cols_s[c * L:(c + 1) * L, :] = sum(_dot_tn(term[c * 8:(c + 1) * 8, :], eye) for term in terms)

    pad_lane = lax.broadcasted_iota(jnp.int32, (L, LANES - dh), 1)
    ones_pad = jnp.where(pad_lane == 0, 1.0, 0.0).astype(BF16)
    og = og_ref[...]
    head_mean = jnp.where(lax.broadcasted_iota(jnp.int32, (M_WIDTH, M_WIDTH), 0) // dh
                          == lax.broadcasted_iota(jnp.int32, (M_WIDTH, M_WIDTH), 1) // dh,
                          1.0 / dh, 0.0).astype(BF16)

    def chunk(c, carry):
        r0 = pl.multiple_of(c * L, L)
        gcol = gates_ref[pl.ds(r0, L), :] + brow_ref[...]
        cs_col = pm_col = cols_s[pl.ds(r0, L), :]
        w_row = wrow_s[c]
        qk = qk_s[pl.ds(r0, L), :]
        vv = mv_ref[pl.ds(r0, L), :]
        new_carry = []
        hs = []
        for h in range(M_HEADS):
            cst, m = carry[h]
            q = qk[:, h * dh:(h + 1) * dh]
            k = qk[:, M_WIDTH + h * dh:M_WIDTH + (h + 1) * dh]
            v_aug = jnp.concatenate([vv[:, h * dh:(h + 1) * dh], ones_pad], axis=1)
            bc = cs_col[:, M_HEADS + h:M_HEADS + h + 1]
            ic = gcol[:, h:h + 1]
            m_row = jnp.maximum(m, pm_col[:, h:h + 1])
            mt = bc + m_row
            inter = jnp.exp(m - m_row)
            sqk = _dot_nt(q, k) * jnp.exp(jnp.where(causal, w_row[h:h + 1, :] - m_row, -jnp.inf))
            nd = inter * _dot(q, cst.astype(BF16)) + _dot(sqk.astype(BF16), v_aug)
            num = nd[:, :dh]
            den = nd[:, dh:dh + 1]
            hs.append(num / jnp.maximum(jnp.abs(den), jnp.exp(-mt)))
            bl = bc[L - 1:L, :]
            gs = bl - bc + ic
            m_new = jnp.maximum(bl + m, jnp.max(gs, axis=0, keepdims=True))
            decay = jnp.exp(bl + m - m_new)
            wts = jnp.exp(gs - m_new)
            kw = (k.astype(F32) * wts).astype(BF16)
            new_carry.append((decay * cst + _dot_tn(kw, v_aug), m_new))
        hcat = jnp.concatenate(hs, axis=1)
        hcat = hcat * lax.rsqrt(_dot((hcat * hcat).astype(BF16), head_mean) + EPS) * og
        o = mo_ref[pl.ds(r0, L), :].astype(F32)
        y_ref[pl.ds(r0, L), :] = (hcat * _sigmoid(o)).astype(y_ref.dtype)
        return tuple(new_carry)

    init = tuple((jnp.zeros((dh, LANES), F32), jnp.zeros((1, 1), F32)) for _ in range(M_HEADS))
    lax.fori_loop(0, M_NCHUNK, chunk, init, unroll=2)


def _mlstm(mqk, mv, mo, gates, gatest, conv_w, conv_b, bias_row, bias_col, out_g):
    per_b = lambda b: (b, 0)
    const = lambda b: (0, 0)
    return pl.pallas_call(
        _mlstm_kernel,
        grid=(BATCH,),
        in_specs=[
            pl.BlockSpec((SEQ, 2 * M_WIDTH), per_b),
            pl.BlockSpec((SEQ, M_WIDTH), per_b),
            pl.BlockSpec((SEQ, M_WIDTH), per_b),
            pl.BlockSpec((SEQ, LANES), per_b),
            pl.BlockSpec((M_NCHUNK, 8, M_CHUNK), lambda b: (b, 0, 0)),
            pl.BlockSpec((CONV_K, 2 * M_WIDTH), const),
            pl.BlockSpec((1, 2 * M_WIDTH), const),
            pl.BlockSpec((1, LANES), const),
            pl.BlockSpec((8, 1), const),
            pl.BlockSpec((1, M_WIDTH), const),
        ],
        out_specs=pl.BlockSpec((SEQ, M_WIDTH), per_b),
        out_shape=jax.ShapeDtypeStruct((TOKENS, M_WIDTH), BF16),
        scratch_shapes=[
            pltpu.VMEM((SEQ, 2 * M_WIDTH), BF16),
            pltpu.VMEM((SEQ, LANES), F32),
            pltpu.VMEM((M_NCHUNK, 8, M_CHUNK), F32),
        ],
        compiler_params=pltpu.CompilerParams(
            dimension_semantics=("arbitrary",), vmem_limit_bytes=VMEM_LIMIT),
        name="mlstm",
    )(mqk, mv, mo, gates, gatest, conv_w, conv_b, bias_row, bias_col, out_g)


def _qk_norm_rope(x, gain, cos, sin):
    lane = lax.broadcasted_iota(jnp.int32, x.shape, 1)
    first = lane < A_HEAD_DIM
    sq = x * x
    tot = jnp.sum(sq, axis=-1, keepdims=True)
    s0 = jnp.sum(jnp.where(first, sq, 0.0), axis=-1, keepdims=True)
    ms = jnp.where(first, s0, tot - s0) * (1.0 / A_HEAD_DIM)
    xn = x * lax.rsqrt(ms + EPS) * gain
    in_low = (lane % A_HEAD_DIM) < ROPE_HALF
    swapped = jnp.where(in_low, pltpu.roll(xn, LANES - ROPE_HALF, 1), pltpu.roll(xn, ROPE_HALF, 1))
    return xn * cos + swapped * sin


def _moba_kernel(aq_ref, ak_ref, av_ref, cos_ref, sin_ref, qg_ref, kg_ref, og_ref, y_ref, kr_s, s_s):
    blk = MOBA_BLOCK
    dh = A_HEAD_DIM
    km_first = lax.broadcasted_iota(jnp.int32, (MOBA_NB, HEAD_PAIR), 1) < dh
    first = lax.broadcasted_iota(jnp.int32, (blk, HEAD_PAIR), 1) < dh
    blk_id = lax.broadcasted_iota(jnp.int32, (MOBA_NB, blk), 0)
    eye = (lax.broadcasted_iota(jnp.int32, (MOBA_NB, LANES), 0)
           == lax.broadcasted_iota(jnp.int32, (MOBA_NB, LANES), 1)).astype(F32)
    q_pos = lax.broadcasted_iota(jnp.int32, (2 * blk, blk), 0) % blk
    k_pos = lax.broadcasted_iota(jnp.int32, (2 * blk, blk), 1)
    causal = k_pos <= q_pos
    og = og_ref[...]

    km_rows = []
    prepped = {}

    def key_prep(j):
        rows = slice(j * blk, (j + 1) * blk)
        kr = _qk_norm_rope(ak_ref[rows, :].astype(F32), kg_ref[...], cos_ref[rows, :], sin_ref[rows, :])
        kr_s[rows, :] = kr.astype(BF16)
        km_rows.append(jnp.mean(kr, axis=0, keepdims=True))
        yield

    def query_prep(i):
        rows = slice(i * blk, (i + 1) * blk)
        qr = _qk_norm_rope(aq_ref[rows, :].astype(F32), qg_ref[...], cos_ref[rows, :], sin_ref[rows, :])
        qs = qr * (dh ** -0.5 * math.log2(math.e))
        q2 = jnp.concatenate([jnp.where(first, qs, 0.0), jnp.where(first, 0.0, qs)], axis=0).astype(BF16)
        yield

        bias = None
        if i > MOBA_TOPK:
            km = jnp.concatenate(km_rows[:i] + [jnp.zeros((MOBA_NB - i, HEAD_PAIR), F32)], axis=0)
            km_heads = (jnp.where(km_first, km, 0.0), jnp.where(km_first, 0.0, km))
            past = blk_id < i
            cols = []
            for kmh in km_heads:
                gate = jnp.where(past, _dot_nt(kmh, qr, precision=HIGHEST), -jnp.inf)
                rank = jnp.zeros((MOBA_NB, blk), F32)
                for m_blk in range(i):
                    gm = gate[m_blk:m_blk + 1, :]
                    ahead = (gm > gate) | ((gm == gate) & (m_blk < blk_id))
                    rank = rank + jnp.where(ahead, 1.0, 0.0)
                sel = jnp.where(past & (rank < float(MOBA_TOPK)), 1.0, 0.0)
                cols.append(_dot_tn(sel, eye))
            bias = jnp.where(jnp.concatenate(cols, axis=0) > 0.5, 0.0, -jnp.inf)
        prepped[i] = (q2, bias)
        yield

    def scores(i, out):
        q2, bias = prepped.pop(i)
        m_run = None
        for j in range(i + 1):
            s = _dot_nt(q2, kr_s[j * blk:(j + 1) * blk, :])
            if j == i:
                s = jnp.where(causal, s, -jnp.inf)
            elif bias is not None:
                s = s + bias[:, j:j + 1]
            s_s[i % 2, :, j * blk:(j + 1) * blk] = s
            part = jnp.maximum(s[:, :LANES], s[:, LANES:])
            m_run = part if m_run is None else jnp.maximum(m_run, part)
            yield
        out.append(jnp.max(m_run, axis=1, keepdims=True))

    def attend(i, m_fin):
        rows = slice(i * blk, (i + 1) * blk)
        l_run = jnp.zeros((2 * blk, LANES), F32)
        acc = jnp.zeros((2 * blk, HEAD_PAIR), F32)
        m_lanes = jnp.broadcast_to(m_fin, (2 * blk, LANES))
        for j in range(i + 1):
            p_lo = jnp.exp2(s_s[i % 2, :, j * blk:j * blk + LANES] - m_lanes)
            p_hi = jnp.exp2(s_s[i % 2, :, j * blk + LANES:(j + 1) * blk] - m_lanes)
            l_run = l_run + (p_lo + p_hi)
            p = jnp.concatenate([p_lo, p_hi], axis=1)
            acc = acc + _dot(p.astype(BF16), av_ref[j * blk:(j + 1) * blk, :])
            yield
        o2 = acc / jnp.sum(l_run, axis=1, keepdims=True)
        o = jnp.where(first, o2[:blk, :], o2[blk:, :])
        sq = o * o
        tot = jnp.sum(sq, axis=-1, keepdims=True)
        s0 = jnp.sum(jnp.where(first, sq, 0.0), axis=-1, keepdims=True)
        ms = jnp.where(first, s0, tot - s0) * (1.0 / dh)
        y_ref[rows, :] = (o * lax.rsqrt(ms + EPS) * og).astype(y_ref.dtype)

    def interleave(*stages):
        stages = list(stages)
        while stages:
            for stage in list(stages):
                try:
                    next(stage)
                except StopIteration:
                    stages.remove(stage)

    m_cur = []
    interleave(key_prep(0), key_prep(1), query_prep(0), query_prep(1))
    interleave(scores(0, m_cur))
    for i in range(MOBA_NB):
        m_next = []
        stages = [attend(i, m_cur[0])]
        if i + 1 < MOBA_NB:
            stages.insert(0, scores(i + 1, m_next))
        if i + 2 < MOBA_NB:
            stages += [key_prep(i + 2), query_prep(i + 2)]
        interleave(*stages)
        m_cur = m_next


def _moba(aq, ak, av, cos_t, sin_t, q_gain, k_gain, out_gain):
    n_pairs = A_HEADS // 2
    const = lambda b, p: (0, 0)
    pair = lambda b, p: (b, p)
    return pl.pallas_call(
        _moba_kernel,
        grid=(BATCH, n_pairs),
        in_specs=[
            pl.BlockSpec((SEQ, HEAD_PAIR), pair),
            pl.BlockSpec((SEQ, HEAD_PAIR), pair),
            pl.BlockSpec((SEQ, HEAD_PAIR), pair),
            pl.BlockSpec((SEQ, HEAD_PAIR), const),
            pl.BlockSpec((SEQ, HEAD_PAIR), const),
            pl.BlockSpec((1, HEAD_PAIR), const),
            pl.BlockSpec((1, HEAD_PAIR), const),
            pl.BlockSpec((None, 1, HEAD_PAIR), lambda b, p: (p, 0, 0)),
        ],
        out_specs=pl.BlockSpec((SEQ, HEAD_PAIR), pair),
        out_shape=jax.ShapeDtypeStruct((TOKENS, A_WIDTH), BF16),
        scratch_shapes=[
            pltpu.VMEM((SEQ, HEAD_PAIR), BF16),
            pltpu.VMEM((2, 2 * MOBA_BLOCK, SEQ), F32),
        ],
        compiler_params=pltpu.CompilerParams(
            dimension_semantics=("arbitrary", "arbitrary"), vmem_limit_bytes=VMEM_LIMIT),
        name="moba",
    )(aq, ak, av, cos_t, sin_t, q_gain, k_gain, out_gain)


def _s5_kernel(u_ref, abar_ref, bre_ref, bim_ref, cre_ref, cim_ref, d_ref, gw_ref, gb_ref, gmat_ref, og_ref,
               y_ref, xr_s, xi_s, st_s, u_s, y_s):
    step = pl.program_id(0)
    rows = S5_STEPS * BATCH
    tile = 256
    n_tiles = rows // tile
    halves = S_WIDTH // LANES

    @pl.when(step == 0)
    def _init():
        st_s[...] = jnp.zeros_like(st_s)

    for b in range(BATCH):
        for hv in range(halves):
            lanes = slice(b * S_WIDTH + hv * LANES, b * S_WIDTH + (hv + 1) * LANES)
            u_s[hv, pl.ds(b, S5_STEPS, stride=BATCH), :] = u_ref[:, lanes].astype(F32)

    def u_rows(r):
        return jnp.concatenate([u_s[hv, r * tile:(r + 1) * tile, :] for hv in range(halves)], axis=1)

    for r in range(n_tiles):
        u = u_rows(r).astype(BF16)
        t0 = r * tile // BATCH
        xr_s[t0:t0 + tile // BATCH] = _dot(u, bre_ref[...]).reshape(tile // BATCH, BATCH, S_FLAT)
        xi_s[t0:t0 + tile // BATCH] = _dot(u, bim_ref[...]).reshape(tile // BATCH, BATCH, S_FLAT)

    ar = jnp.broadcast_to(abar_ref[0:1, :], (BATCH, S_FLAT))
    ai = jnp.broadcast_to(abar_ref[1:2, :], (BATCH, S_FLAT))

    def scan_step(t, carry):
        sr, si = carry
        nr = ar * sr - ai * si + xr_s[t]
        ni = ar * si + ai * sr + xi_s[t]
        xr_s[t] = nr
        xi_s[t] = ni
        return nr, ni

    sr, si = lax.fori_loop(0, S5_STEPS, scan_step, (st_s[0], st_s[1]), unroll=4)
    st_s[0] = sr
    st_s[1] = si

    for r in range(n_tiles):
        t0 = r * tile // BATCH
        xr = xr_s[t0:t0 + tile // BATCH].reshape(tile, S_FLAT).astype(BF16)
        xi = xi_s[t0:t0 + tile // BATCH].reshape(tile, S_FLAT).astype(BF16)
        y = _dot(xr, cre_ref[...]) - _dot(xi, cim_ref[...]) + d_ref[...] * u_rows(r)
        z = 0.5 * y * (1.0 + jnp.tanh(math.sqrt(2.0 / math.pi) * (y + 0.044715 * (y * y * y))))
        y2 = z * _sigmoid(_dot(z.astype(BF16), gw_ref[...]) + gb_ref[...])
        ms = _dot((y2 * y2).astype(BF16), gmat_ref[...])
        yn = y2 * lax.rsqrt(ms + EPS) * og_ref[...]
        for hv in range(halves):
            y_s[hv, r * tile:(r + 1) * tile, :] = yn[:, hv * LANES:(hv + 1) * LANES]

    for b in range(BATCH):
        for hv in range(halves):
            lanes = slice(b * S_WIDTH + hv * LANES, b * S_WIDTH + (hv + 1) * LANES)
            y_ref[:, lanes] = y_s[hv, pl.ds(b, S5_STEPS, stride=BATCH), :].astype(y_ref.dtype)


def _s5(u_sb, abar, bre, bim, cre, cim, d_row, glu_w, glu_b, gmat, out_g):
    rows = S5_STEPS * BATCH
    const = lambda s: (0, 0)
    return pl.pallas_call(
        _s5_kernel,
        grid=(SEQ // S5_STEPS,),
        in_specs=[
            pl.BlockSpec((S5_STEPS, BATCH * S_WIDTH), lambda s: (s, 0)),
            pl.BlockSpec((2, S_FLAT), const),
            pl.BlockSpec((S_WIDTH, S_FLAT), const),
            pl.BlockSpec((S_WIDTH, S_FLAT), const),
            pl.BlockSpec((S_FLAT, S_WIDTH), const),
            pl.BlockSpec((S_FLAT, S_WIDTH), const),
            pl.BlockSpec((1, S_WIDTH), const),
            pl.BlockSpec((S_WIDTH, S_WIDTH), const),
            pl.BlockSpec((1, S_WIDTH), const),
            pl.BlockSpec((S_WIDTH, S_WIDTH), const),
            pl.BlockSpec((1, S_WIDTH), const),
        ],
        out_specs=pl.BlockSpec((S5_STEPS, BATCH * S_WIDTH), lambda s: (s, 0)),
        out_shape=jax.ShapeDtypeStruct((SEQ, BATCH * S_WIDTH), BF16),
        scratch_shapes=[
            pltpu.VMEM((S5_STEPS, BATCH, S_FLAT), F32),
            pltpu.VMEM((S5_STEPS, BATCH, S_FLAT), F32),
            pltpu.VMEM((2, BATCH, S_FLAT), F32),
            pltpu.VMEM((S_WIDTH // LANES, rows, LANES), F32),
            pltpu.VMEM((S_WIDTH // LANES, rows, LANES), F32),
        ],
        compiler_params=pltpu.CompilerParams(
            dimension_semantics=("arbitrary",), vmem_limit_bytes=VMEM_LIMIT),
        name="s5",
    )(u_sb, abar, bre, bim, cre, cim, d_row, glu_w, glu_b, gmat, out_g)


def _outproj_kernel(x_ref, ym_ref, ya_ref, ys_ref, w_ref, g_ref, wr_ref, br_ref,
                    x1_ref, xn_ref, route_ref):
    x1 = x_ref[...]
    x1 = x1 + _dot(ym_ref[...], w_ref[0:M_WIDTH, :])
    x1 = x1 + _dot(ya_ref[...], w_ref[M_WIDTH:M_WIDTH + A_WIDTH, :])
    x1 = x1 + _dot(ys_ref[...], w_ref[M_WIDTH + A_WIDTH:, :])
    x1_ref[...] = x1
    xn = x1 * lax.rsqrt(jnp.mean(x1 * x1, axis=-1, keepdims=True) + EPS) * g_ref[...]
    xn_words = _pack_rows(xn)
    for j in range(ROW_TILES):
        xn_ref[pl.ds(j, TM_PROJ, stride=ROW_TILES), :] = xn_words[:, j * LANES:(j + 1) * LANES]
    logits = _split_dot(xn, wr_ref[0], wr_ref[1]) + br_ref[...]

    lane = lax.broadcasted_iota(jnp.int32, logits.shape, 1).astype(F32)
    is_grp = lane < N_GROUPS
    neg = -jnp.inf
    gl = jnp.where(is_grp, logits, neg)
    gmax = jnp.max(gl, axis=-1, keepdims=True)
    gsum = jnp.sum(jnp.where(is_grp, jnp.exp(logits - gmax), 0.0), axis=-1, keepdims=True)
    g_sel = jnp.min(jnp.where(gl == gmax, lane, float(LANES)), axis=-1, keepdims=True)
    lo = N_GROUPS + EXPERTS_PER_GROUP * g_sel
    el = jnp.where((lane >= lo) & (lane < lo + EXPERTS_PER_GROUP), logits, neg)
    t1 = jnp.max(el, axis=-1, keepdims=True)
    i1 = jnp.min(jnp.where(el == t1, lane, float(LANES)), axis=-1, keepdims=True)
    el2 = jnp.where(lane == i1, neg, el)
    t2 = jnp.max(el2, axis=-1, keepdims=True)
    i2 = jnp.min(jnp.where(el2 == t2, lane, float(LANES)), axis=-1, keepdims=True)
    e21 = jnp.exp(t2 - t1)
    w1 = 1.0 / ((1.0 + e21) * gsum)
    w2 = e21 / ((1.0 + e21) * gsum)
    route = jnp.where(lane == 0.0, i1 - N_GROUPS,
                      jnp.where(lane == 1.0, i2 - N_GROUPS,
                                jnp.where(lane == 2.0, w1, jnp.where(lane == 3.0, w2, 0.0))))
    route_ref[...] = route


def _outproj(x, y_m, y_a, y_s_sb, w_out, g2, w_router, b_router):
    n_s = SEQ // TM_PROJ
    row = lambda b, s: (b * n_s + s, 0)
    const = lambda b, s: (0, 0)
    return pl.pallas_call(
        _outproj_kernel,
        grid=(BATCH, n_s),
        in_specs=[
            pl.BlockSpec((TM_PROJ, D_MODEL), row),
            pl.BlockSpec((TM_PROJ, M_WIDTH), row),
            pl.BlockSpec((TM_PROJ, A_WIDTH), row),
            pl.BlockSpec((TM_PROJ, S_WIDTH), lambda b, s: (s, b)),
            pl.BlockSpec((D_MIX, D_MODEL), const),
            pl.BlockSpec((1, D_MODEL), const),
            pl.BlockSpec((2, D_MODEL, LANES), lambda b, s: (0, 0, 0)),
            pl.BlockSpec((1, LANES), const),
        ],
        out_specs=(
            pl.BlockSpec((TM_PROJ, D_MODEL), row),
            pl.BlockSpec((TM_PROJ * ROW_TILES, LANES), row),
            pl.BlockSpec((TM_PROJ, LANES), row),
        ),
        out_shape=(
            jax.ShapeDtypeStruct((TOKENS, D_MODEL), F32),
            jax.ShapeDtypeStruct((TOKENS * ROW_TILES, LANES), U32),
            jax.ShapeDtypeStruct((TOKENS, LANES), F32),
        ),
        compiler_params=pltpu.CompilerParams(
            dimension_semantics=("arbitrary", "arbitrary"), vmem_limit_bytes=VMEM_LIMIT),
        name="outproj",
    )(x, y_m, y_a, y_s_sb, w_out, g2, w_router, b_router)


def _ffn_kernel(be_ref, xs_ref, wg_ref, wu_ref, wd_ref, ys_ref, wg_s, wu_s, wd_s):
    i = pl.program_id(0)
    prev = be_ref[jnp.maximum(i - 1, 0)]

    @pl.when((i == 0) | (be_ref[i] != prev))
    def _load_expert():
        wg_s[...] = wg_ref[...].astype(BF16)
        wu_s[...] = wu_ref[...].astype(BF16)
        wd_s[...] = wd_ref[...].astype(BF16)

    words = jnp.concatenate([xs_ref[pl.ds(j, FFN_BLOCK, stride=ROW_TILES), :] for j in range(ROW_TILES)], axis=1)
    xb = jnp.concatenate([_unpack_lo(words), _unpack_hi(words)], axis=1).astype(BF16)
    g = _dot(xb, wg_s[...])
    u = _dot(xb, wu_s[...])
    hmid = (g * _sigmoid(g) * u).astype(BF16)
    y_words = _pack_rows(_dot(hmid, wd_s[...]))
    for j in range(ROW_TILES):
        ys_ref[pl.ds(j, FFN_BLOCK, stride=ROW_TILES), :] = y_words[:, j * LANES:(j + 1) * LANES]


def _expert_ffn(layer, blk_exp, xs, w_gate, w_up, w_down):
    nblk = blk_exp.shape[0]
    rows = pl.BlockSpec((FFN_BLOCK * ROW_TILES, LANES), lambda i, be: (i, 0))
    grid_spec = pltpu.PrefetchScalarGridSpec(
        num_scalar_prefetch=1,
        grid=(nblk,),
        in_specs=[
            rows,
            pl.BlockSpec((None, None, D_MODEL, D_EXPERT), lambda i, be: (layer, be[i], 0, 0)),
            pl.BlockSpec((None, None, D_MODEL, D_EXPERT), lambda i, be: (layer, be[i], 0, 0)),
            pl.BlockSpec((None, None, D_EXPERT, D_MODEL), lambda i, be: (layer, be[i], 0, 0)),
        ],
        out_specs=rows,
        scratch_shapes=[
            pltpu.VMEM((D_MODEL, D_EXPERT), BF16),
            pltpu.VMEM((D_MODEL, D_EXPERT), BF16),
            pltpu.VMEM((D_EXPERT, D_MODEL), BF16),
        ],
    )
    return pl.pallas_call(
        _ffn_kernel,
        grid_spec=grid_spec,
        out_shape=jax.ShapeDtypeStruct((nblk * FFN_BLOCK * ROW_TILES, LANES), U32),
        compiler_params=pltpu.CompilerParams(
            dimension_semantics=("arbitrary",), vmem_limit_bytes=VMEM_LIMIT),
        name="expert_ffn",
    )(blk_exp, xs, w_gate, w_up, w_down)


def _plan_kernel(e_ref, ps_ref, dest_ref, run_s, tri_s):
    i = pl.program_id(0)

    @pl.when(i == 0)
    def _init():
        run_s[...] = ps_ref[...]
        s_i = lax.broadcasted_iota(jnp.int32, (PLAN_TILE, PLAN_TILE), 0)
        t_i = lax.broadcasted_iota(jnp.int32, (PLAN_TILE, PLAN_TILE), 1)
        tri_s[...] = (s_i < t_i).astype(BF16)

    expert = lax.broadcasted_iota(jnp.int32, (N_EXPERTS, PLAN_TILE), 0)
    onehot = jnp.where(e_ref[...] == expert, 1.0, 0.0)
    before = _dot(onehot.astype(BF16), tri_s[...])
    dest = jnp.sum(onehot * (before + run_s[:, 0:1]), axis=0, keepdims=True)
    dest_ref[...] = dest.astype(jnp.int32)
    run_s[...] = run_s[...] + jnp.sum(onehot, axis=1, keepdims=True)


def _plan(e_rows, pstart_col):
    n_tiles = e_rows.shape[0]
    return pl.pallas_call(
        _plan_kernel,
        grid=(n_tiles,),
        in_specs=[
            pl.BlockSpec((None, 1, PLAN_TILE), lambda i: (i, 0, 0)),
            pl.BlockSpec((N_EXPERTS, LANES), lambda i: (0, 0)),
        ],
        out_specs=pl.BlockSpec((None, 1, PLAN_TILE), lambda i: (i, 0, 0)),
        out_shape=jax.ShapeDtypeStruct((n_tiles, 1, PLAN_TILE), jnp.int32),
        scratch_shapes=[pltpu.VMEM((N_EXPERTS, LANES), F32), pltpu.VMEM((PLAN_TILE, PLAN_TILE), BF16)],
        compiler_params=pltpu.CompilerParams(
            dimension_semantics=("arbitrary",), vmem_limit_bytes=VMEM_LIMIT),
        name="moe_plan",
    )(e_rows, pstart_col)


def _dispatch_kernel(fill_ref, d0_ref, d1_ref, xn_ref, xs_hbm, zero_s, sem, fill_sem):
    @pl.when(pl.program_id(0) == 0)
    def _zero_padding():
        zero_s[...] = jnp.zeros_like(zero_s)

        def fill(c):
            start = pl.multiple_of(fill_ref[0, c] * ROW_TILES, ROW_TILES)
            return pltpu.make_async_copy(zero_s, xs_hbm.at[pl.ds(start, FFN_BLOCK * ROW_TILES), :], fill_sem)

        for c in range(N_FILLS):
            pl.when(fill_ref[0, c] >= 0)(lambda c=c: fill(c).start())
        for c in range(N_FILLS):
            pl.when(fill_ref[0, c] >= 0)(lambda c=c: fill(c).wait())

    def row_copy(dref, t):
        src = xn_ref.at[pl.ds(pl.multiple_of(t * ROW_TILES, ROW_TILES), ROW_TILES), :]
        dst = xs_hbm.at[pl.ds(pl.multiple_of(dref[0, t] * ROW_TILES, ROW_TILES), ROW_TILES), :]
        return pltpu.make_async_copy(src, dst, sem)

    def start(t, carry):
        row_copy(d0_ref, t).start()
        row_copy(d1_ref, t).start(priority=1)
        return carry

    lax.fori_loop(0, DISPATCH_TOKENS, start, 0, unroll=8)
    for _ in range(EXPERT_TOPK):
        pltpu.make_async_copy(xn_ref, xs_hbm.at[pl.ds(0, DISPATCH_TOKENS * ROW_TILES), :], sem).wait()


def _dispatch(fill_rows, dest, xn, n_rows):
    n_tiles = TOKENS // DISPATCH_TOKENS
    dest4 = dest.reshape(EXPERT_TOPK, n_tiles, 1, DISPATCH_TOKENS)
    smem_tile = lambda k: pl.BlockSpec((None, None, 1, DISPATCH_TOKENS), lambda i: (k, i, 0, 0),
                                       memory_space=pltpu.SMEM)
    return pl.pallas_call(
        _dispatch_kernel,
        grid=(n_tiles,),
        in_specs=[
            pl.BlockSpec(memory_space=pltpu.SMEM),
            smem_tile(0),
            smem_tile(1),
            pl.BlockSpec((DISPATCH_TOKENS * ROW_TILES, LANES), lambda i: (i, 0)),
        ],
        out_specs=pl.BlockSpec(memory_space=pl.ANY),
        out_shape=jax.ShapeDtypeStruct((n_rows * ROW_TILES, LANES), U32),
        scratch_shapes=[
            pltpu.VMEM((FFN_BLOCK * ROW_TILES, LANES), U32),
            pltpu.SemaphoreType.DMA(()),
            pltpu.SemaphoreType.DMA(()),
        ],
        compiler_params=pltpu.CompilerParams(
            dimension_semantics=("arbitrary",), vmem_limit_bytes=VMEM_LIMIT),
        name="moe_dispatch",
    )(fill_rows, dest4, dest4, xn)


def _combine_kernel(d0_ref, d1_ref, d0_next_ref, d1_next_ref, x1_ref, route_ref, ys_hbm, out_ref, buf, sem):
    i = pl.program_id(0)
    n = pl.num_programs(0)

    def start_tile(drefs, slot):
        def body(t, carry):
            for k in range(EXPERT_TOPK):
                src = ys_hbm.at[pl.ds(pl.multiple_of(drefs[k][0, t] * ROW_TILES, ROW_TILES), ROW_TILES), :]
                dst = buf.at[slot, k, pl.ds(pl.multiple_of(t * ROW_TILES, ROW_TILES), ROW_TILES), :]
                pltpu.make_async_copy(src, dst, sem.at[slot]).start(priority=k)
            return carry
        lax.fori_loop(0, COMBINE_TOKENS, body, 0, unroll=8)

    slot = i % 2

    @pl.when(i == 0)
    def _first():
        start_tile((d0_ref, d1_ref), 0)

    @pl.when(i + 1 < n)
    def _prefetch():
        start_tile((d0_next_ref, d1_next_ref), 1 - slot)

    for k in range(EXPERT_TOPK):
        pltpu.make_async_copy(ys_hbm.at[pl.ds(0, COMBINE_TOKENS * ROW_TILES), :], buf.at[slot, k],
                              sem.at[slot]).wait()
    w = route_ref[...]
    w1 = w[:, 2:3]
    w2 = w[:, 3:4]
    for j in range(ROW_TILES):
        chunk = pl.ds(j, COMBINE_TOKENS, stride=ROW_TILES)
        words1 = buf[slot, 0, chunk, :]
        words2 = buf[slot, 1, chunk, :]
        for unpack, base in ((_unpack_lo, 0), (_unpack_hi, D_MODEL // 2)):
            cols = slice(base + j * LANES, base + (j + 1) * LANES)
            out_ref[:, cols] = x1_ref[:, cols] + w1 * unpack(words1) + w2 * unpack(words2)


def _combine(dest, x1, route, ys3):
    n_tiles = TOKENS // COMBINE_TOKENS
    dest4 = dest.reshape(EXPERT_TOPK, n_tiles, 1, COMBINE_TOKENS)
    smem_tile = lambda k, nxt: pl.BlockSpec(
        (None, None, 1, COMBINE_TOKENS), lambda i: (k, jnp.minimum(i + nxt, n_tiles - 1), 0, 0),
        memory_space=pltpu.SMEM)
    return pl.pallas_call(
        _combine_kernel,
        grid=(n_tiles,),
        in_specs=[
            smem_tile(0, 0), smem_tile(1, 0), smem_tile(0, 1), smem_tile(1, 1),
            pl.BlockSpec((COMBINE_TOKENS, D_MODEL), lambda i: (i, 0)),
            pl.BlockSpec((COMBINE_TOKENS, LANES), lambda i: (i, 0)),
            pl.BlockSpec(memory_space=pl.ANY),
        ],
        out_specs=pl.BlockSpec((COMBINE_TOKENS, D_MODEL), lambda i: (i, 0)),
        out_shape=jax.ShapeDtypeStruct((TOKENS, D_MODEL), F32),
        scratch_shapes=[
            pltpu.VMEM((2, EXPERT_TOPK, COMBINE_TOKENS * ROW_TILES, LANES), U32),
            pltpu.SemaphoreType.DMA((2,)),
        ],
        compiler_params=pltpu.CompilerParams(
            dimension_semantics=("arbitrary",), vmem_limit_bytes=VMEM_LIMIT),
        name="moe_combine",
    )(dest4, dest4, dest4, dest4, x1, route, ys3)


def _rope_tables():
    inv = ROPE_THETA ** (-np.arange(ROPE_HALF, dtype=np.float64) * 2.0 / ROPE_DIM)
    ang = np.arange(SEQ, dtype=np.float64)[:, None] * inv[None, :]
    cos = np.ones((SEQ, A_HEAD_DIM))
    sin = np.zeros((SEQ, A_HEAD_DIM))
    cos[:, :ROPE_HALF] = np.cos(ang)
    cos[:, ROPE_HALF:ROPE_DIM] = np.cos(ang)
    sin[:, :ROPE_HALF] = -np.sin(ang)
    sin[:, ROPE_HALF:ROPE_DIM] = np.sin(ang)
    return (jnp.asarray(np.tile(cos, (1, 2)), F32), jnp.asarray(np.tile(sin, (1, 2)), F32))


def _split_w_in(w_in):
    sizes = [M_WIDTH, M_WIDTH, M_WIDTH, M_WIDTH, M_HEADS, M_HEADS, A_WIDTH, A_WIDTH, A_WIDTH, S_WIDTH]
    offs = np.cumsum([0] + sizes)
    mq, mk, mv, mo, mi, mf, aq, ak, av, su = (w_in[:, offs[n]:offs[n + 1]] for n in range(len(sizes)))
    w_main = jnp.concatenate([aq, ak, av, mq, mk, mv, mo, su], axis=1).astype(BF16)
    gate = jnp.concatenate([mi, mf], axis=1)
    return w_main, _hi_lo(jnp.pad(gate, ((0, 0), (0, LANES - 2 * M_HEADS))))


def _s5_params(a_re, a_im, b_re, b_im, c_re, c_im, log_dt):
    dt = jnp.exp(log_dt)[:, None]
    mag = jnp.exp(a_re * dt)
    abar_re, abar_im = mag * jnp.cos(a_im * dt), mag * jnp.sin(a_im * dt)
    zr, zi = abar_re - 1.0, abar_im
    den = a_re * a_re + a_im * a_im
    fr, fi = (zr * a_re + zi * a_im) / den, (zi * a_re - zr * a_im) / den
    bbar_re = fr[..., None] * b_re - fi[..., None] * b_im
    bbar_im = fr[..., None] * b_im + fi[..., None] * b_re
    eye = jnp.eye(S_GROUPS, dtype=F32)
    dense_b = lambda t: jnp.einsum('gph,gk->ghkp', t, eye).reshape(S_WIDTH, S_FLAT).astype(BF16)
    dense_c = lambda t: jnp.einsum('ghp,gk->gpkh', t, eye).reshape(S_FLAT, S_WIDTH).astype(BF16)
    abar = jnp.stack([abar_re.reshape(S_FLAT), abar_im.reshape(S_FLAT)])
    return abar, dense_b(bbar_re), dense_b(bbar_im), dense_c(c_re), dense_c(c_im)


def _moe(layer, x1, xn, route, w_gate, w_up, w_down):
    n_assign = TOKENS * EXPERT_TOPK
    e_flat = route[:, :EXPERT_TOPK].astype(jnp.int32).T.reshape(n_assign)
    experts = jnp.arange(N_EXPERTS, dtype=jnp.int32)
    counts = jnp.sum((e_flat[:, None] == experts[None, :]).astype(jnp.int32), axis=0)
    padded = (counts + FFN_BLOCK - 1) // FFN_BLOCK * FFN_BLOCK
    pends = jnp.cumsum(padded)
    pstarts = pends - padded
    n_rows = (n_assign + N_EXPERTS * (FFN_BLOCK - 1) + FFN_BLOCK - 1) // FFN_BLOCK * FFN_BLOCK
    nblk = n_rows // FFN_BLOCK
    blk_start = jnp.arange(nblk, dtype=jnp.int32) * FFN_BLOCK
    blk_exp = jnp.minimum(jnp.sum((pends[None, :] <= blk_start[:, None]).astype(jnp.int32), axis=1),
                          N_EXPERTS - 1).astype(jnp.int32)
    pstart_col = jnp.broadcast_to(pstarts.astype(F32)[:, None], (N_EXPERTS, LANES))
    dest = _plan(e_flat.reshape(n_assign // PLAN_TILE, 1, PLAN_TILE), pstart_col)
    tail = pends[-1] + jnp.arange(N_EXPERTS, dtype=jnp.int32) * FFN_BLOCK
    fill_rows = jnp.concatenate([jnp.where(padded > 0, pends - FFN_BLOCK, -1), jnp.where(tail < n_rows, tail, -1)])
    fill_rows = fill_rows.astype(jnp.int32).reshape(1, N_FILLS)
    xs = _dispatch(fill_rows, dest, xn, n_rows)
    ys = _expert_ffn(layer, blk_exp, xs, w_gate, w_up, w_down)
    return _combine(dest, x1, route, ys)


def kernel(x, norm1_g, w_in, m_bias_i, m_bias_f, m_conv_w, m_conv_b, m_out_g, a_q_g, a_k_g, a_out_g, s_a_re, s_a_im, s_b_re, s_b_im, s_c_re, s_c_im, s_d, s_log_dt, s_glu_w, s_glu_b, s_out_g, w_out, norm2_g, r_group_w, r_group_b, r_expert_w, r_expert_b, e_w_gate, e_w_up, e_w_down):
    cos_t, sin_t = _rope_tables()
    gmat = jnp.asarray(np.kron(np.eye(S_GROUPS), np.full((S_GROUP_DIM, S_GROUP_DIM), 1.0 / S_GROUP_DIM)), BF16)
    xf = x.reshape(TOKENS, D_MODEL)
    for l in range(DEPTH):
        w_main, w_gate = _split_w_in(w_in[l])
        aq, ak, av, mqk, mv, mo, su, gates, gatest = _inproj(xf, norm1_g[l].reshape(1, D_MODEL), w_main, w_gate)

        bias = jnp.concatenate([m_bias_i[l], m_bias_f[l]])
        bias_row = jnp.pad(bias, (0, LANES - 2 * M_HEADS)).reshape(1, LANES)
        y_m = _mlstm(mqk, mv, mo, gates, gatest, m_conv_w[l], m_conv_b[l].reshape(1, 2 * M_WIDTH),
                     bias_row, bias.reshape(2 * M_HEADS, 1), m_out_g[l].reshape(1, M_WIDTH))

        y_a = _moba(aq, ak, av, cos_t, sin_t,
                    jnp.tile(a_q_g[l], 2).reshape(1, HEAD_PAIR), jnp.tile(a_k_g[l], 2).reshape(1, HEAD_PAIR),
                    a_out_g[l].reshape(A_HEADS // 2, 1, HEAD_PAIR))

        abar, bre, bim, cre, cim = _s5_params(s_a_re[l], s_a_im[l], s_b_re[l], s_b_im[l],
                                              s_c_re[l], s_c_im[l], s_log_dt[l])
        y_s = _s5(su, abar, bre, bim, cre, cim,
                  s_d[l].reshape(1, S_WIDTH), s_glu_w[l].astype(BF16), s_glu_b[l].reshape(1, S_WIDTH),
                  gmat, s_out_g[l].reshape(1, S_WIDTH))

        w_router = jnp.pad(jnp.concatenate([r_group_w[l], r_expert_w[l]], axis=1),
                           ((0, 0), (0, LANES - N_GROUPS - N_EXPERTS)))
        b_router = jnp.pad(jnp.concatenate([r_group_b[l], r_expert_b[l]]),
                           (0, LANES - N_GROUPS - N_EXPERTS)).reshape(1, LANES)
        x1, xn, route = _outproj(xf, y_m, y_a, y_s, w_out[l].astype(BF16),
                                 norm2_g[l].reshape(1, D_MODEL), _hi_lo(w_router), b_router)
        xf = _moe(l, x1, xn, route, e_w_gate, e_w_up, e_w_down)
    return xf.reshape(BATCH, SEQ, D_MODEL)
```

```python
import functools
import math

import numpy as np
import jax
import jax.numpy as jnp
from jax import lax
from jax.experimental import pallas as pl
from jax.experimental.pallas import tpu as pltpu

F32 = jnp.float32
BF16 = jnp.bfloat16

D_MODEL = 1024
BATCH = 8
SEQ = 2048
DEPTH = 2
TOKENS = BATCH * SEQ

M_HEADS = 4
M_HEAD_DIM = 64
M_WIDTH = M_HEADS * M_HEAD_DIM
A_HEADS = 8
A_HEAD_DIM = 64
A_WIDTH = A_HEADS * A_HEAD_DIM
S_GROUPS = 16
S_GROUP_DIM = 16
S_WIDTH = S_GROUPS * S_GROUP_DIM
S_STATE = 64
S_FLAT = S_GROUPS * S_STATE
D_MIX = M_WIDTH + A_WIDTH + S_WIDTH

CONV_K = 4
MOBA_BLOCK = 256
MOBA_NB = SEQ // MOBA_BLOCK
MOBA_TOPK = 3
ROPE_THETA = 500000.0
ROPE_DIM = A_HEAD_DIM // 4
ROPE_HALF = ROPE_DIM // 2

N_GROUPS = 4
EXPERTS_PER_GROUP = 8
N_EXPERTS = N_GROUPS * EXPERTS_PER_GROUP
EXPERT_TOPK = 2
D_EXPERT = 512
DISPATCH_BLOCK = 128
EPS = 1e-6

LANES = 128
ROW_TILES = D_MODEL // LANES
HEAD_PAIR = 2 * A_HEAD_DIM

W_MAIN = 3 * A_WIDTH + 4 * M_WIDTH + S_WIDTH
TM_PROJ = 512
M_CHUNK = 256
M_NCHUNK = SEQ // M_CHUNK
S5_STEPS = 256
FFN_BLOCK = 256
N_FILLS = 2 * N_EXPERTS
PLAN_TILE = 1024
DISPATCH_TOKENS = 512
COMBINE_TOKENS = 512
VMEM_LIMIT = 56 * 1024 * 1024

HIGHEST = lax.Precision.HIGHEST


def _dot(a, b, precision=None):
    return jnp.dot(a, b, preferred_element_type=F32, precision=precision)


def _dot_nt(a, b, precision=None):
    return lax.dot_general(a, b, (((1,), (1,)), ((), ())), preferred_element_type=F32, precision=precision)


def _dot_tn(a, b, precision=None):
    return lax.dot_general(a, b, (((0,), (0,)), ((), ())), preferred_element_type=F32, precision=precision)


def _split_dot(a, w_cat):
    n = w_cat.shape[1] // 2
    a_hi = a.astype(BF16)
    a_lo = (a - a_hi.astype(F32)).astype(BF16)
    both = _dot(a_hi, w_cat)
    return both[:, :n] + both[:, n:] + _dot(a_lo, w_cat[:, :n])


def _bf16_terms(x):
    hi = x.astype(BF16)
    rest = x - hi.astype(F32)
    mid = rest.astype(BF16)
    return hi, mid, (rest - mid.astype(F32)).astype(BF16)


def _hi_lo(w):
    w_hi = w.astype(BF16)
    return jnp.concatenate([w_hi, (w - w_hi.astype(F32)).astype(BF16)], axis=1)


def _log_sigmoid(x):
    return jnp.minimum(x, 0.0) - jnp.log1p(jnp.exp(-jnp.abs(x)))


def _sigmoid(x):
    return 1.0 / (1.0 + jnp.exp(-x))


def _inproj_kernel(x_ref, g_ref, w_ref, wg_ref,
                   aq_ref, ak_ref, av_ref, mqk_ref, mv_ref, mo_ref, su_ref, gates_ref, gatest_ref):
    x = x_ref[...]
    h = x * lax.rsqrt(jnp.mean(x * x, axis=-1, keepdims=True) + EPS) * g_ref[...]
    hb = h.astype(BF16)
    off = 0
    for ref, width in ((aq_ref, A_WIDTH), (ak_ref, A_WIDTH), (av_ref, A_WIDTH), (mqk_ref, 2 * M_WIDTH),
                       (mv_ref, M_WIDTH), (mo_ref, M_WIDTH), (su_ref, S_WIDTH)):
        ref[...] = _dot(hb, w_ref[:, off:off + width]).astype(ref.dtype)
        off += width
    gates = _split_dot(h, wg_ref[...])
    gates_ref[...] = gates
    gt = gates.T[:8, :]
    for c in range(TM_PROJ // M_CHUNK):
        gatest_ref[c] = gt[:, c * M_CHUNK:(c + 1) * M_CHUNK]


def _inproj(x, g, w_main, w_gate):
    n_s = SEQ // TM_PROJ
    row = lambda b, s: (b * n_s + s, 0)
    const = lambda b, s: (0, 0)
    out_shapes = (
        jax.ShapeDtypeStruct((TOKENS, A_WIDTH), BF16),
        jax.ShapeDtypeStruct((TOKENS, A_WIDTH), BF16),
        jax.ShapeDtypeStruct((TOKENS, A_WIDTH), BF16),
        jax.ShapeDtypeStruct((TOKENS, 2 * M_WIDTH), BF16),
        jax.ShapeDtypeStruct((TOKENS, M_WIDTH), BF16),
        jax.ShapeDtypeStruct((TOKENS, M_WIDTH), BF16),
        jax.ShapeDtypeStruct((SEQ, BATCH * S_WIDTH), BF16),
        jax.ShapeDtypeStruct((TOKENS, LANES), F32),
        jax.ShapeDtypeStruct((TOKENS // M_CHUNK, 8, M_CHUNK), F32),
    )
    out_specs = (
        pl.BlockSpec((TM_PROJ, A_WIDTH), row),
        pl.BlockSpec((TM_PROJ, A_WIDTH), row),
        pl.BlockSpec((TM_PROJ, A_WIDTH), row),
        pl.BlockSpec((TM_PROJ, 2 * M_WIDTH), row),
        pl.BlockSpec((TM_PROJ, M_WIDTH), row),
        pl.BlockSpec((TM_PROJ, M_WIDTH), row),
        pl.BlockSpec((TM_PROJ, S_WIDTH), lambda b, s: (s, b)),
        pl.BlockSpec((TM_PROJ, LANES), row),
        pl.BlockSpec((TM_PROJ // M_CHUNK, 8, M_CHUNK), lambda b, s: (b * n_s + s, 0, 0)),
    )
    return pl.pallas_call(
        _inproj_kernel,
        grid=(BATCH, n_s),
        in_specs=[
            pl.BlockSpec((TM_PROJ, D_MODEL), row),
            pl.BlockSpec((1, D_MODEL), const),
            pl.BlockSpec((D_MODEL, W_MAIN), const),
            pl.BlockSpec((D_MODEL, 2 * LANES), const),
        ],
        out_specs=out_specs,
        out_shape=out_shapes,
        compiler_params=pltpu.CompilerParams(
            dimension_semantics=("arbitrary", "arbitrary"), vmem_limit_bytes=VMEM_LIMIT),
        name="inproj",
    )(x, g, w_main, w_gate)


def _mlstm_kernel(mqk_ref, mv_ref, mo_ref, gates_ref, gatest_ref, cw_ref, cb_ref, brow_ref, bcol_ref, og_ref,
                  y_ref, qk_s, cols_s, wrow_s):
    L = M_CHUNK
    dh = M_HEAD_DIM
    r_i = lax.broadcasted_iota(jnp.int32, (L, L), 0)
    c_i = lax.broadcasted_iota(jnp.int32, (L, L), 1)
    causal = c_i <= r_i
    tri_u = (r_i <= c_i).astype(BF16)
    eye = (lax.broadcasted_iota(jnp.int32, (8, LANES), 0)
           == lax.broadcasted_iota(jnp.int32, (8, LANES), 1)).astype(BF16)
    w = cw_ref[...]
    cb = cb_ref[...]
    lane_q = lax.broadcasted_iota(jnp.int32, (1, 2 * M_WIDTH), 1) < M_WIDTH
    kscale = jnp.where(lane_q, 1.0, dh ** -0.5).astype(F32)
    for c in range(M_NCHUNK):
        halo = 16 if c > 0 else 0
        for strip in range(2 * M_WIDTH // LANES):
            cols = slice(strip * LANES, (strip + 1) * LANES)
            xt = mqk_ref[c * L - halo:(c + 1) * L, cols].astype(F32)
            rows = lax.broadcasted_iota(jnp.int32, xt.shape, 0)
            y = cb[:, cols] + w[CONV_K - 1:CONV_K, cols] * xt
            for k in range(1, CONV_K):
                sh = pltpu.roll(xt, k, 0)
                if c == 0:
                    sh = jnp.where(rows >= k, sh, 0.0)
                y = y + w[CONV_K - 1 - k:CONV_K - k, cols] * sh
            y = y[halo:, :]
            y = y * _sigmoid(y) * kscale[:, cols]
            qk_s[c * L:(c + 1) * L, cols] = y.astype(BF16)

    rows_all = M_NCHUNK * 8
    grow = gatest_ref[...].reshape(rows_all, L) + jnp.tile(bcol_ref[...], (M_NCHUNK, 1))
    cs_row = sum(_dot(term, tri_u) for term in _bf16_terms(_log_sigmoid(grow)))
    w_row = grow - pltpu.roll(cs_row, rows_all - M_HEADS, 0)
    lane_t = lax.broadcasted_iota(jnp.int32, (rows_all, L), 1)
    pm = w_row
    shift = 1
    while shift < L:
        pm = jnp.maximum(pm, jnp.where(lane_t >= shift, pltpu.roll(pm, shift, 1), -jnp.inf))
        shift *= 2
    wrow_s[...] = w_row.reshape(M_NCHUNK, 8, L)
    is_i_row = lax.broadcasted_iota(jnp.int32, (rows_all, L), 0) % 8 < M_HEADS
    terms = _bf16_terms(jnp.where(is_i_row, pm, cs_row))
    for c in range(M_NCHUNK):
        cols_s[c * L:(c + 1) * L, :] = sum(_dot_tn(term[c * 8:(c + 1) * 8, :], eye) for term in terms)

    pad_lane = lax.broadcasted_iota(jnp.int32, (L, LANES - dh), 1)
    ones_pad = jnp.where(pad_lane == 0, 1.0, 0.0).astype(BF16)
    og = og_ref[...]
    head_mean = jnp.where(lax.broadcasted_iota(jnp.int32, (M_WIDTH, M_WIDTH), 0) // dh
                          == lax.broadcasted_iota(jnp.int32, (M_WIDTH, M_WIDTH), 1) // dh,
                          1.0 / dh, 0.0).astype(BF16)

    def chunk(c, carry):
        r0 = pl.multiple_of(c * L, L)
        gcol = gates_ref[pl.ds(r0, L), :] + brow_ref[...]
        cs_col = pm_col = cols_s[pl.ds(r0, L), :]
        w_row = wrow_s[c]
        qk = qk_s[pl.ds(r0, L), :]
        vv = mv_ref[pl.ds(r0, L), :]
        new_carry = []
        hs = []
        for h in range(M_HEADS):
            cst, m = carry[h]
            q = qk[:, h * dh:(h + 1) * dh]
            k = qk[:, M_WIDTH + h * dh:M_WIDTH + (h + 1) * dh]
            v_aug = jnp.concatenate([vv[:, h * dh:(h + 1) * dh], ones_pad], axis=1)
            bc = cs_col[:, M_HEADS + h:M_HEADS + h + 1]
            ic = gcol[:, h:h + 1]
            m_row = jnp.maximum(m, pm_col[:, h:h + 1])
            mt = bc + m_row
            inter = jnp.exp(m - m_row)
            sqk = _dot_nt(q, k) * jnp.exp(jnp.where(causal, w_row[h:h + 1, :] - m_row, -jnp.inf))
            nd = inter * _dot(q, cst.astype(BF16)) + _dot(sqk.astype(BF16), v_aug)
            num = nd[:, :dh]
            den = nd[:, dh:dh + 1]
            hs.append(num / jnp.maximum(jnp.abs(den), jnp.exp(-mt)))
            bl = bc[L - 1:L, :]
            gs = bl - bc + ic
            m_new = jnp.maximum(bl + m, jnp.max(gs, axis=0, keepdims=True))
            decay = jnp.exp(bl + m - m_new)
            wts = jnp.exp(gs - m_new)
            kw = (k.astype(F32) * wts).astype(BF16)
            new_carry.append((decay * cst + _dot_tn(kw, v_aug), m_new))
        hcat = jnp.concatenate(hs, axis=1)
        hcat = hcat * lax.rsqrt(_dot((hcat * hcat).astype(BF16), head_mean) + EPS) * og
        o = mo_ref[pl.ds(r0, L), :].astype(F32)
        y_ref[pl.ds(r0, L), :] = (hcat * _sigmoid(o)).astype(y_ref.dtype)
        return tuple(new_carry)

    init = tuple((jnp.zeros((dh, LANES), F32), jnp.zeros((1, 1), F32)) for _ in range(M_HEADS))
    lax.fori_loop(0, M_NCHUNK, chunk, init, unroll=2)


def _mlstm(mqk, mv, mo, gates, gatest, conv_w, conv_b, bias_row, bias_col, out_g):
    per_b = lambda b: (b, 0)
    const = lambda b: (0, 0)
    return pl.pallas_call(
        _mlstm_kernel,
        grid=(BATCH,),
        in_specs=[
            pl.BlockSpec((SEQ, 2 * M_WIDTH), per_b),
            pl.BlockSpec((SEQ, M_WIDTH), per_b),
            pl.BlockSpec((SEQ, M_WIDTH), per_b),
            pl.BlockSpec((SEQ, LANES), per_b),
            pl.BlockSpec((M_NCHUNK, 8, M_CHUNK), lambda b: (b, 0, 0)),
            pl.BlockSpec((CONV_K, 2 * M_WIDTH), const),
            pl.BlockSpec((1, 2 * M_WIDTH), const),
            pl.BlockSpec((1, LANES), const),
            pl.BlockSpec((8, 1), const),
            pl.BlockSpec((1, M_WIDTH), const),
        ],
        out_specs=pl.BlockSpec((SEQ, M_WIDTH), per_b),
        out_shape=jax.ShapeDtypeStruct((TOKENS, M_WIDTH), BF16),
        scratch_shapes=[
            pltpu.VMEM((SEQ, 2 * M_WIDTH), BF16),
            pltpu.VMEM((SEQ, LANES), F32),
            pltpu.VMEM((M_NCHUNK, 8, M_CHUNK), F32),
        ],
        compiler_params=pltpu.CompilerParams(
            dimension_semantics=("arbitrary",), vmem_limit_bytes=VMEM_LIMIT),
        name="mlstm",
    )(mqk, mv, mo, gates, gatest, conv_w, conv_b, bias_row, bias_col, out_g)


def _qk_norm_rope(x, gain, cos, sin):
    lane = lax.broadcasted_iota(jnp.int32, x.shape, 1)
    first = lane < A_HEAD_DIM
    sq = x * x
    tot = jnp.sum(sq, axis=-1, keepdims=True)
    s0 = jnp.sum(jnp.where(first, sq, 0.0), axis=-1, keepdims=True)
    ms = jnp.where(first, s0, tot - s0) * (1.0 / A_HEAD_DIM)
    xn = x * lax.rsqrt(ms + EPS) * gain
    in_low = (lane % A_HEAD_DIM) < ROPE_HALF
    swapped = jnp.where(in_low, pltpu.roll(xn, LANES - ROPE_HALF, 1), pltpu.roll(xn, ROPE_HALF, 1))
    return xn * cos + swapped * sin


def _moba_kernel(aq_ref, ak_ref, av_ref, cos_ref, sin_ref, qg_ref, kg_ref, og_ref, y_ref, kr_s, s_s):
    blk = MOBA_BLOCK
    dh = A_HEAD_DIM
    km_first = lax.broadcasted_iota(jnp.int32, (MOBA_NB, HEAD_PAIR), 1) < dh
    first = lax.broadcasted_iota(jnp.int32, (blk, HEAD_PAIR), 1) < dh
    blk_id = lax.broadcasted_iota(jnp.int32, (MOBA_NB, blk), 0)
    eye = (lax.broadcasted_iota(jnp.int32, (MOBA_NB, LANES), 0)
           == lax.broadcasted_iota(jnp.int32, (MOBA_NB, LANES), 1)).astype(F32)
    q_pos = lax.broadcasted_iota(jnp.int32, (2 * blk, blk), 0) % blk
    k_pos = lax.broadcasted_iota(jnp.int32, (2 * blk, blk), 1)
    causal = k_pos <= q_pos
    og = og_ref[...]

    km_rows = []
    prepped = {}

    def key_prep(j):
        rows = slice(j * blk, (j + 1) * blk)
        kr = _qk_norm_rope(ak_ref[rows, :].astype(F32), kg_ref[...], cos_ref[rows, :], sin_ref[rows, :])
        kr_s[rows, :] = kr.astype(BF16)
        km_rows.append(jnp.mean(kr, axis=0, keepdims=True))
        yield

    def query_prep(i):
        rows = slice(i * blk, (i + 1) * blk)
        qr = _qk_norm_rope(aq_ref[rows, :].astype(F32), qg_ref[...], cos_ref[rows, :], sin_ref[rows, :])
        qs = qr * (dh ** -0.5 * math.log2(math.e))
        q2 = jnp.concatenate([jnp.where(first, qs, 0.0), jnp.where(first, 0.0, qs)], axis=0).astype(BF16)
        yield

        bias = None
        if i > MOBA_TOPK:
            km = jnp.concatenate(km_rows[:i] + [jnp.zeros((MOBA_NB - i, HEAD_PAIR), F32)], axis=0)
            km_heads = (jnp.where(km_first, km, 0.0), jnp.where(km_first, 0.0, km))
            past = blk_id < i
            cols = []
            for kmh in km_heads:
                gate = jnp.where(past, _dot_nt(kmh, qr, precision=HIGHEST), -jnp.inf)
                rank = jnp.zeros((MOBA_NB, blk), F32)
                for m_blk in range(i):
                    gm = gate[m_blk:m_blk + 1, :]
                    ahead = (gm > gate) | ((gm == gate) & (m_blk < blk_id))
                    rank = rank + jnp.where(ahead, 1.0, 0.0)
                sel = jnp.where(past & (rank < float(MOBA_TOPK)), 1.0, 0.0)
                cols.append(_dot_tn(sel, eye))
            bias = jnp.where(jnp.concatenate(cols, axis=0) > 0.5, 0.0, -jnp.inf)
        prepped[i] = (q2, bias)
        yield

    def scores(i, out):
        q2, bias = prepped.pop(i)
        m_run = None
        for j in range(i + 1):
            s = _dot_nt(q2, kr_s[j * blk:(j + 1) * blk, :])
            if j == i:
                s = jnp.where(causal, s, -jnp.inf)
            elif bias is not None:
                s = s + bias[:, j:j + 1]
            s_s[i % 2, :, j * blk:(j + 1) * blk] = s
            part = jnp.maximum(s[:, :LANES], s[:, LANES:])
            m_run = part if m_run is None else jnp.maximum(m_run, part)
            yield
        out.append(jnp.max(m_run, axis=1, keepdims=True))

    def attend(i, m_fin):
        rows = slice(i * blk, (i + 1) * blk)
        l_run = jnp.zeros((2 * blk, LANES), F32)
        acc = jnp.zeros((2 * blk, HEAD_PAIR), F32)
        m_lanes = jnp.broadcast_to(m_fin, (2 * blk, LANES))
        for j in range(i + 1):
            p_lo = jnp.exp2(s_s[i % 2, :, j * blk:j * blk + LANES] - m_lanes)
            p_hi = jnp.exp2(s_s[i % 2, :, j * blk + LANES:(j + 1) * blk] - m_lanes)
            l_run = l_run + (p_lo + p_hi)
            p = jnp.concatenate([p_lo, p_hi], axis=1)
            acc = acc + _dot(p.astype(BF16), av_ref[j * blk:(j + 1) * blk, :])
            yield
        o2 = acc / jnp.sum(l_run, axis=1, keepdims=True)
        o = jnp.where(first, o2[:blk, :], o2[blk:, :])
        sq = o * o
        tot = jnp.sum(sq, axis=-1, keepdims=True)
        s0 = jnp.sum(jnp.where(first, sq, 0.0), axis=-1, keepdims=True)
        ms = jnp.where(first, s0, tot - s0) * (1.0 / dh)
        y_ref[rows, :] = (o * lax.rsqrt(ms + EPS) * og).astype(y_ref.dtype)

    def interleave(*stages):
        stages = list(stages)
        while stages:
            for stage in list(stages):
                try:
                    next(stage)
                except StopIteration:
                    stages.remove(stage)

    m_cur = []
    interleave(key_prep(0), key_prep(1), query_prep(0), query_prep(1))
    interleave(scores(0, m_cur))
    for i in range(MOBA_NB):
        m_next = []
        stages = [attend(i, m_cur[0])]
        if i + 1 < MOBA_NB:
            stages.insert(0, scores(i + 1, m_next))
        if i + 2 < MOBA_NB:
            stages += [key_prep(i + 2), query_prep(i + 2)]
        interleave(*stages)
        m_cur = m_next


def _moba(aq, ak, av, cos_t, sin_t, q_gain, k_gain, out_gain):
    n_pairs = A_HEADS // 2
    const = lambda b, p: (0, 0)
    pair = lambda b, p: (b, p)
    return pl.pallas_call(
        _moba_kernel,
        grid=(BATCH, n_pairs),
        in_specs=[
            pl.BlockSpec((SEQ, HEAD_PAIR), pair),
            pl.BlockSpec((SEQ, HEAD_PAIR), pair),
            pl.BlockSpec((SEQ, HEAD_PAIR), pair),
            pl.BlockSpec((SEQ, HEAD_PAIR), const),
            pl.BlockSpec((SEQ, HEAD_PAIR), const),
            pl.BlockSpec((1, HEAD_PAIR), const),
            pl.BlockSpec((1, HEAD_PAIR), const),
            pl.BlockSpec((None, 1, HEAD_PAIR), lambda b, p: (p, 0, 0)),
        ],
        out_specs=pl.BlockSpec((SEQ, HEAD_PAIR), pair),
        out_shape=jax.ShapeDtypeStruct((TOKENS, A_WIDTH), BF16),
        scratch_shapes=[
            pltpu.VMEM((SEQ, HEAD_PAIR), BF16),
            pltpu.VMEM((2, 2 * MOBA_BLOCK, SEQ), F32),
        ],
        compiler_params=pltpu.CompilerParams(
            dimension_semantics=("arbitrary", "arbitrary"), vmem_limit_bytes=VMEM_LIMIT),
        name="moba",
    )(aq, ak, av, cos_t, sin_t, q_gain, k_gain, out_gain)


def _s5_kernel(u_ref, abar_ref, bre_ref, bim_ref, cre_ref, cim_ref, d_ref, gw_ref, gb_ref, gmat_ref, og_ref,
               y_ref, xr_s, xi_s, st_s, u_s, y_s):
    step = pl.program_id(0)
    rows = S5_STEPS * BATCH
    tile = 256
    n_tiles = rows // tile
    halves = S_WIDTH // LANES

    @pl.when(step == 0)
    def _init():
        st_s[...] = jnp.zeros_like(st_s)

    for b in range(BATCH):
        for hv in range(halves):
            lanes = slice(b * S_WIDTH + hv * LANES, b * S_WIDTH + (hv + 1) * LANES)
            u_s[hv, pl.ds(b, S5_STEPS, stride=BATCH), :] = u_ref[:, lanes].astype(F32)

    def u_rows(r):
        return jnp.concatenate([u_s[hv, r * tile:(r + 1) * tile, :] for hv in range(halves)], axis=1)

    for r in range(n_tiles):
        u = u_rows(r).astype(BF16)
        t0 = r * tile // BATCH
        xr_s[t0:t0 + tile // BATCH] = _dot(u, bre_ref[...]).reshape(tile // BATCH, BATCH, S_FLAT)
        xi_s[t0:t0 + tile // BATCH] = _dot(u, bim_ref[...]).reshape(tile // BATCH, BATCH, S_FLAT)

    ar = jnp.broadcast_to(abar_ref[0:1, :], (BATCH, S_FLAT))
    ai = jnp.broadcast_to(abar_ref[1:2, :], (BATCH, S_FLAT))

    def scan_step(t, carry):
        sr, si = carry
        nr = ar * sr - ai * si + xr_s[t]
        ni = ar * si + ai * sr + xi_s[t]
        xr_s[t] = nr
        xi_s[t] = ni
        return nr, ni

    sr, si = lax.fori_loop(0, S5_STEPS, scan_step, (st_s[0], st_s[1]), unroll=4)
    st_s[0] = sr
    st_s[1] = si

    for r in range(n_tiles):
        t0 = r * tile // BATCH
        xr = xr_s[t0:t0 + tile // BATCH].reshape(tile, S_FLAT).astype(BF16)
        xi = xi_s[t0:t0 + tile // BATCH].reshape(tile, S_FLAT).astype(BF16)
        y = _dot(xr, cre_ref[...]) - _dot(xi, cim_ref[...]) + d_ref[...] * u_rows(r)
        z = 0.5 * y * (1.0 + jnp.tanh(math.sqrt(2.0 / math.pi) * (y + 0.044715 * (y * y * y))))
        y2 = z * _sigmoid(_dot(z.astype(BF16), gw_ref[...]) + gb_ref[...])
        ms = _dot((y2 * y2).astype(BF16), gmat_ref[...])
        yn = y2 * lax.rsqrt(ms + EPS) * og_ref[...]
        for hv in range(halves):
            y_s[hv, r * tile:(r + 1) * tile, :] = yn[:, hv * LANES:(hv + 1) * LANES]

    for b in range(BATCH):
        for hv in range(halves):
            lanes = slice(b * S_WIDTH + hv * LANES, b * S_WIDTH + (hv + 1) * LANES)
            y_ref[:, lanes] = y_s[hv, pl.ds(b, S5_STEPS, stride=BATCH), :].astype(y_ref.dtype)


def _s5(u_sb, abar, bre, bim, cre, cim, d_row, glu_w, glu_b, gmat, out_g):
    rows = S5_STEPS * BATCH
    const = lambda s: (0, 0)
    return pl.pallas_call(
        _s5_kernel,
        grid=(SEQ // S5_STEPS,),
        in_specs=[
            pl.BlockSpec((S5_STEPS, BATCH * S_WIDTH), lambda s: (s, 0)),
            pl.BlockSpec((2, S_FLAT), const),
            pl.BlockSpec((S_WIDTH, S_FLAT), const),
            pl.BlockSpec((S_WIDTH, S_FLAT), const),
            pl.BlockSpec((S_FLAT, S_WIDTH), const),
            pl.BlockSpec((S_FLAT, S_WIDTH), const),
            pl.BlockSpec((1, S_WIDTH), const),
            pl.BlockSpec((S_WIDTH, S_WIDTH), const),
            pl.BlockSpec((1, S_WIDTH), const),
            pl.BlockSpec((S_WIDTH, S_WIDTH), const),
            pl.BlockSpec((1, S_WIDTH), const),
        ],
        out_specs=pl.BlockSpec((S5_STEPS, BATCH * S_WIDTH), lambda s: (s, 0)),
        out_shape=jax.ShapeDtypeStruct((SEQ, BATCH * S_WIDTH), BF16),
        scratch_shapes=[
            pltpu.VMEM((S5_STEPS, BATCH, S_FLAT), F32),
            pltpu.VMEM((S5_STEPS, BATCH, S_FLAT), F32),
            pltpu.VMEM((2, BATCH, S_FLAT), F32),
            pltpu.VMEM((S_WIDTH // LANES, rows, LANES), F32),
            pltpu.VMEM((S_WIDTH // LANES, rows, LANES), F32),
        ],
        compiler_params=pltpu.CompilerParams(
            dimension_semantics=("arbitrary",), vmem_limit_bytes=VMEM_LIMIT),
        name="s5",
    )(u_sb, abar, bre, bim, cre, cim, d_row, glu_w, glu_b, gmat, out_g)


def _outproj_kernel(x_ref, ym_ref, ya_ref, ys_ref, w_ref, g_ref, wr_ref, br_ref,
                    x1_ref, xn_ref, route_ref):
    x1 = x_ref[...]
    x1 = x1 + _dot(ym_ref[...], w_ref[0:M_WIDTH, :])
    x1 = x1 + _dot(ya_ref[...], w_ref[M_WIDTH:M_WIDTH + A_WIDTH, :])
    x1 = x1 + _dot(ys_ref[...], w_ref[M_WIDTH + A_WIDTH:, :])
    x1_ref[...] = x1
    xn = x1 * lax.rsqrt(jnp.mean(x1 * x1, axis=-1, keepdims=True) + EPS) * g_ref[...]
    for j in range(ROW_TILES):
        xn_ref[pl.ds(j, TM_PROJ, stride=ROW_TILES), :] = xn[:, j * LANES:(j + 1) * LANES]
    logits = _split_dot(xn, wr_ref[...]) + br_ref[...]

    lane = lax.broadcasted_iota(jnp.int32, logits.shape, 1).astype(F32)
    is_grp = lane < N_GROUPS
    neg = -jnp.inf
    gl = jnp.where(is_grp, logits, neg)
    gmax = jnp.max(gl, axis=-1, keepdims=True)
    gsum = jnp.sum(jnp.where(is_grp, jnp.exp(logits - gmax), 0.0), axis=-1, keepdims=True)
    g_sel = jnp.min(jnp.where(gl == gmax, lane, float(LANES)), axis=-1, keepdims=True)
    lo = N_GROUPS + EXPERTS_PER_GROUP * g_sel
    el = jnp.where((lane >= lo) & (lane < lo + EXPERTS_PER_GROUP), logits, neg)
    t1 = jnp.max(el, axis=-1, keepdims=True)
    i1 = jnp.min(jnp.where(el == t1, lane, float(LANES)), axis=-1, keepdims=True)
    el2 = jnp.where(lane == i1, neg, el)
    t2 = jnp.max(el2, axis=-1, keepdims=True)
    i2 = jnp.min(jnp.where(el2 == t2, lane, float(LANES)), axis=-1, keepdims=True)
    e21 = jnp.exp(t2 - t1)
    w1 = 1.0 / ((1.0 + e21) * gsum)
    w2 = e21 / ((1.0 + e21) * gsum)
    route = jnp.where(lane == 0.0, i1 - N_GROUPS,
                      jnp.where(lane == 1.0, i2 - N_GROUPS,
                                jnp.where(lane == 2.0, w1, jnp.where(lane == 3.0, w2, 0.0))))
    route_ref[...] = route


def _outproj(x, y_m, y_a, y_s_sb, w_out, g2, w_router, b_router):
    n_s = SEQ // TM_PROJ
    row = lambda b, s: (b * n_s + s, 0)
    const = lambda b, s: (0, 0)
    return pl.pallas_call(
        _outproj_kernel,
        grid=(BATCH, n_s),
        in_specs=[
            pl.BlockSpec((TM_PROJ, D_MODEL), row),
            pl.BlockSpec((TM_PROJ, M_WIDTH), row),
            pl.BlockSpec((TM_PROJ, A_WIDTH), row),
            pl.BlockSpec((TM_PROJ, S_WIDTH), lambda b, s: (s, b)),
            pl.BlockSpec((D_MIX, D_MODEL), const),
            pl.BlockSpec((1, D_MODEL), const),
            pl.BlockSpec((D_MODEL, 2 * LANES), const),
            pl.BlockSpec((1, LANES), const),
        ],
        out_specs=(
            pl.BlockSpec((TM_PROJ, D_MODEL), row),
            pl.BlockSpec((TM_PROJ * ROW_TILES, LANES), row),
            pl.BlockSpec((TM_PROJ, LANES), row),
        ),
        out_shape=(
            jax.ShapeDtypeStruct((TOKENS, D_MODEL), F32),
            jax.ShapeDtypeStruct((TOKENS * ROW_TILES, LANES), F32),
            jax.ShapeDtypeStruct((TOKENS, LANES), F32),
        ),
        compiler_params=pltpu.CompilerParams(
            dimension_semantics=("arbitrary", "arbitrary"), vmem_limit_bytes=VMEM_LIMIT),
        name="outproj",
    )(x, y_m, y_a, y_s_sb, w_out, g2, w_router, b_router)


def _ffn_kernel(be_ref, xs_ref, wg_ref, wu_ref, wd_ref, ys_ref, wg_s, wu_s, wd_s):
    i = pl.program_id(0)
    prev = be_ref[jnp.maximum(i - 1, 0)]

    @pl.when((i == 0) | (be_ref[i] != prev))
    def _load_expert():
        wg_s[...] = wg_ref[...].astype(BF16)
        wu_s[...] = wu_ref[...].astype(BF16)
        wd_s[...] = wd_ref[...].astype(BF16)

    xb = jnp.concatenate([xs_ref[pl.ds(j, FFN_BLOCK, stride=ROW_TILES), :] for j in range(ROW_TILES)],
                         axis=1).astype(BF16)
    g = _dot(xb, wg_s[...])
    u = _dot(xb, wu_s[...])
    hmid = (g * _sigmoid(g) * u).astype(BF16)
    y = _dot(hmid, wd_s[...])
    for j in range(ROW_TILES):
        ys_ref[pl.ds(j, FFN_BLOCK, stride=ROW_TILES), :] = y[:, j * LANES:(j + 1) * LANES]


def _expert_ffn(layer, blk_exp, xs, w_gate, w_up, w_down):
    nblk = blk_exp.shape[0]
    rows = pl.BlockSpec((FFN_BLOCK * ROW_TILES, LANES), lambda i, be: (i, 0))
    grid_spec = pltpu.PrefetchScalarGridSpec(
        num_scalar_prefetch=1,
        grid=(nblk,),
        in_specs=[
            rows,
            pl.BlockSpec((None, None, D_MODEL, D_EXPERT), lambda i, be: (layer, be[i], 0, 0)),
            pl.BlockSpec((None, None, D_MODEL, D_EXPERT), lambda i, be: (layer, be[i], 0, 0)),
            pl.BlockSpec((None, None, D_EXPERT, D_MODEL), lambda i, be: (layer, be[i], 0, 0)),
        ],
        out_specs=rows,
        scratch_shapes=[
            pltpu.VMEM((D_MODEL, D_EXPERT), BF16),
            pltpu.VMEM((D_MODEL, D_EXPERT), BF16),
            pltpu.VMEM((D_EXPERT, D_MODEL), BF16),
        ],
    )
    return pl.pallas_call(
        _ffn_kernel,
        grid_spec=grid_spec,
        out_shape=jax.ShapeDtypeStruct((nblk * FFN_BLOCK * ROW_TILES, LANES), F32),
        compiler_params=pltpu.CompilerParams(
            dimension_semantics=("arbitrary",), vmem_limit_bytes=VMEM_LIMIT),
        name="expert_ffn",
    )(blk_exp, xs, w_gate, w_up, w_down)


def _plan_kernel(e_ref, ps_ref, dest_ref, run_s, tri_s):
    i = pl.program_id(0)

    @pl.when(i == 0)
    def _init():
        run_s[...] = ps_ref[...]
        s_i = lax.broadcasted_iota(jnp.int32, (PLAN_TILE, PLAN_TILE), 0)
        t_i = lax.broadcasted_iota(jnp.int32, (PLAN_TILE, PLAN_TILE), 1)
        tri_s[...] = (s_i < t_i).astype(BF16)

    expert = lax.broadcasted_iota(jnp.int32, (N_EXPERTS, PLAN_TILE), 0)
    onehot = jnp.where(e_ref[...] == expert, 1.0, 0.0)
    before = _dot(onehot.astype(BF16), tri_s[...])
    dest = jnp.sum(onehot * (before + run_s[:, 0:1]), axis=0, keepdims=True)
    dest_ref[...] = dest.astype(jnp.int32)
    run_s[...] = run_s[...] + jnp.sum(onehot, axis=1, keepdims=True)


def _plan(e_rows, pstart_col):
    n_tiles = e_rows.shape[0]
    return pl.pallas_call(
        _plan_kernel,
        grid=(n_tiles,),
        in_specs=[
            pl.BlockSpec((None, 1, PLAN_TILE), lambda i: (i, 0, 0)),
            pl.BlockSpec((N_EXPERTS, LANES), lambda i: (0, 0)),
        ],
        out_specs=pl.BlockSpec((None, 1, PLAN_TILE), lambda i: (i, 0, 0)),
        out_shape=jax.ShapeDtypeStruct((n_tiles, 1, PLAN_TILE), jnp.int32),
        scratch_shapes=[pltpu.VMEM((N_EXPERTS, LANES), F32), pltpu.VMEM((PLAN_TILE, PLAN_TILE), BF16)],
        compiler_params=pltpu.CompilerParams(
            dimension_semantics=("arbitrary",), vmem_limit_bytes=VMEM_LIMIT),
        name="moe_plan",
    )(e_rows, pstart_col)


def _dispatch_kernel(fill_ref, d0_ref, d1_ref, xn_ref, xs_hbm, zero_s, sem, fill_sem):
    @pl.when(pl.program_id(0) == 0)
    def _zero_padding():
        zero_s[...] = jnp.zeros_like(zero_s)

        def fill(c):
            start = pl.multiple_of(fill_ref[0, c] * ROW_TILES, ROW_TILES)
            return pltpu.make_async_copy(zero_s, xs_hbm.at[pl.ds(start, FFN_BLOCK * ROW_TILES), :], fill_sem)

        for c in range(N_FILLS):
            pl.when(fill_ref[0, c] >= 0)(lambda c=c: fill(c).start())
        for c in range(N_FILLS):
            pl.when(fill_ref[0, c] >= 0)(lambda c=c: fill(c).wait())

    def row_copy(dref, t):
        src = xn_ref.at[pl.ds(pl.multiple_of(t * ROW_TILES, ROW_TILES), ROW_TILES), :]
        dst = xs_hbm.at[pl.ds(pl.multiple_of(dref[0, t] * ROW_TILES, ROW_TILES), ROW_TILES), :]
        return pltpu.make_async_copy(src, dst, sem)

    def start(t, carry):
        row_copy(d0_ref, t).start()
        row_copy(d1_ref, t).start(priority=1)
        return carry

    lax.fori_loop(0, DISPATCH_TOKENS, start, 0, unroll=8)
    for _ in range(EXPERT_TOPK):
        pltpu.make_async_copy(xn_ref, xs_hbm.at[pl.ds(0, DISPATCH_TOKENS * ROW_TILES), :], sem).wait()


def _dispatch(fill_rows, dest, xn, n_rows):
    n_tiles = TOKENS // DISPATCH_TOKENS
    dest4 = dest.reshape(EXPERT_TOPK, n_tiles, 1, DISPATCH_TOKENS)
    smem_tile = lambda k: pl.BlockSpec((None, None, 1, DISPATCH_TOKENS), lambda i: (k, i, 0, 0),
                                       memory_space=pltpu.SMEM)
    return pl.pallas_call(
        _dispatch_kernel,
        grid=(n_tiles,),
        in_specs=[
            pl.BlockSpec(memory_space=pltpu.SMEM),
            smem_tile(0),
            smem_tile(1),
            pl.BlockSpec((DISPATCH_TOKENS * ROW_TILES, LANES), lambda i: (i, 0)),
        ],
        out_specs=pl.BlockSpec(memory_space=pl.ANY),
        out_shape=jax.ShapeDtypeStruct((n_rows * ROW_TILES, LANES), F32),
        scratch_shapes=[
            pltpu.VMEM((FFN_BLOCK * ROW_TILES, LANES), F32),
            pltpu.SemaphoreType.DMA(()),
            pltpu.SemaphoreType.DMA(()),
        ],
        compiler_params=pltpu.CompilerParams(
            dimension_semantics=("arbitrary",), vmem_limit_bytes=VMEM_LIMIT),
        name="moe_dispatch",
    )(fill_rows, dest4, dest4, xn)


def _combine_kernel(d0_ref, d1_ref, d0_next_ref, d1_next_ref, x1_ref, route_ref, ys_hbm, out_ref, buf, sem):
    i = pl.program_id(0)
    n = pl.num_programs(0)

    def start_tile(drefs, slot):
        def body(t, carry):
            for k in range(EXPERT_TOPK):
                src = ys_hbm.at[pl.ds(pl.multiple_of(drefs[k][0, t] * ROW_TILES, ROW_TILES), ROW_TILES), :]
                dst = buf.at[slot, k, pl.ds(pl.multiple_of(t * ROW_TILES, ROW_TILES), ROW_TILES), :]
                pltpu.make_async_copy(src, dst, sem.at[slot]).start(priority=k)
            return carry
        lax.fori_loop(0, COMBINE_TOKENS, body, 0, unroll=8)

    slot = i % 2

    @pl.when(i == 0)
    def _first():
        start_tile((d0_ref, d1_ref), 0)

    @pl.when(i + 1 < n)
    def _prefetch():
        start_tile((d0_next_ref, d1_next_ref), 1 - slot)

    for k in range(EXPERT_TOPK):
        pltpu.make_async_copy(ys_hbm.at[pl.ds(0, COMBINE_TOKENS * ROW_TILES), :], buf.at[slot, k],
                              sem.at[slot]).wait()
    w = route_ref[...]
    w1 = w[:, 2:3]
    w2 = w[:, 3:4]
    for j in range(ROW_TILES):
        cols = slice(j * LANES, (j + 1) * LANES)
        chunk = pl.ds(j, COMBINE_TOKENS, stride=ROW_TILES)
        out_ref[:, cols] = x1_ref[:, cols] + w1 * buf[slot, 0, chunk, :] + w2 * buf[slot, 1, chunk, :]


def _combine(dest, x1, route, ys3):
    n_tiles = TOKENS // COMBINE_TOKENS
    dest4 = dest.reshape(EXPERT_TOPK, n_tiles, 1, COMBINE_TOKENS)
    smem_tile = lambda k, nxt: pl.BlockSpec(
        (None, None, 1, COMBINE_TOKENS), lambda i: (k, jnp.minimum(i + nxt, n_tiles - 1), 0, 0),
        memory_space=pltpu.SMEM)
    return pl.pallas_call(
        _combine_kernel,
        grid=(n_tiles,),
        in_specs=[
            smem_tile(0, 0), smem_tile(1, 0), smem_tile(0, 1), smem_tile(1, 1),
            pl.BlockSpec((COMBINE_TOKENS, D_MODEL), lambda i: (i, 0)),
            pl.BlockSpec((COMBINE_TOKENS, LANES), lambda i: (i, 0)),
            pl.BlockSpec(memory_space=pl.ANY),
        ],
        out_specs=pl.BlockSpec((COMBINE_TOKENS, D_MODEL), lambda i: (i, 0)),
        out_shape=jax.ShapeDtypeStruct((TOKENS, D_MODEL), F32),
        scratch_shapes=[
            pltpu.VMEM((2, EXPERT_TOPK, COMBINE_TOKENS * ROW_TILES, LANES), F32),
            pltpu.SemaphoreType.DMA((2,)),
        ],
        compiler_params=pltpu.CompilerParams(
            dimension_semantics=("arbitrary",), vmem_limit_bytes=VMEM_LIMIT),
        name="moe_combine",
    )(dest4, dest4, dest4, dest4, x1, route, ys3)


def _rope_tables():
    inv = ROPE_THETA ** (-np.arange(ROPE_HALF, dtype=np.float64) * 2.0 / ROPE_DIM)
    ang = np.arange(SEQ, dtype=np.float64)[:, None] * inv[None, :]
    cos = np.ones((SEQ, A_HEAD_DIM))
    sin = np.zeros((SEQ, A_HEAD_DIM))
    cos[:, :ROPE_HALF] = np.cos(ang)
    cos[:, ROPE_HALF:ROPE_DIM] = np.cos(ang)
    sin[:, :ROPE_HALF] = -np.sin(ang)
    sin[:, ROPE_HALF:ROPE_DIM] = np.sin(ang)
    return (jnp.asarray(np.tile(cos, (1, 2)), F32), jnp.asarray(np.tile(sin, (1, 2)), F32))


def _split_w_in(w_in):
    sizes = [M_WIDTH, M_WIDTH, M_WIDTH, M_WIDTH, M_HEADS, M_HEADS, A_WIDTH, A_WIDTH, A_WIDTH, S_WIDTH]
    offs = np.cumsum([0] + sizes)
    mq, mk, mv, mo, mi, mf, aq, ak, av, su = (w_in[:, offs[n]:offs[n + 1]] for n in range(len(sizes)))
    w_main = jnp.concatenate([aq, ak, av, mq, mk, mv, mo, su], axis=1).astype(BF16)
    gate = jnp.concatenate([mi, mf], axis=1)
    return w_main, _hi_lo(jnp.pad(gate, ((0, 0), (0, LANES - 2 * M_HEADS))))


def _s5_params(a_re, a_im, b_re, b_im, c_re, c_im, log_dt):
    dt = jnp.exp(log_dt)[:, None]
    mag = jnp.exp(a_re * dt)
    abar_re, abar_im = mag * jnp.cos(a_im * dt), mag * jnp.sin(a_im * dt)
    zr, zi = abar_re - 1.0, abar_im
    den = a_re * a_re + a_im * a_im
    fr, fi = (zr * a_re + zi * a_im) / den, (zi * a_re - zr * a_im) / den
    bbar_re = fr[..., None] * b_re - fi[..., None] * b_im
    bbar_im = fr[..., None] * b_im + fi[..., None] * b_re
    eye = jnp.eye(S_GROUPS, dtype=F32)
    dense_b = lambda t: jnp.einsum('gph,gk->ghkp', t, eye).reshape(S_WIDTH, S_FLAT).astype(BF16)
    dense_c = lambda t: jnp.einsum('ghp,gk->gpkh', t, eye).reshape(S_FLAT, S_WIDTH).astype(BF16)
    abar = jnp.stack([abar_re.reshape(S_FLAT), abar_im.reshape(S_FLAT)])
    return abar, dense_b(bbar_re), dense_b(bbar_im), dense_c(c_re), dense_c(c_im)


def _moe(layer, x1, xn, route, w_gate, w_up, w_down):
    n_assign = TOKENS * EXPERT_TOPK
    e_flat = route[:, :EXPERT_TOPK].astype(jnp.int32).T.reshape(n_assign)
    experts = jnp.arange(N_EXPERTS, dtype=jnp.int32)
    counts = jnp.sum((e_flat[:, None] == experts[None, :]).astype(jnp.int32), axis=0)
    padded = (counts + FFN_BLOCK - 1) // FFN_BLOCK * FFN_BLOCK
    pends = jnp.cumsum(padded)
    pstarts = pends - padded
    n_rows = (n_assign + N_EXPERTS * (FFN_BLOCK - 1) + FFN_BLOCK - 1) // FFN_BLOCK * FFN_BLOCK
    nblk = n_rows // FFN_BLOCK
    blk_start = jnp.arange(nblk, dtype=jnp.int32) * FFN_BLOCK
    blk_exp = jnp.minimum(jnp.sum((pends[None, :] <= blk_start[:, None]).astype(jnp.int32), axis=1),
                          N_EXPERTS - 1).astype(jnp.int32)
    pstart_col = jnp.broadcast_to(pstarts.astype(F32)[:, None], (N_EXPERTS, LANES))
    dest = _plan(e_flat.reshape(n_assign // PLAN_TILE, 1, PLAN_TILE), pstart_col)
    tail = pends[-1] + jnp.arange(N_EXPERTS, dtype=jnp.int32) * FFN_BLOCK
    fill_rows = jnp.concatenate([jnp.where(padded > 0, pends - FFN_BLOCK, -1), jnp.where(tail < n_rows, tail, -1)])
    fill_rows = fill_rows.astype(jnp.int32).reshape(1, N_FILLS)
    xs = _dispatch(fill_rows, dest, xn, n_rows)
    ys = _expert_ffn(layer, blk_exp, xs, w_gate, w_up, w_down)
    return _combine(dest, x1, route, ys)


def kernel(x, norm1_g, w_in, m_bias_i, m_bias_f, m_conv_w, m_conv_b, m_out_g, a_q_g, a_k_g, a_out_g, s_a_re, s_a_im, s_b_re, s_b_im, s_c_re, s_c_im, s_d, s_log_dt, s_glu_w, s_glu_b, s_out_g, w_out, norm2_g, r_group_w, r_group_b, r_expert_w, r_expert_b, e_w_gate, e_w_up, e_w_down):
    cos_t, sin_t = _rope_tables()
    gmat = jnp.asarray(np.kron(np.eye(S_GROUPS), np.full((S_GROUP_DIM, S_GROUP_DIM), 1.0 / S_GROUP_DIM)), BF16)
    xf = x.reshape(TOKENS, D_MODEL)
    for l in range(DEPTH):
        w_main, w_gate = _split_w_in(w_in[l])
        aq, ak, av, mqk, mv, mo, su, gates, gatest = _inproj(xf, norm1_g[l].reshape(1, D_MODEL), w_main, w_gate)

        bias = jnp.concatenate([m_bias_i[l], m_bias_f[l]])
        bias_row = jnp.pad(bias, (0, LANES - 2 * M_HEADS)).reshape(1, LANES)
        y_m = _mlstm(mqk, mv, mo, gates, gatest, m_conv_w[l], m_conv_b[l].reshape(1, 2 * M_WIDTH),
                     bias_row, bias.reshape(2 * M_HEADS, 1), m_out_g[l].reshape(1, M_WIDTH))

        y_a = _moba(aq, ak, av, cos_t, sin_t,
                    jnp.tile(a_q_g[l], 2).reshape(1, HEAD_PAIR), jnp.tile(a_k_g[l], 2).reshape(1, HEAD_PAIR),
                    a_out_g[l].reshape(A_HEADS // 2, 1, HEAD_PAIR))

        abar, bre, bim, cre, cim = _s5_params(s_a_re[l], s_a_im[l], s_b_re[l], s_b_im[l],
                                              s_c_re[l], s_c_im[l], s_log_dt[l])
        y_s = _s5(su, abar, bre, bim, cre, cim,
                  s_d[l].reshape(1, S_WIDTH), s_glu_w[l].astype(BF16), s_glu_b[l].reshape(1, S_WIDTH),
                  gmat, s_out_g[l].reshape(1, S_WIDTH))

        w_router = jnp.pad(jnp.concatenate([r_group_w[l], r_expert_w[l]], axis=1),
                           ((0, 0), (0, LANES - N_GROUPS - N_EXPERTS)))
        b_router = jnp.pad(jnp.concatenate([r_group_b[l], r_expert_b[l]]),
                           (0, LANES - N_GROUPS - N_EXPERTS)).reshape(1, LANES)
        x1, xn, route = _outproj(xf, y_m, y_a, y_s, w_out[l].astype(BF16),
                                 norm2_g[l].reshape(1, D_MODEL), _hi_lo(w_router), b_router)
        xf = _moe(l, x1, xn, route, e_w_gate, e_w_up, e_w_down)
    return xf.reshape(BATCH, SEQ, D_MODEL)
```

```python
import functools
import math

import numpy as np
import jax
import jax.numpy as jnp
from jax import lax
from jax.experimental import pallas as pl
from jax.experimental.pallas import tpu as pltpu

F32 = jnp.float32
BF16 = jnp.bfloat16

D_MODEL = 1024
BATCH = 8
SEQ = 2048
DEPTH = 2
TOKENS = BATCH * SEQ

M_HEADS = 4
M_HEAD_DIM = 64
M_WIDTH = M_HEADS * M_HEAD_DIM
A_HEADS = 8
A_HEAD_DIM = 64
A_WIDTH = A_HEADS * A_HEAD_DIM
S_GROUPS = 16
S_GROUP_DIM = 16
S_WIDTH = S_GROUPS * S_GROUP_DIM
S_STATE = 64
S_FLAT = S_GROUPS * S_STATE
D_MIX = M_WIDTH + A_WIDTH + S_WIDTH

CONV_K = 4
MOBA_BLOCK = 256
MOBA_NB = SEQ // MOBA_BLOCK
MOBA_TOPK = 3
ROPE_THETA = 500000.0
ROPE_DIM = A_HEAD_DIM // 4
ROPE_HALF = ROPE_DIM // 2

N_GROUPS = 4
EXPERTS_PER_GROUP = 8
N_EXPERTS = N_GROUPS * EXPERTS_PER_GROUP
EXPERT_TOPK = 2
D_EXPERT = 512
DISPATCH_BLOCK = 128
EPS = 1e-6

LANES = 128
ROW_TILES = D_MODEL // LANES
HEAD_PAIR = 2 * A_HEAD_DIM

W_MAIN = 3 * A_WIDTH + 4 * M_WIDTH + S_WIDTH
TM_PROJ = 512
M_CHUNK = 256
M_NCHUNK = SEQ // M_CHUNK
S5_STEPS = 256
FFN_BLOCK = 256
N_FILLS = 2 * N_EXPERTS
PLAN_TILE = 1024
DISPATCH_TOKENS = 512
COMBINE_TOKENS = 256
VMEM_LIMIT = 56 * 1024 * 1024

HIGHEST = lax.Precision.HIGHEST


def _dot(a, b, precision=None):
    return jnp.dot(a, b, preferred_element_type=F32, precision=precision)


def _dot_nt(a, b, precision=None):
    return lax.dot_general(a, b, (((1,), (1,)), ((), ())), preferred_element_type=F32, precision=precision)


def _dot_tn(a, b, precision=None):
    return lax.dot_general(a, b, (((0,), (0,)), ((), ())), preferred_element_type=F32, precision=precision)


def _split_dot(a, w_cat):
    n = w_cat.shape[1] // 2
    a_hi = a.astype(BF16)
    a_lo = (a - a_hi.astype(F32)).astype(BF16)
    both = _dot(a_hi, w_cat)
    return both[:, :n] + both[:, n:] + _dot(a_lo, w_cat[:, :n])


def _bf16_terms(x):
    hi = x.astype(BF16)
    rest = x - hi.astype(F32)
    mid = rest.astype(BF16)
    return hi, mid, (rest - mid.astype(F32)).astype(BF16)


def _hi_lo(w):
    w_hi = w.astype(BF16)
    return jnp.concatenate([w_hi, (w - w_hi.astype(F32)).astype(BF16)], axis=1)


def _log_sigmoid(x):
    return jnp.minimum(x, 0.0) - jnp.log1p(jnp.exp(-jnp.abs(x)))


def _sigmoid(x):
    return 1.0 / (1.0 + jnp.exp(-x))


def _inproj_kernel(x_ref, g_ref, w_ref, wg_ref,
                   aq_ref, ak_ref, av_ref, mqk_ref, mv_ref, mo_ref, su_ref, gates_ref, gatest_ref):
    x = x_ref[...]
    h = x * lax.rsqrt(jnp.mean(x * x, axis=-1, keepdims=True) + EPS) * g_ref[...]
    hb = h.astype(BF16)
    off = 0
    for ref, width in ((aq_ref, A_WIDTH), (ak_ref, A_WIDTH), (av_ref, A_WIDTH), (mqk_ref, 2 * M_WIDTH),
                       (mv_ref, M_WIDTH), (mo_ref, M_WIDTH), (su_ref, S_WIDTH)):
        ref[...] = _dot(hb, w_ref[:, off:off + width]).astype(ref.dtype)
        off += width
    gates = _split_dot(h, wg_ref[...])
    gates_ref[...] = gates
    gt = gates.T[:8, :]
    for c in range(TM_PROJ // M_CHUNK):
        gatest_ref[c] = gt[:, c * M_CHUNK:(c + 1) * M_CHUNK]


def _inproj(x, g, w_main, w_gate):
    n_s = SEQ // TM_PROJ
    row = lambda b, s: (b * n_s + s, 0)
    const = lambda b, s: (0, 0)
    out_shapes = (
        jax.ShapeDtypeStruct((TOKENS, A_WIDTH), BF16),
        jax.ShapeDtypeStruct((TOKENS, A_WIDTH), BF16),
        jax.ShapeDtypeStruct((TOKENS, A_WIDTH), BF16),
        jax.ShapeDtypeStruct((TOKENS, 2 * M_WIDTH), BF16),
        jax.ShapeDtypeStruct((TOKENS, M_WIDTH), BF16),
        jax.ShapeDtypeStruct((TOKENS, M_WIDTH), BF16),
        jax.ShapeDtypeStruct((SEQ, BATCH * S_WIDTH), BF16),
        jax.ShapeDtypeStruct((TOKENS, LANES), F32),
        jax.ShapeDtypeStruct((TOKENS // M_CHUNK, 8, M_CHUNK), F32),
    )
    out_specs = (
        pl.BlockSpec((TM_PROJ, A_WIDTH), row),
        pl.BlockSpec((TM_PROJ, A_WIDTH), row),
        pl.BlockSpec((TM_PROJ, A_WIDTH), row),
        pl.BlockSpec((TM_PROJ, 2 * M_WIDTH), row),
        pl.BlockSpec((TM_PROJ, M_WIDTH), row),
        pl.BlockSpec((TM_PROJ, M_WIDTH), row),
        pl.BlockSpec((TM_PROJ, S_WIDTH), lambda b, s: (s, b)),
        pl.BlockSpec((TM_PROJ, LANES), row),
        pl.BlockSpec((TM_PROJ // M_CHUNK, 8, M_CHUNK), lambda b, s: (b * n_s + s, 0, 0)),
    )
    return pl.pallas_call(
        _inproj_kernel,
        grid=(BATCH, n_s),
        in_specs=[
            pl.BlockSpec((TM_PROJ, D_MODEL), row),
            pl.BlockSpec((1, D_MODEL), const),
            pl.BlockSpec((D_MODEL, W_MAIN), const),
            pl.BlockSpec((D_MODEL, 2 * LANES), const),
        ],
        out_specs=out_specs,
        out_shape=out_shapes,
        compiler_params=pltpu.CompilerParams(
            dimension_semantics=("arbitrary", "arbitrary"), vmem_limit_bytes=VMEM_LIMIT),
        name="inproj",
    )(x, g, w_main, w_gate)


def _mlstm_kernel(mqk_ref, mv_ref, mo_ref, gates_ref, gatest_ref, cw_ref, cb_ref, brow_ref, bcol_ref, og_ref,
                  y_ref, qk_s, cols_s, wrow_s):
    L = M_CHUNK
    dh = M_HEAD_DIM
    r_i = lax.broadcasted_iota(jnp.int32, (L, L), 0)
    c_i = lax.broadcasted_iota(jnp.int32, (L, L), 1)
    causal = c_i <= r_i
    tri_u = (r_i <= c_i).astype(BF16)
    eye = (lax.broadcasted_iota(jnp.int32, (8, LANES), 0)
           == lax.broadcasted_iota(jnp.int32, (8, LANES), 1)).astype(BF16)
    w = cw_ref[...]
    cb = cb_ref[...]
    lane_q = lax.broadcasted_iota(jnp.int32, (1, 2 * M_WIDTH), 1) < M_WIDTH
    kscale = jnp.where(lane_q, 1.0, dh ** -0.5).astype(F32)
    for c in range(M_NCHUNK):
        halo = 16 if c > 0 else 0
        for strip in range(2 * M_WIDTH // LANES):
            cols = slice(strip * LANES, (strip + 1) * LANES)
            xt = mqk_ref[c * L - halo:(c + 1) * L, cols].astype(F32)
            rows = lax.broadcasted_iota(jnp.int32, xt.shape, 0)
            y = cb[:, cols] + w[CONV_K - 1:CONV_K, cols] * xt
            for k in range(1, CONV_K):
                sh = pltpu.roll(xt, k, 0)
                if c == 0:
                    sh = jnp.where(rows >= k, sh, 0.0)
                y = y + w[CONV_K - 1 - k:CONV_K - k, cols] * sh
            y = y[halo:, :]
            y = y * _sigmoid(y) * kscale[:, cols]
            qk_s[c * L:(c + 1) * L, cols] = y.astype(BF16)

    rows_all = M_NCHUNK * 8
    grow = gatest_ref[...].reshape(rows_all, L) + jnp.tile(bcol_ref[...], (M_NCHUNK, 1))
    cs_row = sum(_dot(term, tri_u) for term in _bf16_terms(_log_sigmoid(grow)))
    w_row = grow - pltpu.roll(cs_row, rows_all - M_HEADS, 0)
    lane_t = lax.broadcasted_iota(jnp.int32, (rows_all, L), 1)
    pm = w_row
    shift = 1
    while shift < L:
        pm = jnp.maximum(pm, jnp.where(lane_t >= shift, pltpu.roll(pm, shift, 1), -jnp.inf))
        shift *= 2
    wrow_s[...] = w_row.reshape(M_NCHUNK, 8, L)
    is_i_row = lax.broadcasted_iota(jnp.int32, (rows_all, L), 0) % 8 < M_HEADS
    terms = _bf16_terms(jnp.where(is_i_row, pm, cs_row))
    for c in range(M_NCHUNK):
        cols_s[c * L:(c + 1) * L, :] = sum(_dot_tn(term[c * 8:(c + 1) * 8, :], eye) for term in terms)

    pad_lane = lax.broadcasted_iota(jnp.int32, (L, LANES - dh), 1)
    ones_pad = jnp.where(pad_lane == 0, 1.0, 0.0).astype(BF16)
    og = og_ref[...]
    head_mean = jnp.where(lax.broadcasted_iota(jnp.int32, (M_WIDTH, M_WIDTH), 0) // dh
                          == lax.broadcasted_iota(jnp.int32, (M_WIDTH, M_WIDTH), 1) // dh,
                          1.0 / dh, 0.0).astype(BF16)

    def chunk(c, carry):
        r0 = pl.multiple_of(c * L, L)
        gcol = gates_ref[pl.ds(r0, L), :] + brow_ref[...]
        cs_col = pm_col = cols_s[pl.ds(r0, L), :]
        w_row = wrow_s[c]
        qk = qk_s[pl.ds(r0, L), :]
        vv = mv_ref[pl.ds(r0, L), :]
        new_carry = []
        hs = []
        for h in range(M_HEADS):
            cst, m = carry[h]
            q = qk[:, h * dh:(h + 1) * dh]
            k = qk[:, M_WIDTH + h * dh:M_WIDTH + (h + 1) * dh]
            v_aug = jnp.concatenate([vv[:, h * dh:(h + 1) * dh], ones_pad], axis=1)
            bc = cs_col[:, M_HEADS + h:M_HEADS + h + 1]
            ic = gcol[:, h:h + 1]
            m_row = jnp.maximum(m, pm_col[:, h:h + 1])
            mt = bc + m_row
            inter = jnp.exp(m - m_row)
            sqk = _dot_nt(q, k) * jnp.exp(jnp.where(causal, w_row[h:h + 1, :] - m_row, -jnp.inf))
            nd = inter * _dot(q, cst.astype(BF16)) + _dot(sqk.astype(BF16), v_aug)
            num = nd[:, :dh]
            den = nd[:, dh:dh + 1]
            hs.append(num / jnp.maximum(jnp.abs(den), jnp.exp(-mt)))
            bl = bc[L - 1:L, :]
            gs = bl - bc + ic
            m_new = jnp.maximum(bl + m, jnp.max(gs, axis=0, keepdims=True))
            decay = jnp.exp(bl + m - m_new)
            wts = jnp.exp(gs - m_new)
            kw = (k.astype(F32) * wts).astype(BF16)
            new_carry.append((decay * cst + _dot_tn(kw, v_aug), m_new))
        hcat = jnp.concatenate(hs, axis=1)
        hcat = hcat * lax.rsqrt(_dot((hcat * hcat).astype(BF16), head_mean) + EPS) * og
        o = mo_ref[pl.ds(r0, L), :].astype(F32)
        y_ref[pl.ds(r0, L), :] = (hcat * _sigmoid(o)).astype(y_ref.dtype)
        return tuple(new_carry)

    init = tuple((jnp.zeros((dh, LANES), F32), jnp.zeros((1, 1), F32)) for _ in range(M_HEADS))
    lax.fori_loop(0, M_NCHUNK, chunk, init, unroll=2)


def _mlstm(mqk, mv, mo, gates, gatest, conv_w, conv_b, bias_row, bias_col, out_g):
    per_b = lambda b: (b, 0)
    const = lambda b: (0, 0)
    return pl.pallas_call(
        _mlstm_kernel,
        grid=(BATCH,),
        in_specs=[
            pl.BlockSpec((SEQ, 2 * M_WIDTH), per_b),
            pl.BlockSpec((SEQ, M_WIDTH), per_b),
            pl.BlockSpec((SEQ, M_WIDTH), per_b),
            pl.BlockSpec((SEQ, LANES), per_b),
            pl.BlockSpec((M_NCHUNK, 8, M_CHUNK), lambda b: (b, 0, 0)),
            pl.BlockSpec((CONV_K, 2 * M_WIDTH), const),
            pl.BlockSpec((1, 2 * M_WIDTH), const),
            pl.BlockSpec((1, LANES), const),
            pl.BlockSpec((8, 1), const),
            pl.BlockSpec((1, M_WIDTH), const),
        ],
        out_specs=pl.BlockSpec((SEQ, M_WIDTH), per_b),
        out_shape=jax.ShapeDtypeStruct((TOKENS, M_WIDTH), BF16),
        scratch_shapes=[
            pltpu.VMEM((SEQ, 2 * M_WIDTH), BF16),
            pltpu.VMEM((SEQ, LANES), F32),
            pltpu.VMEM((M_NCHUNK, 8, M_CHUNK), F32),
        ],
        compiler_params=pltpu.CompilerParams(
            dimension_semantics=("arbitrary",), vmem_limit_bytes=VMEM_LIMIT),
        name="mlstm",
    )(mqk, mv, mo, gates, gatest, conv_w, conv_b, bias_row, bias_col, out_g)


def _qk_norm_rope(x, gain, cos, sin):
    lane = lax.broadcasted_iota(jnp.int32, x.shape, 1)
    first = lane < A_HEAD_DIM
    sq = x * x
    tot = jnp.sum(sq, axis=-1, keepdims=True)
    s0 = jnp.sum(jnp.where(first, sq, 0.0), axis=-1, keepdims=True)
    ms = jnp.where(first, s0, tot - s0) * (1.0 / A_HEAD_DIM)
    xn = x * lax.rsqrt(ms + EPS) * gain
    in_low = (lane % A_HEAD_DIM) < ROPE_HALF
    swapped = jnp.where(in_low, pltpu.roll(xn, LANES - ROPE_HALF, 1), pltpu.roll(xn, ROPE_HALF, 1))
    return xn * cos + swapped * sin


def _moba_kernel(aq_ref, ak_ref, av_ref, cos_ref, sin_ref, qg_ref, kg_ref, og_ref, y_ref, kr_s, s_s):
    blk = MOBA_BLOCK
    dh = A_HEAD_DIM
    km_first = lax.broadcasted_iota(jnp.int32, (MOBA_NB, HEAD_PAIR), 1) < dh
    first = lax.broadcasted_iota(jnp.int32, (blk, HEAD_PAIR), 1) < dh
    blk_id = lax.broadcasted_iota(jnp.int32, (MOBA_NB, blk), 0)
    eye = (lax.broadcasted_iota(jnp.int32, (MOBA_NB, LANES), 0)
           == lax.broadcasted_iota(jnp.int32, (MOBA_NB, LANES), 1)).astype(F32)
    q_pos = lax.broadcasted_iota(jnp.int32, (2 * blk, blk), 0) % blk
    k_pos = lax.broadcasted_iota(jnp.int32, (2 * blk, blk), 1)
    causal = k_pos <= q_pos
    og = og_ref[...]

    km_rows = []
    prepped = {}

    def key_prep(j):
        rows = slice(j * blk, (j + 1) * blk)
        kr = _qk_norm_rope(ak_ref[rows, :].astype(F32), kg_ref[...], cos_ref[rows, :], sin_ref[rows, :])
        kr_s[rows, :] = kr.astype(BF16)
        km_rows.append(jnp.mean(kr, axis=0, keepdims=True))
        yield

    def query_prep(i):
        rows = slice(i * blk, (i + 1) * blk)
        qr = _qk_norm_rope(aq_ref[rows, :].astype(F32), qg_ref[...], cos_ref[rows, :], sin_ref[rows, :])
        qs = qr * (dh ** -0.5 * math.log2(math.e))
        q2 = jnp.concatenate([jnp.where(first, qs, 0.0), jnp.where(first, 0.0, qs)], axis=0).astype(BF16)
        yield

        bias = None
        if i > MOBA_TOPK:
            km = jnp.concatenate(km_rows[:i] + [jnp.zeros((MOBA_NB - i, HEAD_PAIR), F32)], axis=0)
            km_heads = (jnp.where(km_first, km, 0.0), jnp.where(km_first, 0.0, km))
            past = blk_id < i
            cols = []
            for kmh in km_heads:
                gate = jnp.where(past, _dot_nt(kmh, qr, precision=HIGHEST), -jnp.inf)
                rank = jnp.zeros((MOBA_NB, blk), F32)
                for m_blk in range(i):
                    gm = gate[m_blk:m_blk + 1, :]
                    ahead = (gm > gate) | ((gm == gate) & (m_blk < blk_id))
                    rank = rank + jnp.where(ahead, 1.0, 0.0)
                sel = jnp.where(past & (rank < float(MOBA_TOPK)), 1.0, 0.0)
                cols.append(_dot_tn(sel, eye))
            bias = jnp.where(jnp.concatenate(cols, axis=0) > 0.5, 0.0, -jnp.inf)
        prepped[i] = (q2, bias)
        yield

    def scores(i, out):
        q2, bias = prepped.pop(i)
        m_run = None
        for j in range(i + 1):
            s = _dot_nt(q2, kr_s[j * blk:(j + 1) * blk, :])
            if j == i:
                s = jnp.where(causal, s, -jnp.inf)
            elif bias is not None:
                s = s + bias[:, j:j + 1]
            s_s[i % 2, :, j * blk:(j + 1) * blk] = s
            part = jnp.maximum(s[:, :LANES], s[:, LANES:])
            m_run = part if m_run is None else jnp.maximum(m_run, part)
            yield
        out.append(jnp.max(m_run, axis=1, keepdims=True))

    def attend(i, m_fin):
        rows = slice(i * blk, (i + 1) * blk)
        l_run = jnp.zeros((2 * blk, LANES), F32)
        acc = jnp.zeros((2 * blk, HEAD_PAIR), F32)
        m_lanes = jnp.broadcast_to(m_fin, (2 * blk, LANES))
        for j in range(i + 1):
            p_lo = jnp.exp2(s_s[i % 2, :, j * blk:j * blk + LANES] - m_lanes)
            p_hi = jnp.exp2(s_s[i % 2, :, j * blk + LANES:(j + 1) * blk] - m_lanes)
            l_run = l_run + (p_lo + p_hi)
            p = jnp.concatenate([p_lo, p_hi], axis=1)
            acc = acc + _dot(p.astype(BF16), av_ref[j * blk:(j + 1) * blk, :])
            yield
        o2 = acc / jnp.sum(l_run, axis=1, keepdims=True)
        o = jnp.where(first, o2[:blk, :], o2[blk:, :])
        sq = o * o
        tot = jnp.sum(sq, axis=-1, keepdims=True)
        s0 = jnp.sum(jnp.where(first, sq, 0.0), axis=-1, keepdims=True)
        ms = jnp.where(first, s0, tot - s0) * (1.0 / dh)
        y_ref[rows, :] = (o * lax.rsqrt(ms + EPS) * og).astype(y_ref.dtype)

    def interleave(*stages):
        stages = list(stages)
        while stages:
            for stage in list(stages):
                try:
                    next(stage)
                except StopIteration:
                    stages.remove(stage)

    m_cur = []
    interleave(key_prep(0), key_prep(1), query_prep(0), query_prep(1))
    interleave(scores(0, m_cur))
    for i in range(MOBA_NB):
        m_next = []
        stages = [attend(i, m_cur[0])]
        if i + 1 < MOBA_NB:
            stages.insert(0, scores(i + 1, m_next))
        if i + 2 < MOBA_NB:
            stages += [key_prep(i + 2), query_prep(i + 2)]
        interleave(*stages)
        m_cur = m_next


def _moba(aq, ak, av, cos_t, sin_t, q_gain, k_gain, out_gain):
    n_pairs = A_HEADS // 2
    const = lambda b, p: (0, 0)
    pair = lambda b, p: (b, p)
    return pl.pallas_call(
        _moba_kernel,
        grid=(BATCH, n_pairs),
        in_specs=[
            pl.BlockSpec((SEQ, HEAD_PAIR), pair),
            pl.BlockSpec((SEQ, HEAD_PAIR), pair),
            pl.BlockSpec((SEQ, HEAD_PAIR), pair),
            pl.BlockSpec((SEQ, HEAD_PAIR), const),
            pl.BlockSpec((SEQ, HEAD_PAIR), const),
            pl.BlockSpec((1, HEAD_PAIR), const),
            pl.BlockSpec((1, HEAD_PAIR), const),
            pl.BlockSpec((None, 1, HEAD_PAIR), lambda b, p: (p, 0, 0)),
        ],
        out_specs=pl.BlockSpec((SEQ, HEAD_PAIR), pair),
        out_shape=jax.ShapeDtypeStruct((TOKENS, A_WIDTH), BF16),
        scratch_shapes=[
            pltpu.VMEM((SEQ, HEAD_PAIR), BF16),
            pltpu.VMEM((2, 2 * MOBA_BLOCK, SEQ), F32),
        ],
        compiler_params=pltpu.CompilerParams(
            dimension_semantics=("arbitrary", "arbitrary"), vmem_limit_bytes=VMEM_LIMIT),
        name="moba",
    )(aq, ak, av, cos_t, sin_t, q_gain, k_gain, out_gain)


def _s5_kernel(u_ref, abar_ref, bre_ref, bim_ref, cre_ref, cim_ref, d_ref, gw_ref, gb_ref, gmat_ref, og_ref,
               y_ref, xr_s, xi_s, st_s, u_s, y_s):
    step = pl.program_id(0)
    rows = S5_STEPS * BATCH
    tile = 256
    n_tiles = rows // tile
    halves = S_WIDTH // LANES

    @pl.when(step == 0)
    def _init():
        st_s[...] = jnp.zeros_like(st_s)

    for b in range(BATCH):
        for hv in range(halves):
            lanes = slice(b * S_WIDTH + hv * LANES, b * S_WIDTH + (hv + 1) * LANES)
            u_s[hv, pl.ds(b, S5_STEPS, stride=BATCH), :] = u_ref[:, lanes].astype(F32)

    def u_rows(r):
        return jnp.concatenate([u_s[hv, r * tile:(r + 1) * tile, :] for hv in range(halves)], axis=1)

    for r in range(n_tiles):
        u = u_rows(r).astype(BF16)
        t0 = r * tile // BATCH
        xr_s[t0:t0 + tile // BATCH] = _dot(u, bre_ref[...]).reshape(tile // BATCH, BATCH, S_FLAT)
        xi_s[t0:t0 + tile // BATCH] = _dot(u, bim_ref[...]).reshape(tile // BATCH, BATCH, S_FLAT)

    ar = jnp.broadcast_to(abar_ref[0:1, :], (BATCH, S_FLAT))
    ai = jnp.broadcast_to(abar_ref[1:2, :], (BATCH, S_FLAT))

    def scan_step(t, carry):
        sr, si = carry
        nr = ar * sr - ai * si + xr_s[t]
        ni = ar * si + ai * sr + xi_s[t]
        xr_s[t] = nr
        xi_s[t] = ni
        return nr, ni

    sr, si = lax.fori_loop(0, S5_STEPS, scan_step, (st_s[0], st_s[1]), unroll=4)
    st_s[0] = sr
    st_s[1] = si

    for r in range(n_tiles):
        t0 = r * tile // BATCH
        xr = xr_s[t0:t0 + tile // BATCH].reshape(tile, S_FLAT).astype(BF16)
        xi = xi_s[t0:t0 + tile // BATCH].reshape(tile, S_FLAT).astype(BF16)
        y = _dot(xr, cre_ref[...]) - _dot(xi, cim_ref[...]) + d_ref[...] * u_rows(r)
        z = 0.5 * y * (1.0 + jnp.tanh(math.sqrt(2.0 / math.pi) * (y + 0.044715 * (y * y * y))))
        y2 = z * _sigmoid(_dot(z.astype(BF16), gw_ref[...]) + gb_ref[...])
        ms = _dot((y2 * y2).astype(BF16), gmat_ref[...])
        yn = y2 * lax.rsqrt(ms + EPS) * og_ref[...]
        for hv in range(halves):
            y_s[hv, r * tile:(r + 1) * tile, :] = yn[:, hv * LANES:(hv + 1) * LANES]

    for b in range(BATCH):
        for hv in range(halves):
            lanes = slice(b * S_WIDTH + hv * LANES, b * S_WIDTH + (hv + 1) * LANES)
            y_ref[:, lanes] = y_s[hv, pl.ds(b, S5_STEPS, stride=BATCH), :].astype(y_ref.dtype)


def _s5(u_sb, abar, bre, bim, cre, cim, d_row, glu_w, glu_b, gmat, out_g):
    rows = S5_STEPS * BATCH
    const = lambda s: (0, 0)
    return pl.pallas_call(
        _s5_kernel,
        grid=(SEQ // S5_STEPS,),
        in_specs=[
            pl.BlockSpec((S5_STEPS, BATCH * S_WIDTH), lambda s: (s, 0)),
            pl.BlockSpec((2, S_FLAT), const),
            pl.BlockSpec((S_WIDTH, S_FLAT), const),
            pl.BlockSpec((S_WIDTH, S_FLAT), const),
            pl.BlockSpec((S_FLAT, S_WIDTH), const),
            pl.BlockSpec((S_FLAT, S_WIDTH), const),
            pl.BlockSpec((1, S_WIDTH), const),
            pl.BlockSpec((S_WIDTH, S_WIDTH), const),
            pl.BlockSpec((1, S_WIDTH), const),
            pl.BlockSpec((S_WIDTH, S_WIDTH), const),
            pl.BlockSpec((1, S_WIDTH), const),
        ],
        out_specs=pl.BlockSpec((S5_STEPS, BATCH * S_WIDTH), lambda s: (s, 0)),
        out_shape=jax.ShapeDtypeStruct((SEQ, BATCH * S_WIDTH), BF16),
        scratch_shapes=[
            pltpu.VMEM((S5_STEPS, BATCH, S_FLAT), F32),
            pltpu.VMEM((S5_STEPS, BATCH, S_FLAT), F32),
            pltpu.VMEM((2, BATCH, S_FLAT), F32),
            pltpu.VMEM((S_WIDTH // LANES, rows, LANES), F32),
            pltpu.VMEM((S_WIDTH // LANES, rows, LANES), F32),
        ],
        compiler_params=pltpu.CompilerParams(
            dimension_semantics=("arbitrary",), vmem_limit_bytes=VMEM_LIMIT),
        name="s5",
    )(u_sb, abar, bre, bim, cre, cim, d_row, glu_w, glu_b, gmat, out_g)


def _outproj_kernel(x_ref, ym_ref, ya_ref, ys_ref, w_ref, g_ref, wr_ref, br_ref,
                    x1_ref, xn_ref, route_ref):
    x1 = x_ref[...]
    x1 = x1 + _dot(ym_ref[...], w_ref[0:M_WIDTH, :])
    x1 = x1 + _dot(ya_ref[...], w_ref[M_WIDTH:M_WIDTH + A_WIDTH, :])
    x1 = x1 + _dot(ys_ref[...], w_ref[M_WIDTH + A_WIDTH:, :])
    x1_ref[...] = x1
    xn = x1 * lax.rsqrt(jnp.mean(x1 * x1, axis=-1, keepdims=True) + EPS) * g_ref[...]
    for j in range(ROW_TILES):
        xn_ref[pl.ds(j, TM_PROJ, stride=ROW_TILES), :] = xn[:, j * LANES:(j + 1) * LANES]
    logits = _split_dot(xn, wr_ref[...]) + br_ref[...]

    lane = lax.broadcasted_iota(jnp.int32, logits.shape, 1).astype(F32)
    is_grp = lane < N_GROUPS
    neg = -jnp.inf
    gl = jnp.where(is_grp, logits, neg)
    gmax = jnp.max(gl, axis=-1, keepdims=True)
    gsum = jnp.sum(jnp.where(is_grp, jnp.exp(logits - gmax), 0.0), axis=-1, keepdims=True)
    g_sel = jnp.min(jnp.where(gl == gmax, lane, float(LANES)), axis=-1, keepdims=True)
    lo = N_GROUPS + EXPERTS_PER_GROUP * g_sel
    el = jnp.where((lane >= lo) & (lane < lo + EXPERTS_PER_GROUP), logits, neg)
    t1 = jnp.max(el, axis=-1, keepdims=True)
    i1 = jnp.min(jnp.where(el == t1, lane, float(LANES)), axis=-1, keepdims=True)
    el2 = jnp.where(lane == i1, neg, el)
    t2 = jnp.max(el2, axis=-1, keepdims=True)
    i2 = jnp.min(jnp.where(el2 == t2, lane, float(LANES)), axis=-1, keepdims=True)
    e21 = jnp.exp(t2 - t1)
    w1 = 1.0 / ((1.0 + e21) * gsum)
    w2 = e21 / ((1.0 + e21) * gsum)
    route = jnp.where(lane == 0.0, i1 - N_GROUPS,
                      jnp.where(lane == 1.0, i2 - N_GROUPS,
                                jnp.where(lane == 2.0, w1, jnp.where(lane == 3.0, w2, 0.0))))
    route_ref[...] = route


def _outproj(x, y_m, y_a, y_s_sb, w_out, g2, w_router, b_router):
    n_s = SEQ // TM_PROJ
    row = lambda b, s: (b * n_s + s, 0)
    const = lambda b, s: (0, 0)
    return pl.pallas_call(
        _outproj_kernel,
        grid=(BATCH, n_s),
        in_specs=[
            pl.BlockSpec((TM_PROJ, D_MODEL), row),
            pl.BlockSpec((TM_PROJ, M_WIDTH), row),
            pl.BlockSpec((TM_PROJ, A_WIDTH), row),
            pl.BlockSpec((TM_PROJ, S_WIDTH), lambda b, s: (s, b)),
            pl.BlockSpec((D_MIX, D_MODEL), const),
            pl.BlockSpec((1, D_MODEL), const),
            pl.BlockSpec((D_MODEL, 2 * LANES), const),
            pl.BlockSpec((1, LANES), const),
        ],
        out_specs=(
            pl.BlockSpec((TM_PROJ, D_MODEL), row),
            pl.BlockSpec((TM_PROJ * ROW_TILES, LANES), row),
            pl.BlockSpec((TM_PROJ, LANES), row),
        ),
        out_shape=(
            jax.ShapeDtypeStruct((TOKENS, D_MODEL), F32),
            jax.ShapeDtypeStruct((TOKENS * ROW_TILES, LANES), F32),
            jax.ShapeDtypeStruct((TOKENS, LANES), F32),
        ),
        compiler_params=pltpu.CompilerParams(
            dimension_semantics=("arbitrary", "arbitrary"), vmem_limit_bytes=VMEM_LIMIT),
        name="outproj",
    )(x, y_m, y_a, y_s_sb, w_out, g2, w_router, b_router)


def _ffn_kernel(be_ref, xs_ref, wg_ref, wu_ref, wd_ref, ys_ref):
    del be_ref
    xb = jnp.concatenate([xs_ref[pl.ds(j, FFN_BLOCK, stride=ROW_TILES), :] for j in range(ROW_TILES)],
                         axis=1).astype(BF16)
    g = _dot(xb, wg_ref[...])
    u = _dot(xb, wu_ref[...])
    hmid = (g * _sigmoid(g) * u).astype(BF16)
    y = _dot(hmid, wd_ref[...])
    for j in range(ROW_TILES):
        ys_ref[pl.ds(j, FFN_BLOCK, stride=ROW_TILES), :] = y[:, j * LANES:(j + 1) * LANES]


def _expert_ffn(layer, blk_exp, xs, w_gate, w_up, w_down):
    nblk = blk_exp.shape[0]
    rows = pl.BlockSpec((FFN_BLOCK * ROW_TILES, LANES), lambda i, be: (i, 0))
    grid_spec = pltpu.PrefetchScalarGridSpec(
        num_scalar_prefetch=1,
        grid=(nblk,),
        in_specs=[
            rows,
            pl.BlockSpec((None, None, D_MODEL, D_EXPERT), lambda i, be: (layer, be[i], 0, 0)),
            pl.BlockSpec((None, None, D_MODEL, D_EXPERT), lambda i, be: (layer, be[i], 0, 0)),
            pl.BlockSpec((None, None, D_EXPERT, D_MODEL), lambda i, be: (layer, be[i], 0, 0)),
        ],
        out_specs=rows,
    )
    return pl.pallas_call(
        _ffn_kernel,
        grid_spec=grid_spec,
        out_shape=jax.ShapeDtypeStruct((nblk * FFN_BLOCK * ROW_TILES, LANES), F32),
        compiler_params=pltpu.CompilerParams(
            dimension_semantics=("arbitrary",), vmem_limit_bytes=VMEM_LIMIT),
        name="expert_ffn",
    )(blk_exp, xs, w_gate, w_up, w_down)


def _plan_kernel(e_ref, ps_ref, dest_ref, run_s, tri_s):
    i = pl.program_id(0)

    @pl.when(i == 0)
    def _init():
        run_s[...] = ps_ref[...]
        s_i = lax.broadcasted_iota(jnp.int32, (PLAN_TILE, PLAN_TILE), 0)
        t_i = lax.broadcasted_iota(jnp.int32, (PLAN_TILE, PLAN_TILE), 1)
        tri_s[...] = (s_i < t_i).astype(BF16)

    expert = lax.broadcasted_iota(jnp.int32, (N_EXPERTS, PLAN_TILE), 0)
    onehot = jnp.where(e_ref[...] == expert, 1.0, 0.0)
    before = _dot(onehot.astype(BF16), tri_s[...])
    dest = jnp.sum(onehot * (before + run_s[:, 0:1]), axis=0, keepdims=True)
    dest_ref[...] = dest.astype(jnp.int32)
    run_s[...] = run_s[...] + jnp.sum(onehot, axis=1, keepdims=True)


def _plan(e_rows, pstart_col):
    n_tiles = e_rows.shape[0]
    return pl.pallas_call(
        _plan_kernel,
        grid=(n_tiles,),
        in_specs=[
            pl.BlockSpec((None, 1, PLAN_TILE), lambda i: (i, 0, 0)),
            pl.BlockSpec((N_EXPERTS, LANES), lambda i: (0, 0)),
        ],
        out_specs=pl.BlockSpec((None, 1, PLAN_TILE), lambda i: (i, 0, 0)),
        out_shape=jax.ShapeDtypeStruct((n_tiles, 1, PLAN_TILE), jnp.int32),
        scratch_shapes=[pltpu.VMEM((N_EXPERTS, LANES), F32), pltpu.VMEM((PLAN_TILE, PLAN_TILE), BF16)],
        compiler_params=pltpu.CompilerParams(
            dimension_semantics=("arbitrary",), vmem_limit_bytes=VMEM_LIMIT),
        name="moe_plan",
    )(e_rows, pstart_col)


def _dispatch_kernel(fill_ref, d0_ref, d1_ref, xn_ref, xs_hbm, zero_s, sem, fill_sem):
    @pl.when(pl.program_id(0) == 0)
    def _zero_padding():
        zero_s[...] = jnp.zeros_like(zero_s)

        def fill(c):
            start = pl.multiple_of(fill_ref[0, c] * ROW_TILES, ROW_TILES)
            return pltpu.make_async_copy(zero_s, xs_hbm.at[pl.ds(start, FFN_BLOCK * ROW_TILES), :], fill_sem)

        for c in range(N_FILLS):
            pl.when(fill_ref[0, c] >= 0)(lambda c=c: fill(c).start())
        for c in range(N_FILLS):
            pl.when(fill_ref[0, c] >= 0)(lambda c=c: fill(c).wait())

    def row_copy(dref, t):
        src = xn_ref.at[pl.ds(pl.multiple_of(t * ROW_TILES, ROW_TILES), ROW_TILES), :]
        dst = xs_hbm.at[pl.ds(pl.multiple_of(dref[0, t] * ROW_TILES, ROW_TILES), ROW_TILES), :]
        return pltpu.make_async_copy(src, dst, sem)

    def start(t, carry):
        row_copy(d0_ref, t).start()
        row_copy(d1_ref, t).start(priority=1)
        return carry

    lax.fori_loop(0, DISPATCH_TOKENS, start, 0, unroll=8)
    for _ in range(EXPERT_TOPK):
        pltpu.make_async_copy(xn_ref, xs_hbm.at[pl.ds(0, DISPATCH_TOKENS * ROW_TILES), :], sem).wait()


def _dispatch(fill_rows, dest, xn, n_rows):
    n_tiles = TOKENS // DISPATCH_TOKENS
    dest4 = dest.reshape(EXPERT_TOPK, n_tiles, 1, DISPATCH_TOKENS)
    smem_tile = lambda k: pl.BlockSpec((None, None, 1, DISPATCH_TOKENS), lambda i: (k, i, 0, 0),
                                       memory_space=pltpu.SMEM)
    return pl.pallas_call(
        _dispatch_kernel,
        grid=(n_tiles,),
        in_specs=[
            pl.BlockSpec(memory_space=pltpu.SMEM),
            smem_tile(0),
            smem_tile(1),
            pl.BlockSpec((DISPATCH_TOKENS * ROW_TILES, LANES), lambda i: (i, 0)),
        ],
        out_specs=pl.BlockSpec(memory_space=pl.ANY),
        out_shape=jax.ShapeDtypeStruct((n_rows * ROW_TILES, LANES), F32),
        scratch_shapes=[
            pltpu.VMEM((FFN_BLOCK * ROW_TILES, LANES), F32),
            pltpu.SemaphoreType.DMA(()),
            pltpu.SemaphoreType.DMA(()),
        ],
        compiler_params=pltpu.CompilerParams(
            dimension_semantics=("arbitrary",), vmem_limit_bytes=VMEM_LIMIT),
        name="moe_dispatch",
    )(fill_rows, dest4, dest4, xn)


def _combine_kernel(d0_ref, d1_ref, d0_next_ref, d1_next_ref, x1_ref, route_ref, ys_hbm, out_ref, buf, sem):
    i = pl.program_id(0)
    n = pl.num_programs(0)

    def start_tile(drefs, slot):
        def body(t, carry):
            for k in range(EXPERT_TOPK):
                src = ys_hbm.at[pl.ds(pl.multiple_of(drefs[k][0, t] * ROW_TILES, ROW_TILES), ROW_TILES), :]
                dst = buf.at[slot, k, pl.ds(pl.multiple_of(t * ROW_TILES, ROW_TILES), ROW_TILES), :]
                pltpu.make_async_copy(src, dst, sem.at[slot]).start(priority=k)
            return carry
        lax.fori_loop(0, COMBINE_TOKENS, body, 0, unroll=8)

    slot = i % 2

    @pl.when(i == 0)
    def _first():
        start_tile((d0_ref, d1_ref), 0)

    @pl.when(i + 1 < n)
    def _prefetch():
        start_tile((d0_next_ref, d1_next_ref), 1 - slot)

    for k in range(EXPERT_TOPK):
        pltpu.make_async_copy(ys_hbm.at[pl.ds(0, COMBINE_TOKENS * ROW_TILES), :], buf.at[slot, k],
                              sem.at[slot]).wait()
    w = route_ref[...]
    w1 = w[:, 2:3]
    w2 = w[:, 3:4]
    for j in range(ROW_TILES):
        cols = slice(j * LANES, (j + 1) * LANES)
        chunk = pl.ds(j, COMBINE_TOKENS, stride=ROW_TILES)
        out_ref[:, cols] = x1_ref[:, cols] + w1 * buf[slot, 0, chunk, :] + w2 * buf[slot, 1, chunk, :]


def _combine(dest, x1, route, ys3):
    n_tiles = TOKENS // COMBINE_TOKENS
    dest4 = dest.reshape(EXPERT_TOPK, n_tiles, 1, COMBINE_TOKENS)
    smem_tile = lambda k, nxt: pl.BlockSpec(
        (None, None, 1, COMBINE_TOKENS), lambda i: (k, jnp.minimum(i + nxt, n_tiles - 1), 0, 0),
        memory_space=pltpu.SMEM)
    return pl.pallas_call(
        _combine_kernel,
        grid=(n_tiles,),
        in_specs=[
            smem_tile(0, 0), smem_tile(1, 0), smem_tile(0, 1), smem_tile(1, 1),
            pl.BlockSpec((COMBINE_TOKENS, D_MODEL), lambda i: (i, 0)),
            pl.BlockSpec((COMBINE_TOKENS, LANES), lambda i: (i, 0)),
            pl.BlockSpec(memory_space=pl.ANY),
        ],
        out_specs=pl.BlockSpec((COMBINE_TOKENS, D_MODEL), lambda i: (i, 0)),
        out_shape=jax.ShapeDtypeStruct((TOKENS, D_MODEL), F32),
        scratch_shapes=[
            pltpu.VMEM((2, EXPERT_TOPK, COMBINE_TOKENS * ROW_TILES, LANES), F32),
            pltpu.SemaphoreType.DMA((2,)),
        ],
        compiler_params=pltpu.CompilerParams(
            dimension_semantics=("arbitrary",), vmem_limit_bytes=VMEM_LIMIT),
        name="moe_combine",
    )(dest4, dest4, dest4, dest4, x1, route, ys3)


def _rope_tables():
    inv = ROPE_THETA ** (-np.arange(ROPE_HALF, dtype=np.float64) * 2.0 / ROPE_DIM)
    ang = np.arange(SEQ, dtype=np.float64)[:, None] * inv[None, :]
    cos = np.ones((SEQ, A_HEAD_DIM))
    sin = np.zeros((SEQ, A_HEAD_DIM))
    cos[:, :ROPE_HALF] = np.cos(ang)
    cos[:, ROPE_HALF:ROPE_DIM] = np.cos(ang)
    sin[:, :ROPE_HALF] = -np.sin(ang)
    sin[:, ROPE_HALF:ROPE_DIM] = np.sin(ang)
    return (jnp.asarray(np.tile(cos, (1, 2)), F32), jnp.asarray(np.tile(sin, (1, 2)), F32))


def _split_w_in(w_in):
    sizes = [M_WIDTH, M_WIDTH, M_WIDTH, M_WIDTH, M_HEADS, M_HEADS, A_WIDTH, A_WIDTH, A_WIDTH, S_WIDTH]
    offs = np.cumsum([0] + sizes)
    mq, mk, mv, mo, mi, mf, aq, ak, av, su = (w_in[:, offs[n]:offs[n + 1]] for n in range(len(sizes)))
    w_main = jnp.concatenate([aq, ak, av, mq, mk, mv, mo, su], axis=1).astype(BF16)
    gate = jnp.concatenate([mi, mf], axis=1)
    return w_main, _hi_lo(jnp.pad(gate, ((0, 0), (0, LANES - 2 * M_HEADS))))


def _s5_params(a_re, a_im, b_re, b_im, c_re, c_im, log_dt):
    dt = jnp.exp(log_dt)[:, None]
    mag = jnp.exp(a_re * dt)
    abar_re, abar_im = mag * jnp.cos(a_im * dt), mag * jnp.sin(a_im * dt)
    zr, zi = abar_re - 1.0, abar_im
    den = a_re * a_re + a_im * a_im
    fr, fi = (zr * a_re + zi * a_im) / den, (zi * a_re - zr * a_im) / den
    bbar_re = fr[..., None] * b_re - fi[..., None] * b_im
    bbar_im = fr[..., None] * b_im + fi[..., None] * b_re
    eye = jnp.eye(S_GROUPS, dtype=F32)
    dense_b = lambda t: jnp.einsum('gph,gk->ghkp', t, eye).reshape(S_WIDTH, S_FLAT).astype(BF16)
    dense_c = lambda t: jnp.einsum('ghp,gk->gpkh', t, eye).reshape(S_FLAT, S_WIDTH).astype(BF16)
    abar = jnp.stack([abar_re.reshape(S_FLAT), abar_im.reshape(S_FLAT)])
    return abar, dense_b(bbar_re), dense_b(bbar_im), dense_c(c_re), dense_c(c_im)


def _moe(layer, x1, xn, route, w_gate, w_up, w_down):
    n_assign = TOKENS * EXPERT_TOPK
    e_flat = route[:, :EXPERT_TOPK].astype(jnp.int32).T.reshape(n_assign)
    experts = jnp.arange(N_EXPERTS, dtype=jnp.int32)
    counts = jnp.sum((e_flat[:, None] == experts[None, :]).astype(jnp.int32), axis=0)
    padded = (counts + FFN_BLOCK - 1) // FFN_BLOCK * FFN_BLOCK
    pends = jnp.cumsum(padded)
    pstarts = pends - padded
    n_rows = (n_assign + N_EXPERTS * (FFN_BLOCK - 1) + FFN_BLOCK - 1) // FFN_BLOCK * FFN_BLOCK
    nblk = n_rows // FFN_BLOCK
    blk_start = jnp.arange(nblk, dtype=jnp.int32) * FFN_BLOCK
    blk_exp = jnp.minimum(jnp.sum((pends[None, :] <= blk_start[:, None]).astype(jnp.int32), axis=1),
                          N_EXPERTS - 1).astype(jnp.int32)
    pstart_col = jnp.broadcast_to(pstarts.astype(F32)[:, None], (N_EXPERTS, LANES))
    dest = _plan(e_flat.reshape(n_assign // PLAN_TILE, 1, PLAN_TILE), pstart_col)
    tail = pends[-1] + jnp.arange(N_EXPERTS, dtype=jnp.int32) * FFN_BLOCK
    fill_rows = jnp.concatenate([jnp.where(padded > 0, pends - FFN_BLOCK, -1), jnp.where(tail < n_rows, tail, -1)])
    fill_rows = fill_rows.astype(jnp.int32).reshape(1, N_FILLS)
    xs = _dispatch(fill_rows, dest, xn, n_rows)
    ys = _expert_ffn(layer, blk_exp, xs, w_gate, w_up, w_down)
    return _combine(dest, x1, route, ys)


def kernel(x, norm1_g, w_in, m_bias_i, m_bias_f, m_conv_w, m_conv_b, m_out_g, a_q_g, a_k_g, a_out_g, s_a_re, s_a_im, s_b_re, s_b_im, s_c_re, s_c_im, s_d, s_log_dt, s_glu_w, s_glu_b, s_out_g, w_out, norm2_g, r_group_w, r_group_b, r_expert_w, r_expert_b, e_w_gate, e_w_up, e_w_down):
    cos_t, sin_t = _rope_tables()
    gmat = jnp.asarray(np.kron(np.eye(S_GROUPS), np.full((S_GROUP_DIM, S_GROUP_DIM), 1.0 / S_GROUP_DIM)), BF16)
    xf = x.reshape(TOKENS, D_MODEL)
    for l in range(DEPTH):
        w_main, w_gate = _split_w_in(w_in[l])
        aq, ak, av, mqk, mv, mo, su, gates, gatest = _inproj(xf, norm1_g[l].reshape(1, D_MODEL), w_main, w_gate)

        bias = jnp.concatenate([m_bias_i[l], m_bias_f[l]])
        bias_row = jnp.pad(bias, (0, LANES - 2 * M_HEADS)).reshape(1, LANES)
        y_m = _mlstm(mqk, mv, mo, gates, gatest, m_conv_w[l], m_conv_b[l].reshape(1, 2 * M_WIDTH),
                     bias_row, bias.reshape(2 * M_HEADS, 1), m_out_g[l].reshape(1, M_WIDTH))

        y_a = _moba(aq, ak, av, cos_t, sin_t,
                    jnp.tile(a_q_g[l], 2).reshape(1, HEAD_PAIR), jnp.tile(a_k_g[l], 2).reshape(1, HEAD_PAIR),
                    a_out_g[l].reshape(A_HEADS // 2, 1, HEAD_PAIR))

        abar, bre, bim, cre, cim = _s5_params(s_a_re[l], s_a_im[l], s_b_re[l], s_b_im[l],
                                              s_c_re[l], s_c_im[l], s_log_dt[l])
        y_s = _s5(su, abar, bre, bim, cre, cim,
                  s_d[l].reshape(1, S_WIDTH), s_glu_w[l].astype(BF16), s_glu_b[l].reshape(1, S_WIDTH),
                  gmat, s_out_g[l].reshape(1, S_WIDTH))

        w_router = jnp.pad(jnp.concatenate([r_group_w[l], r_expert_w[l]], axis=1),
                           ((0, 0), (0, LANES - N_GROUPS - N_EXPERTS)))
        b_router = jnp.pad(jnp.concatenate([r_group_b[l], r_expert_b[l]]),
                           (0, LANES - N_GROUPS - N_EXPERTS)).reshape(1, LANES)
        x1, xn, route = _outproj(xf, y_m, y_a, y_s, w_out[l].astype(BF16),
                                 norm2_g[l].reshape(1, D_MODEL), _hi_lo(w_router), b_router)
        xf = _moe(l, x1, xn, route, e_w_gate, e_w_up, e_w_down)
    return xf.reshape(BATCH, SEQ, D_MODEL)
```

```python
import functools
import math

import numpy as np
import jax
import jax.numpy as jnp
from jax import lax
from jax.experimental import pallas as pl
from jax.experimental.pallas import tpu as pltpu

F32 = jnp.float32
BF16 = jnp.bfloat16

D_MODEL = 1024
BATCH = 8
SEQ = 2048
DEPTH = 2
TOKENS = BATCH * SEQ

M_HEADS = 4
M_HEAD_DIM = 64
M_WIDTH = M_HEADS * M_HEAD_DIM
A_HEADS = 8
A_HEAD_DIM = 64
A_WIDTH = A_HEADS * A_HEAD_DIM
S_GROUPS = 16
S_GROUP_DIM = 16
S_WIDTH = S_GROUPS * S_GROUP_DIM
S_STATE = 64
S_FLAT = S_GROUPS * S_STATE
D_MIX = M_WIDTH + A_WIDTH + S_WIDTH

CONV_K = 4
MOBA_BLOCK = 256
MOBA_NB = SEQ // MOBA_BLOCK
MOBA_TOPK = 3
ROPE_THETA = 500000.0
ROPE_DIM = A_HEAD_DIM // 4
ROPE_HALF = ROPE_DIM // 2

N_GROUPS = 4
EXPERTS_PER_GROUP = 8
N_EXPERTS = N_GROUPS * EXPERTS_PER_GROUP
EXPERT_TOPK = 2
D_EXPERT = 512
DISPATCH_BLOCK = 128
EPS = 1e-6

LANES = 128
ROW_TILES = D_MODEL // LANES
HEAD_PAIR = 2 * A_HEAD_DIM

W_MAIN = 3 * A_WIDTH + 4 * M_WIDTH + S_WIDTH
TM_PROJ = 512
M_CHUNK = 256
M_NCHUNK = SEQ // M_CHUNK
S5_STEPS = 256
FFN_BLOCK = 512
N_FILLS = 2 * N_EXPERTS
PLAN_TILE = 1024
DISPATCH_TOKENS = 512
COMBINE_TOKENS = 256
VMEM_LIMIT = 56 * 1024 * 1024

HIGHEST = lax.Precision.HIGHEST


def _dot(a, b, precision=None):
    return jnp.dot(a, b, preferred_element_type=F32, precision=precision)


def _dot_nt(a, b, precision=None):
    return lax.dot_general(a, b, (((1,), (1,)), ((), ())), preferred_element_type=F32, precision=precision)


def _dot_tn(a, b, precision=None):
    return lax.dot_general(a, b, (((0,), (0,)), ((), ())), preferred_element_type=F32, precision=precision)


def _split_dot(a, w_cat):
    n = w_cat.shape[1] // 2
    a_hi = a.astype(BF16)
    a_lo = (a - a_hi.astype(F32)).astype(BF16)
    both = _dot(a_hi, w_cat)
    return both[:, :n] + both[:, n:] + _dot(a_lo, w_cat[:, :n])


def _bf16_terms(x):
    hi = x.astype(BF16)
    rest = x - hi.astype(F32)
    mid = rest.astype(BF16)
    return hi, mid, (rest - mid.astype(F32)).astype(BF16)


def _hi_lo(w):
    w_hi = w.astype(BF16)
    return jnp.concatenate([w_hi, (w - w_hi.astype(F32)).astype(BF16)], axis=1)


def _log_sigmoid(x):
    return jnp.minimum(x, 0.0) - jnp.log1p(jnp.exp(-jnp.abs(x)))


def _sigmoid(x):
    return 1.0 / (1.0 + jnp.exp(-x))


def _inproj_kernel(x_ref, g_ref, w_ref, wg_ref,
                   aq_ref, ak_ref, av_ref, mqk_ref, mv_ref, mo_ref, su_ref, gates_ref, gatest_ref):
    x = x_ref[...]
    h = x * lax.rsqrt(jnp.mean(x * x, axis=-1, keepdims=True) + EPS) * g_ref[...]
    hb = h.astype(BF16)
    off = 0
    for ref, width in ((aq_ref, A_WIDTH), (ak_ref, A_WIDTH), (av_ref, A_WIDTH), (mqk_ref, 2 * M_WIDTH),
                       (mv_ref, M_WIDTH), (mo_ref, M_WIDTH), (su_ref, S_WIDTH)):
        ref[...] = _dot(hb, w_ref[:, off:off + width]).astype(ref.dtype)
        off += width
    gates = _split_dot(h, wg_ref[...])
    gates_ref[...] = gates
    gt = gates.T[:8, :]
    for c in range(TM_PROJ // M_CHUNK):
        gatest_ref[c] = gt[:, c * M_CHUNK:(c + 1) * M_CHUNK]


def _inproj(x, g, w_main, w_gate):
    n_s = SEQ // TM_PROJ
    row = lambda b, s: (b * n_s + s, 0)
    const = lambda b, s: (0, 0)
    out_shapes = (
        jax.ShapeDtypeStruct((TOKENS, A_WIDTH), BF16),
        jax.ShapeDtypeStruct((TOKENS, A_WIDTH), BF16),
        jax.ShapeDtypeStruct((TOKENS, A_WIDTH), BF16),
        jax.ShapeDtypeStruct((TOKENS, 2 * M_WIDTH), BF16),
        jax.ShapeDtypeStruct((TOKENS, M_WIDTH), BF16),
        jax.ShapeDtypeStruct((TOKENS, M_WIDTH), BF16),
        jax.ShapeDtypeStruct((SEQ, BATCH * S_WIDTH), BF16),
        jax.ShapeDtypeStruct((TOKENS, LANES), F32),
        jax.ShapeDtypeStruct((TOKENS // M_CHUNK, 8, M_CHUNK), F32),
    )
    out_specs = (
        pl.BlockSpec((TM_PROJ, A_WIDTH), row),
        pl.BlockSpec((TM_PROJ, A_WIDTH), row),
        pl.BlockSpec((TM_PROJ, A_WIDTH), row),
        pl.BlockSpec((TM_PROJ, 2 * M_WIDTH), row),
        pl.BlockSpec((TM_PROJ, M_WIDTH), row),
        pl.BlockSpec((TM_PROJ, M_WIDTH), row),
        pl.BlockSpec((TM_PROJ, S_WIDTH), lambda b, s: (s, b)),
        pl.BlockSpec((TM_PROJ, LANES), row),
        pl.BlockSpec((TM_PROJ // M_CHUNK, 8, M_CHUNK), lambda b, s: (b * n_s + s, 0, 0)),
    )
    return pl.pallas_call(
        _inproj_kernel,
        grid=(BATCH, n_s),
        in_specs=[
            pl.BlockSpec((TM_PROJ, D_MODEL), row),
            pl.BlockSpec((1, D_MODEL), const),
            pl.BlockSpec((D_MODEL, W_MAIN), const),
            pl.BlockSpec((D_MODEL, 2 * LANES), const),
        ],
        out_specs=out_specs,
        out_shape=out_shapes,
        compiler_params=pltpu.CompilerParams(
            dimension_semantics=("arbitrary", "arbitrary"), vmem_limit_bytes=VMEM_LIMIT),
        name="inproj",
    )(x, g, w_main, w_gate)


def _mlstm_kernel(mqk_ref, mv_ref, mo_ref, gates_ref, gatest_ref, cw_ref, cb_ref, brow_ref, bcol_ref, og_ref,
                  y_ref, qk_s, cols_s, wrow_s):
    L = M_CHUNK
    dh = M_HEAD_DIM
    r_i = lax.broadcasted_iota(jnp.int32, (L, L), 0)
    c_i = lax.broadcasted_iota(jnp.int32, (L, L), 1)
    causal = c_i <= r_i
    tri_u = (r_i <= c_i).astype(BF16)
    eye = (lax.broadcasted_iota(jnp.int32, (8, LANES), 0)
           == lax.broadcasted_iota(jnp.int32, (8, LANES), 1)).astype(BF16)
    w = cw_ref[...]
    cb = cb_ref[...]
    lane_q = lax.broadcasted_iota(jnp.int32, (1, 2 * M_WIDTH), 1) < M_WIDTH
    kscale = jnp.where(lane_q, 1.0, dh ** -0.5).astype(F32)
    for c in range(M_NCHUNK):
        halo = 16 if c > 0 else 0
        for strip in range(2 * M_WIDTH // LANES):
            cols = slice(strip * LANES, (strip + 1) * LANES)
            xt = mqk_ref[c * L - halo:(c + 1) * L, cols].astype(F32)
            rows = lax.broadcasted_iota(jnp.int32, xt.shape, 0)
            y = cb[:, cols] + w[CONV_K - 1:CONV_K, cols] * xt
            for k in range(1, CONV_K):
                sh = pltpu.roll(xt, k, 0)
                if c == 0:
                    sh = jnp.where(rows >= k, sh, 0.0)
                y = y + w[CONV_K - 1 - k:CONV_K - k, cols] * sh
            y = y[halo:, :]
            y = y * _sigmoid(y) * kscale[:, cols]
            qk_s[c * L:(c + 1) * L, cols] = y.astype(BF16)

    rows_all = M_NCHUNK * 8
    grow = gatest_ref[...].reshape(rows_all, L) + jnp.tile(bcol_ref[...], (M_NCHUNK, 1))
    cs_row = sum(_dot(term, tri_u) for term in _bf16_terms(_log_sigmoid(grow)))
    w_row = grow - pltpu.roll(cs_row, rows_all - M_HEADS, 0)
    lane_t = lax.broadcasted_iota(jnp.int32, (rows_all, L), 1)
    pm = w_row
    shift = 1
    while shift < L:
        pm = jnp.maximum(pm, jnp.where(lane_t >= shift, pltpu.roll(pm, shift, 1), -jnp.inf))
        shift *= 2
    wrow_s[...] = w_row.reshape(M_NCHUNK, 8, L)
    is_i_row = lax.broadcasted_iota(jnp.int32, (rows_all, L), 0) % 8 < M_HEADS
    terms = _bf16_terms(jnp.where(is_i_row, pm, cs_row))
    for c in range(M_NCHUNK):
        cols_s[c * L:(c + 1) * L, :] = sum(_dot_tn(term[c * 8:(c + 1) * 8, :], eye) for term in terms)

    pad_lane = lax.broadcasted_iota(jnp.int32, (L, LANES - dh), 1)
    ones_pad = jnp.where(pad_lane == 0, 1.0, 0.0).astype(BF16)
    og = og_ref[...]
    head_mean = jnp.where(lax.broadcasted_iota(jnp.int32, (M_WIDTH, M_WIDTH), 0) // dh
                          == lax.broadcasted_iota(jnp.int32, (M_WIDTH, M_WIDTH), 1) // dh,
                          1.0 / dh, 0.0).astype(BF16)

    def chunk(c, carry):
        r0 = pl.multiple_of(c * L, L)
        gcol = gates_ref[pl.ds(r0, L), :] + brow_ref[...]
        cs_col = pm_col = cols_s[pl.ds(r0, L), :]
        w_row = wrow_s[c]
        qk = qk_s[pl.ds(r0, L), :]
        vv = mv_ref[pl.ds(r0, L), :]
        new_carry = []
        hs = []
        for h in range(M_HEADS):
            cst, m = carry[h]
            q = qk[:, h * dh:(h + 1) * dh]
            k = qk[:, M_WIDTH + h * dh:M_WIDTH + (h + 1) * dh]
            v_aug = jnp.concatenate([vv[:, h * dh:(h + 1) * dh], ones_pad], axis=1)
            bc = cs_col[:, M_HEADS + h:M_HEADS + h + 1]
            ic = gcol[:, h:h + 1]
            m_row = jnp.maximum(m, pm_col[:, h:h + 1])
            mt = bc + m_row
            inter = jnp.exp(m - m_row)
            sqk = _dot_nt(q, k) * jnp.exp(jnp.where(causal, w_row[h:h + 1, :] - m_row, -jnp.inf))
            nd = inter * _dot(q, cst.astype(BF16)) + _dot(sqk.astype(BF16), v_aug)
            num = nd[:, :dh]
            den = nd[:, dh:dh + 1]
            hs.append(num / jnp.maximum(jnp.abs(den), jnp.exp(-mt)))
            bl = bc[L - 1:L, :]
            gs = bl - bc + ic
            m_new = jnp.maximum(bl + m, jnp.max(gs, axis=0, keepdims=True))
            decay = jnp.exp(bl + m - m_new)
            wts = jnp.exp(gs - m_new)
            kw = (k.astype(F32) * wts).astype(BF16)
            new_carry.append((decay * cst + _dot_tn(kw, v_aug), m_new))
        hcat = jnp.concatenate(hs, axis=1)
        hcat = hcat * lax.rsqrt(_dot((hcat * hcat).astype(BF16), head_mean) + EPS) * og
        o = mo_ref[pl.ds(r0, L), :].astype(F32)
        y_ref[pl.ds(r0, L), :] = (hcat * _sigmoid(o)).astype(y_ref.dtype)
        return tuple(new_carry)

    init = tuple((jnp.zeros((dh, LANES), F32), jnp.zeros((1, 1), F32)) for _ in range(M_HEADS))
    lax.fori_loop(0, M_NCHUNK, chunk, init, unroll=2)


def _mlstm(mqk, mv, mo, gates, gatest, conv_w, conv_b, bias_row, bias_col, out_g):
    per_b = lambda b: (b, 0)
    const = lambda b: (0, 0)
    return pl.pallas_call(
        _mlstm_kernel,
        grid=(BATCH,),
        in_specs=[
            pl.BlockSpec((SEQ, 2 * M_WIDTH), per_b),
            pl.BlockSpec((SEQ, M_WIDTH), per_b),
            pl.BlockSpec((SEQ, M_WIDTH), per_b),
            pl.BlockSpec((SEQ, LANES), per_b),
            pl.BlockSpec((M_NCHUNK, 8, M_CHUNK), lambda b: (b, 0, 0)),
            pl.BlockSpec((CONV_K, 2 * M_WIDTH), const),
            pl.BlockSpec((1, 2 * M_WIDTH), const),
            pl.BlockSpec((1, LANES), const),
            pl.BlockSpec((8, 1), const),
            pl.BlockSpec((1, M_WIDTH), const),
        ],
        out_specs=pl.BlockSpec((SEQ, M_WIDTH), per_b),
        out_shape=jax.ShapeDtypeStruct((TOKENS, M_WIDTH), BF16),
        scratch_shapes=[
            pltpu.VMEM((SEQ, 2 * M_WIDTH), BF16),
            pltpu.VMEM((SEQ, LANES), F32),
            pltpu.VMEM((M_NCHUNK, 8, M_CHUNK), F32),
        ],
        compiler_params=pltpu.CompilerParams(
            dimension_semantics=("arbitrary",), vmem_limit_bytes=VMEM_LIMIT),
        name="mlstm",
    )(mqk, mv, mo, gates, gatest, conv_w, conv_b, bias_row, bias_col, out_g)


def _qk_norm_rope(x, gain, cos, sin):
    lane = lax.broadcasted_iota(jnp.int32, x.shape, 1)
    first = lane < A_HEAD_DIM
    sq = x * x
    tot = jnp.sum(sq, axis=-1, keepdims=True)
    s0 = jnp.sum(jnp.where(first, sq, 0.0), axis=-1, keepdims=True)
    ms = jnp.where(first, s0, tot - s0) * (1.0 / A_HEAD_DIM)
    xn = x * lax.rsqrt(ms + EPS) * gain
    in_low = (lane % A_HEAD_DIM) < ROPE_HALF
    swapped = jnp.where(in_low, pltpu.roll(xn, LANES - ROPE_HALF, 1), pltpu.roll(xn, ROPE_HALF, 1))
    return xn * cos + swapped * sin


def _moba_kernel(aq_ref, ak_ref, av_ref, cos_ref, sin_ref, qg_ref, kg_ref, og_ref, y_ref, kr_s, s_s):
    blk = MOBA_BLOCK
    dh = A_HEAD_DIM
    km_first = lax.broadcasted_iota(jnp.int32, (MOBA_NB, HEAD_PAIR), 1) < dh
    first = lax.broadcasted_iota(jnp.int32, (blk, HEAD_PAIR), 1) < dh
    blk_id = lax.broadcasted_iota(jnp.int32, (MOBA_NB, blk), 0)
    eye = (lax.broadcasted_iota(jnp.int32, (MOBA_NB, LANES), 0)
           == lax.broadcasted_iota(jnp.int32, (MOBA_NB, LANES), 1)).astype(F32)
    q_pos = lax.broadcasted_iota(jnp.int32, (2 * blk, blk), 0) % blk
    k_pos = lax.broadcasted_iota(jnp.int32, (2 * blk, blk), 1)
    causal = k_pos <= q_pos
    og = og_ref[...]

    km_rows = []
    prepped = {}

    def key_prep(j):
        rows = slice(j * blk, (j + 1) * blk)
        kr = _qk_norm_rope(ak_ref[rows, :].astype(F32), kg_ref[...], cos_ref[rows, :], sin_ref[rows, :])
        kr_s[rows, :] = kr.astype(BF16)
        km_rows.append(jnp.mean(kr, axis=0, keepdims=True))
        yield

    def query_prep(i):
        rows = slice(i * blk, (i + 1) * blk)
        qr = _qk_norm_rope(aq_ref[rows, :].astype(F32), qg_ref[...], cos_ref[rows, :], sin_ref[rows, :])
        qs = qr * (dh ** -0.5 * math.log2(math.e))
        q2 = jnp.concatenate([jnp.where(first, qs, 0.0), jnp.where(first, 0.0, qs)], axis=0).astype(BF16)
        yield

        bias = None
        if i > MOBA_TOPK:
            km = jnp.concatenate(km_rows[:i] + [jnp.zeros((MOBA_NB - i, HEAD_PAIR), F32)], axis=0)
            km_heads = (jnp.where(km_first, km, 0.0), jnp.where(km_first, 0.0, km))
            past = blk_id < i
            cols = []
            for kmh in km_heads:
                gate = jnp.where(past, _dot_nt(kmh, qr, precision=HIGHEST), -jnp.inf)
                rank = jnp.zeros((MOBA_NB, blk), F32)
                for m_blk in range(i):
                    gm = gate[m_blk:m_blk + 1, :]
                    ahead = (gm > gate) | ((gm == gate) & (m_blk < blk_id))
                    rank = rank + jnp.where(ahead, 1.0, 0.0)
                sel = jnp.where(past & (rank < float(MOBA_TOPK)), 1.0, 0.0)
                cols.append(_dot_tn(sel, eye))
            bias = jnp.where(jnp.concatenate(cols, axis=0) > 0.5, 0.0, -jnp.inf)
        prepped[i] = (q2, bias)
        yield

    def scores(i, out):
        q2, bias = prepped.pop(i)
        m_run = None
        for j in range(i + 1):
            s = _dot_nt(q2, kr_s[j * blk:(j + 1) * blk, :])
            if j == i:
                s = jnp.where(causal, s, -jnp.inf)
            elif bias is not None:
                s = s + bias[:, j:j + 1]
            s_s[i % 2, :, j * blk:(j + 1) * blk] = s
            part = jnp.maximum(s[:, :LANES], s[:, LANES:])
            m_run = part if m_run is None else jnp.maximum(m_run, part)
            yield
        out.append(jnp.max(m_run, axis=1, keepdims=True))

    def attend(i, m_fin):
        rows = slice(i * blk, (i + 1) * blk)
        l_run = jnp.zeros((2 * blk, LANES), F32)
        acc = jnp.zeros((2 * blk, HEAD_PAIR), F32)
        m_lanes = jnp.broadcast_to(m_fin, (2 * blk, LANES))
        for j in range(i + 1):
            p_lo = jnp.exp2(s_s[i % 2, :, j * blk:j * blk + LANES] - m_lanes)
            p_hi = jnp.exp2(s_s[i % 2, :, j * blk + LANES:(j + 1) * blk] - m_lanes)
            l_run = l_run + (p_lo + p_hi)
            p = jnp.concatenate([p_lo, p_hi], axis=1)
            acc = acc + _dot(p.astype(BF16), av_ref[j * blk:(j + 1) * blk, :])
            yield
        o2 = acc / jnp.sum(l_run, axis=1, keepdims=True)
        o = jnp.where(first, o2[:blk, :], o2[blk:, :])
        sq = o * o
        tot = jnp.sum(sq, axis=-1, keepdims=True)
        s0 = jnp.sum(jnp.where(first, sq, 0.0), axis=-1, keepdims=True)
        ms = jnp.where(first, s0, tot - s0) * (1.0 / dh)
        y_ref[rows, :] = (o * lax.rsqrt(ms + EPS) * og).astype(y_ref.dtype)

    def interleave(*stages):
        stages = list(stages)
        while stages:
            for stage in list(stages):
                try:
                    next(stage)
                except StopIteration:
                    stages.remove(stage)

    m_cur = []
    interleave(key_prep(0), key_prep(1), query_prep(0), query_prep(1))
    interleave(scores(0, m_cur))
    for i in range(MOBA_NB):
        m_next = []
        stages = [attend(i, m_cur[0])]
        if i + 1 < MOBA_NB:
            stages.insert(0, scores(i + 1, m_next))
        if i + 2 < MOBA_NB:
            stages += [key_prep(i + 2), query_prep(i + 2)]
        interleave(*stages)
        m_cur = m_next


def _moba(aq, ak, av, cos_t, sin_t, q_gain, k_gain, out_gain):
    n_pairs = A_HEADS // 2
    const = lambda b, p: (0, 0)
    pair = lambda b, p: (b, p)
    return pl.pallas_call(
        _moba_kernel,
        grid=(BATCH, n_pairs),
        in_specs=[
            pl.BlockSpec((SEQ, HEAD_PAIR), pair),
            pl.BlockSpec((SEQ, HEAD_PAIR), pair),
            pl.BlockSpec((SEQ, HEAD_PAIR), pair),
            pl.BlockSpec((SEQ, HEAD_PAIR), const),
            pl.BlockSpec((SEQ, HEAD_PAIR), const),
            pl.BlockSpec((1, HEAD_PAIR), const),
            pl.BlockSpec((1, HEAD_PAIR), const),
            pl.BlockSpec((None, 1, HEAD_PAIR), lambda b, p: (p, 0, 0)),
        ],
        out_specs=pl.BlockSpec((SEQ, HEAD_PAIR), pair),
        out_shape=jax.ShapeDtypeStruct((TOKENS, A_WIDTH), BF16),
        scratch_shapes=[
            pltpu.VMEM((SEQ, HEAD_PAIR), BF16),
            pltpu.VMEM((2, 2 * MOBA_BLOCK, SEQ), F32),
        ],
        compiler_params=pltpu.CompilerParams(
            dimension_semantics=("arbitrary", "arbitrary"), vmem_limit_bytes=VMEM_LIMIT),
        name="moba",
    )(aq, ak, av, cos_t, sin_t, q_gain, k_gain, out_gain)


def _s5_kernel(u_ref, abar_ref, bre_ref, bim_ref, cre_ref, cim_ref, d_ref, gw_ref, gb_ref, gmat_ref, og_ref,
               y_ref, xr_s, xi_s, st_s, u_s, y_s):
    step = pl.program_id(0)
    rows = S5_STEPS * BATCH
    tile = 256
    n_tiles = rows // tile
    halves = S_WIDTH // LANES

    @pl.when(step == 0)
    def _init():
        st_s[...] = jnp.zeros_like(st_s)

    for b in range(BATCH):
        for hv in range(halves):
            lanes = slice(b * S_WIDTH + hv * LANES, b * S_WIDTH + (hv + 1) * LANES)
            u_s[hv, pl.ds(b, S5_STEPS, stride=BATCH), :] = u_ref[:, lanes].astype(F32)

    def u_rows(r):
        return jnp.concatenate([u_s[hv, r * tile:(r + 1) * tile, :] for hv in range(halves)], axis=1)

    for r in range(n_tiles):
        u = u_rows(r).astype(BF16)
        t0 = r * tile // BATCH
        xr_s[t0:t0 + tile // BATCH] = _dot(u, bre_ref[...]).reshape(tile // BATCH, BATCH, S_FLAT)
        xi_s[t0:t0 + tile // BATCH] = _dot(u, bim_ref[...]).reshape(tile // BATCH, BATCH, S_FLAT)

    ar = jnp.broadcast_to(abar_ref[0:1, :], (BATCH, S_FLAT))
    ai = jnp.broadcast_to(abar_ref[1:2, :], (BATCH, S_FLAT))

    def scan_step(t, carry):
        sr, si = carry
        nr = ar * sr - ai * si + xr_s[t]
        ni = ar * si + ai * sr + xi_s[t]
        xr_s[t] = nr
        xi_s[t] = ni
        return nr, ni

    sr, si = lax.fori_loop(0, S5_STEPS, scan_step, (st_s[0], st_s[1]), unroll=4)
    st_s[0] = sr
    st_s[1] = si

    for r in range(n_tiles):
        t0 = r * tile // BATCH
        xr = xr_s[t0:t0 + tile // BATCH].reshape(tile, S_FLAT).astype(BF16)
        xi = xi_s[t0:t0 + tile // BATCH].reshape(tile, S_FLAT).astype(BF16)
        y = _dot(xr, cre_ref[...]) - _dot(xi, cim_ref[...]) + d_ref[...] * u_rows(r)
        z = 0.5 * y * (1.0 + jnp.tanh(math.sqrt(2.0 / math.pi) * (y + 0.044715 * (y * y * y))))
        y2 = z * _sigmoid(_dot(z.astype(BF16), gw_ref[...]) + gb_ref[...])
        ms = _dot((y2 * y2).astype(BF16), gmat_ref[...])
        yn = y2 * lax.rsqrt(ms + EPS) * og_ref[...]
        for hv in range(halves):
            y_s[hv, r * tile:(r + 1) * tile, :] = yn[:, hv * LANES:(hv + 1) * LANES]

    for b in range(BATCH):
        for hv in range(halves):
            lanes = slice(b * S_WIDTH + hv * LANES, b * S_WIDTH + (hv + 1) * LANES)
            y_ref[:, lanes] = y_s[hv, pl.ds(b, S5_STEPS, stride=BATCH), :].astype(y_ref.dtype)


def _s5(u_sb, abar, bre, bim, cre, cim, d_row, glu_w, glu_b, gmat, out_g):
    rows = S5_STEPS * BATCH
    const = lambda s: (0, 0)
    return pl.pallas_call(
        _s5_kernel,
        grid=(SEQ // S5_STEPS,),
        in_specs=[
            pl.BlockSpec((S5_STEPS, BATCH * S_WIDTH), lambda s: (s, 0)),
            pl.BlockSpec((2, S_FLAT), const),
            pl.BlockSpec((S_WIDTH, S_FLAT), const),
            pl.BlockSpec((S_WIDTH, S_FLAT), const),
            pl.BlockSpec((S_FLAT, S_WIDTH), const),
            pl.BlockSpec((S_FLAT, S_WIDTH), const),
            pl.BlockSpec((1, S_WIDTH), const),
            pl.BlockSpec((S_WIDTH, S_WIDTH), const),
            pl.BlockSpec((1, S_WIDTH), const),
            pl.BlockSpec((S_WIDTH, S_WIDTH), const),
            pl.BlockSpec((1, S_WIDTH), const),
        ],
        out_specs=pl.BlockSpec((S5_STEPS, BATCH * S_WIDTH), lambda s: (s, 0)),
        out_shape=jax.ShapeDtypeStruct((SEQ, BATCH * S_WIDTH), BF16),
        scratch_shapes=[
            pltpu.VMEM((S5_STEPS, BATCH, S_FLAT), F32),
            pltpu.VMEM((S5_STEPS, BATCH, S_FLAT), F32),
            pltpu.VMEM((2, BATCH, S_FLAT), F32),
            pltpu.VMEM((S_WIDTH // LANES, rows, LANES), F32),
            pltpu.VMEM((S_WIDTH // LANES, rows, LANES), F32),
        ],
        compiler_params=pltpu.CompilerParams(
            dimension_semantics=("arbitrary",), vmem_limit_bytes=VMEM_LIMIT),
        name="s5",
    )(u_sb, abar, bre, bim, cre, cim, d_row, glu_w, glu_b, gmat, out_g)


def _outproj_kernel(x_ref, ym_ref, ya_ref, ys_ref, w_ref, g_ref, wr_ref, br_ref,
                    x1_ref, xn_ref, route_ref):
    x1 = x_ref[...]
    x1 = x1 + _dot(ym_ref[...], w_ref[0:M_WIDTH, :])
    x1 = x1 + _dot(ya_ref[...], w_ref[M_WIDTH:M_WIDTH + A_WIDTH, :])
    x1 = x1 + _dot(ys_ref[...], w_ref[M_WIDTH + A_WIDTH:, :])
    x1_ref[...] = x1
    xn = x1 * lax.rsqrt(jnp.mean(x1 * x1, axis=-1, keepdims=True) + EPS) * g_ref[...]
    for j in range(ROW_TILES):
        xn_ref[pl.ds(j, TM_PROJ, stride=ROW_TILES), :] = xn[:, j * LANES:(j + 1) * LANES]
    logits = _split_dot(xn, wr_ref[...]) + br_ref[...]

    lane = lax.broadcasted_iota(jnp.int32, logits.shape, 1).astype(F32)
    is_grp = lane < N_GROUPS
    neg = -jnp.inf
    gl = jnp.where(is_grp, logits, neg)
    gmax = jnp.max(gl, axis=-1, keepdims=True)
    gsum = jnp.sum(jnp.where(is_grp, jnp.exp(logits - gmax), 0.0), axis=-1, keepdims=True)
    g_sel = jnp.min(jnp.where(gl == gmax, lane, float(LANES)), axis=-1, keepdims=True)
    lo = N_GROUPS + EXPERTS_PER_GROUP * g_sel
    el = jnp.where((lane >= lo) & (lane < lo + EXPERTS_PER_GROUP), logits, neg)
    t1 = jnp.max(el, axis=-1, keepdims=True)
    i1 = jnp.min(jnp.where(el == t1, lane, float(LANES)), axis=-1, keepdims=True)
    el2 = jnp.where(lane == i1, neg, el)
    t2 = jnp.max(el2, axis=-1, keepdims=True)
    i2 = jnp.min(jnp.where(el2 == t2, lane, float(LANES)), axis=-1, keepdims=True)
    e21 = jnp.exp(t2 - t1)
    w1 = 1.0 / ((1.0 + e21) * gsum)
    w2 = e21 / ((1.0 + e21) * gsum)
    route = jnp.where(lane == 0.0, i1 - N_GROUPS,
                      jnp.where(lane == 1.0, i2 - N_GROUPS,
                                jnp.where(lane == 2.0, w1, jnp.where(lane == 3.0, w2, 0.0))))
    route_ref[...] = route


def _outproj(x, y_m, y_a, y_s_sb, w_out, g2, w_router, b_router):
    n_s = SEQ // TM_PROJ
    row = lambda b, s: (b * n_s + s, 0)
    const = lambda b, s: (0, 0)
    return pl.pallas_call(
        _outproj_kernel,
        grid=(BATCH, n_s),
        in_specs=[
            pl.BlockSpec((TM_PROJ, D_MODEL), row),
            pl.BlockSpec((TM_PROJ, M_WIDTH), row),
            pl.BlockSpec((TM_PROJ, A_WIDTH), row),
            pl.BlockSpec((TM_PROJ, S_WIDTH), lambda b, s: (s, b)),
            pl.BlockSpec((D_MIX, D_MODEL), const),
            pl.BlockSpec((1, D_MODEL), const),
            pl.BlockSpec((D_MODEL, 2 * LANES), const),
            pl.BlockSpec((1, LANES), const),
        ],
        out_specs=(
            pl.BlockSpec((TM_PROJ, D_MODEL), row),
            pl.BlockSpec((TM_PROJ * ROW_TILES, LANES), row),
            pl.BlockSpec((TM_PROJ, LANES), row),
        ),
        out_shape=(
            jax.ShapeDtypeStruct((TOKENS, D_MODEL), F32),
            jax.ShapeDtypeStruct((TOKENS * ROW_TILES, LANES), F32),
            jax.ShapeDtypeStruct((TOKENS, LANES), F32),
        ),
        compiler_params=pltpu.CompilerParams(
            dimension_semantics=("arbitrary", "arbitrary"), vmem_limit_bytes=VMEM_LIMIT),
        name="outproj",
    )(x, y_m, y_a, y_s_sb, w_out, g2, w_router, b_router)


def _ffn_kernel(be_ref, src_ref, xs_ref, wg_ref, wu_ref, wd_ref, ys_ref):
    del be_ref
    i = pl.program_id(0)

    @pl.when(src_ref[i] == i)
    def _expert_block():
        xb = jnp.concatenate([xs_ref[pl.ds(j, FFN_BLOCK, stride=ROW_TILES), :] for j in range(ROW_TILES)],
                             axis=1).astype(BF16)
        g = _dot(xb, wg_ref[...])
        u = _dot(xb, wu_ref[...])
        hmid = (g * _sigmoid(g) * u).astype(BF16)
        y = _dot(hmid, wd_ref[...])
        for j in range(ROW_TILES):
            ys_ref[pl.ds(j, FFN_BLOCK, stride=ROW_TILES), :] = y[:, j * LANES:(j + 1) * LANES]

    @pl.when(src_ref[i] != i)
    def _unused_tail_block():
        ys_ref[...] = jnp.zeros_like(ys_ref)


def _expert_ffn(layer, blk_exp, blk_src, xs, w_gate, w_up, w_down):
    nblk = blk_exp.shape[0]
    block = (FFN_BLOCK * ROW_TILES, LANES)
    grid_spec = pltpu.PrefetchScalarGridSpec(
        num_scalar_prefetch=2,
        grid=(nblk,),
        in_specs=[
            pl.BlockSpec(block, lambda i, be, src: (src[i], 0)),
            pl.BlockSpec((None, None, D_MODEL, D_EXPERT), lambda i, be, src: (layer, be[i], 0, 0)),
            pl.BlockSpec((None, None, D_MODEL, D_EXPERT), lambda i, be, src: (layer, be[i], 0, 0)),
            pl.BlockSpec((None, None, D_EXPERT, D_MODEL), lambda i, be, src: (layer, be[i], 0, 0)),
        ],
        out_specs=pl.BlockSpec(block, lambda i, be, src: (i, 0)),
    )
    return pl.pallas_call(
        _ffn_kernel,
        grid_spec=grid_spec,
        out_shape=jax.ShapeDtypeStruct((nblk * FFN_BLOCK * ROW_TILES, LANES), F32),
        compiler_params=pltpu.CompilerParams(
            dimension_semantics=("arbitrary",), vmem_limit_bytes=VMEM_LIMIT),
        name="expert_ffn",
    )(blk_exp, blk_src, xs, w_gate, w_up, w_down)


def _plan_kernel(e_ref, ps_ref, dest_ref, run_s, tri_s):
    i = pl.program_id(0)

    @pl.when(i == 0)
    def _init():
        run_s[...] = ps_ref[...]
        s_i = lax.broadcasted_iota(jnp.int32, (PLAN_TILE, PLAN_TILE), 0)
        t_i = lax.broadcasted_iota(jnp.int32, (PLAN_TILE, PLAN_TILE), 1)
        tri_s[...] = (s_i < t_i).astype(BF16)

    expert = lax.broadcasted_iota(jnp.int32, (N_EXPERTS, PLAN_TILE), 0)
    onehot = jnp.where(e_ref[...] == expert, 1.0, 0.0)
    before = _dot(onehot.astype(BF16), tri_s[...])
    dest = jnp.sum(onehot * (before + run_s[:, 0:1]), axis=0, keepdims=True)
    dest_ref[...] = dest.astype(jnp.int32)
    run_s[...] = run_s[...] + jnp.sum(onehot, axis=1, keepdims=True)


def _plan(e_rows, pstart_col):
    n_tiles = e_rows.shape[0]
    return pl.pallas_call(
        _plan_kernel,
        grid=(n_tiles,),
        in_specs=[
            pl.BlockSpec((None, 1, PLAN_TILE), lambda i: (i, 0, 0)),
            pl.BlockSpec((N_EXPERTS, LANES), lambda i: (0, 0)),
        ],
        out_specs=pl.BlockSpec((None, 1, PLAN_TILE), lambda i: (i, 0, 0)),
        out_shape=jax.ShapeDtypeStruct((n_tiles, 1, PLAN_TILE), jnp.int32),
        scratch_shapes=[pltpu.VMEM((N_EXPERTS, LANES), F32), pltpu.VMEM((PLAN_TILE, PLAN_TILE), BF16)],
        compiler_params=pltpu.CompilerParams(
            dimension_semantics=("arbitrary",), vmem_limit_bytes=VMEM_LIMIT),
        name="moe_plan",
    )(e_rows, pstart_col)


def _dispatch_kernel(fill_ref, d0_ref, d1_ref, xn_ref, xs_hbm, zero_s, sem, fill_sem):
    @pl.when(pl.program_id(0) == 0)
    def _zero_padding():
        zero_s[...] = jnp.zeros_like(zero_s)

        def fill(c):
            start = pl.multiple_of(fill_ref[0, c] * ROW_TILES, ROW_TILES)
            return pltpu.make_async_copy(zero_s, xs_hbm.at[pl.ds(start, FFN_BLOCK * ROW_TILES), :], fill_sem)

        for c in range(N_FILLS):
            pl.when(fill_ref[0, c] >= 0)(lambda c=c: fill(c).start())
        for c in range(N_FILLS):
            pl.when(fill_ref[0, c] >= 0)(lambda c=c: fill(c).wait())

    def row_copy(dref, t):
        src = xn_ref.at[pl.ds(pl.multiple_of(t * ROW_TILES, ROW_TILES), ROW_TILES), :]
        dst = xs_hbm.at[pl.ds(pl.multiple_of(dref[0, t] * ROW_TILES, ROW_TILES), ROW_TILES), :]
        return pltpu.make_async_copy(src, dst, sem)

    def start(t, carry):
        row_copy(d0_ref, t).start()
        row_copy(d1_ref, t).start(priority=1)
        return carry

    lax.fori_loop(0, DISPATCH_TOKENS, start, 0, unroll=8)
    for _ in range(EXPERT_TOPK):
        pltpu.make_async_copy(xn_ref, xs_hbm.at[pl.ds(0, DISPATCH_TOKENS * ROW_TILES), :], sem).wait()


def _dispatch(fill_rows, dest, xn, n_rows):
    n_tiles = TOKENS // DISPATCH_TOKENS
    dest4 = dest.reshape(EXPERT_TOPK, n_tiles, 1, DISPATCH_TOKENS)
    smem_tile = lambda k: pl.BlockSpec((None, None, 1, DISPATCH_TOKENS), lambda i: (k, i, 0, 0),
                                       memory_space=pltpu.SMEM)
    return pl.pallas_call(
        _dispatch_kernel,
        grid=(n_tiles,),
        in_specs=[
            pl.BlockSpec(memory_space=pltpu.SMEM),
            smem_tile(0),
            smem_tile(1),
            pl.BlockSpec((DISPATCH_TOKENS * ROW_TILES, LANES), lambda i: (i, 0)),
        ],
        out_specs=pl.BlockSpec(memory_space=pl.ANY),
        out_shape=jax.ShapeDtypeStruct((n_rows * ROW_TILES, LANES), F32),
        scratch_shapes=[
            pltpu.VMEM((FFN_BLOCK * ROW_TILES, LANES), F32),
            pltpu.SemaphoreType.DMA(()),
            pltpu.SemaphoreType.DMA(()),
        ],
        compiler_params=pltpu.CompilerParams(
            dimension_semantics=("arbitrary",), vmem_limit_bytes=VMEM_LIMIT),
        name="moe_dispatch",
    )(fill_rows, dest4, dest4, xn)


def _combine_kernel(d0_ref, d1_ref, d0_next_ref, d1_next_ref, x1_ref, route_ref, ys_hbm, out_ref, buf, sem):
    i = pl.program_id(0)
    n = pl.num_programs(0)

    def start_tile(drefs, slot):
        def body(t, carry):
            for k in range(EXPERT_TOPK):
                src = ys_hbm.at[pl.ds(pl.multiple_of(drefs[k][0, t] * ROW_TILES, ROW_TILES), ROW_TILES), :]
                dst = buf.at[slot, k, pl.ds(pl.multiple_of(t * ROW_TILES, ROW_TILES), ROW_TILES), :]
                pltpu.make_async_copy(src, dst, sem.at[slot]).start(priority=k)
            return carry
        lax.fori_loop(0, COMBINE_TOKENS, body, 0, unroll=8)

    slot = i % 2

    @pl.when(i == 0)
    def _first():
        start_tile((d0_ref, d1_ref), 0)

    @pl.when(i + 1 < n)
    def _prefetch():
        start_tile((d0_next_ref, d1_next_ref), 1 - slot)

    for k in range(EXPERT_TOPK):
        pltpu.make_async_copy(ys_hbm.at[pl.ds(0, COMBINE_TOKENS * ROW_TILES), :], buf.at[slot, k],
                              sem.at[slot]).wait()
    w = route_ref[...]
    w1 = w[:, 2:3]
    w2 = w[:, 3:4]
    for j in range(ROW_TILES):
        cols = slice(j * LANES, (j + 1) * LANES)
        chunk = pl.ds(j, COMBINE_TOKENS, stride=ROW_TILES)
        out_ref[:, cols] = x1_ref[:, cols] + w1 * buf[slot, 0, chunk, :] + w2 * buf[slot, 1, chunk, :]


def _combine(dest, x1, route, ys3):
    n_tiles = TOKENS // COMBINE_TOKENS
    dest4 = dest.reshape(EXPERT_TOPK, n_tiles, 1, COMBINE_TOKENS)
    smem_tile = lambda k, nxt: pl.BlockSpec(
        (None, None, 1, COMBINE_TOKENS), lambda i: (k, jnp.minimum(i + nxt, n_tiles - 1), 0, 0),
        memory_space=pltpu.SMEM)
    return pl.pallas_call(
        _combine_kernel,
        grid=(n_tiles,),
        in_specs=[
            smem_tile(0, 0), smem_tile(1, 0), smem_tile(0, 1), smem_tile(1, 1),
            pl.BlockSpec((COMBINE_TOKENS, D_MODEL), lambda i: (i, 0)),
            pl.BlockSpec((COMBINE_TOKENS, LANES), lambda i: (i, 0)),
            pl.BlockSpec(memory_space=pl.ANY),
        ],
        out_specs=pl.BlockSpec((COMBINE_TOKENS, D_MODEL), lambda i: (i, 0)),
        out_shape=jax.ShapeDtypeStruct((TOKENS, D_MODEL), F32),
        scratch_shapes=[
            pltpu.VMEM((2, EXPERT_TOPK, COMBINE_TOKENS * ROW_TILES, LANES), F32),
            pltpu.SemaphoreType.DMA((2,)),
        ],
        compiler_params=pltpu.CompilerParams(
            dimension_semantics=("arbitrary",), vmem_limit_bytes=VMEM_LIMIT),
        name="moe_combine",
    )(dest4, dest4, dest4, dest4, x1, route, ys3)


def _rope_tables():
    inv = ROPE_THETA ** (-np.arange(ROPE_HALF, dtype=np.float64) * 2.0 / ROPE_DIM)
    ang = np.arange(SEQ, dtype=np.float64)[:, None] * inv[None, :]
    cos = np.ones((SEQ, A_HEAD_DIM))
    sin = np.zeros((SEQ, A_HEAD_DIM))
    cos[:, :ROPE_HALF] = np.cos(ang)
    cos[:, ROPE_HALF:ROPE_DIM] = np.cos(ang)
    sin[:, :ROPE_HALF] = -np.sin(ang)
    sin[:, ROPE_HALF:ROPE_DIM] = np.sin(ang)
    return (jnp.asarray(np.tile(cos, (1, 2)), F32), jnp.asarray(np.tile(sin, (1, 2)), F32))


def _split_w_in(w_in):
    sizes = [M_WIDTH, M_WIDTH, M_WIDTH, M_WIDTH, M_HEADS, M_HEADS, A_WIDTH, A_WIDTH, A_WIDTH, S_WIDTH]
    offs = np.cumsum([0] + sizes)
    mq, mk, mv, mo, mi, mf, aq, ak, av, su = (w_in[:, offs[n]:offs[n + 1]] for n in range(len(sizes)))
    w_main = jnp.concatenate([aq, ak, av, mq, mk, mv, mo, su], axis=1).astype(BF16)
    gate = jnp.concatenate([mi, mf], axis=1)
    return w_main, _hi_lo(jnp.pad(gate, ((0, 0), (0, LANES - 2 * M_HEADS))))


def _s5_params(a_re, a_im, b_re, b_im, c_re, c_im, log_dt):
    dt = jnp.exp(log_dt)[:, None]
    mag = jnp.exp(a_re * dt)
    abar_re, abar_im = mag * jnp.cos(a_im * dt), mag * jnp.sin(a_im * dt)
    zr, zi = abar_re - 1.0, abar_im
    den = a_re * a_re + a_im * a_im
    fr, fi = (zr * a_re + zi * a_im) / den, (zi * a_re - zr * a_im) / den
    bbar_re = fr[..., None] * b_re - fi[..., None] * b_im
    bbar_im = fr[..., None] * b_im + fi[..., None] * b_re
    eye = jnp.eye(S_GROUPS, dtype=F32)
    dense_b = lambda t: jnp.einsum('gph,gk->ghkp', t, eye).reshape(S_WIDTH, S_FLAT).astype(BF16)
    dense_c = lambda t: jnp.einsum('ghp,gk->gpkh', t, eye).reshape(S_FLAT, S_WIDTH).astype(BF16)
    abar = jnp.stack([abar_re.reshape(S_FLAT), abar_im.reshape(S_FLAT)])
    return abar, dense_b(bbar_re), dense_b(bbar_im), dense_c(c_re), dense_c(c_im)


def _moe(layer, x1, xn, route, w_gate, w_up, w_down):
    n_assign = TOKENS * EXPERT_TOPK
    e_flat = route[:, :EXPERT_TOPK].astype(jnp.int32).T.reshape(n_assign)
    experts = jnp.arange(N_EXPERTS, dtype=jnp.int32)
    counts = jnp.sum((e_flat[:, None] == experts[None, :]).astype(jnp.int32), axis=0)
    padded = (counts + FFN_BLOCK - 1) // FFN_BLOCK * FFN_BLOCK
    pends = jnp.cumsum(padded)
    pstarts = pends - padded
    n_rows = (n_assign + N_EXPERTS * (FFN_BLOCK - 1) + FFN_BLOCK - 1) // FFN_BLOCK * FFN_BLOCK
    nblk = n_rows // FFN_BLOCK
    blk_start = jnp.arange(nblk, dtype=jnp.int32) * FFN_BLOCK
    blk_exp = jnp.minimum(jnp.sum((pends[None, :] <= blk_start[:, None]).astype(jnp.int32), axis=1),
                          N_EXPERTS - 1).astype(jnp.int32)
    pstart_col = jnp.broadcast_to(pstarts.astype(F32)[:, None], (N_EXPERTS, LANES))
    dest = _plan(e_flat.reshape(n_assign // PLAN_TILE, 1, PLAN_TILE), pstart_col)
    tail = pends[-1] + jnp.arange(N_EXPERTS, dtype=jnp.int32) * FFN_BLOCK
    fill_rows = jnp.concatenate([jnp.where(padded > 0, pends - FFN_BLOCK, -1), jnp.where(tail < n_rows, tail, -1)])
    fill_rows = fill_rows.astype(jnp.int32).reshape(1, N_FILLS)
    xs = _dispatch(fill_rows, dest, xn, n_rows)
    blk_src = jnp.minimum(jnp.arange(nblk, dtype=jnp.int32), pends[-1] // FFN_BLOCK - 1).astype(jnp.int32)
    ys = _expert_ffn(layer, blk_exp, blk_src, xs, w_gate, w_up, w_down)
    return _combine(dest, x1, route, ys)


def kernel(x, norm1_g, w_in, m_bias_i, m_bias_f, m_conv_w, m_conv_b, m_out_g, a_q_g, a_k_g, a_out_g, s_a_re, s_a_im, s_b_re, s_b_im, s_c_re, s_c_im, s_d, s_log_dt, s_glu_w, s_glu_b, s_out_g, w_out, norm2_g, r_group_w, r_group_b, r_expert_w, r_expert_b, e_w_gate, e_w_up, e_w_down):
    cos_t, sin_t = _rope_tables()
    gmat = jnp.asarray(np.kron(np.eye(S_GROUPS), np.full((S_GROUP_DIM, S_GROUP_DIM), 1.0 / S_GROUP_DIM)), BF16)
    xf = x.reshape(TOKENS, D_MODEL)
    for l in range(DEPTH):
        w_main, w_gate = _split_w_in(w_in[l])
        aq, ak, av, mqk, mv, mo, su, gates, gatest = _inproj(xf, norm1_g[l].reshape(1, D_MODEL), w_main, w_gate)

        bias = jnp.concatenate([m_bias_i[l], m_bias_f[l]])
        bias_row = jnp.pad(bias, (0, LANES - 2 * M_HEADS)).reshape(1, LANES)
        y_m = _mlstm(mqk, mv, mo, gates, gatest, m_conv_w[l], m_conv_b[l].reshape(1, 2 * M_WIDTH),
                     bias_row, bias.reshape(2 * M_HEADS, 1), m_out_g[l].reshape(1, M_WIDTH))

        y_a = _moba(aq, ak, av, cos_t, sin_t,
                    jnp.tile(a_q_g[l], 2).reshape(1, HEAD_PAIR), jnp.tile(a_k_g[l], 2).reshape(1, HEAD_PAIR),
                    a_out_g[l].reshape(A_HEADS // 2, 1, HEAD_PAIR))

        abar, bre, bim, cre, cim = _s5_params(s_a_re[l], s_a_im[l], s_b_re[l], s_b_im[l],
                                              s_c_re[l], s_c_im[l], s_log_dt[l])
        y_s = _s5(su, abar, bre, bim, cre, cim,
                  s_d[l].reshape(1, S_WIDTH), s_glu_w[l].astype(BF16), s_glu_b[l].reshape(1, S_WIDTH),
                  gmat, s_out_g[l].reshape(1, S_WIDTH))

        w_router = jnp.pad(jnp.concatenate([r_group_w[l], r_expert_w[l]], axis=1),
                           ((0, 0), (0, LANES - N_GROUPS - N_EXPERTS)))
        b_router = jnp.pad(jnp.concatenate([r_group_b[l], r_expert_b[l]]),
                           (0, LANES - N_GROUPS - N_EXPERTS)).reshape(1, LANES)
        x1, xn, route = _outproj(xf, y_m, y_a, y_s, w_out[l].astype(BF16),
                                 norm2_g[l].reshape(1, D_MODEL), _hi_lo(w_router), b_router)
        xf = _moe(l, x1, xn, route, e_w_gate, e_w_up, e_w_down)
    return xf.reshape(BATCH, SEQ, D_MODEL)
```

```python
import functools
import math

import numpy as np
import jax
import jax.numpy as jnp
from jax import lax
from jax.experimental import pallas as pl
from jax.experimental.pallas import tpu as pltpu

F32 = jnp.float32
BF16 = jnp.bfloat16

D_MODEL = 1024
BATCH = 8
SEQ = 2048
DEPTH = 2
TOKENS = BATCH * SEQ

M_HEADS = 4
M_HEAD_DIM = 64
M_WIDTH = M_HEADS * M_HEAD_DIM
A_HEADS = 8
A_HEAD_DIM = 64
A_WIDTH = A_HEADS * A_HEAD_DIM
S_GROUPS = 16
S_GROUP_DIM = 16
S_WIDTH = S_GROUPS * S_GROUP_DIM
S_STATE = 64
S_FLAT = S_GROUPS * S_STATE
D_MIX = M_WIDTH + A_WIDTH + S_WIDTH

CONV_K = 4
MOBA_BLOCK = 256
MOBA_NB = SEQ // MOBA_BLOCK
MOBA_TOPK = 3
ROPE_THETA = 500000.0
ROPE_DIM = A_HEAD_DIM // 4
ROPE_HALF = ROPE_DIM // 2

N_GROUPS = 4
EXPERTS_PER_GROUP = 8
N_EXPERTS = N_GROUPS * EXPERTS_PER_GROUP
EXPERT_TOPK = 2
D_EXPERT = 512
DISPATCH_BLOCK = 128
EPS = 1e-6

LANES = 128
ROW_TILES = D_MODEL // LANES
HEAD_PAIR = 2 * A_HEAD_DIM

W_MAIN = 3 * A_WIDTH + 4 * M_WIDTH + S_WIDTH
TM_PROJ = 512
M_CHUNK = 256
M_NCHUNK = SEQ // M_CHUNK
S5_STEPS = 256
FFN_BLOCK = 512
N_FILLS = 2 * N_EXPERTS
PLAN_TILE = 1024
DISPATCH_TOKENS = 512
COMBINE_TOKENS = 256
VMEM_LIMIT = 56 * 1024 * 1024

HIGHEST = lax.Precision.HIGHEST


def _dot(a, b, precision=None):
    return jnp.dot(a, b, preferred_element_type=F32, precision=precision)


def _dot_nt(a, b, precision=None):
    return lax.dot_general(a, b, (((1,), (1,)), ((), ())), preferred_element_type=F32, precision=precision)


def _dot_tn(a, b, precision=None):
    return lax.dot_general(a, b, (((0,), (0,)), ((), ())), preferred_element_type=F32, precision=precision)


def _split_dot(a, w_cat):
    n = w_cat.shape[1] // 2
    a_hi = a.astype(BF16)
    a_lo = (a - a_hi.astype(F32)).astype(BF16)
    both = _dot(a_hi, w_cat)
    return both[:, :n] + both[:, n:] + _dot(a_lo, w_cat[:, :n])


def _bf16_terms(x):
    hi = x.astype(BF16)
    rest = x - hi.astype(F32)
    mid = rest.astype(BF16)
    return hi, mid, (rest - mid.astype(F32)).astype(BF16)


def _hi_lo(w):
    w_hi = w.astype(BF16)
    return jnp.concatenate([w_hi, (w - w_hi.astype(F32)).astype(BF16)], axis=1)


def _log_sigmoid(x):
    return jnp.minimum(x, 0.0) - jnp.log1p(jnp.exp(-jnp.abs(x)))


def _sigmoid(x):
    return 1.0 / (1.0 + jnp.exp(-x))


def _inproj_kernel(x_ref, g_ref, w_ref, wg_ref,
                   aq_ref, ak_ref, av_ref, mqk_ref, mv_ref, mo_ref, su_ref, gates_ref, gatest_ref):
    x = x_ref[...]
    h = x * lax.rsqrt(jnp.mean(x * x, axis=-1, keepdims=True) + EPS) * g_ref[...]
    hb = h.astype(BF16)
    off = 0
    for ref, width in ((aq_ref, A_WIDTH), (ak_ref, A_WIDTH), (av_ref, A_WIDTH), (mqk_ref, 2 * M_WIDTH),
                       (mv_ref, M_WIDTH), (mo_ref, M_WIDTH), (su_ref, S_WIDTH)):
        ref[...] = _dot(hb, w_ref[:, off:off + width]).astype(ref.dtype)
        off += width
    gates = _split_dot(h, wg_ref[...])
    gates_ref[...] = gates
    gt = gates.T[:8, :]
    for c in range(TM_PROJ // M_CHUNK):
        gatest_ref[c] = gt[:, c * M_CHUNK:(c + 1) * M_CHUNK]


def _inproj(x, g, w_main, w_gate):
    n_s = SEQ // TM_PROJ
    row = lambda b, s: (b * n_s + s, 0)
    const = lambda b, s: (0, 0)
    out_shapes = (
        jax.ShapeDtypeStruct((TOKENS, A_WIDTH), BF16),
        jax.ShapeDtypeStruct((TOKENS, A_WIDTH), BF16),
        jax.ShapeDtypeStruct((TOKENS, A_WIDTH), BF16),
        jax.ShapeDtypeStruct((TOKENS, 2 * M_WIDTH), BF16),
        jax.ShapeDtypeStruct((TOKENS, M_WIDTH), BF16),
        jax.ShapeDtypeStruct((TOKENS, M_WIDTH), BF16),
        jax.ShapeDtypeStruct((SEQ, BATCH * S_WIDTH), BF16),
        jax.ShapeDtypeStruct((TOKENS, LANES), F32),
        jax.ShapeDtypeStruct((TOKENS // M_CHUNK, 8, M_CHUNK), F32),
    )
    out_specs = (
        pl.BlockSpec((TM_PROJ, A_WIDTH), row),
        pl.BlockSpec((TM_PROJ, A_WIDTH), row),
        pl.BlockSpec((TM_PROJ, A_WIDTH), row),
        pl.BlockSpec((TM_PROJ, 2 * M_WIDTH), row),
        pl.BlockSpec((TM_PROJ, M_WIDTH), row),
        pl.BlockSpec((TM_PROJ, M_WIDTH), row),
        pl.BlockSpec((TM_PROJ, S_WIDTH), lambda b, s: (s, b)),
        pl.BlockSpec((TM_PROJ, LANES), row),
        pl.BlockSpec((TM_PROJ // M_CHUNK, 8, M_CHUNK), lambda b, s: (b * n_s + s, 0, 0)),
    )
    return pl.pallas_call(
        _inproj_kernel,
        grid=(BATCH, n_s),
        in_specs=[
            pl.BlockSpec((TM_PROJ, D_MODEL), row),
            pl.BlockSpec((1, D_MODEL), const),
            pl.BlockSpec((D_MODEL, W_MAIN), const),
            pl.BlockSpec((D_MODEL, 2 * LANES), const),
        ],
        out_specs=out_specs,
        out_shape=out_shapes,
        compiler_params=pltpu.CompilerParams(
            dimension_semantics=("arbitrary", "arbitrary"), vmem_limit_bytes=VMEM_LIMIT),
        name="inproj",
    )(x, g, w_main, w_gate)


def _mlstm_kernel(mqk_ref, mv_ref, mo_ref, gates_ref, gatest_ref, cw_ref, cb_ref, brow_ref, bcol_ref, og_ref,
                  y_ref, qk_s, cols_s, wrow_s):
    L = M_CHUNK
    dh = M_HEAD_DIM
    r_i = lax.broadcasted_iota(jnp.int32, (L, L), 0)
    c_i = lax.broadcasted_iota(jnp.int32, (L, L), 1)
    causal = c_i <= r_i
    tri_u = (r_i <= c_i).astype(BF16)
    eye = (lax.broadcasted_iota(jnp.int32, (8, LANES), 0)
           == lax.broadcasted_iota(jnp.int32, (8, LANES), 1)).astype(BF16)
    w = cw_ref[...]
    cb = cb_ref[...]
    lane_q = lax.broadcasted_iota(jnp.int32, (1, 2 * M_WIDTH), 1) < M_WIDTH
    kscale = jnp.where(lane_q, 1.0, dh ** -0.5).astype(F32)
    for c in range(M_NCHUNK):
        halo = 16 if c > 0 else 0
        for strip in range(2 * M_WIDTH // LANES):
            cols = slice(strip * LANES, (strip + 1) * LANES)
            xt = mqk_ref[c * L - halo:(c + 1) * L, cols].astype(F32)
            rows = lax.broadcasted_iota(jnp.int32, xt.shape, 0)
            y = cb[:, cols] + w[CONV_K - 1:CONV_K, cols] * xt
            for k in range(1, CONV_K):
                sh = pltpu.roll(xt, k, 0)
                if c == 0:
                    sh = jnp.where(rows >= k, sh, 0.0)
                y = y + w[CONV_K - 1 - k:CONV_K - k, cols] * sh
            y = y[halo:, :]
            y = y * _sigmoid(y) * kscale[:, cols]
            qk_s[c * L:(c + 1) * L, cols] = y.astype(BF16)

    rows_all = M_NCHUNK * 8
    grow = gatest_ref[...].reshape(rows_all, L) + jnp.tile(bcol_ref[...], (M_NCHUNK, 1))
    cs_row = sum(_dot(term, tri_u) for term in _bf16_terms(_log_sigmoid(grow)))
    w_row = grow - pltpu.roll(cs_row, rows_all - M_HEADS, 0)
    lane_t = lax.broadcasted_iota(jnp.int32, (rows_all, L), 1)
    pm = w_row
    shift = 1
    while shift < L:
        pm = jnp.maximum(pm, jnp.where(lane_t >= shift, pltpu.roll(pm, shift, 1), -jnp.inf))
        shift *= 2
    wrow_s[...] = w_row.reshape(M_NCHUNK, 8, L)
    is_i_row = lax.broadcasted_iota(jnp.int32, (rows_all, L), 0) % 8 < M_HEADS
    terms = _bf16_terms(jnp.where(is_i_row, pm, cs_row))
    for c in range(M_NCHUNK):
        cols_s[c * L:(c + 1) * L, :] = sum(_dot_tn(term[c * 8:(c + 1) * 8, :], eye) for term in terms)

    pad_lane = lax.broadcasted_iota(jnp.int32, (L, LANES - dh), 1)
    ones_pad = jnp.where(pad_lane == 0, 1.0, 0.0).astype(BF16)
    og = og_ref[...]
    head_mean = jnp.where(lax.broadcasted_iota(jnp.int32, (M_WIDTH, M_WIDTH), 0) // dh
                          == lax.broadcasted_iota(jnp.int32, (M_WIDTH, M_WIDTH), 1) // dh,
                          1.0 / dh, 0.0).astype(BF16)

    def chunk(c, carry):
        r0 = pl.multiple_of(c * L, L)
        gcol = gates_ref[pl.ds(r0, L), :] + brow_ref[...]
        cs_col = pm_col = cols_s[pl.ds(r0, L), :]
        w_row = wrow_s[c]
        qk = qk_s[pl.ds(r0, L), :]
        vv = mv_ref[pl.ds(r0, L), :]
        new_carry = []
        hs = []
        for h in range(M_HEADS):
            cst, m = carry[h]
            q = qk[:, h * dh:(h + 1) * dh]
            k = qk[:, M_WIDTH + h * dh:M_WIDTH + (h + 1) * dh]
            v_aug = jnp.concatenate([vv[:, h * dh:(h + 1) * dh], ones_pad], axis=1)
            bc = cs_col[:, M_HEADS + h:M_HEADS + h + 1]
            ic = gcol[:, h:h + 1]
            m_row = jnp.maximum(m, pm_col[:, h:h + 1])
            mt = bc + m_row
            inter = jnp.exp(m - m_row)
            sqk = _dot_nt(q, k) * jnp.exp(jnp.where(causal, w_row[h:h + 1, :] - m_row, -jnp.inf))
            nd = inter * _dot(q, cst.astype(BF16)) + _dot(sqk.astype(BF16), v_aug)
            num = nd[:, :dh]
            den = nd[:, dh:dh + 1]
            hs.append(num / jnp.maximum(jnp.abs(den), jnp.exp(-mt)))
            bl = bc[L - 1:L, :]
            gs = bl - bc + ic
            m_new = jnp.maximum(bl + m, jnp.max(gs, axis=0, keepdims=True))
            decay = jnp.exp(bl + m - m_new)
            wts = jnp.exp(gs - m_new)
            kw = (k.astype(F32) * wts).astype(BF16)
            new_carry.append((decay * cst + _dot_tn(kw, v_aug), m_new))
        hcat = jnp.concatenate(hs, axis=1)
        hcat = hcat * lax.rsqrt(_dot((hcat * hcat).astype(BF16), head_mean) + EPS) * og
        o = mo_ref[pl.ds(r0, L), :].astype(F32)
        y_ref[pl.ds(r0, L), :] = (hcat * _sigmoid(o)).astype(y_ref.dtype)
        return tuple(new_carry)

    init = tuple((jnp.zeros((dh, LANES), F32), jnp.zeros((1, 1), F32)) for _ in range(M_HEADS))
    lax.fori_loop(0, M_NCHUNK, chunk, init, unroll=2)


def _mlstm(mqk, mv, mo, gates, gatest, conv_w, conv_b, bias_row, bias_col, out_g):
    per_b = lambda b: (b, 0)
    const = lambda b: (0, 0)
    return pl.pallas_call(
        _mlstm_kernel,
        grid=(BATCH,),
        in_specs=[
            pl.BlockSpec((SEQ, 2 * M_WIDTH), per_b),
            pl.BlockSpec((SEQ, M_WIDTH), per_b),
            pl.BlockSpec((SEQ, M_WIDTH), per_b),
            pl.BlockSpec((SEQ, LANES), per_b),
            pl.BlockSpec((M_NCHUNK, 8, M_CHUNK), lambda b: (b, 0, 0)),
            pl.BlockSpec((CONV_K, 2 * M_WIDTH), const),
            pl.BlockSpec((1, 2 * M_WIDTH), const),
            pl.BlockSpec((1, LANES), const),
            pl.BlockSpec((8, 1), const),
            pl.BlockSpec((1, M_WIDTH), const),
        ],
        out_specs=pl.BlockSpec((SEQ, M_WIDTH), per_b),
        out_shape=jax.ShapeDtypeStruct((TOKENS, M_WIDTH), BF16),
        scratch_shapes=[
            pltpu.VMEM((SEQ, 2 * M_WIDTH), BF16),
            pltpu.VMEM((SEQ, LANES), F32),
            pltpu.VMEM((M_NCHUNK, 8, M_CHUNK), F32),
        ],
        compiler_params=pltpu.CompilerParams(
            dimension_semantics=("arbitrary",), vmem_limit_bytes=VMEM_LIMIT),
        name="mlstm",
    )(mqk, mv, mo, gates, gatest, conv_w, conv_b, bias_row, bias_col, out_g)


def _qk_norm_rope(x, gain, cos, sin):
    lane = lax.broadcasted_iota(jnp.int32, x.shape, 1)
    first = lane < A_HEAD_DIM
    sq = x * x
    tot = jnp.sum(sq, axis=-1, keepdims=True)
    s0 = jnp.sum(jnp.where(first, sq, 0.0), axis=-1, keepdims=True)
    ms = jnp.where(first, s0, tot - s0) * (1.0 / A_HEAD_DIM)
    xn = x * lax.rsqrt(ms + EPS) * gain
    in_low = (lane % A_HEAD_DIM) < ROPE_HALF
    swapped = jnp.where(in_low, pltpu.roll(xn, LANES - ROPE_HALF, 1), pltpu.roll(xn, ROPE_HALF, 1))
    return xn * cos + swapped * sin


def _moba_kernel(aq_ref, ak_ref, av_ref, cos_ref, sin_ref, qg_ref, kg_ref, og_ref, y_ref, kr_s, s_s):
    blk = MOBA_BLOCK
    dh = A_HEAD_DIM
    km_first = lax.broadcasted_iota(jnp.int32, (MOBA_NB, HEAD_PAIR), 1) < dh
    first = lax.broadcasted_iota(jnp.int32, (blk, HEAD_PAIR), 1) < dh
    blk_id = lax.broadcasted_iota(jnp.int32, (MOBA_NB, blk), 0)
    eye = (lax.broadcasted_iota(jnp.int32, (MOBA_NB, LANES), 0)
           == lax.broadcasted_iota(jnp.int32, (MOBA_NB, LANES), 1)).astype(F32)
    q_pos = lax.broadcasted_iota(jnp.int32, (2 * blk, blk), 0) % blk
    k_pos = lax.broadcasted_iota(jnp.int32, (2 * blk, blk), 1)
    causal = k_pos <= q_pos
    og = og_ref[...]

    km_rows = []
    prepped = {}

    def key_prep(j):
        rows = slice(j * blk, (j + 1) * blk)
        kr = _qk_norm_rope(ak_ref[rows, :].astype(F32), kg_ref[...], cos_ref[rows, :], sin_ref[rows, :])
        kr_s[rows, :] = kr.astype(BF16)
        km_rows.append(jnp.mean(kr, axis=0, keepdims=True))
        yield

    def query_prep(i):
        rows = slice(i * blk, (i + 1) * blk)
        qr = _qk_norm_rope(aq_ref[rows, :].astype(F32), qg_ref[...], cos_ref[rows, :], sin_ref[rows, :])
        qs = qr * (dh ** -0.5 * math.log2(math.e))
        q2 = jnp.concatenate([jnp.where(first, qs, 0.0), jnp.where(first, 0.0, qs)], axis=0).astype(BF16)
        yield

        bias = None
        if i > MOBA_TOPK:
            km = jnp.concatenate(km_rows[:i] + [jnp.zeros((MOBA_NB - i, HEAD_PAIR), F32)], axis=0)
            km_heads = (jnp.where(km_first, km, 0.0), jnp.where(km_first, 0.0, km))
            past = blk_id < i
            cols = []
            for kmh in km_heads:
                gate = jnp.where(past, _dot_nt(kmh, qr, precision=HIGHEST), -jnp.inf)
                rank = jnp.zeros((MOBA_NB, blk), F32)
                for m_blk in range(i):
                    gm = gate[m_blk:m_blk + 1, :]
                    ahead = (gm > gate) | ((gm == gate) & (m_blk < blk_id))
                    rank = rank + jnp.where(ahead, 1.0, 0.0)
                sel = jnp.where(past & (rank < float(MOBA_TOPK)), 1.0, 0.0)
                cols.append(_dot_tn(sel, eye))
            bias = jnp.where(jnp.concatenate(cols, axis=0) > 0.5, 0.0, -jnp.inf)
        prepped[i] = (q2, bias)
        yield

    def scores(i, out):
        q2, bias = prepped.pop(i)
        m_run = None
        for j in range(i + 1):
            s = _dot_nt(q2, kr_s[j * blk:(j + 1) * blk, :])
            if j == i:
                s = jnp.where(causal, s, -jnp.inf)
            elif bias is not None:
                s = s + bias[:, j:j + 1]
            s_s[i % 2, :, j * blk:(j + 1) * blk] = s
            part = jnp.maximum(s[:, :LANES], s[:, LANES:])
            m_run = part if m_run is None else jnp.maximum(m_run, part)
            yield
        out.append(jnp.max(m_run, axis=1, keepdims=True))

    def attend(i, m_fin):
        rows = slice(i * blk, (i + 1) * blk)
        l_run = jnp.zeros((2 * blk, LANES), F32)
        acc = jnp.zeros((2 * blk, HEAD_PAIR), F32)
        m_lanes = jnp.broadcast_to(m_fin, (2 * blk, LANES))
        for j in range(i + 1):
            p_lo = jnp.exp2(s_s[i % 2, :, j * blk:j * blk + LANES] - m_lanes)
            p_hi = jnp.exp2(s_s[i % 2, :, j * blk + LANES:(j + 1) * blk] - m_lanes)
            l_run = l_run + (p_lo + p_hi)
            p = jnp.concatenate([p_lo, p_hi], axis=1)
            acc = acc + _dot(p.astype(BF16), av_ref[j * blk:(j + 1) * blk, :])
            yield
        o2 = acc / jnp.sum(l_run, axis=1, keepdims=True)
        o = jnp.where(first, o2[:blk, :], o2[blk:, :])
        sq = o * o
        tot = jnp.sum(sq, axis=-1, keepdims=True)
        s0 = jnp.sum(jnp.where(first, sq, 0.0), axis=-1, keepdims=True)
        ms = jnp.where(first, s0, tot - s0) * (1.0 / dh)
        y_ref[rows, :] = (o * lax.rsqrt(ms + EPS) * og).astype(y_ref.dtype)

    def interleave(*stages):
        stages = list(stages)
        while stages:
            for stage in list(stages):
                try:
                    next(stage)
                except StopIteration:
                    stages.remove(stage)

    m_cur = []
    interleave(key_prep(0), key_prep(1), query_prep(0), query_prep(1))
    interleave(scores(0, m_cur))
    for i in range(MOBA_NB):
        m_next = []
        stages = [attend(i, m_cur[0])]
        if i + 1 < MOBA_NB:
            stages.insert(0, scores(i + 1, m_next))
        if i + 2 < MOBA_NB:
            stages += [key_prep(i + 2), query_prep(i + 2)]
        interleave(*stages)
        m_cur = m_next


def _moba(aq, ak, av, cos_t, sin_t, q_gain, k_gain, out_gain):
    n_pairs = A_HEADS // 2
    const = lambda b, p: (0, 0)
    pair = lambda b, p: (b, p)
    return pl.pallas_call(
        _moba_kernel,
        grid=(BATCH, n_pairs),
        in_specs=[
            pl.BlockSpec((SEQ, HEAD_PAIR), pair),
            pl.BlockSpec((SEQ, HEAD_PAIR), pair),
            pl.BlockSpec((SEQ, HEAD_PAIR), pair),
            pl.BlockSpec((SEQ, HEAD_PAIR), const),
            pl.BlockSpec((SEQ, HEAD_PAIR), const),
            pl.BlockSpec((1, HEAD_PAIR), const),
            pl.BlockSpec((1, HEAD_PAIR), const),
            pl.BlockSpec((None, 1, HEAD_PAIR), lambda b, p: (p, 0, 0)),
        ],
        out_specs=pl.BlockSpec((SEQ, HEAD_PAIR), pair),
        out_shape=jax.ShapeDtypeStruct((TOKENS, A_WIDTH), BF16),
        scratch_shapes=[
            pltpu.VMEM((SEQ, HEAD_PAIR), BF16),
            pltpu.VMEM((2, 2 * MOBA_BLOCK, SEQ), F32),
        ],
        compiler_params=pltpu.CompilerParams(
            dimension_semantics=("arbitrary", "arbitrary"), vmem_limit_bytes=VMEM_LIMIT),
        name="moba",
    )(aq, ak, av, cos_t, sin_t, q_gain, k_gain, out_gain)


def _s5_kernel(u_ref, abar_ref, bre_ref, bim_ref, cre_ref, cim_ref, d_ref, gw_ref, gb_ref, gmat_ref, og_ref,
               y_ref, xr_s, xi_s, st_s, u_s, y_s):
    step = pl.program_id(0)
    rows = S5_STEPS * BATCH
    tile = 256
    n_tiles = rows // tile
    halves = S_WIDTH // LANES

    @pl.when(step == 0)
    def _init():
        st_s[...] = jnp.zeros_like(st_s)

    for b in range(BATCH):
        for hv in range(halves):
            lanes = slice(b * S_WIDTH + hv * LANES, b * S_WIDTH + (hv + 1) * LANES)
            u_s[hv, pl.ds(b, S5_STEPS, stride=BATCH), :] = u_ref[:, lanes].astype(F32)

    def u_rows(r):
        return jnp.concatenate([u_s[hv, r * tile:(r + 1) * tile, :] for hv in range(halves)], axis=1)

    for r in range(n_tiles):
        u = u_rows(r).astype(BF16)
        t0 = r * tile // BATCH
        xr_s[t0:t0 + tile // BATCH] = _dot(u, bre_ref[...]).reshape(tile // BATCH, BATCH, S_FLAT)
        xi_s[t0:t0 + tile // BATCH] = _dot(u, bim_ref[...]).reshape(tile // BATCH, BATCH, S_FLAT)

    ar = jnp.broadcast_to(abar_ref[0:1, :], (BATCH, S_FLAT))
    ai = jnp.broadcast_to(abar_ref[1:2, :], (BATCH, S_FLAT))

    def scan_step(t, carry):
        sr, si = carry
        nr = ar * sr - ai * si + xr_s[t]
        ni = ar * si + ai * sr + xi_s[t]
        xr_s[t] = nr
        xi_s[t] = ni
        return nr, ni

    sr, si = lax.fori_loop(0, S5_STEPS, scan_step, (st_s[0], st_s[1]), unroll=4)
    st_s[0] = sr
    st_s[1] = si

    for r in range(n_tiles):
        t0 = r * tile // BATCH
        xr = xr_s[t0:t0 + tile // BATCH].reshape(tile, S_FLAT).astype(BF16)
        xi = xi_s[t0:t0 + tile // BATCH].reshape(tile, S_FLAT).astype(BF16)
        y = _dot(xr, cre_ref[...]) - _dot(xi, cim_ref[...]) + d_ref[...] * u_rows(r)
        z = 0.5 * y * (1.0 + jnp.tanh(math.sqrt(2.0 / math.pi) * (y + 0.044715 * (y * y * y))))
        y2 = z * _sigmoid(_dot(z.astype(BF16), gw_ref[...]) + gb_ref[...])
        ms = _dot((y2 * y2).astype(BF16), gmat_ref[...])
        yn = y2 * lax.rsqrt(ms + EPS) * og_ref[...]
        for hv in range(halves):
            y_s[hv, r * tile:(r + 1) * tile, :] = yn[:, hv * LANES:(hv + 1) * LANES]

    for b in range(BATCH):
        for hv in range(halves):
            lanes = slice(b * S_WIDTH + hv * LANES, b * S_WIDTH + (hv + 1) * LANES)
            y_ref[:, lanes] = y_s[hv, pl.ds(b, S5_STEPS, stride=BATCH), :].astype(y_ref.dtype)


def _s5(u_sb, abar, bre, bim, cre, cim, d_row, glu_w, glu_b, gmat, out_g):
    rows = S5_STEPS * BATCH
    const = lambda s: (0, 0)
    return pl.pallas_call(
        _s5_kernel,
        grid=(SEQ // S5_STEPS,),
        in_specs=[
            pl.BlockSpec((S5_STEPS, BATCH * S_WIDTH), lambda s: (s, 0)),
            pl.BlockSpec((2, S_FLAT), const),
            pl.BlockSpec((S_WIDTH, S_FLAT), const),
            pl.BlockSpec((S_WIDTH, S_FLAT), const),
            pl.BlockSpec((S_FLAT, S_WIDTH), const),
            pl.BlockSpec((S_FLAT, S_WIDTH), const),
            pl.BlockSpec((1, S_WIDTH), const),
            pl.BlockSpec((S_WIDTH, S_WIDTH), const),
            pl.BlockSpec((1, S_WIDTH), const),
            pl.BlockSpec((S_WIDTH, S_WIDTH), const),
            pl.BlockSpec((1, S_WIDTH), const),
        ],
        out_specs=pl.BlockSpec((S5_STEPS, BATCH * S_WIDTH), lambda s: (s, 0)),
        out_shape=jax.ShapeDtypeStruct((SEQ, BATCH * S_WIDTH), BF16),
        scratch_shapes=[
            pltpu.VMEM((S5_STEPS, BATCH, S_FLAT), F32),
            pltpu.VMEM((S5_STEPS, BATCH, S_FLAT), F32),
            pltpu.VMEM((2, BATCH, S_FLAT), F32),
            pltpu.VMEM((S_WIDTH // LANES, rows, LANES), F32),
            pltpu.VMEM((S_WIDTH // LANES, rows, LANES), F32),
        ],
        compiler_params=pltpu.CompilerParams(
            dimension_semantics=("arbitrary",), vmem_limit_bytes=VMEM_LIMIT),
        name="s5",
    )(u_sb, abar, bre, bim, cre, cim, d_row, glu_w, glu_b, gmat, out_g)


def _outproj_kernel(x_ref, ym_ref, ya_ref, ys_ref, w_ref, g_ref, wr_ref, br_ref,
                    x1_ref, xn_ref, route_ref, rows_s):
    x1 = x_ref[...]
    x1 = x1 + _dot(ym_ref[...], w_ref[0:M_WIDTH, :])
    x1 = x1 + _dot(ya_ref[...], w_ref[M_WIDTH:M_WIDTH + A_WIDTH, :])
    x1 = x1 + _dot(ys_ref[...], w_ref[M_WIDTH + A_WIDTH:, :])
    x1_ref[...] = x1
    xn = x1 * lax.rsqrt(jnp.mean(x1 * x1, axis=-1, keepdims=True) + EPS) * g_ref[...]
    for j in range(ROW_TILES):
        rows_s[pl.ds(j, TM_PROJ, stride=ROW_TILES), :] = xn[:, j * LANES:(j + 1) * LANES]
    xn_ref[...] = rows_s[...].astype(xn_ref.dtype)
    logits = _split_dot(xn, wr_ref[...]) + br_ref[...]

    lane = lax.broadcasted_iota(jnp.int32, logits.shape, 1).astype(F32)
    is_grp = lane < N_GROUPS
    neg = -jnp.inf
    gl = jnp.where(is_grp, logits, neg)
    gmax = jnp.max(gl, axis=-1, keepdims=True)
    gsum = jnp.sum(jnp.where(is_grp, jnp.exp(logits - gmax), 0.0), axis=-1, keepdims=True)
    g_sel = jnp.min(jnp.where(gl == gmax, lane, float(LANES)), axis=-1, keepdims=True)
    lo = N_GROUPS + EXPERTS_PER_GROUP * g_sel
    el = jnp.where((lane >= lo) & (lane < lo + EXPERTS_PER_GROUP), logits, neg)
    t1 = jnp.max(el, axis=-1, keepdims=True)
    i1 = jnp.min(jnp.where(el == t1, lane, float(LANES)), axis=-1, keepdims=True)
    el2 = jnp.where(lane == i1, neg, el)
    t2 = jnp.max(el2, axis=-1, keepdims=True)
    i2 = jnp.min(jnp.where(el2 == t2, lane, float(LANES)), axis=-1, keepdims=True)
    e21 = jnp.exp(t2 - t1)
    w1 = 1.0 / ((1.0 + e21) * gsum)
    w2 = e21 / ((1.0 + e21) * gsum)
    route = jnp.where(lane == 0.0, i1 - N_GROUPS,
                      jnp.where(lane == 1.0, i2 - N_GROUPS,
                                jnp.where(lane == 2.0, w1, jnp.where(lane == 3.0, w2, 0.0))))
    route_ref[...] = route


def _outproj(x, y_m, y_a, y_s_sb, w_out, g2, w_router, b_router):
    n_s = SEQ // TM_PROJ
    row = lambda b, s: (b * n_s + s, 0)
    const = lambda b, s: (0, 0)
    return pl.pallas_call(
        _outproj_kernel,
        grid=(BATCH, n_s),
        in_specs=[
            pl.BlockSpec((TM_PROJ, D_MODEL), row),
            pl.BlockSpec((TM_PROJ, M_WIDTH), row),
            pl.BlockSpec((TM_PROJ, A_WIDTH), row),
            pl.BlockSpec((TM_PROJ, S_WIDTH), lambda b, s: (s, b)),
            pl.BlockSpec((D_MIX, D_MODEL), const),
            pl.BlockSpec((1, D_MODEL), const),
            pl.BlockSpec((D_MODEL, 2 * LANES), const),
            pl.BlockSpec((1, LANES), const),
        ],
        out_specs=(
            pl.BlockSpec((TM_PROJ, D_MODEL), row),
            pl.BlockSpec((TM_PROJ * ROW_TILES, LANES), row),
            pl.BlockSpec((TM_PROJ, LANES), row),
        ),
        out_shape=(
            jax.ShapeDtypeStruct((TOKENS, D_MODEL), F32),
            jax.ShapeDtypeStruct((TOKENS * ROW_TILES, LANES), BF16),
            jax.ShapeDtypeStruct((TOKENS, LANES), F32),
        ),
        scratch_shapes=[pltpu.VMEM((TM_PROJ * ROW_TILES, LANES), F32)],
        compiler_params=pltpu.CompilerParams(
            dimension_semantics=("arbitrary", "arbitrary"), vmem_limit_bytes=VMEM_LIMIT),
        name="outproj",
    )(x, y_m, y_a, y_s_sb, w_out, g2, w_router, b_router)


def _ffn_kernel(be_ref, src_ref, xs_ref, wg_ref, wu_ref, wd_ref, ys_ref, rows_s):
    del be_ref
    i = pl.program_id(0)

    @pl.when(src_ref[i] == i)
    def _expert_block():
        rows_s[...] = xs_ref[...].astype(F32)
        xb = jnp.concatenate([rows_s[pl.ds(j, FFN_BLOCK, stride=ROW_TILES), :] for j in range(ROW_TILES)],
                             axis=1).astype(BF16)
        g = _dot(xb, wg_ref[...])
        u = _dot(xb, wu_ref[...])
        hmid = (g * _sigmoid(g) * u).astype(BF16)
        y = _dot(hmid, wd_ref[...])
        for j in range(ROW_TILES):
            rows_s[pl.ds(j, FFN_BLOCK, stride=ROW_TILES), :] = y[:, j * LANES:(j + 1) * LANES]
        ys_ref[...] = rows_s[...].astype(ys_ref.dtype)

    @pl.when(src_ref[i] != i)
    def _unused_tail_block():
        ys_ref[...] = jnp.zeros_like(ys_ref)


def _expert_ffn(layer, blk_exp, blk_src, xs, w_gate, w_up, w_down):
    nblk = blk_exp.shape[0]
    block = (FFN_BLOCK * ROW_TILES, LANES)
    grid_spec = pltpu.PrefetchScalarGridSpec(
        num_scalar_prefetch=2,
        grid=(nblk,),
        in_specs=[
            pl.BlockSpec(block, lambda i, be, src: (src[i], 0)),
            pl.BlockSpec((None, None, D_MODEL, D_EXPERT), lambda i, be, src: (layer, be[i], 0, 0)),
            pl.BlockSpec((None, None, D_MODEL, D_EXPERT), lambda i, be, src: (layer, be[i], 0, 0)),
            pl.BlockSpec((None, None, D_EXPERT, D_MODEL), lambda i, be, src: (layer, be[i], 0, 0)),
        ],
        out_specs=pl.BlockSpec(block, lambda i, be, src: (i, 0)),
        scratch_shapes=[pltpu.VMEM(block, F32)],
    )
    return pl.pallas_call(
        _ffn_kernel,
        grid_spec=grid_spec,
        out_shape=jax.ShapeDtypeStruct((nblk * FFN_BLOCK * ROW_TILES, LANES), BF16),
        compiler_params=pltpu.CompilerParams(
            dimension_semantics=("arbitrary",), vmem_limit_bytes=VMEM_LIMIT),
        name="expert_ffn",
    )(blk_exp, blk_src, xs, w_gate, w_up, w_down)


def _plan_kernel(e_ref, ps_ref, dest_ref, run_s, tri_s):
    i = pl.program_id(0)

    @pl.when(i == 0)
    def _init():
        run_s[...] = ps_ref[...]
        s_i = lax.broadcasted_iota(jnp.int32, (PLAN_TILE, PLAN_TILE), 0)
        t_i = lax.broadcasted_iota(jnp.int32, (PLAN_TILE, PLAN_TILE), 1)
        tri_s[...] = (s_i < t_i).astype(BF16)

    expert = lax.broadcasted_iota(jnp.int32, (N_EXPERTS, PLAN_TILE), 0)
    onehot = jnp.where(e_ref[...] == expert, 1.0, 0.0)
    before = _dot(onehot.astype(BF16), tri_s[...])
    dest = jnp.sum(onehot * (before + run_s[:, 0:1]), axis=0, keepdims=True)
    dest_ref[...] = dest.astype(jnp.int32)
    run_s[...] = run_s[...] + jnp.sum(onehot, axis=1, keepdims=True)


def _plan(e_rows, pstart_col):
    n_tiles = e_rows.shape[0]
    return pl.pallas_call(
        _plan_kernel,
        grid=(n_tiles,),
        in_specs=[
            pl.BlockSpec((None, 1, PLAN_TILE), lambda i: (i, 0, 0)),
            pl.BlockSpec((N_EXPERTS, LANES), lambda i: (0, 0)),
        ],
        out_specs=pl.BlockSpec((None, 1, PLAN_TILE), lambda i: (i, 0, 0)),
        out_shape=jax.ShapeDtypeStruct((n_tiles, 1, PLAN_TILE), jnp.int32),
        scratch_shapes=[pltpu.VMEM((N_EXPERTS, LANES), F32), pltpu.VMEM((PLAN_TILE, PLAN_TILE), BF16)],
        compiler_params=pltpu.CompilerParams(
            dimension_semantics=("arbitrary",), vmem_limit_bytes=VMEM_LIMIT),
        name="moe_plan",
    )(e_rows, pstart_col)


def _dispatch_kernel(fill_ref, d0_ref, d1_ref, xn_ref, xs_hbm, zero_s, sem, fill_sem):
    @pl.when(pl.program_id(0) == 0)
    def _zero_padding():
        zero_s[...] = jnp.zeros_like(zero_s)

        def fill(c):
            start = pl.multiple_of(fill_ref[0, c] * ROW_TILES, ROW_TILES)
            return pltpu.make_async_copy(zero_s, xs_hbm.at[pl.ds(start, FFN_BLOCK * ROW_TILES), :], fill_sem)

        for c in range(N_FILLS):
            pl.when(fill_ref[0, c] >= 0)(lambda c=c: fill(c).start())
        for c in range(N_FILLS):
            pl.when(fill_ref[0, c] >= 0)(lambda c=c: fill(c).wait())

    def row_copy(dref, t):
        src = xn_ref.at[pl.ds(pl.multiple_of(t * ROW_TILES, ROW_TILES), ROW_TILES), :]
        dst = xs_hbm.at[pl.ds(pl.multiple_of(dref[0, t] * ROW_TILES, ROW_TILES), ROW_TILES), :]
        return pltpu.make_async_copy(src, dst, sem)

    def start(t, carry):
        row_copy(d0_ref, t).start()
        row_copy(d1_ref, t).start(priority=1)
        return carry

    lax.fori_loop(0, DISPATCH_TOKENS, start, 0, unroll=8)
    for _ in range(EXPERT_TOPK):
        pltpu.make_async_copy(xn_ref, xs_hbm.at[pl.ds(0, DISPATCH_TOKENS * ROW_TILES), :], sem).wait()


def _dispatch(fill_rows, dest, xn, n_rows):
    n_tiles = TOKENS // DISPATCH_TOKENS
    dest4 = dest.reshape(EXPERT_TOPK, n_tiles, 1, DISPATCH_TOKENS)
    smem_tile = lambda k: pl.BlockSpec((None, None, 1, DISPATCH_TOKENS), lambda i: (k, i, 0, 0),
                                       memory_space=pltpu.SMEM)
    return pl.pallas_call(
        _dispatch_kernel,
        grid=(n_tiles,),
        in_specs=[
            pl.BlockSpec(memory_space=pltpu.SMEM),
            smem_tile(0),
            smem_tile(1),
            pl.BlockSpec((DISPATCH_TOKENS * ROW_TILES, LANES), lambda i: (i, 0)),
        ],
        out_specs=pl.BlockSpec(memory_space=pl.ANY),
        out_shape=jax.ShapeDtypeStruct((n_rows * ROW_TILES, LANES), BF16),
        scratch_shapes=[
            pltpu.VMEM((FFN_BLOCK * ROW_TILES, LANES), BF16),
            pltpu.SemaphoreType.DMA(()),
            pltpu.SemaphoreType.DMA(()),
        ],
        compiler_params=pltpu.CompilerParams(
            dimension_semantics=("arbitrary",), vmem_limit_bytes=VMEM_LIMIT),
        name="moe_dispatch",
    )(fill_rows, dest4, dest4, xn)


def _combine_kernel(d0_ref, d1_ref, d0_next_ref, d1_next_ref, x1_ref, route_ref, ys_hbm, out_ref, buf, rows_s,
                    sem):
    i = pl.program_id(0)
    n = pl.num_programs(0)

    def start_tile(drefs, slot):
        def body(t, carry):
            for k in range(EXPERT_TOPK):
                src = ys_hbm.at[pl.ds(pl.multiple_of(drefs[k][0, t] * ROW_TILES, ROW_TILES), ROW_TILES), :]
                dst = buf.at[slot, k, pl.ds(pl.multiple_of(t * ROW_TILES, ROW_TILES), ROW_TILES), :]
                pltpu.make_async_copy(src, dst, sem.at[slot]).start(priority=k)
            return carry
        lax.fori_loop(0, COMBINE_TOKENS, body, 0, unroll=8)

    slot = i % 2

    @pl.when(i == 0)
    def _first():
        start_tile((d0_ref, d1_ref), 0)

    @pl.when(i + 1 < n)
    def _prefetch():
        start_tile((d0_next_ref, d1_next_ref), 1 - slot)

    for k in range(EXPERT_TOPK):
        pltpu.make_async_copy(ys_hbm.at[pl.ds(0, COMBINE_TOKENS * ROW_TILES), :], buf.at[slot, k],
                              sem.at[slot]).wait()
    w = route_ref[...]
    w1 = w[:, 2:3]
    w2 = w[:, 3:4]
    for k in range(EXPERT_TOPK):
        rows_s[k] = buf[slot, k].astype(F32)
    for j in range(ROW_TILES):
        cols = slice(j * LANES, (j + 1) * LANES)
        chunk = pl.ds(j, COMBINE_TOKENS, stride=ROW_TILES)
        out_ref[:, cols] = x1_ref[:, cols] + w1 * rows_s[0, chunk, :] + w2 * rows_s[1, chunk, :]


def _combine(dest, x1, route, ys3):
    n_tiles = TOKENS // COMBINE_TOKENS
    dest4 = dest.reshape(EXPERT_TOPK, n_tiles, 1, COMBINE_TOKENS)
    smem_tile = lambda k, nxt: pl.BlockSpec(
        (None, None, 1, COMBINE_TOKENS), lambda i: (k, jnp.minimum(i + nxt, n_tiles - 1), 0, 0),
        memory_space=pltpu.SMEM)
    return pl.pallas_call(
        _combine_kernel,
        grid=(n_tiles,),
        in_specs=[
            smem_tile(0, 0), smem_tile(1, 0), smem_tile(0, 1), smem_tile(1, 1),
            pl.BlockSpec((COMBINE_TOKENS, D_MODEL), lambda i: (i, 0)),
            pl.BlockSpec((COMBINE_TOKENS, LANES), lambda i: (i, 0)),
            pl.BlockSpec(memory_space=pl.ANY),
        ],
        out_specs=pl.BlockSpec((COMBINE_TOKENS, D_MODEL), lambda i: (i, 0)),
        out_shape=jax.ShapeDtypeStruct((TOKENS, D_MODEL), F32),
        scratch_shapes=[
            pltpu.VMEM((2, EXPERT_TOPK, COMBINE_TOKENS * ROW_TILES, LANES), BF16),
            pltpu.VMEM((EXPERT_TOPK, COMBINE_TOKENS * ROW_TILES, LANES), F32),
            pltpu.SemaphoreType.DMA((2,)),
        ],
        compiler_params=pltpu.CompilerParams(
            dimension_semantics=("arbitrary",), vmem_limit_bytes=VMEM_LIMIT),
        name="moe_combine",
    )(dest4, dest4, dest4, dest4, x1, route, ys3)


def _rope_tables():
    inv = ROPE_THETA ** (-np.arange(ROPE_HALF, dtype=np.float64) * 2.0 / ROPE_DIM)
    ang = np.arange(SEQ, dtype=np.float64)[:, None] * inv[None, :]
    cos = np.ones((SEQ, A_HEAD_DIM))
    sin = np.zeros((SEQ, A_HEAD_DIM))
    cos[:, :ROPE_HALF] = np.cos(ang)
    cos[:, ROPE_HALF:ROPE_DIM] = np.cos(ang)
    sin[:, :ROPE_HALF] = -np.sin(ang)
    sin[:, ROPE_HALF:ROPE_DIM] = np.sin(ang)
    return (jnp.asarray(np.tile(cos, (1, 2)), F32), jnp.asarray(np.tile(sin, (1, 2)), F32))


def _split_w_in(w_in):
    sizes = [M_WIDTH, M_WIDTH, M_WIDTH, M_WIDTH, M_HEADS, M_HEADS, A_WIDTH, A_WIDTH, A_WIDTH, S_WIDTH]
    offs = np.cumsum([0] + sizes)
    mq, mk, mv, mo, mi, mf, aq, ak, av, su = (w_in[:, offs[n]:offs[n + 1]] for n in range(len(sizes)))
    w_main = jnp.concatenate([aq, ak, av, mq, mk, mv, mo, su], axis=1).astype(BF16)
    gate = jnp.concatenate([mi, mf], axis=1)
    return w_main, _hi_lo(jnp.pad(gate, ((0, 0), (0, LANES - 2 * M_HEADS))))


def _s5_params(a_re, a_im, b_re, b_im, c_re, c_im, log_dt):
    dt = jnp.exp(log_dt)[:, None]
    mag = jnp.exp(a_re * dt)
    abar_re, abar_im = mag * jnp.cos(a_im * dt), mag * jnp.sin(a_im * dt)
    zr, zi = abar_re - 1.0, abar_im
    den = a_re * a_re + a_im * a_im
    fr, fi = (zr * a_re + zi * a_im) / den, (zi * a_re - zr * a_im) / den
    bbar_re = fr[..., None] * b_re - fi[..., None] * b_im
    bbar_im = fr[..., None] * b_im + fi[..., None] * b_re
    eye = jnp.eye(S_GROUPS, dtype=F32)
    dense_b = lambda t: jnp.einsum('gph,gk->ghkp', t, eye).reshape(S_WIDTH, S_FLAT).astype(BF16)
    dense_c = lambda t: jnp.einsum('ghp,gk->gpkh', t, eye).reshape(S_FLAT, S_WIDTH).astype(BF16)
    abar = jnp.stack([abar_re.reshape(S_FLAT), abar_im.reshape(S_FLAT)])
    return abar, dense_b(bbar_re), dense_b(bbar_im), dense_c(c_re), dense_c(c_im)


def _moe(layer, x1, xn, route, w_gate, w_up, w_down):
    n_assign = TOKENS * EXPERT_TOPK
    e_flat = route[:, :EXPERT_TOPK].astype(jnp.int32).T.reshape(n_assign)
    experts = jnp.arange(N_EXPERTS, dtype=jnp.int32)
    counts = jnp.sum((e_flat[:, None] == experts[None, :]).astype(jnp.int32), axis=0)
    padded = (counts + FFN_BLOCK - 1) // FFN_BLOCK * FFN_BLOCK
    pends = jnp.cumsum(padded)
    pstarts = pends - padded
    n_rows = (n_assign + N_EXPERTS * (FFN_BLOCK - 1) + FFN_BLOCK - 1) // FFN_BLOCK * FFN_BLOCK
    nblk = n_rows // FFN_BLOCK
    blk_start = jnp.arange(nblk, dtype=jnp.int32) * FFN_BLOCK
    blk_exp = jnp.minimum(jnp.sum((pends[None, :] <= blk_start[:, None]).astype(jnp.int32), axis=1),
                          N_EXPERTS - 1).astype(jnp.int32)
    pstart_col = jnp.broadcast_to(pstarts.astype(F32)[:, None], (N_EXPERTS, LANES))
    dest = _plan(e_flat.reshape(n_assign // PLAN_TILE, 1, PLAN_TILE), pstart_col)
    tail = pends[-1] + jnp.arange(N_EXPERTS, dtype=jnp.int32) * FFN_BLOCK
    fill_rows = jnp.concatenate([jnp.where(padded > 0, pends - FFN_BLOCK, -1), jnp.where(tail < n_rows, tail, -1)])
    fill_rows = fill_rows.astype(jnp.int32).reshape(1, N_FILLS)
    xs = _dispatch(fill_rows, dest, xn, n_rows)
    blk_src = jnp.minimum(jnp.arange(nblk, dtype=jnp.int32), pends[-1] // FFN_BLOCK - 1).astype(jnp.int32)
    ys = _expert_ffn(layer, blk_exp, blk_src, xs, w_gate, w_up, w_down)
    return _combine(dest, x1, route, ys)


def kernel(x, norm1_g, w_in, m_bias_i, m_bias_f, m_conv_w, m_conv_b, m_out_g, a_q_g, a_k_g, a_out_g, s_a_re, s_a_im, s_b_re, s_b_im, s_c_re, s_c_im, s_d, s_log_dt, s_glu_w, s_glu_b, s_out_g, w_out, norm2_g, r_group_w, r_group_b, r_expert_w, r_expert_b, e_w_gate, e_w_up, e_w_down):
    cos_t, sin_t = _rope_tables()
    gmat = jnp.asarray(np.kron(np.eye(S_GROUPS), np.full((S_GROUP_DIM, S_GROUP_DIM), 1.0 / S_GROUP_DIM)), BF16)
    xf = x.reshape(TOKENS, D_MODEL)
    for l in range(DEPTH):
        w_main, w_gate = _split_w_in(w_in[l])
        aq, ak, av, mqk, mv, mo, su, gates, gatest = _inproj(xf, norm1_g[l].reshape(1, D_MODEL), w_main, w_gate)

        bias = jnp.concatenate([m_bias_i[l], m_bias_f[l]])
        bias_row = jnp.pad(bias, (0, LANES - 2 * M_HEADS)).reshape(1, LANES)
        y_m = _mlstm(mqk, mv, mo, gates, gatest, m_conv_w[l], m_conv_b[l].reshape(1, 2 * M_WIDTH),
                     bias_row, bias.reshape(2 * M_HEADS, 1), m_out_g[l].reshape(1, M_WIDTH))

        y_a = _moba(aq, ak, av, cos_t, sin_t,
                    jnp.tile(a_q_g[l], 2).reshape(1, HEAD_PAIR), jnp.tile(a_k_g[l], 2).reshape(1, HEAD_PAIR),
                    a_out_g[l].reshape(A_HEADS // 2, 1, HEAD_PAIR))

        abar, bre, bim, cre, cim = _s5_params(s_a_re[l], s_a_im[l], s_b_re[l], s_b_im[l],
                                              s_c_re[l], s_c_im[l], s_log_dt[l])
        y_s = _s5(su, abar, bre, bim, cre, cim,
                  s_d[l].reshape(1, S_WIDTH), s_glu_w[l].astype(BF16), s_glu_b[l].reshape(1, S_WIDTH),
                  gmat, s_out_g[l].reshape(1, S_WIDTH))

        w_router = jnp.pad(jnp.concatenate([r_group_w[l], r_expert_w[l]], axis=1),
                           ((0, 0), (0, LANES - N_GROUPS - N_EXPERTS)))
        b_router = jnp.pad(jnp.concatenate([r_group_b[l], r_expert_b[l]]),
                           (0, LANES - N_GROUPS - N_EXPERTS)).reshape(1, LANES)
        x1, xn, route = _outproj(xf, y_m, y_a, y_s, w_out[l].astype(BF16),
                                 norm2_g[l].reshape(1, D_MODEL), _hi_lo(w_router), b_router)
        xf = _moe(l, x1, xn, route, e_w_gate, e_w_up, e_w_down)
    return xf.reshape(BATCH, SEQ, D_MODEL)
```

```python
import functools
import math

import numpy as np
import jax
import jax.numpy as jnp
from jax import lax
from jax.experimental import pallas as pl
from jax.experimental.pallas import tpu as pltpu

F32 = jnp.float32
BF16 = jnp.bfloat16

D_MODEL = 1024
BATCH = 8
SEQ = 2048
DEPTH = 2
TOKENS = BATCH * SEQ

M_HEADS = 4
M_HEAD_DIM = 64
M_WIDTH = M_HEADS * M_HEAD_DIM
A_HEADS = 8
A_HEAD_DIM = 64
A_WIDTH = A_HEADS * A_HEAD_DIM
S_GROUPS = 16
S_GROUP_DIM = 16
S_WIDTH = S_GROUPS * S_GROUP_DIM
S_STATE = 64
S_FLAT = S_GROUPS * S_STATE
D_MIX = M_WIDTH + A_WIDTH + S_WIDTH

CONV_K = 4
MOBA_BLOCK = 256
MOBA_NB = SEQ // MOBA_BLOCK
MOBA_TOPK = 3
ROPE_THETA = 500000.0
ROPE_DIM = A_HEAD_DIM // 4
ROPE_HALF = ROPE_DIM // 2

N_GROUPS = 4
EXPERTS_PER_GROUP = 8
N_EXPERTS = N_GROUPS * EXPERTS_PER_GROUP
EXPERT_TOPK = 2
D_EXPERT = 512
DISPATCH_BLOCK = 128
EPS = 1e-6

LANES = 128
ROW_TILES = D_MODEL // LANES
HEAD_PAIR = 2 * A_HEAD_DIM

W_MAIN = 3 * A_WIDTH + 4 * M_WIDTH + S_WIDTH
TM_PROJ = 512
M_CHUNK = 256
M_NCHUNK = SEQ // M_CHUNK
S5_STEPS = 256
FFN_BLOCK = 512
N_FILLS = 2 * N_EXPERTS
PLAN_TILE = 1024
DISPATCH_TOKENS = 512
COMBINE_TOKENS = 256
VMEM_LIMIT = 56 * 1024 * 1024

HIGHEST = lax.Precision.HIGHEST


def _dot(a, b, precision=None):
    return jnp.dot(a, b, preferred_element_type=F32, precision=precision)


def _dot_nt(a, b, precision=None):
    return lax.dot_general(a, b, (((1,), (1,)), ((), ())), preferred_element_type=F32, precision=precision)


def _dot_tn(a, b, precision=None):
    return lax.dot_general(a, b, (((0,), (0,)), ((), ())), preferred_element_type=F32, precision=precision)


def _split_dot(a, w_cat):
    n = w_cat.shape[1] // 2
    a_hi = a.astype(BF16)
    a_lo = (a - a_hi.astype(F32)).astype(BF16)
    both = _dot(a_hi, w_cat)
    return both[:, :n] + both[:, n:] + _dot(a_lo, w_cat[:, :n])


def _bf16_terms(x):
    hi = x.astype(BF16)
    rest = x - hi.astype(F32)
    mid = rest.astype(BF16)
    return hi, mid, (rest - mid.astype(F32)).astype(BF16)


def _hi_lo(w):
    w_hi = w.astype(BF16)
    return jnp.concatenate([w_hi, (w - w_hi.astype(F32)).astype(BF16)], axis=1)


def _log_sigmoid(x):
    return jnp.minimum(x, 0.0) - jnp.log1p(jnp.exp(-jnp.abs(x)))


def _sigmoid(x):
    return 1.0 / (1.0 + jnp.exp(-x))


def _inproj_kernel(x_ref, g_ref, w_ref, wg_ref,
                   aq_ref, ak_ref, av_ref, mqk_ref, mv_ref, mo_ref, su_ref, gates_ref, gatest_ref):
    x = x_ref[...]
    h = x * lax.rsqrt(jnp.mean(x * x, axis=-1, keepdims=True) + EPS) * g_ref[...]
    hb = h.astype(BF16)
    off = 0
    for ref, width in ((aq_ref, A_WIDTH), (ak_ref, A_WIDTH), (av_ref, A_WIDTH), (mqk_ref, 2 * M_WIDTH),
                       (mv_ref, M_WIDTH), (mo_ref, M_WIDTH), (su_ref, S_WIDTH)):
        ref[...] = _dot(hb, w_ref[:, off:off + width]).astype(ref.dtype)
        off += width
    gates = _split_dot(h, wg_ref[...])
    gates_ref[...] = gates
    gt = gates.T[:8, :]
    for c in range(TM_PROJ // M_CHUNK):
        gatest_ref[c] = gt[:, c * M_CHUNK:(c + 1) * M_CHUNK]


def _inproj(x, g, w_main, w_gate):
    n_s = SEQ // TM_PROJ
    row = lambda b, s: (b * n_s + s, 0)
    const = lambda b, s: (0, 0)
    out_shapes = (
        jax.ShapeDtypeStruct((TOKENS, A_WIDTH), BF16),
        jax.ShapeDtypeStruct((TOKENS, A_WIDTH), BF16),
        jax.ShapeDtypeStruct((TOKENS, A_WIDTH), BF16),
        jax.ShapeDtypeStruct((TOKENS, 2 * M_WIDTH), BF16),
        jax.ShapeDtypeStruct((TOKENS, M_WIDTH), BF16),
        jax.ShapeDtypeStruct((TOKENS, M_WIDTH), BF16),
        jax.ShapeDtypeStruct((SEQ, BATCH * S_WIDTH), BF16),
        jax.ShapeDtypeStruct((TOKENS, LANES), F32),
        jax.ShapeDtypeStruct((TOKENS // M_CHUNK, 8, M_CHUNK), F32),
    )
    out_specs = (
        pl.BlockSpec((TM_PROJ, A_WIDTH), row),
        pl.BlockSpec((TM_PROJ, A_WIDTH), row),
        pl.BlockSpec((TM_PROJ, A_WIDTH), row),
        pl.BlockSpec((TM_PROJ, 2 * M_WIDTH), row),
        pl.BlockSpec((TM_PROJ, M_WIDTH), row),
        pl.BlockSpec((TM_PROJ, M_WIDTH), row),
        pl.BlockSpec((TM_PROJ, S_WIDTH), lambda b, s: (s, b)),
        pl.BlockSpec((TM_PROJ, LANES), row),
        pl.BlockSpec((TM_PROJ // M_CHUNK, 8, M_CHUNK), lambda b, s: (b * n_s + s, 0, 0)),
    )
    return pl.pallas_call(
        _inproj_kernel,
        grid=(BATCH, n_s),
        in_specs=[
            pl.BlockSpec((TM_PROJ, D_MODEL), row),
            pl.BlockSpec((1, D_MODEL), const),
            pl.BlockSpec((D_MODEL, W_MAIN), const),
            pl.BlockSpec((D_MODEL, 2 * LANES), const),
        ],
        out_specs=out_specs,
        out_shape=out_shapes,
        compiler_params=pltpu.CompilerParams(
            dimension_semantics=("arbitrary", "arbitrary"), vmem_limit_bytes=VMEM_LIMIT),
        name="inproj",
    )(x, g, w_main, w_gate)


def _mlstm_kernel(mqk_ref, mv_ref, mo_ref, gates_ref, gatest_ref, cw_ref, cb_ref, brow_ref, bcol_ref, og_ref,
                  y_ref, qk_s, cols_s, wrow_s):
    L = M_CHUNK
    dh = M_HEAD_DIM
    r_i = lax.broadcasted_iota(jnp.int32, (L, L), 0)
    c_i = lax.broadcasted_iota(jnp.int32, (L, L), 1)
    causal = c_i <= r_i
    tri_u = (r_i <= c_i).astype(BF16)
    eye = (lax.broadcasted_iota(jnp.int32, (8, LANES), 0)
           == lax.broadcasted_iota(jnp.int32, (8, LANES), 1)).astype(BF16)
    w = cw_ref[...]
    cb = cb_ref[...]
    lane_q = lax.broadcasted_iota(jnp.int32, (1, 2 * M_WIDTH), 1) < M_WIDTH
    kscale = jnp.where(lane_q, 1.0, dh ** -0.5).astype(F32)
    for c in range(M_NCHUNK):
        halo = 16 if c > 0 else 0
        for strip in range(2 * M_WIDTH // LANES):
            cols = slice(strip * LANES, (strip + 1) * LANES)
            xt = mqk_ref[c * L - halo:(c + 1) * L, cols].astype(F32)
            rows = lax.broadcasted_iota(jnp.int32, xt.shape, 0)
            y = cb[:, cols] + w[CONV_K - 1:CONV_K, cols] * xt
            for k in range(1, CONV_K):
                sh = pltpu.roll(xt, k, 0)
                if c == 0:
                    sh = jnp.where(rows >= k, sh, 0.0)
                y = y + w[CONV_K - 1 - k:CONV_K - k, cols] * sh
            y = y[halo:, :]
            y = y * _sigmoid(y) * kscale[:, cols]
            qk_s[c * L:(c + 1) * L, cols] = y.astype(BF16)

    rows_all = M_NCHUNK * 8
    grow = gatest_ref[...].reshape(rows_all, L) + jnp.tile(bcol_ref[...], (M_NCHUNK, 1))
    cs_row = sum(_dot(term, tri_u) for term in _bf16_terms(_log_sigmoid(grow)))
    w_row = grow - pltpu.roll(cs_row, rows_all - M_HEADS, 0)
    lane_t = lax.broadcasted_iota(jnp.int32, (rows_all, L), 1)
    pm = w_row
    shift = 1
    while shift < L:
        pm = jnp.maximum(pm, jnp.where(lane_t >= shift, pltpu.roll(pm, shift, 1), -jnp.inf))
        shift *= 2
    wrow_s[...] = w_row.reshape(M_NCHUNK, 8, L)
    is_i_row = lax.broadcasted_iota(jnp.int32, (rows_all, L), 0) % 8 < M_HEADS
    terms = _bf16_terms(jnp.where(is_i_row, pm, cs_row))
    for c in range(M_NCHUNK):
        cols_s[c * L:(c + 1) * L, :] = sum(_dot_tn(term[c * 8:(c + 1) * 8, :], eye) for term in terms)

    pad_lane = lax.broadcasted_iota(jnp.int32, (L, LANES - dh), 1)
    ones_pad = jnp.where(pad_lane == 0, 1.0, 0.0).astype(BF16)
    og = og_ref[...]
    head_mean = jnp.where(lax.broadcasted_iota(jnp.int32, (M_WIDTH, M_WIDTH), 0) // dh
                          == lax.broadcasted_iota(jnp.int32, (M_WIDTH, M_WIDTH), 1) // dh,
                          1.0 / dh, 0.0).astype(BF16)

    def chunk(c, carry):
        r0 = pl.multiple_of(c * L, L)
        gcol = gates_ref[pl.ds(r0, L), :] + brow_ref[...]
        cs_col = pm_col = cols_s[pl.ds(r0, L), :]
        w_row = wrow_s[c]
        qk = qk_s[pl.ds(r0, L), :]
        vv = mv_ref[pl.ds(r0, L), :]
        new_carry = []
        hs = []
        for h in range(M_HEADS):
            cst, m = carry[h]
            q = qk[:, h * dh:(h + 1) * dh]
            k = qk[:, M_WIDTH + h * dh:M_WIDTH + (h + 1) * dh]
            v_aug = jnp.concatenate([vv[:, h * dh:(h + 1) * dh], ones_pad], axis=1)
            bc = cs_col[:, M_HEADS + h:M_HEADS + h + 1]
            ic = gcol[:, h:h + 1]
            m_row = jnp.maximum(m, pm_col[:, h:h + 1])
            mt = bc + m_row
            inter = jnp.exp(m - m_row)
            sqk = _dot_nt(q, k) * jnp.exp(jnp.where(causal, w_row[h:h + 1, :] - m_row, -jnp.inf))
            nd = inter * _dot(q, cst.astype(BF16)) + _dot(sqk.astype(BF16), v_aug)
            num = nd[:, :dh]
            den = nd[:, dh:dh + 1]
            hs.append(num / jnp.maximum(jnp.abs(den), jnp.exp(-mt)))
            bl = bc[L - 1:L, :]
            gs = bl - bc + ic
            m_new = jnp.maximum(bl + m, jnp.max(gs, axis=0, keepdims=True))
            decay = jnp.exp(bl + m - m_new)
            wts = jnp.exp(gs - m_new)
            kw = (k.astype(F32) * wts).astype(BF16)
            new_carry.append((decay * cst + _dot_tn(kw, v_aug), m_new))
        hcat = jnp.concatenate(hs, axis=1)
        hcat = hcat * lax.rsqrt(_dot((hcat * hcat).astype(BF16), head_mean) + EPS) * og
        o = mo_ref[pl.ds(r0, L), :].astype(F32)
        y_ref[pl.ds(r0, L), :] = (hcat * _sigmoid(o)).astype(y_ref.dtype)
        return tuple(new_carry)

    init = tuple((jnp.zeros((dh, LANES), F32), jnp.zeros((1, 1), F32)) for _ in range(M_HEADS))
    lax.fori_loop(0, M_NCHUNK, chunk, init, unroll=2)


def _mlstm(mqk, mv, mo, gates, gatest, conv_w, conv_b, bias_row, bias_col, out_g):
    per_b = lambda b: (b, 0)
    const = lambda b: (0, 0)
    return pl.pallas_call(
        _mlstm_kernel,
        grid=(BATCH,),
        in_specs=[
            pl.BlockSpec((SEQ, 2 * M_WIDTH), per_b),
            pl.BlockSpec((SEQ, M_WIDTH), per_b),
            pl.BlockSpec((SEQ, M_WIDTH), per_b),
            pl.BlockSpec((SEQ, LANES), per_b),
            pl.BlockSpec((M_NCHUNK, 8, M_CHUNK), lambda b: (b, 0, 0)),
            pl.BlockSpec((CONV_K, 2 * M_WIDTH), const),
            pl.BlockSpec((1, 2 * M_WIDTH), const),
            pl.BlockSpec((1, LANES), const),
            pl.BlockSpec((8, 1), const),
            pl.BlockSpec((1, M_WIDTH), const),
        ],
        out_specs=pl.BlockSpec((SEQ, M_WIDTH), per_b),
        out_shape=jax.ShapeDtypeStruct((TOKENS, M_WIDTH), BF16),
        scratch_shapes=[
            pltpu.VMEM((SEQ, 2 * M_WIDTH), BF16),
            pltpu.VMEM((SEQ, LANES), F32),
            pltpu.VMEM((M_NCHUNK, 8, M_CHUNK), F32),
        ],
        compiler_params=pltpu.CompilerParams(
            dimension_semantics=("arbitrary",), vmem_limit_bytes=VMEM_LIMIT),
        name="mlstm",
    )(mqk, mv, mo, gates, gatest, conv_w, conv_b, bias_row, bias_col, out_g)


def _qk_norm_rope(x, gain, cos, sin):
    lane = lax.broadcasted_iota(jnp.int32, x.shape, 1)
    first = lane < A_HEAD_DIM
    sq = x * x
    tot = jnp.sum(sq, axis=-1, keepdims=True)
    s0 = jnp.sum(jnp.where(first, sq, 0.0), axis=-1, keepdims=True)
    ms = jnp.where(first, s0, tot - s0) * (1.0 / A_HEAD_DIM)
    xn = x * lax.rsqrt(ms + EPS) * gain
    in_low = (lane % A_HEAD_DIM) < ROPE_HALF
    swapped = jnp.where(in_low, pltpu.roll(xn, LANES - ROPE_HALF, 1), pltpu.roll(xn, ROPE_HALF, 1))
    return xn * cos + swapped * sin


def _moba_kernel(aq_ref, ak_ref, av_ref, cos_ref, sin_ref, qg_ref, kg_ref, og_ref, y_ref, kr_s, s_s):
    blk = MOBA_BLOCK
    dh = A_HEAD_DIM
    km_first = lax.broadcasted_iota(jnp.int32, (MOBA_NB, HEAD_PAIR), 1) < dh
    first = lax.broadcasted_iota(jnp.int32, (blk, HEAD_PAIR), 1) < dh
    blk_id = lax.broadcasted_iota(jnp.int32, (MOBA_NB, blk), 0)
    eye = (lax.broadcasted_iota(jnp.int32, (MOBA_NB, LANES), 0)
           == lax.broadcasted_iota(jnp.int32, (MOBA_NB, LANES), 1)).astype(F32)
    q_pos = lax.broadcasted_iota(jnp.int32, (2 * blk, blk), 0) % blk
    k_pos = lax.broadcasted_iota(jnp.int32, (2 * blk, blk), 1)
    causal = k_pos <= q_pos
    og = og_ref[...]

    km_rows = []
    prepped = {}

    def key_prep(j):
        rows = slice(j * blk, (j + 1) * blk)
        kr = _qk_norm_rope(ak_ref[rows, :].astype(F32), kg_ref[...], cos_ref[rows, :], sin_ref[rows, :])
        kr_s[rows, :] = kr.astype(BF16)
        km_rows.append(jnp.mean(kr, axis=0, keepdims=True))
        yield

    def query_prep(i):
        rows = slice(i * blk, (i + 1) * blk)
        qr = _qk_norm_rope(aq_ref[rows, :].astype(F32), qg_ref[...], cos_ref[rows, :], sin_ref[rows, :])
        qs = qr * (dh ** -0.5 * math.log2(math.e))
        q2 = jnp.concatenate([jnp.where(first, qs, 0.0), jnp.where(first, 0.0, qs)], axis=0).astype(BF16)
        yield

        bias = None
        if i > MOBA_TOPK:
            km = jnp.concatenate(km_rows[:i] + [jnp.zeros((MOBA_NB - i, HEAD_PAIR), F32)], axis=0)
            km_heads = (jnp.where(km_first, km, 0.0), jnp.where(km_first, 0.0, km))
            past = blk_id < i
            cols = []
            for kmh in km_heads:
                gate = jnp.where(past, _dot_nt(kmh, qr, precision=HIGHEST), -jnp.inf)
                rank = jnp.zeros((MOBA_NB, blk), F32)
                for m_blk in range(i):
                    gm = gate[m_blk:m_blk + 1, :]
                    ahead = (gm > gate) | ((gm == gate) & (m_blk < blk_id))
                    rank = rank + jnp.where(ahead, 1.0, 0.0)
                sel = jnp.where(past & (rank < float(MOBA_TOPK)), 1.0, 0.0)
                cols.append(_dot_tn(sel, eye))
            bias = jnp.where(jnp.concatenate(cols, axis=0) > 0.5, 0.0, -jnp.inf)
        prepped[i] = (q2, bias)
        yield

    def scores(i, out):
        q2, bias = prepped.pop(i)
        m_run = None
        for j in range(i + 1):
            s = _dot_nt(q2, kr_s[j * blk:(j + 1) * blk, :])
            if j == i:
                s = jnp.where(causal, s, -jnp.inf)
            elif bias is not None:
                s = s + bias[:, j:j + 1]
            s_s[i % 2, :, j * blk:(j + 1) * blk] = s
            part = jnp.maximum(s[:, :LANES], s[:, LANES:])
            m_run = part if m_run is None else jnp.maximum(m_run, part)
            yield
        out.append(jnp.max(m_run, axis=1, keepdims=True))

    def attend(i, m_fin):
        rows = slice(i * blk, (i + 1) * blk)
        l_run = jnp.zeros((2 * blk, LANES), F32)
        acc = jnp.zeros((2 * blk, HEAD_PAIR), F32)
        m_lanes = jnp.broadcast_to(m_fin, (2 * blk, LANES))
        for j in range(i + 1):
            p_lo = jnp.exp2(s_s[i % 2, :, j * blk:j * blk + LANES] - m_lanes)
            p_hi = jnp.exp2(s_s[i % 2, :, j * blk + LANES:(j + 1) * blk] - m_lanes)
            l_run = l_run + (p_lo + p_hi)
            p = jnp.concatenate([p_lo, p_hi], axis=1)
            acc = acc + _dot(p.astype(BF16), av_ref[j * blk:(j + 1) * blk, :])
            yield
        o2 = acc / jnp.sum(l_run, axis=1, keepdims=True)
        o = jnp.where(first, o2[:blk, :], o2[blk:, :])
        sq = o * o
        tot = jnp.sum(sq, axis=-1, keepdims=True)
        s0 = jnp.sum(jnp.where(first, sq, 0.0), axis=-1, keepdims=True)
        ms = jnp.where(first, s0, tot - s0) * (1.0 / dh)
        y_ref[rows, :] = (o * lax.rsqrt(ms + EPS) * og).astype(y_ref.dtype)

    def interleave(*stages):
        stages = list(stages)
        while stages:
            for stage in list(stages):
                try:
                    next(stage)
                except StopIteration:
                    stages.remove(stage)

    m_cur = []
    interleave(key_prep(0), key_prep(1), query_prep(0), query_prep(1))
    interleave(scores(0, m_cur))
    for i in range(MOBA_NB):
        m_next = []
        stages = [attend(i, m_cur[0])]
        if i + 1 < MOBA_NB:
            stages.insert(0, scores(i + 1, m_next))
        if i + 2 < MOBA_NB:
            stages += [key_prep(i + 2), query_prep(i + 2)]
        interleave(*stages)
        m_cur = m_next


def _moba(aq, ak, av, cos_t, sin_t, q_gain, k_gain, out_gain):
    n_pairs = A_HEADS // 2
    const = lambda b, p: (0, 0)
    pair = lambda b, p: (b, p)
    return pl.pallas_call(
        _moba_kernel,
        grid=(BATCH, n_pairs),
        in_specs=[
            pl.BlockSpec((SEQ, HEAD_PAIR), pair),
            pl.BlockSpec((SEQ, HEAD_PAIR), pair),
            pl.BlockSpec((SEQ, HEAD_PAIR), pair),
            pl.BlockSpec((SEQ, HEAD_PAIR), const),
            pl.BlockSpec((SEQ, HEAD_PAIR), const),
            pl.BlockSpec((1, HEAD_PAIR), const),
            pl.BlockSpec((1, HEAD_PAIR), const),
            pl.BlockSpec((None, 1, HEAD_PAIR), lambda b, p: (p, 0, 0)),
        ],
        out_specs=pl.BlockSpec((SEQ, HEAD_PAIR), pair),
        out_shape=jax.ShapeDtypeStruct((TOKENS, A_WIDTH), BF16),
        scratch_shapes=[
            pltpu.VMEM((SEQ, HEAD_PAIR), BF16),
            pltpu.VMEM((2, 2 * MOBA_BLOCK, SEQ), F32),
        ],
        compiler_params=pltpu.CompilerParams(
            dimension_semantics=("arbitrary", "arbitrary"), vmem_limit_bytes=VMEM_LIMIT),
        name="moba",
    )(aq, ak, av, cos_t, sin_t, q_gain, k_gain, out_gain)


def _s5_kernel(u_ref, abar_ref, bre_ref, bim_ref, cre_ref, cim_ref, d_ref, gw_ref, gb_ref, gmat_ref, og_ref,
               y_ref, xr_s, xi_s, st_s, u_s, y_s):
    step = pl.program_id(0)
    rows = S5_STEPS * BATCH
    tile = 256
    n_tiles = rows // tile
    halves = S_WIDTH // LANES

    @pl.when(step == 0)
    def _init():
        st_s[...] = jnp.zeros_like(st_s)

    for b in range(BATCH):
        for hv in range(halves):
            lanes = slice(b * S_WIDTH + hv * LANES, b * S_WIDTH + (hv + 1) * LANES)
            u_s[hv, pl.ds(b, S5_STEPS, stride=BATCH), :] = u_ref[:, lanes].astype(F32)

    def u_rows(r):
        return jnp.concatenate([u_s[hv, r * tile:(r + 1) * tile, :] for hv in range(halves)], axis=1)

    for r in range(n_tiles):
        u = u_rows(r).astype(BF16)
        t0 = r * tile // BATCH
        xr_s[t0:t0 + tile // BATCH] = _dot(u, bre_ref[...]).reshape(tile // BATCH, BATCH, S_FLAT)
        xi_s[t0:t0 + tile // BATCH] = _dot(u, bim_ref[...]).reshape(tile // BATCH, BATCH, S_FLAT)

    ar = jnp.broadcast_to(abar_ref[0:1, :], (BATCH, S_FLAT))
    ai = jnp.broadcast_to(abar_ref[1:2, :], (BATCH, S_FLAT))

    def scan_step(t, carry):
        sr, si = carry
        nr = ar * sr - ai * si + xr_s[t]
        ni = ar * si + ai * sr + xi_s[t]
        xr_s[t] = nr
        xi_s[t] = ni
        return nr, ni

    sr, si = lax.fori_loop(0, S5_STEPS, scan_step, (st_s[0], st_s[1]), unroll=4)
    st_s[0] = sr
    st_s[1] = si

    for r in range(n_tiles):
        t0 = r * tile // BATCH
        xr = xr_s[t0:t0 + tile // BATCH].reshape(tile, S_FLAT).astype(BF16)
        xi = xi_s[t0:t0 + tile // BATCH].reshape(tile, S_FLAT).astype(BF16)
        y = _dot(xr, cre_ref[...]) - _dot(xi, cim_ref[...]) + d_ref[...] * u_rows(r)
        z = 0.5 * y * (1.0 + jnp.tanh(math.sqrt(2.0 / math.pi) * (y + 0.044715 * (y * y * y))))
        y2 = z * _sigmoid(_dot(z.astype(BF16), gw_ref[...]) + gb_ref[...])
        ms = _dot((y2 * y2).astype(BF16), gmat_ref[...])
        yn = y2 * lax.rsqrt(ms + EPS) * og_ref[...]
        for hv in range(halves):
            y_s[hv, r * tile:(r + 1) * tile, :] = yn[:, hv * LANES:(hv + 1) * LANES]

    for b in range(BATCH):
        for hv in range(halves):
            lanes = slice(b * S_WIDTH + hv * LANES, b * S_WIDTH + (hv + 1) * LANES)
            y_ref[:, lanes] = y_s[hv, pl.ds(b, S5_STEPS, stride=BATCH), :].astype(y_ref.dtype)


def _s5(u_sb, abar, bre, bim, cre, cim, d_row, glu_w, glu_b, gmat, out_g):
    rows = S5_STEPS * BATCH
    const = lambda s: (0, 0)
    return pl.pallas_call(
        _s5_kernel,
        grid=(SEQ // S5_STEPS,),
        in_specs=[
            pl.BlockSpec((S5_STEPS, BATCH * S_WIDTH), lambda s: (s, 0)),
            pl.BlockSpec((2, S_FLAT), const),
            pl.BlockSpec((S_WIDTH, S_FLAT), const),
            pl.BlockSpec((S_WIDTH, S_FLAT), const),
            pl.BlockSpec((S_FLAT, S_WIDTH), const),
            pl.BlockSpec((S_FLAT, S_WIDTH), const),
            pl.BlockSpec((1, S_WIDTH), const),
            pl.BlockSpec((S_WIDTH, S_WIDTH), const),
            pl.BlockSpec((1, S_WIDTH), const),
            pl.BlockSpec((S_WIDTH, S_WIDTH), const),
            pl.BlockSpec((1, S_WIDTH), const),
        ],
        out_specs=pl.BlockSpec((S5_STEPS, BATCH * S_WIDTH), lambda s: (s, 0)),
        out_shape=jax.ShapeDtypeStruct((SEQ, BATCH * S_WIDTH), BF16),
        scratch_shapes=[
            pltpu.VMEM((S5_STEPS, BATCH, S_FLAT), F32),
            pltpu.VMEM((S5_STEPS, BATCH, S_FLAT), F32),
            pltpu.VMEM((2, BATCH, S_FLAT), F32),
            pltpu.VMEM((S_WIDTH // LANES, rows, LANES), F32),
            pltpu.VMEM((S_WIDTH // LANES, rows, LANES), F32),
        ],
        compiler_params=pltpu.CompilerParams(
            dimension_semantics=("arbitrary",), vmem_limit_bytes=VMEM_LIMIT),
        name="s5",
    )(u_sb, abar, bre, bim, cre, cim, d_row, glu_w, glu_b, gmat, out_g)


def _outproj_kernel(x_ref, ym_ref, ya_ref, ys_ref, w_ref, g_ref, wr_ref, br_ref,
                    x1_ref, xn_ref, route_ref, rows_s):
    x1 = x_ref[...]
    x1 = x1 + _dot(ym_ref[...], w_ref[0:M_WIDTH, :])
    x1 = x1 + _dot(ya_ref[...], w_ref[M_WIDTH:M_WIDTH + A_WIDTH, :])
    x1 = x1 + _dot(ys_ref[...], w_ref[M_WIDTH + A_WIDTH:, :])
    x1_ref[...] = x1
    xn = x1 * lax.rsqrt(jnp.mean(x1 * x1, axis=-1, keepdims=True) + EPS) * g_ref[...]
    for j in range(ROW_TILES):
        rows_s[pl.ds(j, TM_PROJ, stride=ROW_TILES), :] = xn[:, j * LANES:(j + 1) * LANES]
    xn_ref[...] = rows_s[...].astype(xn_ref.dtype)
    logits = _split_dot(xn, wr_ref[...]) + br_ref[...]

    lane = lax.broadcasted_iota(jnp.int32, logits.shape, 1).astype(F32)
    is_grp = lane < N_GROUPS
    neg = -jnp.inf
    gl = jnp.where(is_grp, logits, neg)
    gmax = jnp.max(gl, axis=-1, keepdims=True)
    gsum = jnp.sum(jnp.where(is_grp, jnp.exp(logits - gmax), 0.0), axis=-1, keepdims=True)
    g_sel = jnp.min(jnp.where(gl == gmax, lane, float(LANES)), axis=-1, keepdims=True)
    lo = N_GROUPS + EXPERTS_PER_GROUP * g_sel
    el = jnp.where((lane >= lo) & (lane < lo + EXPERTS_PER_GROUP), logits, neg)
    t1 = jnp.max(el, axis=-1, keepdims=True)
    i1 = jnp.min(jnp.where(el == t1, lane, float(LANES)), axis=-1, keepdims=True)
    el2 = jnp.where(lane == i1, neg, el)
    t2 = jnp.max(el2, axis=-1, keepdims=True)
    i2 = jnp.min(jnp.where(el2 == t2, lane, float(LANES)), axis=-1, keepdims=True)
    e21 = jnp.exp(t2 - t1)
    w1 = 1.0 / ((1.0 + e21) * gsum)
    w2 = e21 / ((1.0 + e21) * gsum)
    route = jnp.where(lane == 0.0, i1 - N_GROUPS,
                      jnp.where(lane == 1.0, i2 - N_GROUPS,
                                jnp.where(lane == 2.0, w1, jnp.where(lane == 3.0, w2, 0.0))))
    route_ref[...] = route


def _outproj(x, y_m, y_a, y_s_sb, w_out, g2, w_router, b_router):
    n_s = SEQ // TM_PROJ
    row = lambda b, s: (b * n_s + s, 0)
    const = lambda b, s: (0, 0)
    return pl.pallas_call(
        _outproj_kernel,
        grid=(BATCH, n_s),
        in_specs=[
            pl.BlockSpec((TM_PROJ, D_MODEL), row),
            pl.BlockSpec((TM_PROJ, M_WIDTH), row),
            pl.BlockSpec((TM_PROJ, A_WIDTH), row),
            pl.BlockSpec((TM_PROJ, S_WIDTH), lambda b, s: (s, b)),
            pl.BlockSpec((D_MIX, D_MODEL), const),
            pl.BlockSpec((1, D_MODEL), const),
            pl.BlockSpec((D_MODEL, 2 * LANES), const),
            pl.BlockSpec((1, LANES), const),
        ],
        out_specs=(
            pl.BlockSpec((TM_PROJ, D_MODEL), row),
            pl.BlockSpec((TM_PROJ * ROW_TILES, LANES), row),
            pl.BlockSpec((TM_PROJ, LANES), row),
        ),
        out_shape=(
            jax.ShapeDtypeStruct((TOKENS, D_MODEL), F32),
            jax.ShapeDtypeStruct((TOKENS * ROW_TILES, LANES), BF16),
            jax.ShapeDtypeStruct((TOKENS, LANES), F32),
        ),
        scratch_shapes=[pltpu.VMEM((TM_PROJ * ROW_TILES, LANES), F32)],
        compiler_params=pltpu.CompilerParams(
            dimension_semantics=("arbitrary", "arbitrary"), vmem_limit_bytes=VMEM_LIMIT),
        name="outproj",
    )(x, y_m, y_a, y_s_sb, w_out, g2, w_router, b_router)


def _ffn_kernel(be_ref, src_ref, xs_ref, wg_ref, wu_ref, wd_ref, ys_ref, rows_s):
    del be_ref
    i = pl.program_id(0)

    @pl.when(src_ref[i] == i)
    def _expert_block():
        rows_s[...] = xs_ref[...].astype(F32)
        xb = jnp.concatenate([rows_s[pl.ds(j, FFN_BLOCK, stride=ROW_TILES), :] for j in range(ROW_TILES)],
                             axis=1).astype(BF16)
        g = _dot(xb, wg_ref[...])
        u = _dot(xb, wu_ref[...])
        hmid = (g * _sigmoid(g) * u).astype(BF16)
        y = _dot(hmid, wd_ref[...])
        for j in range(ROW_TILES):
            ys_ref[pl.ds(j, FFN_BLOCK, stride=ROW_TILES), :] = y[:, j * LANES:(j + 1) * LANES]

    @pl.when(src_ref[i] != i)
    def _unused_tail_block():
        ys_ref[...] = jnp.zeros_like(ys_ref)


def _expert_ffn(layer, blk_exp, blk_src, xs, w_gate, w_up, w_down):
    nblk = blk_exp.shape[0]
    block = (FFN_BLOCK * ROW_TILES, LANES)
    grid_spec = pltpu.PrefetchScalarGridSpec(
        num_scalar_prefetch=2,
        grid=(nblk,),
        in_specs=[
            pl.BlockSpec(block, lambda i, be, src: (src[i], 0)),
            pl.BlockSpec((None, None, D_MODEL, D_EXPERT), lambda i, be, src: (layer, be[i], 0, 0)),
            pl.BlockSpec((None, None, D_MODEL, D_EXPERT), lambda i, be, src: (layer, be[i], 0, 0)),
            pl.BlockSpec((None, None, D_EXPERT, D_MODEL), lambda i, be, src: (layer, be[i], 0, 0)),
        ],
        out_specs=pl.BlockSpec(block, lambda i, be, src: (i, 0)),
        scratch_shapes=[pltpu.VMEM(block, F32)],
    )
    return pl.pallas_call(
        _ffn_kernel,
        grid_spec=grid_spec,
        out_shape=jax.ShapeDtypeStruct((nblk * FFN_BLOCK * ROW_TILES, LANES), F32),
        compiler_params=pltpu.CompilerParams(
            dimension_semantics=("arbitrary",), vmem_limit_bytes=VMEM_LIMIT),
        name="expert_ffn",
    )(blk_exp, blk_src, xs, w_gate, w_up, w_down)


def _plan_kernel(e_ref, ps_ref, dest_ref, run_s, tri_s):
    i = pl.program_id(0)

    @pl.when(i == 0)
    def _init():
        run_s[...] = ps_ref[...]
        s_i = lax.broadcasted_iota(jnp.int32, (PLAN_TILE, PLAN_TILE), 0)
        t_i = lax.broadcasted_iota(jnp.int32, (PLAN_TILE, PLAN_TILE), 1)
        tri_s[...] = (s_i < t_i).astype(BF16)

    expert = lax.broadcasted_iota(jnp.int32, (N_EXPERTS, PLAN_TILE), 0)
    onehot = jnp.where(e_ref[...] == expert, 1.0, 0.0)
    before = _dot(onehot.astype(BF16), tri_s[...])
    dest = jnp.sum(onehot * (before + run_s[:, 0:1]), axis=0, keepdims=True)
    dest_ref[...] = dest.astype(jnp.int32)
    run_s[...] = run_s[...] + jnp.sum(onehot, axis=1, keepdims=True)


def _plan(e_rows, pstart_col):
    n_tiles = e_rows.shape[0]
    return pl.pallas_call(
        _plan_kernel,
        grid=(n_tiles,),
        in_specs=[
            pl.BlockSpec((None, 1, PLAN_TILE), lambda i: (i, 0, 0)),
            pl.BlockSpec((N_EXPERTS, LANES), lambda i: (0, 0)),
        ],
        out_specs=pl.BlockSpec((None, 1, PLAN_TILE), lambda i: (i, 0, 0)),
        out_shape=jax.ShapeDtypeStruct((n_tiles, 1, PLAN_TILE), jnp.int32),
        scratch_shapes=[pltpu.VMEM((N_EXPERTS, LANES), F32), pltpu.VMEM((PLAN_TILE, PLAN_TILE), BF16)],
        compiler_params=pltpu.CompilerParams(
            dimension_semantics=("arbitrary",), vmem_limit_bytes=VMEM_LIMIT),
        name="moe_plan",
    )(e_rows, pstart_col)


def _dispatch_kernel(fill_ref, d0_ref, d1_ref, xn_ref, xs_hbm, zero_s, sem, fill_sem):
    @pl.when(pl.program_id(0) == 0)
    def _zero_padding():
        zero_s[...] = jnp.zeros_like(zero_s)

        def fill(c):
            start = pl.multiple_of(fill_ref[0, c] * ROW_TILES, ROW_TILES)
            return pltpu.make_async_copy(zero_s, xs_hbm.at[pl.ds(start, FFN_BLOCK * ROW_TILES), :], fill_sem)

        for c in range(N_FILLS):
            pl.when(fill_ref[0, c] >= 0)(lambda c=c: fill(c).start())
        for c in range(N_FILLS):
            pl.when(fill_ref[0, c] >= 0)(lambda c=c: fill(c).wait())

    def row_copy(dref, t):
        src = xn_ref.at[pl.ds(pl.multiple_of(t * ROW_TILES, ROW_TILES), ROW_TILES), :]
        dst = xs_hbm.at[pl.ds(pl.multiple_of(dref[0, t] * ROW_TILES, ROW_TILES), ROW_TILES), :]
        return pltpu.make_async_copy(src, dst, sem)

    def start(t, carry):
        row_copy(d0_ref, t).start()
        row_copy(d1_ref, t).start(priority=1)
        return carry

    lax.fori_loop(0, DISPATCH_TOKENS, start, 0, unroll=8)
    for _ in range(EXPERT_TOPK):
        pltpu.make_async_copy(xn_ref, xs_hbm.at[pl.ds(0, DISPATCH_TOKENS * ROW_TILES), :], sem).wait()


def _dispatch(fill_rows, dest, xn, n_rows):
    n_tiles = TOKENS // DISPATCH_TOKENS
    dest4 = dest.reshape(EXPERT_TOPK, n_tiles, 1, DISPATCH_TOKENS)
    smem_tile = lambda k: pl.BlockSpec((None, None, 1, DISPATCH_TOKENS), lambda i: (k, i, 0, 0),
                                       memory_space=pltpu.SMEM)
    return pl.pallas_call(
        _dispatch_kernel,
        grid=(n_tiles,),
        in_specs=[
            pl.BlockSpec(memory_space=pltpu.SMEM),
            smem_tile(0),
            smem_tile(1),
            pl.BlockSpec((DISPATCH_TOKENS * ROW_TILES, LANES), lambda i: (i, 0)),
        ],
        out_specs=pl.BlockSpec(memory_space=pl.ANY),
        out_shape=jax.ShapeDtypeStruct((n_rows * ROW_TILES, LANES), BF16),
        scratch_shapes=[
            pltpu.VMEM((FFN_BLOCK * ROW_TILES, LANES), BF16),
            pltpu.SemaphoreType.DMA(()),
            pltpu.SemaphoreType.DMA(()),
        ],
        compiler_params=pltpu.CompilerParams(
            dimension_semantics=("arbitrary",), vmem_limit_bytes=VMEM_LIMIT),
        name="moe_dispatch",
    )(fill_rows, dest4, dest4, xn)


def _combine_kernel(d0_ref, d1_ref, d0_next_ref, d1_next_ref, x1_ref, route_ref, ys_hbm, out_ref, buf, sem):
    i = pl.program_id(0)
    n = pl.num_programs(0)

    def start_tile(drefs, slot):
        def body(t, carry):
            for k in range(EXPERT_TOPK):
                src = ys_hbm.at[pl.ds(pl.multiple_of(drefs[k][0, t] * ROW_TILES, ROW_TILES), ROW_TILES), :]
                dst = buf.at[slot, k, pl.ds(pl.multiple_of(t * ROW_TILES, ROW_TILES), ROW_TILES), :]
                pltpu.make_async_copy(src, dst, sem.at[slot]).start(priority=k)
            return carry
        lax.fori_loop(0, COMBINE_TOKENS, body, 0, unroll=8)

    slot = i % 2

    @pl.when(i == 0)
    def _first():
        start_tile((d0_ref, d1_ref), 0)

    @pl.when(i + 1 < n)
    def _prefetch():
        start_tile((d0_next_ref, d1_next_ref), 1 - slot)

    for k in range(EXPERT_TOPK):
        pltpu.make_async_copy(ys_hbm.at[pl.ds(0, COMBINE_TOKENS * ROW_TILES), :], buf.at[slot, k],
                              sem.at[slot]).wait()
    w = route_ref[...]
    w1 = w[:, 2:3]
    w2 = w[:, 3:4]
    for j in range(ROW_TILES):
        cols = slice(j * LANES, (j + 1) * LANES)
        chunk = pl.ds(j, COMBINE_TOKENS, stride=ROW_TILES)
        out_ref[:, cols] = x1_ref[:, cols] + w1 * buf[slot, 0, chunk, :] + w2 * buf[slot, 1, chunk, :]


def _combine(dest, x1, route, ys3):
    n_tiles = TOKENS // COMBINE_TOKENS
    dest4 = dest.reshape(EXPERT_TOPK, n_tiles, 1, COMBINE_TOKENS)
    smem_tile = lambda k, nxt: pl.BlockSpec(
        (None, None, 1, COMBINE_TOKENS), lambda i: (k, jnp.minimum(i + nxt, n_tiles - 1), 0, 0),
        memory_space=pltpu.SMEM)
    return pl.pallas_call(
        _combine_kernel,
        grid=(n_tiles,),
        in_specs=[
            smem_tile(0, 0), smem_tile(1, 0), smem_tile(0, 1), smem_tile(1, 1),
            pl.BlockSpec((COMBINE_TOKENS, D_MODEL), lambda i: (i, 0)),
            pl.BlockSpec((COMBINE_TOKENS, LANES), lambda i: (i, 0)),
            pl.BlockSpec(memory_space=pl.ANY),
        ],
        out_specs=pl.BlockSpec((COMBINE_TOKENS, D_MODEL), lambda i: (i, 0)),
        out_shape=jax.ShapeDtypeStruct((TOKENS, D_MODEL), F32),
        scratch_shapes=[
            pltpu.VMEM((2, EXPERT_TOPK, COMBINE_TOKENS * ROW_TILES, LANES), F32),
            pltpu.SemaphoreType.DMA((2,)),
        ],
        compiler_params=pltpu.CompilerParams(
            dimension_semantics=("arbitrary",), vmem_limit_bytes=VMEM_LIMIT),
        name="moe_combine",
    )(dest4, dest4, dest4, dest4, x1, route, ys3)


def _rope_tables():
    inv = ROPE_THETA ** (-np.arange(ROPE_HALF, dtype=np.float64) * 2.0 / ROPE_DIM)
    ang = np.arange(SEQ, dtype=np.float64)[:, None] * inv[None, :]
    cos = np.ones((SEQ, A_HEAD_DIM))
    sin = np.zeros((SEQ, A_HEAD_DIM))
    cos[:, :ROPE_HALF] = np.cos(ang)
    cos[:, ROPE_HALF:ROPE_DIM] = np.cos(ang)
    sin[:, :ROPE_HALF] = -np.sin(ang)
    sin[:, ROPE_HALF:ROPE_DIM] = np.sin(ang)
    return (jnp.asarray(np.tile(cos, (1, 2)), F32), jnp.asarray(np.tile(sin, (1, 2)), F32))


def _split_w_in(w_in):
    sizes = [M_WIDTH, M_WIDTH, M_WIDTH, M_WIDTH, M_HEADS, M_HEADS, A_WIDTH, A_WIDTH, A_WIDTH, S_WIDTH]
    offs = np.cumsum([0] + sizes)
    mq, mk, mv, mo, mi, mf, aq, ak, av, su = (w_in[:, offs[n]:offs[n + 1]] for n in range(len(sizes)))
    w_main = jnp.concatenate([aq, ak, av, mq, mk, mv, mo, su], axis=1).astype(BF16)
    gate = jnp.concatenate([mi, mf], axis=1)
    return w_main, _hi_lo(jnp.pad(gate, ((0, 0), (0, LANES - 2 * M_HEADS))))


def _s5_params(a_re, a_im, b_re, b_im, c_re, c_im, log_dt):
    dt = jnp.exp(log_dt)[:, None]
    mag = jnp.exp(a_re * dt)
    abar_re, abar_im = mag * jnp.cos(a_im * dt), mag * jnp.sin(a_im * dt)
    zr, zi = abar_re - 1.0, abar_im
    den = a_re * a_re + a_im * a_im
    fr, fi = (zr * a_re + zi * a_im) / den, (zi * a_re - zr * a_im) / den
    bbar_re = fr[..., None] * b_re - fi[..., None] * b_im
    bbar_im = fr[..., None] * b_im + fi[..., None] * b_re
    eye = jnp.eye(S_GROUPS, dtype=F32)
    dense_b = lambda t: jnp.einsum('gph,gk->ghkp', t, eye).reshape(S_WIDTH, S_FLAT).astype(BF16)
    dense_c = lambda t: jnp.einsum('ghp,gk->gpkh', t, eye).reshape(S_FLAT, S_WIDTH).astype(BF16)
    abar = jnp.stack([abar_re.reshape(S_FLAT), abar_im.reshape(S_FLAT)])
    return abar, dense_b(bbar_re), dense_b(bbar_im), dense_c(c_re), dense_c(c_im)


def _moe(layer, x1, xn, route, w_gate, w_up, w_down):
    n_assign = TOKENS * EXPERT_TOPK
    e_flat = route[:, :EXPERT_TOPK].astype(jnp.int32).T.reshape(n_assign)
    experts = jnp.arange(N_EXPERTS, dtype=jnp.int32)
    counts = jnp.sum((e_flat[:, None] == experts[None, :]).astype(jnp.int32), axis=0)
    padded = (counts + FFN_BLOCK - 1) // FFN_BLOCK * FFN_BLOCK
    pends = jnp.cumsum(padded)
    pstarts = pends - padded
    n_rows = (n_assign + N_EXPERTS * (FFN_BLOCK - 1) + FFN_BLOCK - 1) // FFN_BLOCK * FFN_BLOCK
    nblk = n_rows // FFN_BLOCK
    blk_start = jnp.arange(nblk, dtype=jnp.int32) * FFN_BLOCK
    blk_exp = jnp.minimum(jnp.sum((pends[None, :] <= blk_start[:, None]).astype(jnp.int32), axis=1),
                          N_EXPERTS - 1).astype(jnp.int32)
    pstart_col = jnp.broadcast_to(pstarts.astype(F32)[:, None], (N_EXPERTS, LANES))
    dest = _plan(e_flat.reshape(n_assign // PLAN_TILE, 1, PLAN_TILE), pstart_col)
    tail = pends[-1] + jnp.arange(N_EXPERTS, dtype=jnp.int32) * FFN_BLOCK
    fill_rows = jnp.concatenate([jnp.where(padded > 0, pends - FFN_BLOCK, -1), jnp.where(tail < n_rows, tail, -1)])
    fill_rows = fill_rows.astype(jnp.int32).reshape(1, N_FILLS)
    xs = _dispatch(fill_rows, dest, xn, n_rows)
    blk_src = jnp.minimum(jnp.arange(nblk, dtype=jnp.int32), pends[-1] // FFN_BLOCK - 1).astype(jnp.int32)
    ys = _expert_ffn(layer, blk_exp, blk_src, xs, w_gate, w_up, w_down)
    return _combine(dest, x1, route, ys)


def kernel(x, norm1_g, w_in, m_bias_i, m_bias_f, m_conv_w, m_conv_b, m_out_g, a_q_g, a_k_g, a_out_g, s_a_re, s_a_im, s_b_re, s_b_im, s_c_re, s_c_im, s_d, s_log_dt, s_glu_w, s_glu_b, s_out_g, w_out, norm2_g, r_group_w, r_group_b, r_expert_w, r_expert_b, e_w_gate, e_w_up, e_w_down):
    cos_t, sin_t = _rope_tables()
    gmat = jnp.asarray(np.kron(np.eye(S_GROUPS), np.full((S_GROUP_DIM, S_GROUP_DIM), 1.0 / S_GROUP_DIM)), BF16)
    xf = x.reshape(TOKENS, D_MODEL)
    for l in range(DEPTH):
        w_main, w_gate = _split_w_in(w_in[l])
        aq, ak, av, mqk, mv, mo, su, gates, gatest = _inproj(xf, norm1_g[l].reshape(1, D_MODEL), w_main, w_gate)

        bias = jnp.concatenate([m_bias_i[l], m_bias_f[l]])
        bias_row = jnp.pad(bias, (0, LANES - 2 * M_HEADS)).reshape(1, LANES)
        y_m = _mlstm(mqk, mv, mo, gates, gatest, m_conv_w[l], m_conv_b[l].reshape(1, 2 * M_WIDTH),
                     bias_row, bias.reshape(2 * M_HEADS, 1), m_out_g[l].reshape(1, M_WIDTH))

        y_a = _moba(aq, ak, av, cos_t, sin_t,
                    jnp.tile(a_q_g[l], 2).reshape(1, HEAD_PAIR), jnp.tile(a_k_g[l], 2).reshape(1, HEAD_PAIR),
                    a_out_g[l].reshape(A_HEADS // 2, 1, HEAD_PAIR))

        abar, bre, bim, cre, cim = _s5_params(s_a_re[l], s_a_im[l], s_b_re[l], s_b_im[l],
                                              s_c_re[l], s_c_im[l], s_log_dt[l])
        y_s = _s5(su, abar, bre, bim, cre, cim,
                  s_d[l].reshape(1, S_WIDTH), s_glu_w[l].astype(BF16), s_glu_b[l].reshape(1, S_WIDTH),
                  gmat, s_out_g[l].reshape(1, S_WIDTH))

        w_router = jnp.pad(jnp.concatenate([r_group_w[l], r_expert_w[l]], axis=1),
                           ((0, 0), (0, LANES - N_GROUPS - N_EXPERTS)))
        b_router = jnp.pad(jnp.concatenate([r_group_b[l], r_expert_b[l]]),
                           (0, LANES - N_GROUPS - N_EXPERTS)).reshape(1, LANES)
        x1, xn, route = _outproj(xf, y_m, y_a, y_s, w_out[l].astype(BF16),
                                 norm2_g[l].reshape(1, D_MODEL), _hi_lo(w_router), b_router)
        xf = _moe(l, x1, xn, route, e_w_gate, e_w_up, e_w_down)
    return xf.reshape(BATCH, SEQ, D_MODEL)
```

```python
import functools
import math

import numpy as np
import jax
import jax.numpy as jnp
from jax import lax
from jax.experimental import pallas as pl
from jax.experimental.pallas import tpu as pltpu

F32 = jnp.float32
BF16 = jnp.bfloat16

D_MODEL = 1024
BATCH = 8
SEQ = 2048
DEPTH = 2
TOKENS = BATCH * SEQ

M_HEADS = 4
M_HEAD_DIM = 64
M_WIDTH = M_HEADS * M_HEAD_DIM
A_HEADS = 8
A_HEAD_DIM = 64
A_WIDTH = A_HEADS * A_HEAD_DIM
S_GROUPS = 16
S_GROUP_DIM = 16
S_WIDTH = S_GROUPS * S_GROUP_DIM
S_STATE = 64
S_FLAT = S_GROUPS * S_STATE
D_MIX = M_WIDTH + A_WIDTH + S_WIDTH

CONV_K = 4
MOBA_BLOCK = 256
MOBA_NB = SEQ // MOBA_BLOCK
MOBA_TOPK = 3
ROPE_THETA = 500000.0
ROPE_DIM = A_HEAD_DIM // 4
ROPE_HALF = ROPE_DIM // 2

N_GROUPS = 4
EXPERTS_PER_GROUP = 8
N_EXPERTS = N_GROUPS * EXPERTS_PER_GROUP
EXPERT_TOPK = 2
D_EXPERT = 512
EPS = 1e-6

LANES = 128
ROW_TILES = D_MODEL // LANES
HEAD_PAIR = 2 * A_HEAD_DIM

W_MAIN = 3 * A_WIDTH + 4 * M_WIDTH + S_WIDTH
TM_PROJ = 512
M_CHUNK = 256
M_NCHUNK = SEQ // M_CHUNK
S5_STEPS = 256
FFN_BLOCK = 512
N_FILLS = 2 * N_EXPERTS
PLAN_TILE = 1024
DISPATCH_TOKENS = 1024
COMBINE_TOKENS = 256
VMEM_LIMIT = 56 * 1024 * 1024

HIGHEST = lax.Precision.HIGHEST


def _dot(a, b, precision=None):
    return jnp.dot(a, b, preferred_element_type=F32, precision=precision)


def _dot_nt(a, b, precision=None):
    return lax.dot_general(a, b, (((1,), (1,)), ((), ())), preferred_element_type=F32, precision=precision)


def _dot_tn(a, b, precision=None):
    return lax.dot_general(a, b, (((0,), (0,)), ((), ())), preferred_element_type=F32, precision=precision)


def _split_dot(a, w_cat):
    n = w_cat.shape[1] // 2
    a_hi = a.astype(BF16)
    a_lo = (a - a_hi.astype(F32)).astype(BF16)
    both = _dot(a_hi, w_cat)
    return both[:, :n] + both[:, n:] + _dot(a_lo, w_cat[:, :n])


def _bf16_terms(x):
    hi = x.astype(BF16)
    rest = x - hi.astype(F32)
    mid = rest.astype(BF16)
    return hi, mid, (rest - mid.astype(F32)).astype(BF16)


def _hi_lo(w):
    w_hi = w.astype(BF16)
    return jnp.concatenate([w_hi, (w - w_hi.astype(F32)).astype(BF16)], axis=1)


def _log_sigmoid(x):
    return jnp.minimum(x, 0.0) - jnp.log1p(jnp.exp(-jnp.abs(x)))


def _sigmoid(x):
    return 1.0 / (1.0 + jnp.exp(-x))


def _inproj_kernel(x_ref, g_ref, w_ref, wg_ref,
                   aq_ref, ak_ref, av_ref, mqk_ref, mv_ref, mo_ref, su_ref, gates_ref, gatest_ref):
    x = x_ref[...]
    h = x * lax.rsqrt(jnp.mean(x * x, axis=-1, keepdims=True) + EPS) * g_ref[...]
    hb = h.astype(BF16)
    off = 0
    for ref, width in ((aq_ref, A_WIDTH), (ak_ref, A_WIDTH), (av_ref, A_WIDTH), (mqk_ref, 2 * M_WIDTH),
                       (mv_ref, M_WIDTH), (mo_ref, M_WIDTH), (su_ref, S_WIDTH)):
        ref[...] = _dot(hb, w_ref[:, off:off + width]).astype(ref.dtype)
        off += width
    gates = _split_dot(h, wg_ref[...])
    gates_ref[...] = gates
    gt = gates.T[:8, :]
    for c in range(TM_PROJ // M_CHUNK):
        gatest_ref[c] = gt[:, c * M_CHUNK:(c + 1) * M_CHUNK]


def _inproj(x, g, w_main, w_gate):
    n_s = SEQ // TM_PROJ
    row = lambda b, s: (b * n_s + s, 0)
    const = lambda b, s: (0, 0)
    out_shapes = (
        jax.ShapeDtypeStruct((TOKENS, A_WIDTH), BF16),
        jax.ShapeDtypeStruct((TOKENS, A_WIDTH), BF16),
        jax.ShapeDtypeStruct((TOKENS, A_WIDTH), BF16),
        jax.ShapeDtypeStruct((TOKENS, 2 * M_WIDTH), BF16),
        jax.ShapeDtypeStruct((TOKENS, M_WIDTH), BF16),
        jax.ShapeDtypeStruct((TOKENS, M_WIDTH), BF16),
        jax.ShapeDtypeStruct((SEQ, BATCH * S_WIDTH), BF16),
        jax.ShapeDtypeStruct((TOKENS, LANES), F32),
        jax.ShapeDtypeStruct((TOKENS // M_CHUNK, 8, M_CHUNK), F32),
    )
    out_specs = (
        pl.BlockSpec((TM_PROJ, A_WIDTH), row),
        pl.BlockSpec((TM_PROJ, A_WIDTH), row),
        pl.BlockSpec((TM_PROJ, A_WIDTH), row),
        pl.BlockSpec((TM_PROJ, 2 * M_WIDTH), row),
        pl.BlockSpec((TM_PROJ, M_WIDTH), row),
        pl.BlockSpec((TM_PROJ, M_WIDTH), row),
        pl.BlockSpec((TM_PROJ, S_WIDTH), lambda b, s: (s, b)),
        pl.BlockSpec((TM_PROJ, LANES), row),
        pl.BlockSpec((TM_PROJ // M_CHUNK, 8, M_CHUNK), lambda b, s: (b * n_s + s, 0, 0)),
    )
    return pl.pallas_call(
        _inproj_kernel,
        grid=(BATCH, n_s),
        in_specs=[
            pl.BlockSpec((TM_PROJ, D_MODEL), row),
            pl.BlockSpec((1, D_MODEL), const),
            pl.BlockSpec((D_MODEL, W_MAIN), const),
            pl.BlockSpec((D_MODEL, 2 * LANES), const),
        ],
        out_specs=out_specs,
        out_shape=out_shapes,
        compiler_params=pltpu.CompilerParams(
            dimension_semantics=("arbitrary", "arbitrary"), vmem_limit_bytes=VMEM_LIMIT),
        name="inproj",
    )(x, g, w_main, w_gate)


def _mlstm_kernel(mqk_ref, mv_ref, mo_ref, gates_ref, gatest_ref, cw_ref, cb_ref, brow_ref, bcol_ref, og_ref,
                  y_ref, qk_s, cols_s, wrow_s):
    L = M_CHUNK
    dh = M_HEAD_DIM
    r_i = lax.broadcasted_iota(jnp.int32, (L, L), 0)
    c_i = lax.broadcasted_iota(jnp.int32, (L, L), 1)
    causal = c_i <= r_i
    tri_u = (r_i <= c_i).astype(BF16)
    eye = (lax.broadcasted_iota(jnp.int32, (8, LANES), 0)
           == lax.broadcasted_iota(jnp.int32, (8, LANES), 1)).astype(BF16)
    w = cw_ref[...]
    cb = cb_ref[...]
    lane_q = lax.broadcasted_iota(jnp.int32, (1, 2 * M_WIDTH), 1) < M_WIDTH
    kscale = jnp.where(lane_q, 1.0, dh ** -0.5).astype(F32)
    for c in range(M_NCHUNK):
        halo = 16 if c > 0 else 0
        for strip in range(2 * M_WIDTH // LANES):
            cols = slice(strip * LANES, (strip + 1) * LANES)
            xt = mqk_ref[c * L - halo:(c + 1) * L, cols].astype(F32)
            rows = lax.broadcasted_iota(jnp.int32, xt.shape, 0)
            y = cb[:, cols] + w[CONV_K - 1:CONV_K, cols] * xt
            for k in range(1, CONV_K):
                sh = pltpu.roll(xt, k, 0)
                if c == 0:
                    sh = jnp.where(rows >= k, sh, 0.0)
                y = y + w[CONV_K - 1 - k:CONV_K - k, cols] * sh
            y = y[halo:, :]
            y = y * _sigmoid(y) * kscale[:, cols]
            qk_s[c * L:(c + 1) * L, cols] = y.astype(BF16)

    rows_all = M_NCHUNK * 8
    grow = gatest_ref[...].reshape(rows_all, L) + jnp.tile(bcol_ref[...], (M_NCHUNK, 1))
    cs_row = sum(_dot(term, tri_u) for term in _bf16_terms(_log_sigmoid(grow)))
    w_row = grow - pltpu.roll(cs_row, rows_all - M_HEADS, 0)
    lane_t = lax.broadcasted_iota(jnp.int32, (rows_all, L), 1)
    pm = w_row
    shift = 1
    while shift < L:
        pm = jnp.maximum(pm, jnp.where(lane_t >= shift, pltpu.roll(pm, shift, 1), -jnp.inf))
        shift *= 2
    wrow_s[...] = w_row.reshape(M_NCHUNK, 8, L)
    is_i_row = lax.broadcasted_iota(jnp.int32, (rows_all, L), 0) % 8 < M_HEADS
    terms = _bf16_terms(jnp.where(is_i_row, pm, cs_row))
    for c in range(M_NCHUNK):
        cols_s[c * L:(c + 1) * L, :] = sum(_dot_tn(term[c * 8:(c + 1) * 8, :], eye) for term in terms)

    pad_lane = lax.broadcasted_iota(jnp.int32, (L, LANES - dh), 1)
    ones_pad = jnp.where(pad_lane == 0, 1.0, 0.0).astype(BF16)
    og = og_ref[...]
    head_mean = jnp.where(lax.broadcasted_iota(jnp.int32, (M_WIDTH, M_WIDTH), 0) // dh
                          == lax.broadcasted_iota(jnp.int32, (M_WIDTH, M_WIDTH), 1) // dh,
                          1.0 / dh, 0.0).astype(BF16)

    def chunk(c, carry):
        r0 = pl.multiple_of(c * L, L)
        gcol = gates_ref[pl.ds(r0, L), :] + brow_ref[...]
        cs_col = pm_col = cols_s[pl.ds(r0, L), :]
        w_row = wrow_s[c]
        qk = qk_s[pl.ds(r0, L), :]
        vv = mv_ref[pl.ds(r0, L), :]
        new_carry = []
        hs = []
        for h in range(M_HEADS):
            cst, m = carry[h]
            q = qk[:, h * dh:(h + 1) * dh]
            k = qk[:, M_WIDTH + h * dh:M_WIDTH + (h + 1) * dh]
            v_aug = jnp.concatenate([vv[:, h * dh:(h + 1) * dh], ones_pad], axis=1)
            bc = cs_col[:, M_HEADS + h:M_HEADS + h + 1]
            ic = gcol[:, h:h + 1]
            m_row = jnp.maximum(m, pm_col[:, h:h + 1])
            mt = bc + m_row
            inter = jnp.exp(m - m_row)
            sqk = _dot_nt(q, k) * jnp.exp(jnp.where(causal, w_row[h:h + 1, :] - m_row, -jnp.inf))
            nd = inter * _dot(q, cst.astype(BF16)) + _dot(sqk.astype(BF16), v_aug)
            num = nd[:, :dh]
            den = nd[:, dh:dh + 1]
            hs.append(num / jnp.maximum(jnp.abs(den), jnp.exp(-mt)))
            bl = bc[L - 1:L, :]
            gs = bl - bc + ic
            m_new = jnp.maximum(bl + m, jnp.max(gs, axis=0, keepdims=True))
            decay = jnp.exp(bl + m - m_new)
            wts = jnp.exp(gs - m_new)
            kw = (k.astype(F32) * wts).astype(BF16)
            new_carry.append((decay * cst + _dot_tn(kw, v_aug), m_new))
        hcat = jnp.concatenate(hs, axis=1)
        hcat = hcat * lax.rsqrt(_dot((hcat * hcat).astype(BF16), head_mean) + EPS) * og
        o = mo_ref[pl.ds(r0, L), :].astype(F32)
        y_ref[pl.ds(r0, L), :] = (hcat * _sigmoid(o)).astype(y_ref.dtype)
        return tuple(new_carry)

    init = tuple((jnp.zeros((dh, LANES), F32), jnp.zeros((1, 1), F32)) for _ in range(M_HEADS))
    lax.fori_loop(0, M_NCHUNK, chunk, init, unroll=2)


def _mlstm(mqk, mv, mo, gates, gatest, conv_w, conv_b, bias_row, bias_col, out_g):
    per_b = lambda b: (b, 0)
    const = lambda b: (0, 0)
    return pl.pallas_call(
        _mlstm_kernel,
        grid=(BATCH,),
        in_specs=[
            pl.BlockSpec((SEQ, 2 * M_WIDTH), per_b),
            pl.BlockSpec((SEQ, M_WIDTH), per_b),
            pl.BlockSpec((SEQ, M_WIDTH), per_b),
            pl.BlockSpec((SEQ, LANES), per_b),
            pl.BlockSpec((M_NCHUNK, 8, M_CHUNK), lambda b: (b, 0, 0)),
            pl.BlockSpec((CONV_K, 2 * M_WIDTH), const),
            pl.BlockSpec((1, 2 * M_WIDTH), const),
            pl.BlockSpec((1, LANES), const),
            pl.BlockSpec((8, 1), const),
            pl.BlockSpec((1, M_WIDTH), const),
        ],
        out_specs=pl.BlockSpec((SEQ, M_WIDTH), per_b),
        out_shape=jax.ShapeDtypeStruct((TOKENS, M_WIDTH), BF16),
        scratch_shapes=[
            pltpu.VMEM((SEQ, 2 * M_WIDTH), BF16),
            pltpu.VMEM((SEQ, LANES), F32),
            pltpu.VMEM((M_NCHUNK, 8, M_CHUNK), F32),
        ],
        compiler_params=pltpu.CompilerParams(
            dimension_semantics=("arbitrary",), vmem_limit_bytes=VMEM_LIMIT),
        name="mlstm",
    )(mqk, mv, mo, gates, gatest, conv_w, conv_b, bias_row, bias_col, out_g)


def _qk_norm_rope(x, gain, cos, sin):
    lane = lax.broadcasted_iota(jnp.int32, x.shape, 1)
    first = lane < A_HEAD_DIM
    sq = x * x
    tot = jnp.sum(sq, axis=-1, keepdims=True)
    s0 = jnp.sum(jnp.where(first, sq, 0.0), axis=-1, keepdims=True)
    ms = jnp.where(first, s0, tot - s0) * (1.0 / A_HEAD_DIM)
    xn = x * lax.rsqrt(ms + EPS) * gain
    in_low = (lane % A_HEAD_DIM) < ROPE_HALF
    swapped = jnp.where(in_low, pltpu.roll(xn, LANES - ROPE_HALF, 1), pltpu.roll(xn, ROPE_HALF, 1))
    return xn * cos + swapped * sin


def _moba_kernel(aq_ref, ak_ref, av_ref, cos_ref, sin_ref, qg_ref, kg_ref, og_ref, y_ref, kr_s, s_s):
    blk = MOBA_BLOCK
    dh = A_HEAD_DIM
    km_first = lax.broadcasted_iota(jnp.int32, (MOBA_NB, HEAD_PAIR), 1) < dh
    first = lax.broadcasted_iota(jnp.int32, (blk, HEAD_PAIR), 1) < dh
    blk_id = lax.broadcasted_iota(jnp.int32, (MOBA_NB, blk), 0)
    eye = (lax.broadcasted_iota(jnp.int32, (MOBA_NB, LANES), 0)
           == lax.broadcasted_iota(jnp.int32, (MOBA_NB, LANES), 1)).astype(F32)
    q_pos = lax.broadcasted_iota(jnp.int32, (2 * blk, blk), 0) % blk
    k_pos = lax.broadcasted_iota(jnp.int32, (2 * blk, blk), 1)
    causal = k_pos <= q_pos
    og = og_ref[...]

    km_rows = []
    prepped = {}

    def key_prep(j):
        rows = slice(j * blk, (j + 1) * blk)
        kr = _qk_norm_rope(ak_ref[rows, :].astype(F32), kg_ref[...], cos_ref[rows, :], sin_ref[rows, :])
        kr_s[rows, :] = kr.astype(BF16)
        km_rows.append(jnp.mean(kr, axis=0, keepdims=True))
        yield

    def query_prep(i):
        rows = slice(i * blk, (i + 1) * blk)
        qr = _qk_norm_rope(aq_ref[rows, :].astype(F32), qg_ref[...], cos_ref[rows, :], sin_ref[rows, :])
        qs = qr * (dh ** -0.5 * math.log2(math.e))
        q2 = jnp.concatenate([jnp.where(first, qs, 0.0), jnp.where(first, 0.0, qs)], axis=0).astype(BF16)
        yield

        bias = None
        if i > MOBA_TOPK:
            km = jnp.concatenate(km_rows[:i] + [jnp.zeros((MOBA_NB - i, HEAD_PAIR), F32)], axis=0)
            km_heads = (jnp.where(km_first, km, 0.0), jnp.where(km_first, 0.0, km))
            past = blk_id < i
            cols = []
            for kmh in km_heads:
                gate = jnp.where(past, _dot_nt(kmh, qr, precision=HIGHEST), -jnp.inf)
                rank = jnp.zeros((MOBA_NB, blk), F32)
                for m_blk in range(i):
                    gm = gate[m_blk:m_blk + 1, :]
                    ahead = (gm > gate) | ((gm == gate) & (m_blk < blk_id))
                    rank = rank + jnp.where(ahead, 1.0, 0.0)
                sel = jnp.where(past & (rank < float(MOBA_TOPK)), 1.0, 0.0)
                cols.append(_dot_tn(sel, eye))
            bias = jnp.where(jnp.concatenate(cols, axis=0) > 0.5, 0.0, -jnp.inf)
        prepped[i] = (q2, bias)
        yield

    def scores(i, out):
        q2, bias = prepped.pop(i)
        m_run = None
        for j in range(i + 1):
            s = _dot_nt(q2, kr_s[j * blk:(j + 1) * blk, :])
            if j == i:
                s = jnp.where(causal, s, -jnp.inf)
            elif bias is not None:
                s = s + bias[:, j:j + 1]
            s_s[i % 2, :, j * blk:(j + 1) * blk] = s
            part = jnp.maximum(s[:, :LANES], s[:, LANES:])
            m_run = part if m_run is None else jnp.maximum(m_run, part)
            yield
        out.append(jnp.max(m_run, axis=1, keepdims=True))

    def attend(i, m_fin):
        rows = slice(i * blk, (i + 1) * blk)
        l_run = jnp.zeros((2 * blk, LANES), F32)
        acc = jnp.zeros((2 * blk, HEAD_PAIR), F32)
        m_lanes = jnp.broadcast_to(m_fin, (2 * blk, LANES))
        for j in range(i + 1):
            p_lo = jnp.exp2(s_s[i % 2, :, j * blk:j * blk + LANES] - m_lanes)
            p_hi = jnp.exp2(s_s[i % 2, :, j * blk + LANES:(j + 1) * blk] - m_lanes)
            l_run = l_run + (p_lo + p_hi)
            p = jnp.concatenate([p_lo, p_hi], axis=1)
            acc = acc + _dot(p.astype(BF16), av_ref[j * blk:(j + 1) * blk, :])
            yield
        o2 = acc / jnp.sum(l_run, axis=1, keepdims=True)
        o = jnp.where(first, o2[:blk, :], o2[blk:, :])
        sq = o * o
        tot = jnp.sum(sq, axis=-1, keepdims=True)
        s0 = jnp.sum(jnp.where(first, sq, 0.0), axis=-1, keepdims=True)
        ms = jnp.where(first, s0, tot - s0) * (1.0 / dh)
        y_ref[rows, :] = (o * lax.rsqrt(ms + EPS) * og).astype(y_ref.dtype)

    def interleave(*stages):
        stages = list(stages)
        while stages:
            for stage in list(stages):
                try:
                    next(stage)
                except StopIteration:
                    stages.remove(stage)

    m_cur = []
    interleave(key_prep(0), key_prep(1), query_prep(0), query_prep(1))
    interleave(scores(0, m_cur))
    for i in range(MOBA_NB):
        m_next = []
        stages = [attend(i, m_cur[0])]
        if i + 1 < MOBA_NB:
            stages.insert(0, scores(i + 1, m_next))
        if i + 2 < MOBA_NB:
            stages += [key_prep(i + 2), query_prep(i + 2)]
        interleave(*stages)
        m_cur = m_next


def _moba(aq, ak, av, cos_t, sin_t, q_gain, k_gain, out_gain):
    n_pairs = A_HEADS // 2
    const = lambda b, p: (0, 0)
    pair = lambda b, p: (b, p)
    return pl.pallas_call(
        _moba_kernel,
        grid=(BATCH, n_pairs),
        in_specs=[
            pl.BlockSpec((SEQ, HEAD_PAIR), pair),
            pl.BlockSpec((SEQ, HEAD_PAIR), pair),
            pl.BlockSpec((SEQ, HEAD_PAIR), pair),
            pl.BlockSpec((SEQ, HEAD_PAIR), const),
            pl.BlockSpec((SEQ, HEAD_PAIR), const),
            pl.BlockSpec((1, HEAD_PAIR), const),
            pl.BlockSpec((1, HEAD_PAIR), const),
            pl.BlockSpec((None, 1, HEAD_PAIR), lambda b, p: (p, 0, 0)),
        ],
        out_specs=pl.BlockSpec((SEQ, HEAD_PAIR), pair),
        out_shape=jax.ShapeDtypeStruct((TOKENS, A_WIDTH), BF16),
        scratch_shapes=[
            pltpu.VMEM((SEQ, HEAD_PAIR), BF16),
            pltpu.VMEM((2, 2 * MOBA_BLOCK, SEQ), F32),
        ],
        compiler_params=pltpu.CompilerParams(
            dimension_semantics=("arbitrary", "arbitrary"), vmem_limit_bytes=VMEM_LIMIT),
        name="moba",
    )(aq, ak, av, cos_t, sin_t, q_gain, k_gain, out_gain)


def _s5_kernel(u_ref, abar_ref, bre_ref, bim_ref, cre_ref, cim_ref, d_ref, gw_ref, gb_ref, gmat_ref, og_ref,
               y_ref, xr_s, xi_s, st_s, u_s, y_s):
    step = pl.program_id(0)
    rows = S5_STEPS * BATCH
    tile = 256
    n_tiles = rows // tile
    halves = S_WIDTH // LANES

    @pl.when(step == 0)
    def _init():
        st_s[...] = jnp.zeros_like(st_s)

    for b in range(BATCH):
        for hv in range(halves):
            lanes = slice(b * S_WIDTH + hv * LANES, b * S_WIDTH + (hv + 1) * LANES)
            u_s[hv, pl.ds(b, S5_STEPS, stride=BATCH), :] = u_ref[:, lanes].astype(F32)

    def u_rows(r):
        return jnp.concatenate([u_s[hv, r * tile:(r + 1) * tile, :] for hv in range(halves)], axis=1)

    for r in range(n_tiles):
        u = u_rows(r).astype(BF16)
        t0 = r * tile // BATCH
        xr_s[t0:t0 + tile // BATCH] = _dot(u, bre_ref[...]).reshape(tile // BATCH, BATCH, S_FLAT)
        xi_s[t0:t0 + tile // BATCH] = _dot(u, bim_ref[...]).reshape(tile // BATCH, BATCH, S_FLAT)

    ar = jnp.broadcast_to(abar_ref[0:1, :], (BATCH, S_FLAT))
    ai = jnp.broadcast_to(abar_ref[1:2, :], (BATCH, S_FLAT))

    def scan_step(t, carry):
        sr, si = carry
        nr = ar * sr - ai * si + xr_s[t]
        ni = ar * si + ai * sr + xi_s[t]
        xr_s[t] = nr
        xi_s[t] = ni
        return nr, ni

    sr, si = lax.fori_loop(0, S5_STEPS, scan_step, (st_s[0], st_s[1]), unroll=4)
    st_s[0] = sr
    st_s[1] = si

    for r in range(n_tiles):
        t0 = r * tile // BATCH
        xr = xr_s[t0:t0 + tile // BATCH].reshape(tile, S_FLAT).astype(BF16)
        xi = xi_s[t0:t0 + tile // BATCH].reshape(tile, S_FLAT).astype(BF16)
        y = _dot(xr, cre_ref[...]) - _dot(xi, cim_ref[...]) + d_ref[...] * u_rows(r)
        z = 0.5 * y * (1.0 + jnp.tanh(math.sqrt(2.0 / math.pi) * (y + 0.044715 * (y * y * y))))
        y2 = z * _sigmoid(_dot(z.astype(BF16), gw_ref[...]) + gb_ref[...])
        ms = _dot((y2 * y2).astype(BF16), gmat_ref[...])
        yn = y2 * lax.rsqrt(ms + EPS) * og_ref[...]
        for hv in range(halves):
            y_s[hv, r * tile:(r + 1) * tile, :] = yn[:, hv * LANES:(hv + 1) * LANES]

    for b in range(BATCH):
        for hv in range(halves):
            lanes = slice(b * S_WIDTH + hv * LANES, b * S_WIDTH + (hv + 1) * LANES)
            y_ref[:, lanes] = y_s[hv, pl.ds(b, S5_STEPS, stride=BATCH), :].astype(y_ref.dtype)


def _s5(u_sb, abar, bre, bim, cre, cim, d_row, glu_w, glu_b, gmat, out_g):
    rows = S5_STEPS * BATCH
    const = lambda s: (0, 0)
    return pl.pallas_call(
        _s5_kernel,
        grid=(SEQ // S5_STEPS,),
        in_specs=[
            pl.BlockSpec((S5_STEPS, BATCH * S_WIDTH), lambda s: (s, 0)),
            pl.BlockSpec((2, S_FLAT), const),
            pl.BlockSpec((S_WIDTH, S_FLAT), const),
            pl.BlockSpec((S_WIDTH, S_FLAT), const),
            pl.BlockSpec((S_FLAT, S_WIDTH), const),
            pl.BlockSpec((S_FLAT, S_WIDTH), const),
            pl.BlockSpec((1, S_WIDTH), const),
            pl.BlockSpec((S_WIDTH, S_WIDTH), const),
            pl.BlockSpec((1, S_WIDTH), const),
            pl.BlockSpec((S_WIDTH, S_WIDTH), const),
            pl.BlockSpec((1, S_WIDTH), const),
        ],
        out_specs=pl.BlockSpec((S5_STEPS, BATCH * S_WIDTH), lambda s: (s, 0)),
        out_shape=jax.ShapeDtypeStruct((SEQ, BATCH * S_WIDTH), BF16),
        scratch_shapes=[
            pltpu.VMEM((S5_STEPS, BATCH, S_FLAT), F32),
            pltpu.VMEM((S5_STEPS, BATCH, S_FLAT), F32),
            pltpu.VMEM((2, BATCH, S_FLAT), F32),
            pltpu.VMEM((S_WIDTH // LANES, rows, LANES), F32),
            pltpu.VMEM((S_WIDTH // LANES, rows, LANES), F32),
        ],
        compiler_params=pltpu.CompilerParams(
            dimension_semantics=("arbitrary",), vmem_limit_bytes=VMEM_LIMIT),
        name="s5",
    )(u_sb, abar, bre, bim, cre, cim, d_row, glu_w, glu_b, gmat, out_g)


def _outproj_kernel(x_ref, ym_ref, ya_ref, ys_ref, w_ref, g_ref, wr_ref, br_ref,
                    x1_ref, xn_ref, route_ref, rows_s):
    x1 = x_ref[...]
    x1 = x1 + _dot(ym_ref[...], w_ref[0:M_WIDTH, :])
    x1 = x1 + _dot(ya_ref[...], w_ref[M_WIDTH:M_WIDTH + A_WIDTH, :])
    x1 = x1 + _dot(ys_ref[...], w_ref[M_WIDTH + A_WIDTH:, :])
    x1_ref[...] = x1
    xn = x1 * lax.rsqrt(jnp.mean(x1 * x1, axis=-1, keepdims=True) + EPS) * g_ref[...]
    for j in range(ROW_TILES):
        rows_s[pl.ds(j, TM_PROJ, stride=ROW_TILES), :] = xn[:, j * LANES:(j + 1) * LANES]
    xn_ref[...] = rows_s[...].astype(xn_ref.dtype)
    logits = _split_dot(xn, wr_ref[...]) + br_ref[...]

    lane = lax.broadcasted_iota(jnp.int32, logits.shape, 1).astype(F32)
    is_grp = lane < N_GROUPS
    neg = -jnp.inf
    gl = jnp.where(is_grp, logits, neg)
    gmax = jnp.max(gl, axis=-1, keepdims=True)
    gsum = jnp.sum(jnp.where(is_grp, jnp.exp(logits - gmax), 0.0), axis=-1, keepdims=True)
    g_sel = jnp.min(jnp.where(gl == gmax, lane, float(LANES)), axis=-1, keepdims=True)
    lo = N_GROUPS + EXPERTS_PER_GROUP * g_sel
    el = jnp.where((lane >= lo) & (lane < lo + EXPERTS_PER_GROUP), logits, neg)
    t1 = jnp.max(el, axis=-1, keepdims=True)
    i1 = jnp.min(jnp.where(el == t1, lane, float(LANES)), axis=-1, keepdims=True)
    el2 = jnp.where(lane == i1, neg, el)
    t2 = jnp.max(el2, axis=-1, keepdims=True)
    i2 = jnp.min(jnp.where(el2 == t2, lane, float(LANES)), axis=-1, keepdims=True)
    e21 = jnp.exp(t2 - t1)
    w1 = 1.0 / ((1.0 + e21) * gsum)
    w2 = e21 / ((1.0 + e21) * gsum)
    route = jnp.where(lane == 0.0, i1 - N_GROUPS,
                      jnp.where(lane == 1.0, i2 - N_GROUPS,
                                jnp.where(lane == 2.0, w1, jnp.where(lane == 3.0, w2, 0.0))))
    route_ref[...] = route


def _outproj(x, y_m, y_a, y_s_sb, w_out, g2, w_router, b_router):
    n_s = SEQ // TM_PROJ
    row = lambda b, s: (b * n_s + s, 0)
    const = lambda b, s: (0, 0)
    return pl.pallas_call(
        _outproj_kernel,
        grid=(BATCH, n_s),
        in_specs=[
            pl.BlockSpec((TM_PROJ, D_MODEL), row),
            pl.BlockSpec((TM_PROJ, M_WIDTH), row),
            pl.BlockSpec((TM_PROJ, A_WIDTH), row),
            pl.BlockSpec((TM_PROJ, S_WIDTH), lambda b, s: (s, b)),
            pl.BlockSpec((D_MIX, D_MODEL), const),
            pl.BlockSpec((1, D_MODEL), const),
            pl.BlockSpec((D_MODEL, 2 * LANES), const),
            pl.BlockSpec((1, LANES), const),
        ],
        out_specs=(
            pl.BlockSpec((TM_PROJ, D_MODEL), row),
            pl.BlockSpec((TM_PROJ * ROW_TILES, LANES), row),
            pl.BlockSpec((TM_PROJ, LANES), row),
        ),
        out_shape=(
            jax.ShapeDtypeStruct((TOKENS, D_MODEL), F32),
            jax.ShapeDtypeStruct((TOKENS * ROW_TILES, LANES), BF16),
            jax.ShapeDtypeStruct((TOKENS, LANES), F32),
        ),
        scratch_shapes=[pltpu.VMEM((TM_PROJ * ROW_TILES, LANES), F32)],
        compiler_params=pltpu.CompilerParams(
            dimension_semantics=("arbitrary", "arbitrary"), vmem_limit_bytes=VMEM_LIMIT),
        name="outproj",
    )(x, y_m, y_a, y_s_sb, w_out, g2, w_router, b_router)


def _ffn_kernel(be_ref, src_ref, xs_ref, wg_ref, wu_ref, wd_ref, ys_ref, rows_s):
    del be_ref
    i = pl.program_id(0)

    @pl.when(src_ref[i] == i)
    def _expert_block():
        rows_s[...] = xs_ref[...].astype(F32)
        xb = jnp.concatenate([rows_s[pl.ds(j, FFN_BLOCK, stride=ROW_TILES), :] for j in range(ROW_TILES)],
                             axis=1).astype(BF16)
        g = _dot(xb, wg_ref[...])
        u = _dot(xb, wu_ref[...])
        hmid = (g * _sigmoid(g) * u).astype(BF16)
        y = _dot(hmid, wd_ref[...])
        for j in range(ROW_TILES):
            ys_ref[pl.ds(j, FFN_BLOCK, stride=ROW_TILES), :] = y[:, j * LANES:(j + 1) * LANES]

    @pl.when(src_ref[i] != i)
    def _unused_tail_block():
        ys_ref[...] = jnp.zeros_like(ys_ref)


def _expert_ffn(layer, blk_exp, blk_src, xs, w_gate, w_up, w_down):
    nblk = blk_exp.shape[0]
    block = (FFN_BLOCK * ROW_TILES, LANES)
    grid_spec = pltpu.PrefetchScalarGridSpec(
        num_scalar_prefetch=2,
        grid=(nblk,),
        in_specs=[
            pl.BlockSpec(block, lambda i, be, src: (src[i], 0)),
            pl.BlockSpec((None, None, D_MODEL, D_EXPERT), lambda i, be, src: (layer, be[i], 0, 0)),
            pl.BlockSpec((None, None, D_MODEL, D_EXPERT), lambda i, be, src: (layer, be[i], 0, 0)),
            pl.BlockSpec((None, None, D_EXPERT, D_MODEL), lambda i, be, src: (layer, be[i], 0, 0)),
        ],
        out_specs=pl.BlockSpec(block, lambda i, be, src: (i, 0)),
        scratch_shapes=[pltpu.VMEM(block, F32)],
    )
    return pl.pallas_call(
        _ffn_kernel,
        grid_spec=grid_spec,
        out_shape=jax.ShapeDtypeStruct((nblk * FFN_BLOCK * ROW_TILES, LANES), F32),
        compiler_params=pltpu.CompilerParams(
            dimension_semantics=("arbitrary",), vmem_limit_bytes=VMEM_LIMIT),
        name="expert_ffn",
    )(blk_exp, blk_src, xs, w_gate, w_up, w_down)


def _plan_kernel(e_ref, ps_ref, dest_ref, run_s, tri_s):
    i = pl.program_id(0)

    @pl.when(i == 0)
    def _init():
        run_s[...] = ps_ref[...]
        s_i = lax.broadcasted_iota(jnp.int32, (PLAN_TILE, PLAN_TILE), 0)
        t_i = lax.broadcasted_iota(jnp.int32, (PLAN_TILE, PLAN_TILE), 1)
        tri_s[...] = (s_i < t_i).astype(BF16)

    expert = lax.broadcasted_iota(jnp.int32, (N_EXPERTS, PLAN_TILE), 0)
    onehot = jnp.where(e_ref[...] == expert, 1.0, 0.0)
    before = _dot(onehot.astype(BF16), tri_s[...])
    dest = jnp.sum(onehot * (before + run_s[:, 0:1]), axis=0, keepdims=True)
    dest_ref[...] = dest.astype(jnp.int32)
    run_s[...] = run_s[...] + jnp.sum(onehot, axis=1, keepdims=True)


def _plan(e_rows, pstart_col):
    n_tiles = e_rows.shape[0]
    return pl.pallas_call(
        _plan_kernel,
        grid=(n_tiles,),
        in_specs=[
            pl.BlockSpec((None, 1, PLAN_TILE), lambda i: (i, 0, 0)),
            pl.BlockSpec((N_EXPERTS, LANES), lambda i: (0, 0)),
        ],
        out_specs=pl.BlockSpec((None, 1, PLAN_TILE), lambda i: (i, 0, 0)),
        out_shape=jax.ShapeDtypeStruct((n_tiles, 1, PLAN_TILE), jnp.int32),
        scratch_shapes=[pltpu.VMEM((N_EXPERTS, LANES), F32), pltpu.VMEM((PLAN_TILE, PLAN_TILE), BF16)],
        compiler_params=pltpu.CompilerParams(
            dimension_semantics=("arbitrary",), vmem_limit_bytes=VMEM_LIMIT),
        name="moe_plan",
    )(e_rows, pstart_col)


def _dispatch_kernel(fill_ref, d0_ref, d1_ref, xn_ref, xs_hbm, zero_s, sem, fill_sem):
    @pl.when(pl.program_id(0) == 0)
    def _zero_padding():
        zero_s[...] = jnp.zeros_like(zero_s)

        def fill(c):
            start = pl.multiple_of(fill_ref[0, c] * ROW_TILES, ROW_TILES)
            return pltpu.make_async_copy(zero_s, xs_hbm.at[pl.ds(start, FFN_BLOCK * ROW_TILES), :], fill_sem)

        for c in range(N_FILLS):
            pl.when(fill_ref[0, c] >= 0)(lambda c=c: fill(c).start())
        for c in range(N_FILLS):
            pl.when(fill_ref[0, c] >= 0)(lambda c=c: fill(c).wait())

    def row_copy(dref, t):
        src = xn_ref.at[pl.ds(pl.multiple_of(t * ROW_TILES, ROW_TILES), ROW_TILES), :]
        dst = xs_hbm.at[pl.ds(pl.multiple_of(dref[0, t] * ROW_TILES, ROW_TILES), ROW_TILES), :]
        return pltpu.make_async_copy(src, dst, sem)

    def start(t, carry):
        row_copy(d0_ref, t).start()
        row_copy(d1_ref, t).start(priority=1)
        return carry

    lax.fori_loop(0, DISPATCH_TOKENS, start, 0, unroll=8)
    for _ in range(EXPERT_TOPK):
        pltpu.make_async_copy(xn_ref, xs_hbm.at[pl.ds(0, DISPATCH_TOKENS * ROW_TILES), :], sem).wait()


def _dispatch(fill_rows, dest, xn, n_rows):
    n_tiles = TOKENS // DISPATCH_TOKENS
    dest4 = dest.reshape(EXPERT_TOPK, n_tiles, 1, DISPATCH_TOKENS)
    smem_tile = lambda k: pl.BlockSpec((None, None, 1, DISPATCH_TOKENS), lambda i: (k, i, 0, 0),
                                       memory_space=pltpu.SMEM)
    return pl.pallas_call(
        _dispatch_kernel,
        grid=(n_tiles,),
        in_specs=[
            pl.BlockSpec(memory_space=pltpu.SMEM),
            smem_tile(0),
            smem_tile(1),
            pl.BlockSpec((DISPATCH_TOKENS * ROW_TILES, LANES), lambda i: (i, 0)),
        ],
        out_specs=pl.BlockSpec(memory_space=pl.ANY),
        out_shape=jax.ShapeDtypeStruct((n_rows * ROW_TILES, LANES), BF16),
        scratch_shapes=[
            pltpu.VMEM((FFN_BLOCK * ROW_TILES, LANES), BF16),
            pltpu.SemaphoreType.DMA(()),
            pltpu.SemaphoreType.DMA(()),
        ],
        compiler_params=pltpu.CompilerParams(
            dimension_semantics=("arbitrary",), vmem_limit_bytes=VMEM_LIMIT),
        name="moe_dispatch",
    )(fill_rows, dest4, dest4, xn)


def _combine_kernel(d0_ref, d1_ref, d0_next_ref, d1_next_ref, x1_ref, route_ref, ys_hbm, out_ref, buf, sem):
    i = pl.program_id(0)
    n = pl.num_programs(0)

    def start_tile(drefs, slot):
        def body(t, carry):
            for k in range(EXPERT_TOPK):
                src = ys_hbm.at[pl.ds(pl.multiple_of(drefs[k][0, t] * ROW_TILES, ROW_TILES), ROW_TILES), :]
                dst = buf.at[slot, k, pl.ds(pl.multiple_of(t * ROW_TILES, ROW_TILES), ROW_TILES), :]
                pltpu.make_async_copy(src, dst, sem.at[slot]).start(priority=k)
            return carry
        lax.fori_loop(0, COMBINE_TOKENS, body, 0, unroll=8)

    slot = i % 2

    @pl.when(i == 0)
    def _first():
        start_tile((d0_ref, d1_ref), 0)

    @pl.when(i + 1 < n)
    def _prefetch():
        start_tile((d0_next_ref, d1_next_ref), 1 - slot)

    for k in range(EXPERT_TOPK):
        pltpu.make_async_copy(ys_hbm.at[pl.ds(0, COMBINE_TOKENS * ROW_TILES), :], buf.at[slot, k],
                              sem.at[slot]).wait()
    w = route_ref[...]
    w1 = w[:, 2:3]
    w2 = w[:, 3:4]
    for j in range(ROW_TILES):
        cols = slice(j * LANES, (j + 1) * LANES)
        chunk = pl.ds(j, COMBINE_TOKENS, stride=ROW_TILES)
        out_ref[:, cols] = x1_ref[:, cols] + w1 * buf[slot, 0, chunk, :] + w2 * buf[slot, 1, chunk, :]


def _combine(dest, x1, route, ys3):
    n_tiles = TOKENS // COMBINE_TOKENS
    dest4 = dest.reshape(EXPERT_TOPK, n_tiles, 1, COMBINE_TOKENS)
    smem_tile = lambda k, nxt: pl.BlockSpec(
        (None, None, 1, COMBINE_TOKENS), lambda i: (k, jnp.minimum(i + nxt, n_tiles - 1), 0, 0),
        memory_space=pltpu.SMEM)
    return pl.pallas_call(
        _combine_kernel,
        grid=(n_tiles,),
        in_specs=[
            smem_tile(0, 0), smem_tile(1, 0), smem_tile(0, 1), smem_tile(1, 1),
            pl.BlockSpec((COMBINE_TOKENS, D_MODEL), lambda i: (i, 0)),
            pl.BlockSpec((COMBINE_TOKENS, LANES), lambda i: (i, 0)),
            pl.BlockSpec(memory_space=pl.ANY),
        ],
        out_specs=pl.BlockSpec((COMBINE_TOKENS, D_MODEL), lambda i: (i, 0)),
        out_shape=jax.ShapeDtypeStruct((TOKENS, D_MODEL), F32),
        scratch_shapes=[
            pltpu.VMEM((2, EXPERT_TOPK, COMBINE_TOKENS * ROW_TILES, LANES), F32),
            pltpu.SemaphoreType.DMA((2,)),
        ],
        compiler_params=pltpu.CompilerParams(
            dimension_semantics=("arbitrary",), vmem_limit_bytes=VMEM_LIMIT),
        name="moe_combine",
    )(dest4, dest4, dest4, dest4, x1, route, ys3)


def _rope_tables():
    inv = ROPE_THETA ** (-np.arange(ROPE_HALF, dtype=np.float64) * 2.0 / ROPE_DIM)
    ang = np.arange(SEQ, dtype=np.float64)[:, None] * inv[None, :]
    cos = np.ones((SEQ, A_HEAD_DIM))
    sin = np.zeros((SEQ, A_HEAD_DIM))
    cos[:, :ROPE_HALF] = np.cos(ang)
    cos[:, ROPE_HALF:ROPE_DIM] = np.cos(ang)
    sin[:, :ROPE_HALF] = -np.sin(ang)
    sin[:, ROPE_HALF:ROPE_DIM] = np.sin(ang)
    return (jnp.asarray(np.tile(cos, (1, 2)), F32), jnp.asarray(np.tile(sin, (1, 2)), F32))


def _split_w_in(w_in):
    sizes = [M_WIDTH, M_WIDTH, M_WIDTH, M_WIDTH, M_HEADS, M_HEADS, A_WIDTH, A_WIDTH, A_WIDTH, S_WIDTH]
    offs = np.cumsum([0] + sizes)
    mq, mk, mv, mo, mi, mf, aq, ak, av, su = (w_in[:, offs[n]:offs[n + 1]] for n in range(len(sizes)))
    w_main = jnp.concatenate([aq, ak, av, mq, mk, mv, mo, su], axis=1).astype(BF16)
    gate = jnp.concatenate([mi, mf], axis=1)
    return w_main, _hi_lo(jnp.pad(gate, ((0, 0), (0, LANES - 2 * M_HEADS))))


def _s5_params(a_re, a_im, b_re, b_im, c_re, c_im, log_dt):
    dt = jnp.exp(log_dt)[:, None]
    mag = jnp.exp(a_re * dt)
    abar_re, abar_im = mag * jnp.cos(a_im * dt), mag * jnp.sin(a_im * dt)
    zr, zi = abar_re - 1.0, abar_im
    den = a_re * a_re + a_im * a_im
    fr, fi = (zr * a_re + zi * a_im) / den, (zi * a_re - zr * a_im) / den
    bbar_re = fr[..., None] * b_re - fi[..., None] * b_im
    bbar_im = fr[..., None] * b_im + fi[..., None] * b_re
    eye = jnp.eye(S_GROUPS, dtype=F32)
    dense_b = lambda t: jnp.einsum('gph,gk->ghkp', t, eye).reshape(S_WIDTH, S_FLAT).astype(BF16)
    dense_c = lambda t: jnp.einsum('ghp,gk->gpkh', t, eye).reshape(S_FLAT, S_WIDTH).astype(BF16)
    abar = jnp.stack([abar_re.reshape(S_FLAT), abar_im.reshape(S_FLAT)])
    return abar, dense_b(bbar_re), dense_b(bbar_im), dense_c(c_re), dense_c(c_im)


def _moe(layer, x1, xn, route, w_gate, w_up, w_down):
    n_assign = TOKENS * EXPERT_TOPK
    e_flat = route[:, :EXPERT_TOPK].astype(jnp.int32).T.reshape(n_assign)
    experts = jnp.arange(N_EXPERTS, dtype=jnp.int32)
    counts = jnp.sum((e_flat[:, None] == experts[None, :]).astype(jnp.int32), axis=0)
    padded = (counts + FFN_BLOCK - 1) // FFN_BLOCK * FFN_BLOCK
    pends = jnp.cumsum(padded)
    pstarts = pends - padded
    n_rows = (n_assign + N_EXPERTS * (FFN_BLOCK - 1) + FFN_BLOCK - 1) // FFN_BLOCK * FFN_BLOCK
    nblk = n_rows // FFN_BLOCK
    blk_start = jnp.arange(nblk, dtype=jnp.int32) * FFN_BLOCK
    blk_exp = jnp.minimum(jnp.sum((pends[None, :] <= blk_start[:, None]).astype(jnp.int32), axis=1),
                          N_EXPERTS - 1).astype(jnp.int32)
    pstart_col = jnp.broadcast_to(pstarts.astype(F32)[:, None], (N_EXPERTS, LANES))
    dest = _plan(e_flat.reshape(n_assign // PLAN_TILE, 1, PLAN_TILE), pstart_col)
    tail = pends[-1] + jnp.arange(N_EXPERTS, dtype=jnp.int32) * FFN_BLOCK
    fill_rows = jnp.concatenate([jnp.where(padded > 0, pends - FFN_BLOCK, -1), jnp.where(tail < n_rows, tail, -1)])
    fill_rows = fill_rows.astype(jnp.int32).reshape(1, N_FILLS)
    xs = _dispatch(fill_rows, dest, xn, n_rows)
    blk_src = jnp.minimum(jnp.arange(nblk, dtype=jnp.int32), pends[-1] // FFN_BLOCK - 1).astype(jnp.int32)
    ys = _expert_ffn(layer, blk_exp, blk_src, xs, w_gate, w_up, w_down)
    return _combine(dest, x1, route, ys)


def kernel(x, norm1_g, w_in, m_bias_i, m_bias_f, m_conv_w, m_conv_b, m_out_g, a_q_g, a_k_g, a_out_g, s_a_re, s_a_im, s_b_re, s_b_im, s_c_re, s_c_im, s_d, s_log_dt, s_glu_w, s_glu_b, s_out_g, w_out, norm2_g, r_group_w, r_group_b, r_expert_w, r_expert_b, e_w_gate, e_w_up, e_w_down):
    cos_t, sin_t = _rope_tables()
    gmat = jnp.asarray(np.kron(np.eye(S_GROUPS), np.full((S_GROUP_DIM, S_GROUP_DIM), 1.0 / S_GROUP_DIM)), BF16)
    xf = x.reshape(TOKENS, D_MODEL)
    for l in range(DEPTH):
        w_main, w_gate = _split_w_in(w_in[l])
        aq, ak, av, mqk, mv, mo, su, gates, gatest = _inproj(xf, norm1_g[l].reshape(1, D_MODEL), w_main, w_gate)

        bias = jnp.concatenate([m_bias_i[l], m_bias_f[l]])
        bias_row = jnp.pad(bias, (0, LANES - 2 * M_HEADS)).reshape(1, LANES)
        y_m = _mlstm(mqk, mv, mo, gates, gatest, m_conv_w[l], m_conv_b[l].reshape(1, 2 * M_WIDTH),
                     bias_row, bias.reshape(2 * M_HEADS, 1), m_out_g[l].reshape(1, M_WIDTH))

        y_a = _moba(aq, ak, av, cos_t, sin_t,
                    jnp.tile(a_q_g[l], 2).reshape(1, HEAD_PAIR), jnp.tile(a_k_g[l], 2).reshape(1, HEAD_PAIR),
                    a_out_g[l].reshape(A_HEADS // 2, 1, HEAD_PAIR))

        abar, bre, bim, cre, cim = _s5_params(s_a_re[l], s_a_im[l], s_b_re[l], s_b_im[l],
                                              s_c_re[l], s_c_im[l], s_log_dt[l])
        y_s = _s5(su, abar, bre, bim, cre, cim,
                  s_d[l].reshape(1, S_WIDTH), s_glu_w[l].astype(BF16), s_glu_b[l].reshape(1, S_WIDTH),
                  gmat, s_out_g[l].reshape(1, S_WIDTH))

        w_router = jnp.pad(jnp.concatenate([r_group_w[l], r_expert_w[l]], axis=1),
                           ((0, 0), (0, LANES - N_GROUPS - N_EXPERTS)))
        b_router = jnp.pad(jnp.concatenate([r_group_b[l], r_expert_b[l]]),
                           (0, LANES - N_GROUPS - N_EXPERTS)).reshape(1, LANES)
        x1, xn, route = _outproj(xf, y_m, y_a, y_s, w_out[l].astype(BF16),
                                 norm2_g[l].reshape(1, D_MODEL), _hi_lo(w_router), b_router)
        xf = _moe(l, x1, xn, route, e_w_gate, e_w_up, e_w_down)
    return xf.reshape(BATCH, SEQ, D_MODEL)
```

```python
import functools
import math

import numpy as np
import jax
import jax.numpy as jnp
from jax import lax
from jax.experimental import pallas as pl
from jax.experimental.pallas import tpu as pltpu

F32 = jnp.float32
BF16 = jnp.bfloat16

D_MODEL = 1024
BATCH = 8
SEQ = 2048
DEPTH = 2
TOKENS = BATCH * SEQ

M_HEADS = 4
M_HEAD_DIM = 64
M_WIDTH = M_HEADS * M_HEAD_DIM
A_HEADS = 8
A_HEAD_DIM = 64
A_WIDTH = A_HEADS * A_HEAD_DIM
S_GROUPS = 16
S_GROUP_DIM = 16
S_WIDTH = S_GROUPS * S_GROUP_DIM
S_STATE = 64
S_FLAT = S_GROUPS * S_STATE
D_MIX = M_WIDTH + A_WIDTH + S_WIDTH

CONV_K = 4
MOBA_BLOCK = 256
MOBA_NB = SEQ // MOBA_BLOCK
MOBA_TOPK = 3
ROPE_THETA = 500000.0
ROPE_DIM = A_HEAD_DIM // 4
ROPE_HALF = ROPE_DIM // 2

N_GROUPS = 4
EXPERTS_PER_GROUP = 8
N_EXPERTS = N_GROUPS * EXPERTS_PER_GROUP
EXPERT_TOPK = 2
D_EXPERT = 512
EPS = 1e-6

LANES = 128
ROW_TILES = D_MODEL // LANES
HEAD_PAIR = 2 * A_HEAD_DIM

W_MAIN = 3 * A_WIDTH + 4 * M_WIDTH + S_WIDTH
TM_PROJ = 1024
M_CHUNK = 256
M_NCHUNK = SEQ // M_CHUNK
S5_STEPS = 256
FFN_BLOCK = 512
N_FILLS = 2 * N_EXPERTS
PLAN_TILE = 1024
DISPATCH_TOKENS = 2048
COMBINE_TOKENS = 256
VMEM_LIMIT = 56 * 1024 * 1024

HIGHEST = lax.Precision.HIGHEST


def _dot(a, b, precision=None):
    return jnp.dot(a, b, preferred_element_type=F32, precision=precision)


def _dot_nt(a, b, precision=None):
    return lax.dot_general(a, b, (((1,), (1,)), ((), ())), preferred_element_type=F32, precision=precision)


def _dot_tn(a, b, precision=None):
    return lax.dot_general(a, b, (((0,), (0,)), ((), ())), preferred_element_type=F32, precision=precision)


def _split_dot(a, w_cat):
    n = w_cat.shape[1] // 2
    a_hi = a.astype(BF16)
    a_lo = (a - a_hi.astype(F32)).astype(BF16)
    both = _dot(a_hi, w_cat)
    return both[:, :n] + both[:, n:] + _dot(a_lo, w_cat[:, :n])


def _bf16_terms(x):
    hi = x.astype(BF16)
    rest = x - hi.astype(F32)
    mid = rest.astype(BF16)
    return hi, mid, (rest - mid.astype(F32)).astype(BF16)


def _hi_lo(w):
    w_hi = w.astype(BF16)
    return jnp.concatenate([w_hi, (w - w_hi.astype(F32)).astype(BF16)], axis=1)


def _log_sigmoid(x):
    return jnp.minimum(x, 0.0) - jnp.log1p(jnp.exp(-jnp.abs(x)))


def _sigmoid(x):
    return 1.0 / (1.0 + jnp.exp(-x))


def _inproj_kernel(x_ref, g_ref, w_ref, wg_ref,
                   aq_ref, ak_ref, av_ref, mqk_ref, mv_ref, mo_ref, su_ref, gates_ref, gatest_ref):
    x = x_ref[...]
    h = x * lax.rsqrt(jnp.mean(x * x, axis=-1, keepdims=True) + EPS) * g_ref[...]
    hb = h.astype(BF16)
    off = 0
    for ref, width in ((aq_ref, A_WIDTH), (ak_ref, A_WIDTH), (av_ref, A_WIDTH), (mqk_ref, 2 * M_WIDTH),
                       (mv_ref, M_WIDTH), (mo_ref, M_WIDTH), (su_ref, S_WIDTH)):
        ref[...] = _dot(hb, w_ref[:, off:off + width]).astype(ref.dtype)
        off += width
    gates = _split_dot(h, wg_ref[...])
    gates_ref[...] = gates
    gt = gates.T[:8, :]
    for c in range(TM_PROJ // M_CHUNK):
        gatest_ref[c] = gt[:, c * M_CHUNK:(c + 1) * M_CHUNK]


def _inproj(x, g, w_main, w_gate):
    n_s = SEQ // TM_PROJ
    row = lambda b, s: (b * n_s + s, 0)
    const = lambda b, s: (0, 0)
    out_shapes = (
        jax.ShapeDtypeStruct((TOKENS, A_WIDTH), BF16),
        jax.ShapeDtypeStruct((TOKENS, A_WIDTH), BF16),
        jax.ShapeDtypeStruct((TOKENS, A_WIDTH), BF16),
        jax.ShapeDtypeStruct((TOKENS, 2 * M_WIDTH), BF16),
        jax.ShapeDtypeStruct((TOKENS, M_WIDTH), BF16),
        jax.ShapeDtypeStruct((TOKENS, M_WIDTH), BF16),
        jax.ShapeDtypeStruct((SEQ, BATCH * S_WIDTH), BF16),
        jax.ShapeDtypeStruct((TOKENS, LANES), F32),
        jax.ShapeDtypeStruct((TOKENS // M_CHUNK, 8, M_CHUNK), F32),
    )
    out_specs = (
        pl.BlockSpec((TM_PROJ, A_WIDTH), row),
        pl.BlockSpec((TM_PROJ, A_WIDTH), row),
        pl.BlockSpec((TM_PROJ, A_WIDTH), row),
        pl.BlockSpec((TM_PROJ, 2 * M_WIDTH), row),
        pl.BlockSpec((TM_PROJ, M_WIDTH), row),
        pl.BlockSpec((TM_PROJ, M_WIDTH), row),
        pl.BlockSpec((TM_PROJ, S_WIDTH), lambda b, s: (s, b)),
        pl.BlockSpec((TM_PROJ, LANES), row),
        pl.BlockSpec((TM_PROJ // M_CHUNK, 8, M_CHUNK), lambda b, s: (b * n_s + s, 0, 0)),
    )
    return pl.pallas_call(
        _inproj_kernel,
        grid=(BATCH, n_s),
        in_specs=[
            pl.BlockSpec((TM_PROJ, D_MODEL), row),
            pl.BlockSpec((1, D_MODEL), const),
            pl.BlockSpec((D_MODEL, W_MAIN), const),
            pl.BlockSpec((D_MODEL, 2 * LANES), const),
        ],
        out_specs=out_specs,
        out_shape=out_shapes,
        compiler_params=pltpu.CompilerParams(
            dimension_semantics=("arbitrary", "arbitrary"), vmem_limit_bytes=VMEM_LIMIT),
        name="inproj",
    )(x, g, w_main, w_gate)


def _mlstm_kernel(mqk_ref, mv_ref, mo_ref, gates_ref, gatest_ref, cw_ref, cb_ref, brow_ref, bcol_ref, og_ref,
                  y_ref, qk_s, cols_s, wrow_s):
    L = M_CHUNK
    dh = M_HEAD_DIM
    r_i = lax.broadcasted_iota(jnp.int32, (L, L), 0)
    c_i = lax.broadcasted_iota(jnp.int32, (L, L), 1)
    causal = c_i <= r_i
    tri_u = (r_i <= c_i).astype(BF16)
    eye = (lax.broadcasted_iota(jnp.int32, (8, LANES), 0)
           == lax.broadcasted_iota(jnp.int32, (8, LANES), 1)).astype(BF16)
    w = cw_ref[...]
    cb = cb_ref[...]
    lane_q = lax.broadcasted_iota(jnp.int32, (1, 2 * M_WIDTH), 1) < M_WIDTH
    kscale = jnp.where(lane_q, 1.0, dh ** -0.5).astype(F32)
    for c in range(M_NCHUNK):
        halo = 16 if c > 0 else 0
        for strip in range(2 * M_WIDTH // LANES):
            cols = slice(strip * LANES, (strip + 1) * LANES)
            xt = mqk_ref[c * L - halo:(c + 1) * L, cols].astype(F32)
            rows = lax.broadcasted_iota(jnp.int32, xt.shape, 0)
            y = cb[:, cols] + w[CONV_K - 1:CONV_K, cols] * xt
            for k in range(1, CONV_K):
                sh = pltpu.roll(xt, k, 0)
                if c == 0:
                    sh = jnp.where(rows >= k, sh, 0.0)
                y = y + w[CONV_K - 1 - k:CONV_K - k, cols] * sh
            y = y[halo:, :]
            y = y * _sigmoid(y) * kscale[:, cols]
            qk_s[c * L:(c + 1) * L, cols] = y.astype(BF16)

    rows_all = M_NCHUNK * 8
    grow = gatest_ref[...].reshape(rows_all, L) + jnp.tile(bcol_ref[...], (M_NCHUNK, 1))
    cs_row = sum(_dot(term, tri_u) for term in _bf16_terms(_log_sigmoid(grow)))
    w_row = grow - pltpu.roll(cs_row, rows_all - M_HEADS, 0)
    lane_t = lax.broadcasted_iota(jnp.int32, (rows_all, L), 1)
    pm = w_row
    shift = 1
    while shift < L:
        pm = jnp.maximum(pm, jnp.where(lane_t >= shift, pltpu.roll(pm, shift, 1), -jnp.inf))
        shift *= 2
    wrow_s[...] = w_row.reshape(M_NCHUNK, 8, L)
    is_i_row = lax.broadcasted_iota(jnp.int32, (rows_all, L), 0) % 8 < M_HEADS
    terms = _bf16_terms(jnp.where(is_i_row, pm, cs_row))
    for c in range(M_NCHUNK):
        cols_s[c * L:(c + 1) * L, :] = sum(_dot_tn(term[c * 8:(c + 1) * 8, :], eye) for term in terms)

    pad_lane = lax.broadcasted_iota(jnp.int32, (L, LANES - dh), 1)
    ones_pad = jnp.where(pad_lane == 0, 1.0, 0.0).astype(BF16)
    og = og_ref[...]
    head_mean = jnp.where(lax.broadcasted_iota(jnp.int32, (M_WIDTH, M_WIDTH), 0) // dh
                          == lax.broadcasted_iota(jnp.int32, (M_WIDTH, M_WIDTH), 1) // dh,
                          1.0 / dh, 0.0).astype(BF16)

    def chunk(c, carry):
        r0 = pl.multiple_of(c * L, L)
        gcol = gates_ref[pl.ds(r0, L), :] + brow_ref[...]
        cs_col = pm_col = cols_s[pl.ds(r0, L), :]
        w_row = wrow_s[c]
        qk = qk_s[pl.ds(r0, L), :]
        vv = mv_ref[pl.ds(r0, L), :]
        new_carry = []
        hs = []
        for h in range(M_HEADS):
            cst, m = carry[h]
            q = qk[:, h * dh:(h + 1) * dh]
            k = qk[:, M_WIDTH + h * dh:M_WIDTH + (h + 1) * dh]
            v_aug = jnp.concatenate([vv[:, h * dh:(h + 1) * dh], ones_pad], axis=1)
            bc = cs_col[:, M_HEADS + h:M_HEADS + h + 1]
            ic = gcol[:, h:h + 1]
            m_row = jnp.maximum(m, pm_col[:, h:h + 1])
            mt = bc + m_row
            inter = jnp.exp(m - m_row)
            sqk = _dot_nt(q, k) * jnp.exp(jnp.where(causal, w_row[h:h + 1, :] - m_row, -jnp.inf))
            nd = inter * _dot(q, cst.astype(BF16)) + _dot(sqk.astype(BF16), v_aug)
            num = nd[:, :dh]
            den = nd[:, dh:dh + 1]
            hs.append(num / jnp.maximum(jnp.abs(den), jnp.exp(-mt)))
            bl = bc[L - 1:L, :]
            gs = bl - bc + ic
            m_new = jnp.maximum(bl + m, jnp.max(gs, axis=0, keepdims=True))
            decay = jnp.exp(bl + m - m_new)
            wts = jnp.exp(gs - m_new)
            kw = (k.astype(F32) * wts).astype(BF16)
            new_carry.append((decay * cst + _dot_tn(kw, v_aug), m_new))
        hcat = jnp.concatenate(hs, axis=1)
        hcat = hcat * lax.rsqrt(_dot((hcat * hcat).astype(BF16), head_mean) + EPS) * og
        o = mo_ref[pl.ds(r0, L), :].astype(F32)
        y_ref[pl.ds(r0, L), :] = (hcat * _sigmoid(o)).astype(y_ref.dtype)
        return tuple(new_carry)

    init = tuple((jnp.zeros((dh, LANES), F32), jnp.zeros((1, 1), F32)) for _ in range(M_HEADS))
    lax.fori_loop(0, M_NCHUNK, chunk, init, unroll=2)


def _mlstm(mqk, mv, mo, gates, gatest, conv_w, conv_b, bias_row, bias_col, out_g):
    per_b = lambda b: (b, 0)
    const = lambda b: (0, 0)
    return pl.pallas_call(
        _mlstm_kernel,
        grid=(BATCH,),
        in_specs=[
            pl.BlockSpec((SEQ, 2 * M_WIDTH), per_b),
            pl.BlockSpec((SEQ, M_WIDTH), per_b),
            pl.BlockSpec((SEQ, M_WIDTH), per_b),
            pl.BlockSpec((SEQ, LANES), per_b),
            pl.BlockSpec((M_NCHUNK, 8, M_CHUNK), lambda b: (b, 0, 0)),
            pl.BlockSpec((CONV_K, 2 * M_WIDTH), const),
            pl.BlockSpec((1, 2 * M_WIDTH), const),
            pl.BlockSpec((1, LANES), const),
            pl.BlockSpec((8, 1), const),
            pl.BlockSpec((1, M_WIDTH), const),
        ],
        out_specs=pl.BlockSpec((SEQ, M_WIDTH), per_b),
        out_shape=jax.ShapeDtypeStruct((TOKENS, M_WIDTH), BF16),
        scratch_shapes=[
            pltpu.VMEM((SEQ, 2 * M_WIDTH), BF16),
            pltpu.VMEM((SEQ, LANES), F32),
            pltpu.VMEM((M_NCHUNK, 8, M_CHUNK), F32),
        ],
        compiler_params=pltpu.CompilerParams(
            dimension_semantics=("arbitrary",), vmem_limit_bytes=VMEM_LIMIT),
        name="mlstm",
    )(mqk, mv, mo, gates, gatest, conv_w, conv_b, bias_row, bias_col, out_g)


def _qk_norm_rope(x, gain, cos, sin):
    lane = lax.broadcasted_iota(jnp.int32, x.shape, 1)
    first = lane < A_HEAD_DIM
    sq = x * x
    tot = jnp.sum(sq, axis=-1, keepdims=True)
    s0 = jnp.sum(jnp.where(first, sq, 0.0), axis=-1, keepdims=True)
    ms = jnp.where(first, s0, tot - s0) * (1.0 / A_HEAD_DIM)
    xn = x * lax.rsqrt(ms + EPS) * gain
    in_low = (lane % A_HEAD_DIM) < ROPE_HALF
    swapped = jnp.where(in_low, pltpu.roll(xn, LANES - ROPE_HALF, 1), pltpu.roll(xn, ROPE_HALF, 1))
    return xn * cos + swapped * sin


def _moba_kernel(aq_ref, ak_ref, av_ref, cos_ref, sin_ref, qg_ref, kg_ref, og_ref, y_ref, kr_s, s_s):
    blk = MOBA_BLOCK
    dh = A_HEAD_DIM
    km_first = lax.broadcasted_iota(jnp.int32, (MOBA_NB, HEAD_PAIR), 1) < dh
    first = lax.broadcasted_iota(jnp.int32, (blk, HEAD_PAIR), 1) < dh
    blk_id = lax.broadcasted_iota(jnp.int32, (MOBA_NB, blk), 0)
    eye = (lax.broadcasted_iota(jnp.int32, (MOBA_NB, LANES), 0)
           == lax.broadcasted_iota(jnp.int32, (MOBA_NB, LANES), 1)).astype(F32)
    q_pos = lax.broadcasted_iota(jnp.int32, (2 * blk, blk), 0) % blk
    k_pos = lax.broadcasted_iota(jnp.int32, (2 * blk, blk), 1)
    causal = k_pos <= q_pos
    og = og_ref[...]

    km_rows = []
    prepped = {}

    def key_prep(j):
        rows = slice(j * blk, (j + 1) * blk)
        kr = _qk_norm_rope(ak_ref[rows, :].astype(F32), kg_ref[...], cos_ref[rows, :], sin_ref[rows, :])
        kr_s[rows, :] = kr.astype(BF16)
        km_rows.append(jnp.mean(kr, axis=0, keepdims=True))
        yield

    def query_prep(i):
        rows = slice(i * blk, (i + 1) * blk)
        qr = _qk_norm_rope(aq_ref[rows, :].astype(F32), qg_ref[...], cos_ref[rows, :], sin_ref[rows, :])
        qs = qr * (dh ** -0.5 * math.log2(math.e))
        q2 = jnp.concatenate([jnp.where(first, qs, 0.0), jnp.where(first, 0.0, qs)], axis=0).astype(BF16)
        yield

        bias = None
        if i > MOBA_TOPK:
            km = jnp.concatenate(km_rows[:i] + [jnp.zeros((MOBA_NB - i, HEAD_PAIR), F32)], axis=0)
            km_heads = (jnp.where(km_first, km, 0.0), jnp.where(km_first, 0.0, km))
            past = blk_id < i
            cols = []
            for kmh in km_heads:
                gate = jnp.where(past, _dot_nt(kmh, qr, precision=HIGHEST), -jnp.inf)
                rank = jnp.zeros((MOBA_NB, blk), F32)
                for m_blk in range(i):
                    gm = gate[m_blk:m_blk + 1, :]
                    ahead = (gm > gate) | ((gm == gate) & (m_blk < blk_id))
                    rank = rank + jnp.where(ahead, 1.0, 0.0)
                sel = jnp.where(past & (rank < float(MOBA_TOPK)), 1.0, 0.0)
                cols.append(_dot_tn(sel, eye))
            bias = jnp.where(jnp.concatenate(cols, axis=0) > 0.5, 0.0, -jnp.inf)
        prepped[i] = (q2, bias)
        yield

    def scores(i, out):
        q2, bias = prepped.pop(i)
        m_run = None
        for j in range(i + 1):
            s = _dot_nt(q2, kr_s[j * blk:(j + 1) * blk, :])
            if j == i:
                s = jnp.where(causal, s, -jnp.inf)
            elif bias is not None:
                s = s + bias[:, j:j + 1]
            s_s[i % 2, :, j * blk:(j + 1) * blk] = s
            part = jnp.maximum(s[:, :LANES], s[:, LANES:])
            m_run = part if m_run is None else jnp.maximum(m_run, part)
            yield
        out.append(jnp.max(m_run, axis=1, keepdims=True))

    def attend(i, m_fin):
        rows = slice(i * blk, (i + 1) * blk)
        l_run = jnp.zeros((2 * blk, LANES), F32)
        acc = jnp.zeros((2 * blk, HEAD_PAIR), F32)
        m_lanes = jnp.broadcast_to(m_fin, (2 * blk, LANES))
        for j in range(i + 1):
            p_lo = jnp.exp2(s_s[i % 2, :, j * blk:j * blk + LANES] - m_lanes)
            p_hi = jnp.exp2(s_s[i % 2, :, j * blk + LANES:(j + 1) * blk] - m_lanes)
            l_run = l_run + (p_lo + p_hi)
            p = jnp.concatenate([p_lo, p_hi], axis=1)
            acc = acc + _dot(p.astype(BF16), av_ref[j * blk:(j + 1) * blk, :])
            yield
        o2 = acc / jnp.sum(l_run, axis=1, keepdims=True)
        o = jnp.where(first, o2[:blk, :], o2[blk:, :])
        sq = o * o
        tot = jnp.sum(sq, axis=-1, keepdims=True)
        s0 = jnp.sum(jnp.where(first, sq, 0.0), axis=-1, keepdims=True)
        ms = jnp.where(first, s0, tot - s0) * (1.0 / dh)
        y_ref[rows, :] = (o * lax.rsqrt(ms + EPS) * og).astype(y_ref.dtype)

    def interleave(*stages):
        stages = list(stages)
        while stages:
            for stage in list(stages):
                try:
                    next(stage)
                except StopIteration:
                    stages.remove(stage)

    m_cur = []
    interleave(key_prep(0), key_prep(1), query_prep(0), query_prep(1))
    interleave(scores(0, m_cur))
    for i in range(MOBA_NB):
        m_next = []
        stages = [attend(i, m_cur[0])]
        if i + 1 < MOBA_NB:
            stages.insert(0, scores(i + 1, m_next))
        if i + 2 < MOBA_NB:
            stages += [key_prep(i + 2), query_prep(i + 2)]
        interleave(*stages)
        m_cur = m_next


def _moba(aq, ak, av, cos_t, sin_t, q_gain, k_gain, out_gain):
    n_pairs = A_HEADS // 2
    const = lambda b, p: (0, 0)
    pair = lambda b, p: (b, p)
    return pl.pallas_call(
        _moba_kernel,
        grid=(BATCH, n_pairs),
        in_specs=[
            pl.BlockSpec((SEQ, HEAD_PAIR), pair),
            pl.BlockSpec((SEQ, HEAD_PAIR), pair),
            pl.BlockSpec((SEQ, HEAD_PAIR), pair),
            pl.BlockSpec((SEQ, HEAD_PAIR), const),
            pl.BlockSpec((SEQ, HEAD_PAIR), const),
            pl.BlockSpec((1, HEAD_PAIR), const),
            pl.BlockSpec((1, HEAD_PAIR), const),
            pl.BlockSpec((None, 1, HEAD_PAIR), lambda b, p: (p, 0, 0)),
        ],
        out_specs=pl.BlockSpec((SEQ, HEAD_PAIR), pair),
        out_shape=jax.ShapeDtypeStruct((TOKENS, A_WIDTH), BF16),
        scratch_shapes=[
            pltpu.VMEM((SEQ, HEAD_PAIR), BF16),
            pltpu.VMEM((2, 2 * MOBA_BLOCK, SEQ), F32),
        ],
        compiler_params=pltpu.CompilerParams(
            dimension_semantics=("arbitrary", "arbitrary"), vmem_limit_bytes=VMEM_LIMIT),
        name="moba",
    )(aq, ak, av, cos_t, sin_t, q_gain, k_gain, out_gain)


def _s5_kernel(u_ref, abar_ref, bre_ref, bim_ref, cre_ref, cim_ref, d_ref, gw_ref, gb_ref, gmat_ref, og_ref,
               y_ref, xr_s, xi_s, st_s, u_s, y_s):
    step = pl.program_id(0)
    rows = S5_STEPS * BATCH
    tile = 256
    n_tiles = rows // tile
    halves = S_WIDTH // LANES

    @pl.when(step == 0)
    def _init():
        st_s[...] = jnp.zeros_like(st_s)

    for b in range(BATCH):
        for hv in range(halves):
            lanes = slice(b * S_WIDTH + hv * LANES, b * S_WIDTH + (hv + 1) * LANES)
            u_s[hv, pl.ds(b, S5_STEPS, stride=BATCH), :] = u_ref[:, lanes].astype(F32)

    def u_rows(r):
        return jnp.concatenate([u_s[hv, r * tile:(r + 1) * tile, :] for hv in range(halves)], axis=1)

    for r in range(n_tiles):
        u = u_rows(r).astype(BF16)
        t0 = r * tile // BATCH
        xr_s[t0:t0 + tile // BATCH] = _dot(u, bre_ref[...]).reshape(tile // BATCH, BATCH, S_FLAT)
        xi_s[t0:t0 + tile // BATCH] = _dot(u, bim_ref[...]).reshape(tile // BATCH, BATCH, S_FLAT)

    ar = jnp.broadcast_to(abar_ref[0:1, :], (BATCH, S_FLAT))
    ai = jnp.broadcast_to(abar_ref[1:2, :], (BATCH, S_FLAT))

    def scan_step(t, carry):
        sr, si = carry
        nr = ar * sr - ai * si + xr_s[t]
        ni = ar * si + ai * sr + xi_s[t]
        xr_s[t] = nr
        xi_s[t] = ni
        return nr, ni

    sr, si = lax.fori_loop(0, S5_STEPS, scan_step, (st_s[0], st_s[1]), unroll=4)
    st_s[0] = sr
    st_s[1] = si

    for r in range(n_tiles):
        t0 = r * tile // BATCH
        xr = xr_s[t0:t0 + tile // BATCH].reshape(tile, S_FLAT).astype(BF16)
        xi = xi_s[t0:t0 + tile // BATCH].reshape(tile, S_FLAT).astype(BF16)
        y = _dot(xr, cre_ref[...]) - _dot(xi, cim_ref[...]) + d_ref[...] * u_rows(r)
        z = 0.5 * y * (1.0 + jnp.tanh(math.sqrt(2.0 / math.pi) * (y + 0.044715 * (y * y * y))))
        y2 = z * _sigmoid(_dot(z.astype(BF16), gw_ref[...]) + gb_ref[...])
        ms = _dot((y2 * y2).astype(BF16), gmat_ref[...])
        yn = y2 * lax.rsqrt(ms + EPS) * og_ref[...]
        for hv in range(halves):
            y_s[hv, r * tile:(r + 1) * tile, :] = yn[:, hv * LANES:(hv + 1) * LANES]

    for b in range(BATCH):
        for hv in range(halves):
            lanes = slice(b * S_WIDTH + hv * LANES, b * S_WIDTH + (hv + 1) * LANES)
            y_ref[:, lanes] = y_s[hv, pl.ds(b, S5_STEPS, stride=BATCH), :].astype(y_ref.dtype)


def _s5(u_sb, abar, bre, bim, cre, cim, d_row, glu_w, glu_b, gmat, out_g):
    rows = S5_STEPS * BATCH
    const = lambda s: (0, 0)
    return pl.pallas_call(
        _s5_kernel,
        grid=(SEQ // S5_STEPS,),
        in_specs=[
            pl.BlockSpec((S5_STEPS, BATCH * S_WIDTH), lambda s: (s, 0)),
            pl.BlockSpec((2, S_FLAT), const),
            pl.BlockSpec((S_WIDTH, S_FLAT), const),
            pl.BlockSpec((S_WIDTH, S_FLAT), const),
            pl.BlockSpec((S_FLAT, S_WIDTH), const),
            pl.BlockSpec((S_FLAT, S_WIDTH), const),
            pl.BlockSpec((1, S_WIDTH), const),
            pl.BlockSpec((S_WIDTH, S_WIDTH), const),
            pl.BlockSpec((1, S_WIDTH), const),
            pl.BlockSpec((S_WIDTH, S_WIDTH), const),
            pl.BlockSpec((1, S_WIDTH), const),
        ],
        out_specs=pl.BlockSpec((S5_STEPS, BATCH * S_WIDTH), lambda s: (s, 0)),
        out_shape=jax.ShapeDtypeStruct((SEQ, BATCH * S_WIDTH), BF16),
        scratch_shapes=[
            pltpu.VMEM((S5_STEPS, BATCH, S_FLAT), F32),
            pltpu.VMEM((S5_STEPS, BATCH, S_FLAT), F32),
            pltpu.VMEM((2, BATCH, S_FLAT), F32),
            pltpu.VMEM((S_WIDTH // LANES, rows, LANES), F32),
            pltpu.VMEM((S_WIDTH // LANES, rows, LANES), F32),
        ],
        compiler_params=pltpu.CompilerParams(
            dimension_semantics=("arbitrary",), vmem_limit_bytes=VMEM_LIMIT),
        name="s5",
    )(u_sb, abar, bre, bim, cre, cim, d_row, glu_w, glu_b, gmat, out_g)


def _outproj_kernel(x_ref, ym_ref, ya_ref, ys_ref, w_ref, g_ref, wr_ref, br_ref,
                    x1_ref, xn_ref, route_ref, rows_s):
    x1 = x_ref[...]
    x1 = x1 + _dot(ym_ref[...], w_ref[0:M_WIDTH, :])
    x1 = x1 + _dot(ya_ref[...], w_ref[M_WIDTH:M_WIDTH + A_WIDTH, :])
    x1 = x1 + _dot(ys_ref[...], w_ref[M_WIDTH + A_WIDTH:, :])
    x1_ref[...] = x1
    xn = x1 * lax.rsqrt(jnp.mean(x1 * x1, axis=-1, keepdims=True) + EPS) * g_ref[...]
    for j in range(ROW_TILES):
        rows_s[pl.ds(j, TM_PROJ, stride=ROW_TILES), :] = xn[:, j * LANES:(j + 1) * LANES]
    xn_ref[...] = rows_s[...].astype(xn_ref.dtype)
    logits = _split_dot(xn, wr_ref[...]) + br_ref[...]

    lane = lax.broadcasted_iota(jnp.int32, logits.shape, 1).astype(F32)
    is_grp = lane < N_GROUPS
    neg = -jnp.inf
    gl = jnp.where(is_grp, logits, neg)
    gmax = jnp.max(gl, axis=-1, keepdims=True)
    gsum = jnp.sum(jnp.where(is_grp, jnp.exp(logits - gmax), 0.0), axis=-1, keepdims=True)
    g_sel = jnp.min(jnp.where(gl == gmax, lane, float(LANES)), axis=-1, keepdims=True)
    lo = N_GROUPS + EXPERTS_PER_GROUP * g_sel
    el = jnp.where((lane >= lo) & (lane < lo + EXPERTS_PER_GROUP), logits, neg)
    t1 = jnp.max(el, axis=-1, keepdims=True)
    i1 = jnp.min(jnp.where(el == t1, lane, float(LANES)), axis=-1, keepdims=True)
    el2 = jnp.where(lane == i1, neg, el)
    t2 = jnp.max(el2, axis=-1, keepdims=True)
    i2 = jnp.min(jnp.where(el2 == t2, lane, float(LANES)), axis=-1, keepdims=True)
    e21 = jnp.exp(t2 - t1)
    w1 = 1.0 / ((1.0 + e21) * gsum)
    w2 = e21 / ((1.0 + e21) * gsum)
    route = jnp.where(lane == 0.0, i1 - N_GROUPS,
                      jnp.where(lane == 1.0, i2 - N_GROUPS,
                                jnp.where(lane == 2.0, w1, jnp.where(lane == 3.0, w2, 0.0))))
    route_ref[...] = route


def _outproj(x, y_m, y_a, y_s_sb, w_out, g2, w_router, b_router):
    n_s = SEQ // TM_PROJ
    row = lambda b, s: (b * n_s + s, 0)
    const = lambda b, s: (0, 0)
    return pl.pallas_call(
        _outproj_kernel,
        grid=(BATCH, n_s),
        in_specs=[
            pl.BlockSpec((TM_PROJ, D_MODEL), row),
            pl.BlockSpec((TM_PROJ, M_WIDTH), row),
            pl.BlockSpec((TM_PROJ, A_WIDTH), row),
            pl.BlockSpec((TM_PROJ, S_WIDTH), lambda b, s: (s, b)),
            pl.BlockSpec((D_MIX, D_MODEL), const),
            pl.BlockSpec((1, D_MODEL), const),
            pl.BlockSpec((D_MODEL, 2 * LANES), const),
            pl.BlockSpec((1, LANES), const),
        ],
        out_specs=(
            pl.BlockSpec((TM_PROJ, D_MODEL), row),
            pl.BlockSpec((TM_PROJ * ROW_TILES, LANES), row),
            pl.BlockSpec((TM_PROJ, LANES), row),
        ),
        out_shape=(
            jax.ShapeDtypeStruct((TOKENS, D_MODEL), F32),
            jax.ShapeDtypeStruct((TOKENS * ROW_TILES, LANES), BF16),
            jax.ShapeDtypeStruct((TOKENS, LANES), F32),
        ),
        scratch_shapes=[pltpu.VMEM((TM_PROJ * ROW_TILES, LANES), F32)],
        compiler_params=pltpu.CompilerParams(
            dimension_semantics=("arbitrary", "arbitrary"), vmem_limit_bytes=VMEM_LIMIT),
        name="outproj",
    )(x, y_m, y_a, y_s_sb, w_out, g2, w_router, b_router)


def _ffn_kernel(be_ref, src_ref, xs_ref, wg_ref, wu_ref, wd_ref, ys_ref, rows_s):
    del be_ref
    i = pl.program_id(0)

    @pl.when(src_ref[i] == i)
    def _expert_block():
        rows_s[...] = xs_ref[...].astype(F32)
        xb = jnp.concatenate([rows_s[pl.ds(j, FFN_BLOCK, stride=ROW_TILES), :] for j in range(ROW_TILES)],
                             axis=1).astype(BF16)
        g = _dot(xb, wg_ref[...])
        u = _dot(xb, wu_ref[...])
        hmid = (g * _sigmoid(g) * u).astype(BF16)
        y = _dot(hmid, wd_ref[...])
        for j in range(ROW_TILES):
            ys_ref[pl.ds(j, FFN_BLOCK, stride=ROW_TILES), :] = y[:, j * LANES:(j + 1) * LANES]

    @pl.when(src_ref[i] != i)
    def _unused_tail_block():
        ys_ref[...] = jnp.zeros_like(ys_ref)


def _expert_ffn(layer, blk_exp, blk_src, xs, w_gate, w_up, w_down):
    nblk = blk_exp.shape[0]
    block = (FFN_BLOCK * ROW_TILES, LANES)
    grid_spec = pltpu.PrefetchScalarGridSpec(
        num_scalar_prefetch=2,
        grid=(nblk,),
        in_specs=[
            pl.BlockSpec(block, lambda i, be, src: (src[i], 0)),
            pl.BlockSpec((None, None, D_MODEL, D_EXPERT), lambda i, be, src: (layer, be[i], 0, 0)),
            pl.BlockSpec((None, None, D_MODEL, D_EXPERT), lambda i, be, src: (layer, be[i], 0, 0)),
            pl.BlockSpec((None, None, D_EXPERT, D_MODEL), lambda i, be, src: (layer, be[i], 0, 0)),
        ],
        out_specs=pl.BlockSpec(block, lambda i, be, src: (i, 0)),
        scratch_shapes=[pltpu.VMEM(block, F32)],
    )
    return pl.pallas_call(
        _ffn_kernel,
        grid_spec=grid_spec,
        out_shape=jax.ShapeDtypeStruct((nblk * FFN_BLOCK * ROW_TILES, LANES), F32),
        compiler_params=pltpu.CompilerParams(
            dimension_semantics=("arbitrary",), vmem_limit_bytes=VMEM_LIMIT),
        name="expert_ffn",
    )(blk_exp, blk_src, xs, w_gate, w_up, w_down)


def _plan_kernel(e_ref, ps_ref, dest_ref, run_s, tri_s):
    i = pl.program_id(0)

    @pl.when(i == 0)
    def _init():
        run_s[...] = ps_ref[...]
        s_i = lax.broadcasted_iota(jnp.int32, (PLAN_TILE, PLAN_TILE), 0)
        t_i = lax.broadcasted_iota(jnp.int32, (PLAN_TILE, PLAN_TILE), 1)
        tri_s[...] = (s_i < t_i).astype(BF16)

    expert = lax.broadcasted_iota(jnp.int32, (N_EXPERTS, PLAN_TILE), 0)
    onehot = jnp.where(e_ref[...] == expert, 1.0, 0.0)
    before = _dot(onehot.astype(BF16), tri_s[...])
    dest = jnp.sum(onehot * (before + run_s[:, 0:1]), axis=0, keepdims=True)
    dest_ref[...] = dest.astype(jnp.int32)
    run_s[...] = run_s[...] + jnp.sum(onehot, axis=1, keepdims=True)


def _plan(e_rows, pstart_col):
    n_tiles = e_rows.shape[0]
    return pl.pallas_call(
        _plan_kernel,
        grid=(n_tiles,),
        in_specs=[
            pl.BlockSpec((None, 1, PLAN_TILE), lambda i: (i, 0, 0)),
            pl.BlockSpec((N_EXPERTS, LANES), lambda i: (0, 0)),
        ],
        out_specs=pl.BlockSpec((None, 1, PLAN_TILE), lambda i: (i, 0, 0)),
        out_shape=jax.ShapeDtypeStruct((n_tiles, 1, PLAN_TILE), jnp.int32),
        scratch_shapes=[pltpu.VMEM((N_EXPERTS, LANES), F32), pltpu.VMEM((PLAN_TILE, PLAN_TILE), BF16)],
        compiler_params=pltpu.CompilerParams(
            dimension_semantics=("arbitrary",), vmem_limit_bytes=VMEM_LIMIT),
        name="moe_plan",
    )(e_rows, pstart_col)


def _dispatch_kernel(fill_ref, d0_ref, d1_ref, xn_ref, xs_hbm, zero_s, sem, fill_sem):
    @pl.when(pl.program_id(0) == 0)
    def _zero_padding():
        zero_s[...] = jnp.zeros_like(zero_s)

        def fill(c):
            start = pl.multiple_of(fill_ref[0, c] * ROW_TILES, ROW_TILES)
            return pltpu.make_async_copy(zero_s, xs_hbm.at[pl.ds(start, FFN_BLOCK * ROW_TILES), :], fill_sem)

        for c in range(N_FILLS):
            pl.when(fill_ref[0, c] >= 0)(lambda c=c: fill(c).start())
        for c in range(N_FILLS):
            pl.when(fill_ref[0, c] >= 0)(lambda c=c: fill(c).wait())

    def row_copy(dref, t):
        src = xn_ref.at[pl.ds(pl.multiple_of(t * ROW_TILES, ROW_TILES), ROW_TILES), :]
        dst = xs_hbm.at[pl.ds(pl.multiple_of(dref[0, t] * ROW_TILES, ROW_TILES), ROW_TILES), :]
        return pltpu.make_async_copy(src, dst, sem)

    def start(t, carry):
        row_copy(d0_ref, t).start()
        row_copy(d1_ref, t).start(priority=1)
        return carry

    lax.fori_loop(0, DISPATCH_TOKENS, start, 0, unroll=8)
    for _ in range(EXPERT_TOPK):
        pltpu.make_async_copy(xn_ref, xs_hbm.at[pl.ds(0, DISPATCH_TOKENS * ROW_TILES), :], sem).wait()


def _dispatch(fill_rows, dest, xn, n_rows):
    n_tiles = TOKENS // DISPATCH_TOKENS
    dest4 = dest.reshape(EXPERT_TOPK, n_tiles, 1, DISPATCH_TOKENS)
    smem_tile = lambda k: pl.BlockSpec((None, None, 1, DISPATCH_TOKENS), lambda i: (k, i, 0, 0),
                                       memory_space=pltpu.SMEM)
    return pl.pallas_call(
        _dispatch_kernel,
        grid=(n_tiles,),
        in_specs=[
            pl.BlockSpec(memory_space=pltpu.SMEM),
            smem_tile(0),
            smem_tile(1),
            pl.BlockSpec((DISPATCH_TOKENS * ROW_TILES, LANES), lambda i: (i, 0)),
        ],
        out_specs=pl.BlockSpec(memory_space=pl.ANY),
        out_shape=jax.ShapeDtypeStruct((n_rows * ROW_TILES, LANES), BF16),
        scratch_shapes=[
            pltpu.VMEM((FFN_BLOCK * ROW_TILES, LANES), BF16),
            pltpu.SemaphoreType.DMA(()),
            pltpu.SemaphoreType.DMA(()),
        ],
        compiler_params=pltpu.CompilerParams(
            dimension_semantics=("arbitrary",), vmem_limit_bytes=VMEM_LIMIT),
        name="moe_dispatch",
    )(fill_rows, dest4, dest4, xn)


def _combine_kernel(d0_ref, d1_ref, d0_next_ref, d1_next_ref, x1_ref, route_ref, ys_hbm, out_ref, buf, sem):
    i = pl.program_id(0)
    n = pl.num_programs(0)

    def start_tile(drefs, slot):
        def body(t, carry):
            for k in range(EXPERT_TOPK):
                src = ys_hbm.at[pl.ds(pl.multiple_of(drefs[k][0, t] * ROW_TILES, ROW_TILES), ROW_TILES), :]
                dst = buf.at[slot, k, pl.ds(pl.multiple_of(t * ROW_TILES, ROW_TILES), ROW_TILES), :]
                pltpu.make_async_copy(src, dst, sem.at[slot]).start(priority=k)
            return carry
        lax.fori_loop(0, COMBINE_TOKENS, body, 0, unroll=8)

    slot = i % 2

    @pl.when(i == 0)
    def _first():
        start_tile((d0_ref, d1_ref), 0)

    @pl.when(i + 1 < n)
    def _prefetch():
        start_tile((d0_next_ref, d1_next_ref), 1 - slot)

    for k in range(EXPERT_TOPK):
        pltpu.make_async_copy(ys_hbm.at[pl.ds(0, COMBINE_TOKENS * ROW_TILES), :], buf.at[slot, k],
                              sem.at[slot]).wait()
    w = route_ref[...]
    w1 = w[:, 2:3]
    w2 = w[:, 3:4]
    for j in range(ROW_TILES):
        cols = slice(j * LANES, (j + 1) * LANES)
        chunk = pl.ds(j, COMBINE_TOKENS, stride=ROW_TILES)
        out_ref[:, cols] = x1_ref[:, cols] + w1 * buf[slot, 0, chunk, :] + w2 * buf[slot, 1, chunk, :]


def _combine(dest, x1, route, ys3):
    n_tiles = TOKENS // COMBINE_TOKENS
    dest4 = dest.reshape(EXPERT_TOPK, n_tiles, 1, COMBINE_TOKENS)
    smem_tile = lambda k, nxt: pl.BlockSpec(
        (None, None, 1, COMBINE_TOKENS), lambda i: (k, jnp.minimum(i + nxt, n_tiles - 1), 0, 0),
        memory_space=pltpu.SMEM)
    return pl.pallas_call(
        _combine_kernel,
        grid=(n_tiles,),
        in_specs=[
            smem_tile(0, 0), smem_tile(1, 0), smem_tile(0, 1), smem_tile(1, 1),
            pl.BlockSpec((COMBINE_TOKENS, D_MODEL), lambda i: (i, 0)),
            pl.BlockSpec((COMBINE_TOKENS, LANES), lambda i: (i, 0)),
            pl.BlockSpec(memory_space=pl.ANY),
        ],
        out_specs=pl.BlockSpec((COMBINE_TOKENS, D_MODEL), lambda i: (i, 0)),
        out_shape=jax.ShapeDtypeStruct((TOKENS, D_MODEL), F32),
        scratch_shapes=[
            pltpu.VMEM((2, EXPERT_TOPK, COMBINE_TOKENS * ROW_TILES, LANES), F32),
            pltpu.SemaphoreType.DMA((2,)),
        ],
        compiler_params=pltpu.CompilerParams(
            dimension_semantics=("arbitrary",), vmem_limit_bytes=VMEM_LIMIT),
        name="moe_combine",
    )(dest4, dest4, dest4, dest4, x1, route, ys3)


def _rope_tables():
    inv = ROPE_THETA ** (-np.arange(ROPE_HALF, dtype=np.float64) * 2.0 / ROPE_DIM)
    ang = np.arange(SEQ, dtype=np.float64)[:, None] * inv[None, :]
    cos = np.ones((SEQ, A_HEAD_DIM))
    sin = np.zeros((SEQ, A_HEAD_DIM))
    cos[:, :ROPE_HALF] = np.cos(ang)
    cos[:, ROPE_HALF:ROPE_DIM] = np.cos(ang)
    sin[:, :ROPE_HALF] = -np.sin(ang)
    sin[:, ROPE_HALF:ROPE_DIM] = np.sin(ang)
    return (jnp.asarray(np.tile(cos, (1, 2)), F32), jnp.asarray(np.tile(sin, (1, 2)), F32))


def _split_w_in(w_in):
    sizes = [M_WIDTH, M_WIDTH, M_WIDTH, M_WIDTH, M_HEADS, M_HEADS, A_WIDTH, A_WIDTH, A_WIDTH, S_WIDTH]
    offs = np.cumsum([0] + sizes)
    mq, mk, mv, mo, mi, mf, aq, ak, av, su = (w_in[:, offs[n]:offs[n + 1]] for n in range(len(sizes)))
    w_main = jnp.concatenate([aq, ak, av, mq, mk, mv, mo, su], axis=1).astype(BF16)
    gate = jnp.concatenate([mi, mf], axis=1)
    return w_main, _hi_lo(jnp.pad(gate, ((0, 0), (0, LANES - 2 * M_HEADS))))


def _s5_params(a_re, a_im, b_re, b_im, c_re, c_im, log_dt):
    dt = jnp.exp(log_dt)[:, None]
    mag = jnp.exp(a_re * dt)
    abar_re, abar_im = mag * jnp.cos(a_im * dt), mag * jnp.sin(a_im * dt)
    zr, zi = abar_re - 1.0, abar_im
    den = a_re * a_re + a_im * a_im
    fr, fi = (zr * a_re + zi * a_im) / den, (zi * a_re - zr * a_im) / den
    bbar_re = fr[..., None] * b_re - fi[..., None] * b_im
    bbar_im = fr[..., None] * b_im + fi[..., None] * b_re
    eye = jnp.eye(S_GROUPS, dtype=F32)
    dense_b = lambda t: jnp.einsum('gph,gk->ghkp', t, eye).reshape(S_WIDTH, S_FLAT).astype(BF16)
    dense_c = lambda t: jnp.einsum('ghp,gk->gpkh', t, eye).reshape(S_FLAT, S_WIDTH).astype(BF16)
    abar = jnp.stack([abar_re.reshape(S_FLAT), abar_im.reshape(S_FLAT)])
    return abar, dense_b(bbar_re), dense_b(bbar_im), dense_c(c_re), dense_c(c_im)


def _moe(layer, x1, xn, route, w_gate, w_up, w_down):
    n_assign = TOKENS * EXPERT_TOPK
    e_flat = route[:, :EXPERT_TOPK].astype(jnp.int32).T.reshape(n_assign)
    experts = jnp.arange(N_EXPERTS, dtype=jnp.int32)
    counts = jnp.sum((e_flat[:, None] == experts[None, :]).astype(jnp.int32), axis=0)
    padded = (counts + FFN_BLOCK - 1) // FFN_BLOCK * FFN_BLOCK
    pends = jnp.cumsum(padded)
    pstarts = pends - padded
    n_rows = (n_assign + N_EXPERTS * (FFN_BLOCK - 1) + FFN_BLOCK - 1) // FFN_BLOCK * FFN_BLOCK
    nblk = n_rows // FFN_BLOCK
    blk_start = jnp.arange(nblk, dtype=jnp.int32) * FFN_BLOCK
    blk_exp = jnp.minimum(jnp.sum((pends[None, :] <= blk_start[:, None]).astype(jnp.int32), axis=1),
                          N_EXPERTS - 1).astype(jnp.int32)
    pstart_col = jnp.broadcast_to(pstarts.astype(F32)[:, None], (N_EXPERTS, LANES))
    dest = _plan(e_flat.reshape(n_assign // PLAN_TILE, 1, PLAN_TILE), pstart_col)
    tail = pends[-1] + jnp.arange(N_EXPERTS, dtype=jnp.int32) * FFN_BLOCK
    fill_rows = jnp.concatenate([jnp.where(padded > 0, pends - FFN_BLOCK, -1), jnp.where(tail < n_rows, tail, -1)])
    fill_rows = fill_rows.astype(jnp.int32).reshape(1, N_FILLS)
    xs = _dispatch(fill_rows, dest, xn, n_rows)
    blk_src = jnp.minimum(jnp.arange(nblk, dtype=jnp.int32), pends[-1] // FFN_BLOCK - 1).astype(jnp.int32)
    ys = _expert_ffn(layer, blk_exp, blk_src, xs, w_gate, w_up, w_down)
    return _combine(dest, x1, route, ys)


def kernel(x, norm1_g, w_in, m_bias_i, m_bias_f, m_conv_w, m_conv_b, m_out_g, a_q_g, a_k_g, a_out_g, s_a_re, s_a_im, s_b_re, s_b_im, s_c_re, s_c_im, s_d, s_log_dt, s_glu_w, s_glu_b, s_out_g, w_out, norm2_g, r_group_w, r_group_b, r_expert_w, r_expert_b, e_w_gate, e_w_up, e_w_down):
    cos_t, sin_t = _rope_tables()
    gmat = jnp.asarray(np.kron(np.eye(S_GROUPS), np.full((S_GROUP_DIM, S_GROUP_DIM), 1.0 / S_GROUP_DIM)), BF16)
    xf = x.reshape(TOKENS, D_MODEL)
    for l in range(DEPTH):
        w_main, w_gate = _split_w_in(w_in[l])
        aq, ak, av, mqk, mv, mo, su, gates, gatest = _inproj(xf, norm1_g[l].reshape(1, D_MODEL), w_main, w_gate)

        bias = jnp.concatenate([m_bias_i[l], m_bias_f[l]])
        bias_row = jnp.pad(bias, (0, LANES - 2 * M_HEADS)).reshape(1, LANES)
        y_m = _mlstm(mqk, mv, mo, gates, gatest, m_conv_w[l], m_conv_b[l].reshape(1, 2 * M_WIDTH),
                     bias_row, bias.reshape(2 * M_HEADS, 1), m_out_g[l].reshape(1, M_WIDTH))

        y_a = _moba(aq, ak, av, cos_t, sin_t,
                    jnp.tile(a_q_g[l], 2).reshape(1, HEAD_PAIR), jnp.tile(a_k_g[l], 2).reshape(1, HEAD_PAIR),
                    a_out_g[l].reshape(A_HEADS // 2, 1, HEAD_PAIR))

        abar, bre, bim, cre, cim = _s5_params(s_a_re[l], s_a_im[l], s_b_re[l], s_b_im[l],
                                              s_c_re[l], s_c_im[l], s_log_dt[l])
        y_s = _s5(su, abar, bre, bim, cre, cim,
                  s_d[l].reshape(1, S_WIDTH), s_glu_w[l].astype(BF16), s_glu_b[l].reshape(1, S_WIDTH),
                  gmat, s_out_g[l].reshape(1, S_WIDTH))

        w_router = jnp.pad(jnp.concatenate([r_group_w[l], r_expert_w[l]], axis=1),
                           ((0, 0), (0, LANES - N_GROUPS - N_EXPERTS)))
        b_router = jnp.pad(jnp.concatenate([r_group_b[l], r_expert_b[l]]),
                           (0, LANES - N_GROUPS - N_EXPERTS)).reshape(1, LANES)
        x1, xn, route = _outproj(xf, y_m, y_a, y_s, w_out[l].astype(BF16),
                                 norm2_g[l].reshape(1, D_MODEL), _hi_lo(w_router), b_router)
        xf = _moe(l, x1, xn, route, e_w_gate, e_w_up, e_w_down)
    return xf.reshape(BATCH, SEQ, D_MODEL)
```

```python
import functools
import math

import numpy as np
import jax
import jax.numpy as jnp
from jax import lax
from jax.experimental import pallas as pl
from jax.experimental.pallas import tpu as pltpu

F32 = jnp.float32
BF16 = jnp.bfloat16

D_MODEL = 1024
BATCH = 8
SEQ = 2048
DEPTH = 2
TOKENS = BATCH * SEQ

M_HEADS = 4
M_HEAD_DIM = 64
M_WIDTH = M_HEADS * M_HEAD_DIM
A_HEADS = 8
A_HEAD_DIM = 64
A_WIDTH = A_HEADS * A_HEAD_DIM
S_GROUPS = 16
S_GROUP_DIM = 16
S_WIDTH = S_GROUPS * S_GROUP_DIM
S_STATE = 64
S_FLAT = S_GROUPS * S_STATE
D_MIX = M_WIDTH + A_WIDTH + S_WIDTH

CONV_K = 4
MOBA_BLOCK = 256
MOBA_NB = SEQ // MOBA_BLOCK
MOBA_TOPK = 3
ROPE_THETA = 500000.0
ROPE_DIM = A_HEAD_DIM // 4
ROPE_HALF = ROPE_DIM // 2

N_GROUPS = 4
EXPERTS_PER_GROUP = 8
N_EXPERTS = N_GROUPS * EXPERTS_PER_GROUP
EXPERT_TOPK = 2
D_EXPERT = 512
EPS = 1e-6

LANES = 128
ROW_TILES = D_MODEL // LANES
HEAD_PAIR = 2 * A_HEAD_DIM

W_MAIN = 3 * A_WIDTH + 4 * M_WIDTH + S_WIDTH
TM_PROJ = 1024
M_CHUNK = 256
M_NCHUNK = SEQ // M_CHUNK
S5_STEPS = 256
FFN_BLOCK = 512
N_FILLS = 2 * N_EXPERTS
PLAN_TILE = 1024
DISPATCH_TOKENS = 2048
COMBINE_TOKENS = 256
VMEM_LIMIT = 56 * 1024 * 1024

HIGHEST = lax.Precision.HIGHEST


def _dot(a, b, precision=None):
    return jnp.dot(a, b, preferred_element_type=F32, precision=precision)


def _dot_nt(a, b, precision=None):
    return lax.dot_general(a, b, (((1,), (1,)), ((), ())), preferred_element_type=F32, precision=precision)


def _dot_tn(a, b, precision=None):
    return lax.dot_general(a, b, (((0,), (0,)), ((), ())), preferred_element_type=F32, precision=precision)


def _split_dot(a, w_cat):
    n = w_cat.shape[1] // 2
    a_hi = a.astype(BF16)
    a_lo = (a - a_hi.astype(F32)).astype(BF16)
    both = _dot(a_hi, w_cat)
    return both[:, :n] + both[:, n:] + _dot(a_lo, w_cat[:, :n])


def _bf16_terms(x):
    hi = x.astype(BF16)
    rest = x - hi.astype(F32)
    mid = rest.astype(BF16)
    return hi, mid, (rest - mid.astype(F32)).astype(BF16)


def _hi_lo(w):
    w_hi = w.astype(BF16)
    return jnp.concatenate([w_hi, (w - w_hi.astype(F32)).astype(BF16)], axis=1)


def _log_sigmoid(x):
    return jnp.minimum(x, 0.0) - jnp.log1p(jnp.exp(-jnp.abs(x)))


def _sigmoid(x):
    return 1.0 / (1.0 + jnp.exp(-x))


def _inproj_kernel(x_ref, g_ref, w_ref, wg_ref,
                   aq_ref, ak_ref, av_ref, mqk_ref, mv_ref, mo_ref, su_ref, gates_ref, gatest_ref):
    x = x_ref[...]
    h = x * lax.rsqrt(jnp.mean(x * x, axis=-1, keepdims=True) + EPS) * g_ref[...]
    hb = h.astype(BF16)
    off = 0
    for ref, width in ((aq_ref, A_WIDTH), (ak_ref, A_WIDTH), (av_ref, A_WIDTH), (mqk_ref, 2 * M_WIDTH),
                       (mv_ref, M_WIDTH), (mo_ref, M_WIDTH), (su_ref, S_WIDTH)):
        ref[...] = _dot(hb, w_ref[:, off:off + width]).astype(ref.dtype)
        off += width
    gates = _split_dot(h, wg_ref[...])
    gates_ref[...] = gates
    gt = gates.T[:8, :]
    for c in range(TM_PROJ // M_CHUNK):
        gatest_ref[c] = gt[:, c * M_CHUNK:(c + 1) * M_CHUNK]


def _inproj(x, g, w_main, w_gate):
    n_s = SEQ // TM_PROJ
    row = lambda b, s: (b * n_s + s, 0)
    const = lambda b, s: (0, 0)
    out_shapes = (
        jax.ShapeDtypeStruct((TOKENS, A_WIDTH), BF16),
        jax.ShapeDtypeStruct((TOKENS, A_WIDTH), BF16),
        jax.ShapeDtypeStruct((TOKENS, A_WIDTH), BF16),
        jax.ShapeDtypeStruct((TOKENS, 2 * M_WIDTH), BF16),
        jax.ShapeDtypeStruct((TOKENS, M_WIDTH), BF16),
        jax.ShapeDtypeStruct((TOKENS, M_WIDTH), BF16),
        jax.ShapeDtypeStruct((SEQ, BATCH * S_WIDTH), BF16),
        jax.ShapeDtypeStruct((TOKENS, LANES), F32),
        jax.ShapeDtypeStruct((TOKENS // M_CHUNK, 8, M_CHUNK), F32),
    )
    out_specs = (
        pl.BlockSpec((TM_PROJ, A_WIDTH), row),
        pl.BlockSpec((TM_PROJ, A_WIDTH), row),
        pl.BlockSpec((TM_PROJ, A_WIDTH), row),
        pl.BlockSpec((TM_PROJ, 2 * M_WIDTH), row),
        pl.BlockSpec((TM_PROJ, M_WIDTH), row),
        pl.BlockSpec((TM_PROJ, M_WIDTH), row),
        pl.BlockSpec((TM_PROJ, S_WIDTH), lambda b, s: (s, b)),
        pl.BlockSpec((TM_PROJ, LANES), row),
        pl.BlockSpec((TM_PROJ // M_CHUNK, 8, M_CHUNK), lambda b, s: (b * n_s + s, 0, 0)),
    )
    return pl.pallas_call(
        _inproj_kernel,
        grid=(BATCH, n_s),
        in_specs=[
            pl.BlockSpec((TM_PROJ, D_MODEL), row),
            pl.BlockSpec((1, D_MODEL), const),
            pl.BlockSpec((D_MODEL, W_MAIN), const),
            pl.BlockSpec((D_MODEL, 2 * LANES), const),
        ],
        out_specs=out_specs,
        out_shape=out_shapes,
        compiler_params=pltpu.CompilerParams(
            dimension_semantics=("arbitrary", "arbitrary"), vmem_limit_bytes=VMEM_LIMIT),
        name="inproj",
    )(x, g, w_main, w_gate)


def _mlstm_kernel(mqk_ref, mv_ref, mo_ref, gates_ref, gatest_ref, cw_ref, cb_ref, brow_ref, bcol_ref, og_ref,
                  y_ref, qk_s, cols_s, wrow_s):
    L = M_CHUNK
    dh = M_HEAD_DIM
    r_i = lax.broadcasted_iota(jnp.int32, (L, L), 0)
    c_i = lax.broadcasted_iota(jnp.int32, (L, L), 1)
    causal = c_i <= r_i
    tri_u = (r_i <= c_i).astype(BF16)
    eye = (lax.broadcasted_iota(jnp.int32, (8, LANES), 0)
           == lax.broadcasted_iota(jnp.int32, (8, LANES), 1)).astype(BF16)
    w = cw_ref[...]
    cb = cb_ref[...]
    lane_q = lax.broadcasted_iota(jnp.int32, (1, 2 * M_WIDTH), 1) < M_WIDTH
    kscale = jnp.where(lane_q, 1.0, dh ** -0.5).astype(F32)
    for c in range(M_NCHUNK):
        halo = 16 if c > 0 else 0
        for strip in range(2 * M_WIDTH // LANES):
            cols = slice(strip * LANES, (strip + 1) * LANES)
            xt = mqk_ref[c * L - halo:(c + 1) * L, cols].astype(F32)
            rows = lax.broadcasted_iota(jnp.int32, xt.shape, 0)
            y = cb[:, cols] + w[CONV_K - 1:CONV_K, cols] * xt
            for k in range(1, CONV_K):
                sh = pltpu.roll(xt, k, 0)
                if c == 0:
                    sh = jnp.where(rows >= k, sh, 0.0)
                y = y + w[CONV_K - 1 - k:CONV_K - k, cols] * sh
            y = y[halo:, :]
            y = y * _sigmoid(y) * kscale[:, cols]
            qk_s[c * L:(c + 1) * L, cols] = y.astype(BF16)

    rows_all = M_NCHUNK * 8
    grow = gatest_ref[...].reshape(rows_all, L) + jnp.tile(bcol_ref[...], (M_NCHUNK, 1))
    cs_row = sum(_dot(term, tri_u) for term in _bf16_terms(_log_sigmoid(grow)))
    w_row = grow - pltpu.roll(cs_row, rows_all - M_HEADS, 0)
    lane_t = lax.broadcasted_iota(jnp.int32, (rows_all, L), 1)
    pm = w_row
    shift = 1
    while shift < L:
        pm = jnp.maximum(pm, jnp.where(lane_t >= shift, pltpu.roll(pm, shift, 1), -jnp.inf))
        shift *= 2
    wrow_s[...] = w_row.reshape(M_NCHUNK, 8, L)
    is_i_row = lax.broadcasted_iota(jnp.int32, (rows_all, L), 0) % 8 < M_HEADS
    terms = _bf16_terms(jnp.where(is_i_row, pm, cs_row))
    for c in range(M_NCHUNK):
        cols_s[c * L:(c + 1) * L, :] = sum(_dot_tn(term[c * 8:(c + 1) * 8, :], eye) for term in terms)

    pad_lane = lax.broadcasted_iota(jnp.int32, (L, LANES - dh), 1)
    ones_pad = jnp.where(pad_lane == 0, 1.0, 0.0).astype(BF16)
    og = og_ref[...]
    head_mean = jnp.where(lax.broadcasted_iota(jnp.int32, (M_WIDTH, M_WIDTH), 0) // dh
                          == lax.broadcasted_iota(jnp.int32, (M_WIDTH, M_WIDTH), 1) // dh,
                          1.0 / dh, 0.0).astype(BF16)

    def chunk(c, carry):
        r0 = pl.multiple_of(c * L, L)
        gcol = gates_ref[pl.ds(r0, L), :] + brow_ref[...]
        cs_col = pm_col = cols_s[pl.ds(r0, L), :]
        w_row = wrow_s[c]
        qk = qk_s[pl.ds(r0, L), :]
        vv = mv_ref[pl.ds(r0, L), :]
        hs = [None] * M_HEADS
        new_carry = [None] * M_HEADS

        def head(h):
            cst, m = carry[h]
            q = qk[:, h * dh:(h + 1) * dh]
            k = qk[:, M_WIDTH + h * dh:M_WIDTH + (h + 1) * dh]
            v_aug = jnp.concatenate([vv[:, h * dh:(h + 1) * dh], ones_pad], axis=1)
            bc = cs_col[:, M_HEADS + h:M_HEADS + h + 1]
            ic = gcol[:, h:h + 1]
            m_row = jnp.maximum(m, pm_col[:, h:h + 1])
            mt = bc + m_row
            inter = jnp.exp(m - m_row)
            yield
            sqk = _dot_nt(q, k) * jnp.exp(jnp.where(causal, w_row[h:h + 1, :] - m_row, -jnp.inf))
            yield
            nd = inter * _dot(q, cst.astype(BF16)) + _dot(sqk.astype(BF16), v_aug)
            yield
            num = nd[:, :dh]
            den = nd[:, dh:dh + 1]
            hs[h] = num / jnp.maximum(jnp.abs(den), jnp.exp(-mt))
            bl = bc[L - 1:L, :]
            gs = bl - bc + ic
            m_new = jnp.maximum(bl + m, jnp.max(gs, axis=0, keepdims=True))
            decay = jnp.exp(bl + m - m_new)
            wts = jnp.exp(gs - m_new)
            yield
            kw = (k.astype(F32) * wts).astype(BF16)
            new_carry[h] = (decay * cst + _dot_tn(kw, v_aug), m_new)

        stages = [head(h) for h in range(M_HEADS)]
        while stages:
            for stage in list(stages):
                try:
                    next(stage)
                except StopIteration:
                    stages.remove(stage)
        hcat = jnp.concatenate(hs, axis=1)
        hcat = hcat * lax.rsqrt(_dot((hcat * hcat).astype(BF16), head_mean) + EPS) * og
        o = mo_ref[pl.ds(r0, L), :].astype(F32)
        y_ref[pl.ds(r0, L), :] = (hcat * _sigmoid(o)).astype(y_ref.dtype)
        return tuple(new_carry)

    init = tuple((jnp.zeros((dh, LANES), F32), jnp.zeros((1, 1), F32)) for _ in range(M_HEADS))
    lax.fori_loop(0, M_NCHUNK, chunk, init, unroll=2)


def _mlstm(mqk, mv, mo, gates, gatest, conv_w, conv_b, bias_row, bias_col, out_g):
    per_b = lambda b: (b, 0)
    const = lambda b: (0, 0)
    return pl.pallas_call(
        _mlstm_kernel,
        grid=(BATCH,),
        in_specs=[
            pl.BlockSpec((SEQ, 2 * M_WIDTH), per_b),
            pl.BlockSpec((SEQ, M_WIDTH), per_b),
            pl.BlockSpec((SEQ, M_WIDTH), per_b),
            pl.BlockSpec((SEQ, LANES), per_b),
            pl.BlockSpec((M_NCHUNK, 8, M_CHUNK), lambda b: (b, 0, 0)),
            pl.BlockSpec((CONV_K, 2 * M_WIDTH), const),
            pl.BlockSpec((1, 2 * M_WIDTH), const),
            pl.BlockSpec((1, LANES), const),
            pl.BlockSpec((8, 1), const),
            pl.BlockSpec((1, M_WIDTH), const),
        ],
        out_specs=pl.BlockSpec((SEQ, M_WIDTH), per_b),
        out_shape=jax.ShapeDtypeStruct((TOKENS, M_WIDTH), BF16),
        scratch_shapes=[
            pltpu.VMEM((SEQ, 2 * M_WIDTH), BF16),
            pltpu.VMEM((SEQ, LANES), F32),
            pltpu.VMEM((M_NCHUNK, 8, M_CHUNK), F32),
        ],
        compiler_params=pltpu.CompilerParams(
            dimension_semantics=("arbitrary",), vmem_limit_bytes=VMEM_LIMIT),
        name="mlstm",
    )(mqk, mv, mo, gates, gatest, conv_w, conv_b, bias_row, bias_col, out_g)


def _qk_norm_rope(x, gain, cos, sin):
    lane = lax.broadcasted_iota(jnp.int32, x.shape, 1)
    first = lane < A_HEAD_DIM
    sq = x * x
    tot = jnp.sum(sq, axis=-1, keepdims=True)
    s0 = jnp.sum(jnp.where(first, sq, 0.0), axis=-1, keepdims=True)
    ms = jnp.where(first, s0, tot - s0) * (1.0 / A_HEAD_DIM)
    xn = x * lax.rsqrt(ms + EPS) * gain
    in_low = (lane % A_HEAD_DIM) < ROPE_HALF
    swapped = jnp.where(in_low, pltpu.roll(xn, LANES - ROPE_HALF, 1), pltpu.roll(xn, ROPE_HALF, 1))
    return xn * cos + swapped * sin


def _moba_kernel(aq_ref, ak_ref, av_ref, cos_ref, sin_ref, qg_ref, kg_ref, og_ref, y_ref, kr_s, s_s):
    blk = MOBA_BLOCK
    dh = A_HEAD_DIM
    km_first = lax.broadcasted_iota(jnp.int32, (MOBA_NB, HEAD_PAIR), 1) < dh
    first = lax.broadcasted_iota(jnp.int32, (blk, HEAD_PAIR), 1) < dh
    blk_id = lax.broadcasted_iota(jnp.int32, (MOBA_NB, blk), 0)
    eye = (lax.broadcasted_iota(jnp.int32, (MOBA_NB, LANES), 0)
           == lax.broadcasted_iota(jnp.int32, (MOBA_NB, LANES), 1)).astype(F32)
    q_pos = lax.broadcasted_iota(jnp.int32, (2 * blk, blk), 0) % blk
    k_pos = lax.broadcasted_iota(jnp.int32, (2 * blk, blk), 1)
    causal = k_pos <= q_pos
    og = og_ref[...]

    km_rows = []
    prepped = {}

    def key_prep(j):
        rows = slice(j * blk, (j + 1) * blk)
        kr = _qk_norm_rope(ak_ref[rows, :].astype(F32), kg_ref[...], cos_ref[rows, :], sin_ref[rows, :])
        kr_s[rows, :] = kr.astype(BF16)
        km_rows.append(jnp.mean(kr, axis=0, keepdims=True))
        yield

    def query_prep(i):
        rows = slice(i * blk, (i + 1) * blk)
        qr = _qk_norm_rope(aq_ref[rows, :].astype(F32), qg_ref[...], cos_ref[rows, :], sin_ref[rows, :])
        qs = qr * (dh ** -0.5 * math.log2(math.e))
        q2 = jnp.concatenate([jnp.where(first, qs, 0.0), jnp.where(first, 0.0, qs)], axis=0).astype(BF16)
        yield

        bias = None
        if i > MOBA_TOPK:
            km = jnp.concatenate(km_rows[:i] + [jnp.zeros((MOBA_NB - i, HEAD_PAIR), F32)], axis=0)
            km_heads = (jnp.where(km_first, km, 0.0), jnp.where(km_first, 0.0, km))
            past = blk_id < i
            cols = []
            for kmh in km_heads:
                gate = jnp.where(past, _dot_nt(kmh, qr, precision=HIGHEST), -jnp.inf)
                rank = jnp.zeros((MOBA_NB, blk), F32)
                for m_blk in range(i):
                    gm = gate[m_blk:m_blk + 1, :]
                    ahead = (gm > gate) | ((gm == gate) & (m_blk < blk_id))
                    rank = rank + jnp.where(ahead, 1.0, 0.0)
                sel = jnp.where(past & (rank < float(MOBA_TOPK)), 1.0, 0.0)
                cols.append(_dot_tn(sel, eye))
            bias = jnp.where(jnp.concatenate(cols, axis=0) > 0.5, 0.0, -jnp.inf)
        prepped[i] = (q2, bias)
        yield

    def scores(i, out):
        q2, bias = prepped.pop(i)
        m_run = None
        for j in range(i + 1):
            s = _dot_nt(q2, kr_s[j * blk:(j + 1) * blk, :])
            if j == i:
                s = jnp.where(causal, s, -jnp.inf)
            elif bias is not None:
                s = s + bias[:, j:j + 1]
            s_s[i % 2, :, j * blk:(j + 1) * blk] = s
            part = jnp.maximum(s[:, :LANES], s[:, LANES:])
            m_run = part if m_run is None else jnp.maximum(m_run, part)
            yield
        out.append(jnp.max(m_run, axis=1, keepdims=True))

    def attend(i, m_fin):
        rows = slice(i * blk, (i + 1) * blk)
        l_run = jnp.zeros((2 * blk, LANES), F32)
        acc = jnp.zeros((2 * blk, HEAD_PAIR), F32)
        m_lanes = jnp.broadcast_to(m_fin, (2 * blk, LANES))
        for j in range(i + 1):
            p_lo = jnp.exp2(s_s[i % 2, :, j * blk:j * blk + LANES] - m_lanes)
            p_hi = jnp.exp2(s_s[i % 2, :, j * blk + LANES:(j + 1) * blk] - m_lanes)
            l_run = l_run + (p_lo + p_hi)
            p = jnp.concatenate([p_lo, p_hi], axis=1)
            acc = acc + _dot(p.astype(BF16), av_ref[j * blk:(j + 1) * blk, :])
            yield
        o2 = acc / jnp.sum(l_run, axis=1, keepdims=True)
        o = jnp.where(first, o2[:blk, :], o2[blk:, :])
        sq = o * o
        tot = jnp.sum(sq, axis=-1, keepdims=True)
        s0 = jnp.sum(jnp.where(first, sq, 0.0), axis=-1, keepdims=True)
        ms = jnp.where(first, s0, tot - s0) * (1.0 / dh)
        y_ref[rows, :] = (o * lax.rsqrt(ms + EPS) * og).astype(y_ref.dtype)

    def interleave(*stages):
        stages = list(stages)
        while stages:
            for stage in list(stages):
                try:
                    next(stage)
                except StopIteration:
                    stages.remove(stage)

    m_cur = []
    interleave(key_prep(0), key_prep(1), query_prep(0), query_prep(1))
    interleave(scores(0, m_cur))
    for i in range(MOBA_NB):
        m_next = []
        stages = [attend(i, m_cur[0])]
        if i + 1 < MOBA_NB:
            stages.insert(0, scores(i + 1, m_next))
        if i + 2 < MOBA_NB:
            stages += [key_prep(i + 2), query_prep(i + 2)]
        interleave(*stages)
        m_cur = m_next


def _moba(aq, ak, av, cos_t, sin_t, q_gain, k_gain, out_gain):
    n_pairs = A_HEADS // 2
    const = lambda b, p: (0, 0)
    pair = lambda b, p: (b, p)
    return pl.pallas_call(
        _moba_kernel,
        grid=(BATCH, n_pairs),
        in_specs=[
            pl.BlockSpec((SEQ, HEAD_PAIR), pair),
            pl.BlockSpec((SEQ, HEAD_PAIR), pair),
            pl.BlockSpec((SEQ, HEAD_PAIR), pair),
            pl.BlockSpec((SEQ, HEAD_PAIR), const),
            pl.BlockSpec((SEQ, HEAD_PAIR), const),
            pl.BlockSpec((1, HEAD_PAIR), const),
            pl.BlockSpec((1, HEAD_PAIR), const),
            pl.BlockSpec((None, 1, HEAD_PAIR), lambda b, p: (p, 0, 0)),
        ],
        out_specs=pl.BlockSpec((SEQ, HEAD_PAIR), pair),
        out_shape=jax.ShapeDtypeStruct((TOKENS, A_WIDTH), BF16),
        scratch_shapes=[
            pltpu.VMEM((SEQ, HEAD_PAIR), BF16),
            pltpu.VMEM((2, 2 * MOBA_BLOCK, SEQ), F32),
        ],
        compiler_params=pltpu.CompilerParams(
            dimension_semantics=("arbitrary", "arbitrary"), vmem_limit_bytes=VMEM_LIMIT),
        name="moba",
    )(aq, ak, av, cos_t, sin_t, q_gain, k_gain, out_gain)


def _s5_kernel(u_ref, abar_ref, bre_ref, bim_ref, cre_ref, cim_ref, d_ref, gw_ref, gb_ref, gmat_ref, og_ref,
               y_ref, xr_s, xi_s, st_s, u_s, y_s):
    step = pl.program_id(0)
    rows = S5_STEPS * BATCH
    tile = 256
    n_tiles = rows // tile
    halves = S_WIDTH // LANES

    @pl.when(step == 0)
    def _init():
        st_s[...] = jnp.zeros_like(st_s)

    for b in range(BATCH):
        for hv in range(halves):
            lanes = slice(b * S_WIDTH + hv * LANES, b * S_WIDTH + (hv + 1) * LANES)
            u_s[hv, pl.ds(b, S5_STEPS, stride=BATCH), :] = u_ref[:, lanes].astype(F32)

    def u_rows(r):
        return jnp.concatenate([u_s[hv, r * tile:(r + 1) * tile, :] for hv in range(halves)], axis=1)

    for r in range(n_tiles):
        u = u_rows(r).astype(BF16)
        t0 = r * tile // BATCH
        xr_s[t0:t0 + tile // BATCH] = _dot(u, bre_ref[...]).reshape(tile // BATCH, BATCH, S_FLAT)
        xi_s[t0:t0 + tile // BATCH] = _dot(u, bim_ref[...]).reshape(tile // BATCH, BATCH, S_FLAT)

    ar = jnp.broadcast_to(abar_ref[0:1, :], (BATCH, S_FLAT))
    ai = jnp.broadcast_to(abar_ref[1:2, :], (BATCH, S_FLAT))

    def scan_step(t, carry):
        sr, si = carry
        nr = ar * sr - ai * si + xr_s[t]
        ni = ar * si + ai * sr + xi_s[t]
        xr_s[t] = nr
        xi_s[t] = ni
        return nr, ni

    sr, si = lax.fori_loop(0, S5_STEPS, scan_step, (st_s[0], st_s[1]), unroll=4)
    st_s[0] = sr
    st_s[1] = si

    for r in range(n_tiles):
        t0 = r * tile // BATCH
        xr = xr_s[t0:t0 + tile // BATCH].reshape(tile, S_FLAT).astype(BF16)
        xi = xi_s[t0:t0 + tile // BATCH].reshape(tile, S_FLAT).astype(BF16)
        y = _dot(xr, cre_ref[...]) - _dot(xi, cim_ref[...]) + d_ref[...] * u_rows(r)
        z = 0.5 * y * (1.0 + jnp.tanh(math.sqrt(2.0 / math.pi) * (y + 0.044715 * (y * y * y))))
        y2 = z * _sigmoid(_dot(z.astype(BF16), gw_ref[...]) + gb_ref[...])
        ms = _dot((y2 * y2).astype(BF16), gmat_ref[...])
        yn = y2 * lax.rsqrt(ms + EPS) * og_ref[...]
        for hv in range(halves):
            y_s[hv, r * tile:(r + 1) * tile, :] = yn[:, hv * LANES:(hv + 1) * LANES]

    for b in range(BATCH):
        for hv in range(halves):
            lanes = slice(b * S_WIDTH + hv * LANES, b * S_WIDTH + (hv + 1) * LANES)
            y_ref[:, lanes] = y_s[hv, pl.ds(b, S5_STEPS, stride=BATCH), :].astype(y_ref.dtype)


def _s5(u_sb, abar, bre, bim, cre, cim, d_row, glu_w, glu_b, gmat, out_g):
    rows = S5_STEPS * BATCH
    const = lambda s: (0, 0)
    return pl.pallas_call(
        _s5_kernel,
        grid=(SEQ // S5_STEPS,),
        in_specs=[
            pl.BlockSpec((S5_STEPS, BATCH * S_WIDTH), lambda s: (s, 0)),
            pl.BlockSpec((2, S_FLAT), const),
            pl.BlockSpec((S_WIDTH, S_FLAT), const),
            pl.BlockSpec((S_WIDTH, S_FLAT), const),
            pl.BlockSpec((S_FLAT, S_WIDTH), const),
            pl.BlockSpec((S_FLAT, S_WIDTH), const),
            pl.BlockSpec((1, S_WIDTH), const),
            pl.BlockSpec((S_WIDTH, S_WIDTH), const),
            pl.BlockSpec((1, S_WIDTH), const),
            pl.BlockSpec((S_WIDTH, S_WIDTH), const),
            pl.BlockSpec((1, S_WIDTH), const),
        ],
        out_specs=pl.BlockSpec((S5_STEPS, BATCH * S_WIDTH), lambda s: (s, 0)),
        out_shape=jax.ShapeDtypeStruct((SEQ, BATCH * S_WIDTH), BF16),
        scratch_shapes=[
            pltpu.VMEM((S5_STEPS, BATCH, S_FLAT), F32),
            pltpu.VMEM((S5_STEPS, BATCH, S_FLAT), F32),
            pltpu.VMEM((2, BATCH, S_FLAT), F32),
            pltpu.VMEM((S_WIDTH // LANES, rows, LANES), F32),
            pltpu.VMEM((S_WIDTH // LANES, rows, LANES), F32),
        ],
        compiler_params=pltpu.CompilerParams(
            dimension_semantics=("arbitrary",), vmem_limit_bytes=VMEM_LIMIT),
        name="s5",
    )(u_sb, abar, bre, bim, cre, cim, d_row, glu_w, glu_b, gmat, out_g)


def _outproj_kernel(x_ref, ym_ref, ya_ref, ys_ref, w_ref, g_ref, wr_ref, br_ref,
                    x1_ref, xn_ref, route_ref, rows_s):
    x1 = x_ref[...]
    x1 = x1 + _dot(ym_ref[...], w_ref[0:M_WIDTH, :])
    x1 = x1 + _dot(ya_ref[...], w_ref[M_WIDTH:M_WIDTH + A_WIDTH, :])
    x1 = x1 + _dot(ys_ref[...], w_ref[M_WIDTH + A_WIDTH:, :])
    x1_ref[...] = x1
    xn = x1 * lax.rsqrt(jnp.mean(x1 * x1, axis=-1, keepdims=True) + EPS) * g_ref[...]
    for j in range(ROW_TILES):
        rows_s[pl.ds(j, TM_PROJ, stride=ROW_TILES), :] = xn[:, j * LANES:(j + 1) * LANES]
    xn_ref[...] = rows_s[...].astype(xn_ref.dtype)
    logits = _split_dot(xn, wr_ref[...]) + br_ref[...]

    lane = lax.broadcasted_iota(jnp.int32, logits.shape, 1).astype(F32)
    is_grp = lane < N_GROUPS
    neg = -jnp.inf
    gl = jnp.where(is_grp, logits, neg)
    gmax = jnp.max(gl, axis=-1, keepdims=True)
    gsum = jnp.sum(jnp.where(is_grp, jnp.exp(logits - gmax), 0.0), axis=-1, keepdims=True)
    g_sel = jnp.min(jnp.where(gl == gmax, lane, float(LANES)), axis=-1, keepdims=True)
    lo = N_GROUPS + EXPERTS_PER_GROUP * g_sel
    el = jnp.where((lane >= lo) & (lane < lo + EXPERTS_PER_GROUP), logits, neg)
    t1 = jnp.max(el, axis=-1, keepdims=True)
    i1 = jnp.min(jnp.where(el == t1, lane, float(LANES)), axis=-1, keepdims=True)
    el2 = jnp.where(lane == i1, neg, el)
    t2 = jnp.max(el2, axis=-1, keepdims=True)
    i2 = jnp.min(jnp.where(el2 == t2, lane, float(LANES)), axis=-1, keepdims=True)
    e21 = jnp.exp(t2 - t1)
    w1 = 1.0 / ((1.0 + e21) * gsum)
    w2 = e21 / ((1.0 + e21) * gsum)
    route = jnp.where(lane == 0.0, i1 - N_GROUPS,
                      jnp.where(lane == 1.0, i2 - N_GROUPS,
                                jnp.where(lane == 2.0, w1, jnp.where(lane == 3.0, w2, 0.0))))
    route_ref[...] = route


def _outproj(x, y_m, y_a, y_s_sb, w_out, g2, w_router, b_router):
    n_s = SEQ // TM_PROJ
    row = lambda b, s: (b * n_s + s, 0)
    const = lambda b, s: (0, 0)
    return pl.pallas_call(
        _outproj_kernel,
        grid=(BATCH, n_s),
        in_specs=[
            pl.BlockSpec((TM_PROJ, D_MODEL), row),
            pl.BlockSpec((TM_PROJ, M_WIDTH), row),
            pl.BlockSpec((TM_PROJ, A_WIDTH), row),
            pl.BlockSpec((TM_PROJ, S_WIDTH), lambda b, s: (s, b)),
            pl.BlockSpec((D_MIX, D_MODEL), const),
            pl.BlockSpec((1, D_MODEL), const),
            pl.BlockSpec((D_MODEL, 2 * LANES), const),
            pl.BlockSpec((1, LANES), const),
        ],
        out_specs=(
            pl.BlockSpec((TM_PROJ, D_MODEL), row),
            pl.BlockSpec((TM_PROJ * ROW_TILES, LANES), row),
            pl.BlockSpec((TM_PROJ, LANES), row),
        ),
        out_shape=(
            jax.ShapeDtypeStruct((TOKENS, D_MODEL), F32),
            jax.ShapeDtypeStruct((TOKENS * ROW_TILES, LANES), BF16),
            jax.ShapeDtypeStruct((TOKENS, LANES), F32),
        ),
        scratch_shapes=[pltpu.VMEM((TM_PROJ * ROW_TILES, LANES), F32)],
        compiler_params=pltpu.CompilerParams(
            dimension_semantics=("arbitrary", "arbitrary"), vmem_limit_bytes=VMEM_LIMIT),
        name="outproj",
    )(x, y_m, y_a, y_s_sb, w_out, g2, w_router, b_router)


def _ffn_kernel(be_ref, src_ref, xs_ref, wg_ref, wu_ref, wd_ref, ys_ref, rows_s):
    del be_ref
    i = pl.program_id(0)

    @pl.when(src_ref[i] == i)
    def _expert_block():
        rows_s[...] = xs_ref[...].astype(F32)
        xb = jnp.concatenate([rows_s[pl.ds(j, FFN_BLOCK, stride=ROW_TILES), :] for j in range(ROW_TILES)],
                             axis=1).astype(BF16)
        g = _dot(xb, wg_ref[...])
        u = _dot(xb, wu_ref[...])
        hmid = (g * _sigmoid(g) * u).astype(BF16)
        y = _dot(hmid, wd_ref[...])
        for j in range(ROW_TILES):
            ys_ref[pl.ds(j, FFN_BLOCK, stride=ROW_TILES), :] = y[:, j * LANES:(j + 1) * LANES]

    @pl.when(src_ref[i] != i)
    def _unused_tail_block():
        ys_ref[...] = jnp.zeros_like(ys_ref)


def _expert_ffn(layer, blk_exp, blk_src, xs, w_gate, w_up, w_down):
    nblk = blk_exp.shape[0]
    block = (FFN_BLOCK * ROW_TILES, LANES)
    grid_spec = pltpu.PrefetchScalarGridSpec(
        num_scalar_prefetch=2,
        grid=(nblk,),
        in_specs=[
            pl.BlockSpec(block, lambda i, be, src: (src[i], 0)),
            pl.BlockSpec((None, None, D_MODEL, D_EXPERT), lambda i, be, src: (layer, be[i], 0, 0)),
            pl.BlockSpec((None, None, D_MODEL, D_EXPERT), lambda i, be, src: (layer, be[i], 0, 0)),
            pl.BlockSpec((None, None, D_EXPERT, D_MODEL), lambda i, be, src: (layer, be[i], 0, 0)),
        ],
        out_specs=pl.BlockSpec(block, lambda i, be, src: (i, 0)),
        scratch_shapes=[pltpu.VMEM(block, F32)],
    )
    return pl.pallas_call(
        _ffn_kernel,
        grid_spec=grid_spec,
        out_shape=jax.ShapeDtypeStruct((nblk * FFN_BLOCK * ROW_TILES, LANES), F32),
        compiler_params=pltpu.CompilerParams(
            dimension_semantics=("arbitrary",), vmem_limit_bytes=VMEM_LIMIT),
        name="expert_ffn",
    )(blk_exp, blk_src, xs, w_gate, w_up, w_down)


def _plan_kernel(e_ref, ps_ref, dest_ref, run_s, tri_s):
    i = pl.program_id(0)

    @pl.when(i == 0)
    def _init():
        run_s[...] = ps_ref[...]
        s_i = lax.broadcasted_iota(jnp.int32, (PLAN_TILE, PLAN_TILE), 0)
        t_i = lax.broadcasted_iota(jnp.int32, (PLAN_TILE, PLAN_TILE), 1)
        tri_s[...] = (s_i < t_i).astype(BF16)

    expert = lax.broadcasted_iota(jnp.int32, (N_EXPERTS, PLAN_TILE), 0)
    onehot = jnp.where(e_ref[...] == expert, 1.0, 0.0)
    before = _dot(onehot.astype(BF16), tri_s[...])
    dest = jnp.sum(onehot * (before + run_s[:, 0:1]), axis=0, keepdims=True)
    dest_ref[...] = dest.astype(jnp.int32)
    run_s[...] = run_s[...] + jnp.sum(onehot, axis=1, keepdims=True)


def _plan(e_rows, pstart_col):
    n_tiles = e_rows.shape[0]
    return pl.pallas_call(
        _plan_kernel,
        grid=(n_tiles,),
        in_specs=[
            pl.BlockSpec((None, 1, PLAN_TILE), lambda i: (i, 0, 0)),
            pl.BlockSpec((N_EXPERTS, LANES), lambda i: (0, 0)),
        ],
        out_specs=pl.BlockSpec((None, 1, PLAN_TILE), lambda i: (i, 0, 0)),
        out_shape=jax.ShapeDtypeStruct((n_tiles, 1, PLAN_TILE), jnp.int32),
        scratch_shapes=[pltpu.VMEM((N_EXPERTS, LANES), F32), pltpu.VMEM((PLAN_TILE, PLAN_TILE), BF16)],
        compiler_params=pltpu.CompilerParams(
            dimension_semantics=("arbitrary",), vmem_limit_bytes=VMEM_LIMIT),
        name="moe_plan",
    )(e_rows, pstart_col)


def _dispatch_kernel(fill_ref, d0_ref, d1_ref, xn_ref, xs_hbm, zero_s, sem, fill_sem):
    @pl.when(pl.program_id(0) == 0)
    def _zero_padding():
        zero_s[...] = jnp.zeros_like(zero_s)

        def fill(c):
            start = pl.multiple_of(fill_ref[0, c] * ROW_TILES, ROW_TILES)
            return pltpu.make_async_copy(zero_s, xs_hbm.at[pl.ds(start, FFN_BLOCK * ROW_TILES), :], fill_sem)

        for c in range(N_FILLS):
            pl.when(fill_ref[0, c] >= 0)(lambda c=c: fill(c).start())
        for c in range(N_FILLS):
            pl.when(fill_ref[0, c] >= 0)(lambda c=c: fill(c).wait())

    def row_copy(dref, t):
        src = xn_ref.at[pl.ds(pl.multiple_of(t * ROW_TILES, ROW_TILES), ROW_TILES), :]
        dst = xs_hbm.at[pl.ds(pl.multiple_of(dref[0, t] * ROW_TILES, ROW_TILES), ROW_TILES), :]
        return pltpu.make_async_copy(src, dst, sem)

    def start(t, carry):
        row_copy(d0_ref, t).start()
        row_copy(d1_ref, t).start(priority=1)
        return carry

    lax.fori_loop(0, DISPATCH_TOKENS, start, 0, unroll=8)
    for _ in range(EXPERT_TOPK):
        pltpu.make_async_copy(xn_ref, xs_hbm.at[pl.ds(0, DISPATCH_TOKENS * ROW_TILES), :], sem).wait()


def _dispatch(fill_rows, dest, xn, n_rows):
    n_tiles = TOKENS // DISPATCH_TOKENS
    dest4 = dest.reshape(EXPERT_TOPK, n_tiles, 1, DISPATCH_TOKENS)
    smem_tile = lambda k: pl.BlockSpec((None, None, 1, DISPATCH_TOKENS), lambda i: (k, i, 0, 0),
                                       memory_space=pltpu.SMEM)
    return pl.pallas_call(
        _dispatch_kernel,
        grid=(n_tiles,),
        in_specs=[
            pl.BlockSpec(memory_space=pltpu.SMEM),
            smem_tile(0),
            smem_tile(1),
            pl.BlockSpec((DISPATCH_TOKENS * ROW_TILES, LANES), lambda i: (i, 0)),
        ],
        out_specs=pl.BlockSpec(memory_space=pl.ANY),
        out_shape=jax.ShapeDtypeStruct((n_rows * ROW_TILES, LANES), BF16),
        scratch_shapes=[
            pltpu.VMEM((FFN_BLOCK * ROW_TILES, LANES), BF16),
            pltpu.SemaphoreType.DMA(()),
            pltpu.SemaphoreType.DMA(()),
        ],
        compiler_params=pltpu.CompilerParams(
            dimension_semantics=("arbitrary",), vmem_limit_bytes=VMEM_LIMIT),
        name="moe_dispatch",
    )(fill_rows, dest4, dest4, xn)


def _combine_kernel(d0_ref, d1_ref, d0_next_ref, d1_next_ref, x1_ref, route_ref, ys_hbm, out_ref, buf, sem):
    i = pl.program_id(0)
    n = pl.num_programs(0)

    def start_tile(drefs, slot):
        def body(t, carry):
            for k in range(EXPERT_TOPK):
                src = ys_hbm.at[pl.ds(pl.multiple_of(drefs[k][0, t] * ROW_TILES, ROW_TILES), ROW_TILES), :]
                dst = buf.at[slot, k, pl.ds(pl.multiple_of(t * ROW_TILES, ROW_TILES), ROW_TILES), :]
                pltpu.make_async_copy(src, dst, sem.at[slot]).start(priority=k)
            return carry
        lax.fori_loop(0, COMBINE_TOKENS, body, 0, unroll=8)

    slot = i % 2

    @pl.when(i == 0)
    def _first():
        start_tile((d0_ref, d1_ref), 0)

    @pl.when(i + 1 < n)
    def _prefetch():
        start_tile((d0_next_ref, d1_next_ref), 1 - slot)

    for k in range(EXPERT_TOPK):
        pltpu.make_async_copy(ys_hbm.at[pl.ds(0, COMBINE_TOKENS * ROW_TILES), :], buf.at[slot, k],
                              sem.at[slot]).wait()
    w = route_ref[...]
    w1 = w[:, 2:3]
    w2 = w[:, 3:4]
    for j in range(ROW_TILES):
        cols = slice(j * LANES, (j + 1) * LANES)
        chunk = pl.ds(j, COMBINE_TOKENS, stride=ROW_TILES)
        out_ref[:, cols] = x1_ref[:, cols] + w1 * buf[slot, 0, chunk, :] + w2 * buf[slot, 1, chunk, :]


def _combine(dest, x1, route, ys3):
    n_tiles = TOKENS // COMBINE_TOKENS
    dest4 = dest.reshape(EXPERT_TOPK, n_tiles, 1, COMBINE_TOKENS)
    smem_tile = lambda k, nxt: pl.BlockSpec(
        (None, None, 1, COMBINE_TOKENS), lambda i: (k, jnp.minimum(i + nxt, n_tiles - 1), 0, 0),
        memory_space=pltpu.SMEM)
    return pl.pallas_call(
        _combine_kernel,
        grid=(n_tiles,),
        in_specs=[
            smem_tile(0, 0), smem_tile(1, 0), smem_tile(0, 1), smem_tile(1, 1),
            pl.BlockSpec((COMBINE_TOKENS, D_MODEL), lambda i: (i, 0)),
            pl.BlockSpec((COMBINE_TOKENS, LANES), lambda i: (i, 0)),
            pl.BlockSpec(memory_space=pl.ANY),
        ],
        out_specs=pl.BlockSpec((COMBINE_TOKENS, D_MODEL), lambda i: (i, 0)),
        out_shape=jax.ShapeDtypeStruct((TOKENS, D_MODEL), F32),
        scratch_shapes=[
            pltpu.VMEM((2, EXPERT_TOPK, COMBINE_TOKENS * ROW_TILES, LANES), F32),
            pltpu.SemaphoreType.DMA((2,)),
        ],
        compiler_params=pltpu.CompilerParams(
            dimension_semantics=("arbitrary",), vmem_limit_bytes=VMEM_LIMIT),
        name="moe_combine",
    )(dest4, dest4, dest4, dest4, x1, route, ys3)


def _rope_tables():
    inv = ROPE_THETA ** (-np.arange(ROPE_HALF, dtype=np.float64) * 2.0 / ROPE_DIM)
    ang = np.arange(SEQ, dtype=np.float64)[:, None] * inv[None, :]
    cos = np.ones((SEQ, A_HEAD_DIM))
    sin = np.zeros((SEQ, A_HEAD_DIM))
    cos[:, :ROPE_HALF] = np.cos(ang)
    cos[:, ROPE_HALF:ROPE_DIM] = np.cos(ang)
    sin[:, :ROPE_HALF] = -np.sin(ang)
    sin[:, ROPE_HALF:ROPE_DIM] = np.sin(ang)
    return (jnp.asarray(np.tile(cos, (1, 2)), F32), jnp.asarray(np.tile(sin, (1, 2)), F32))


def _split_w_in(w_in):
    sizes = [M_WIDTH, M_WIDTH, M_WIDTH, M_WIDTH, M_HEADS, M_HEADS, A_WIDTH, A_WIDTH, A_WIDTH, S_WIDTH]
    offs = np.cumsum([0] + sizes)
    mq, mk, mv, mo, mi, mf, aq, ak, av, su = (w_in[:, offs[n]:offs[n + 1]] for n in range(len(sizes)))
    w_main = jnp.concatenate([aq, ak, av, mq, mk, mv, mo, su], axis=1).astype(BF16)
    gate = jnp.concatenate([mi, mf], axis=1)
    return w_main, _hi_lo(jnp.pad(gate, ((0, 0), (0, LANES - 2 * M_HEADS))))


def _s5_params(a_re, a_im, b_re, b_im, c_re, c_im, log_dt):
    dt = jnp.exp(log_dt)[:, None]
    mag = jnp.exp(a_re * dt)
    abar_re, abar_im = mag * jnp.cos(a_im * dt), mag * jnp.sin(a_im * dt)
    zr, zi = abar_re - 1.0, abar_im
    den = a_re * a_re + a_im * a_im
    fr, fi = (zr * a_re + zi * a_im) / den, (zi * a_re - zr * a_im) / den
    bbar_re = fr[..., None] * b_re - fi[..., None] * b_im
    bbar_im = fr[..., None] * b_im + fi[..., None] * b_re
    eye = jnp.eye(S_GROUPS, dtype=F32)
    dense_b = lambda t: jnp.einsum('gph,gk->ghkp', t, eye).reshape(S_WIDTH, S_FLAT).astype(BF16)
    dense_c = lambda t: jnp.einsum('ghp,gk->gpkh', t, eye).reshape(S_FLAT, S_WIDTH).astype(BF16)
    abar = jnp.stack([abar_re.reshape(S_FLAT), abar_im.reshape(S_FLAT)])
    return abar, dense_b(bbar_re), dense_b(bbar_im), dense_c(c_re), dense_c(c_im)


def _moe(layer, x1, xn, route, w_gate, w_up, w_down):
    n_assign = TOKENS * EXPERT_TOPK
    e_flat = route[:, :EXPERT_TOPK].astype(jnp.int32).T.reshape(n_assign)
    experts = jnp.arange(N_EXPERTS, dtype=jnp.int32)
    counts = jnp.sum((e_flat[:, None] == experts[None, :]).astype(jnp.int32), axis=0)
    padded = (counts + FFN_BLOCK - 1) // FFN_BLOCK * FFN_BLOCK
    pends = jnp.cumsum(padded)
    pstarts = pends - padded
    n_rows = (n_assign + N_EXPERTS * (FFN_BLOCK - 1) + FFN_BLOCK - 1) // FFN_BLOCK * FFN_BLOCK
    nblk = n_rows // FFN_BLOCK
    blk_start = jnp.arange(nblk, dtype=jnp.int32) * FFN_BLOCK
    blk_exp = jnp.minimum(jnp.sum((pends[None, :] <= blk_start[:, None]).astype(jnp.int32), axis=1),
                          N_EXPERTS - 1).astype(jnp.int32)
    pstart_col = jnp.broadcast_to(pstarts.astype(F32)[:, None], (N_EXPERTS, LANES))
    dest = _plan(e_flat.reshape(n_assign // PLAN_TILE, 1, PLAN_TILE), pstart_col)
    tail = pends[-1] + jnp.arange(N_EXPERTS, dtype=jnp.int32) * FFN_BLOCK
    fill_rows = jnp.concatenate([jnp.where(padded > 0, pends - FFN_BLOCK, -1), jnp.where(tail < n_rows, tail, -1)])
    fill_rows = fill_rows.astype(jnp.int32).reshape(1, N_FILLS)
    xs = _dispatch(fill_rows, dest, xn, n_rows)
    blk_src = jnp.minimum(jnp.arange(nblk, dtype=jnp.int32), pends[-1] // FFN_BLOCK - 1).astype(jnp.int32)
    ys = _expert_ffn(layer, blk_exp, blk_src, xs, w_gate, w_up, w_down)
    return _combine(dest, x1, route, ys)


def kernel(x, norm1_g, w_in, m_bias_i, m_bias_f, m_conv_w, m_conv_b, m_out_g, a_q_g, a_k_g, a_out_g, s_a_re, s_a_im, s_b_re, s_b_im, s_c_re, s_c_im, s_d, s_log_dt, s_glu_w, s_glu_b, s_out_g, w_out, norm2_g, r_group_w, r_group_b, r_expert_w, r_expert_b, e_w_gate, e_w_up, e_w_down):
    cos_t, sin_t = _rope_tables()
    gmat = jnp.asarray(np.kron(np.eye(S_GROUPS), np.full((S_GROUP_DIM, S_GROUP_DIM), 1.0 / S_GROUP_DIM)), BF16)
    xf = x.reshape(TOKENS, D_MODEL)
    for l in range(DEPTH):
        w_main, w_gate = _split_w_in(w_in[l])
        aq, ak, av, mqk, mv, mo, su, gates, gatest = _inproj(xf, norm1_g[l].reshape(1, D_MODEL), w_main, w_gate)

        bias = jnp.concatenate([m_bias_i[l], m_bias_f[l]])
        bias_row = jnp.pad(bias, (0, LANES - 2 * M_HEADS)).reshape(1, LANES)
        y_m = _mlstm(mqk, mv, mo, gates, gatest, m_conv_w[l], m_conv_b[l].reshape(1, 2 * M_WIDTH),
                     bias_row, bias.reshape(2 * M_HEADS, 1), m_out_g[l].reshape(1, M_WIDTH))

        y_a = _moba(aq, ak, av, cos_t, sin_t,
                    jnp.tile(a_q_g[l], 2).reshape(1, HEAD_PAIR), jnp.tile(a_k_g[l], 2).reshape(1, HEAD_PAIR),
                    a_out_g[l].reshape(A_HEADS // 2, 1, HEAD_PAIR))

        abar, bre, bim, cre, cim = _s5_params(s_a_re[l], s_a_im[l], s_b_re[l], s_b_im[l],
                                              s_c_re[l], s_c_im[l], s_log_dt[l])
        y_s = _s5(su, abar, bre, bim, cre, cim,
                  s_d[l].reshape(1, S_WIDTH), s_glu_w[l].astype(BF16), s_glu_b[l].reshape(1, S_WIDTH),
                  gmat, s_out_g[l].reshape(1, S_WIDTH))

        w_router = jnp.pad(jnp.concatenate([r_group_w[l], r_expert_w[l]], axis=1),
                           ((0, 0), (0, LANES - N_GROUPS - N_EXPERTS)))
        b_router = jnp.pad(jnp.concatenate([r_group_b[l], r_expert_b[l]]),
                           (0, LANES - N_GROUPS - N_EXPERTS)).reshape(1, LANES)
        x1, xn, route = _outproj(xf, y_m, y_a, y_s, w_out[l].astype(BF16),
                                 norm2_g[l].reshape(1, D_MODEL), _hi_lo(w_router), b_router)
        xf = _moe(l, x1, xn, route, e_w_gate, e_w_up, e_w_down)
    return xf.reshape(BATCH, SEQ, D_MODEL)
```

```python
import functools
import math

import numpy as np
import jax
import jax.numpy as jnp
from jax import lax
from jax.experimental import pallas as pl
from jax.experimental.pallas import tpu as pltpu

F32 = jnp.float32
BF16 = jnp.bfloat16

D_MODEL = 1024
BATCH = 8
SEQ = 2048
DEPTH = 2
TOKENS = BATCH * SEQ

M_HEADS = 4
M_HEAD_DIM = 64
M_WIDTH = M_HEADS * M_HEAD_DIM
A_HEADS = 8
A_HEAD_DIM = 64
A_WIDTH = A_HEADS * A_HEAD_DIM
S_GROUPS = 16
S_GROUP_DIM = 16
S_WIDTH = S_GROUPS * S_GROUP_DIM
S_STATE = 64
S_FLAT = S_GROUPS * S_STATE
D_MIX = M_WIDTH + A_WIDTH + S_WIDTH

CONV_K = 4
MOBA_BLOCK = 256
MOBA_NB = SEQ // MOBA_BLOCK
MOBA_TOPK = 3
ROPE_THETA = 500000.0
ROPE_DIM = A_HEAD_DIM // 4
ROPE_HALF = ROPE_DIM // 2

N_GROUPS = 4
EXPERTS_PER_GROUP = 8
N_EXPERTS = N_GROUPS * EXPERTS_PER_GROUP
EXPERT_TOPK = 2
D_EXPERT = 512
EPS = 1e-6

LANES = 128
ROW_TILES = D_MODEL // LANES
HEAD_PAIR = 2 * A_HEAD_DIM

W_MAIN = 3 * A_WIDTH + 4 * M_WIDTH + S_WIDTH
TM_PROJ = 1024
OUT_ROWS = 256
M_CHUNK = 256
M_NCHUNK = SEQ // M_CHUNK
S5_STEPS = 256
FFN_BLOCK = 512
N_FILLS = 2 * N_EXPERTS
PLAN_TILE = 1024
DISPATCH_TOKENS = 2048
COMBINE_TOKENS = 256
VMEM_LIMIT = 56 * 1024 * 1024

HIGHEST = lax.Precision.HIGHEST


def _dot(a, b, precision=None):
    return jnp.dot(a, b, preferred_element_type=F32, precision=precision)


def _dot_nt(a, b, precision=None):
    return lax.dot_general(a, b, (((1,), (1,)), ((), ())), preferred_element_type=F32, precision=precision)


def _dot_tn(a, b, precision=None):
    return lax.dot_general(a, b, (((0,), (0,)), ((), ())), preferred_element_type=F32, precision=precision)


def _split_dot(a, w_cat):
    n = w_cat.shape[1] // 2
    a_hi = a.astype(BF16)
    a_lo = (a - a_hi.astype(F32)).astype(BF16)
    both = _dot(a_hi, w_cat)
    return both[:, :n] + both[:, n:] + _dot(a_lo, w_cat[:, :n])


def _bf16_terms(x):
    hi = x.astype(BF16)
    rest = x - hi.astype(F32)
    mid = rest.astype(BF16)
    return hi, mid, (rest - mid.astype(F32)).astype(BF16)


def _hi_lo(w):
    w_hi = w.astype(BF16)
    return jnp.concatenate([w_hi, (w - w_hi.astype(F32)).astype(BF16)], axis=1)


def _log_sigmoid(x):
    return jnp.minimum(x, 0.0) - jnp.log1p(jnp.exp(-jnp.abs(x)))


def _sigmoid(x):
    return 1.0 / (1.0 + jnp.exp(-x))


def _inproj_kernel(x_ref, g_ref, w_ref, wg_ref,
                   aq_ref, ak_ref, av_ref, mqk_ref, mv_ref, mo_ref, su_ref, gates_ref, gatest_ref):
    x = x_ref[...]
    h = x * lax.rsqrt(jnp.mean(x * x, axis=-1, keepdims=True) + EPS) * g_ref[...]
    hb = h.astype(BF16)
    off = 0
    for ref, width in ((aq_ref, A_WIDTH), (ak_ref, A_WIDTH), (av_ref, A_WIDTH), (mqk_ref, 2 * M_WIDTH),
                       (mv_ref, M_WIDTH), (mo_ref, M_WIDTH), (su_ref, S_WIDTH)):
        ref[...] = _dot(hb, w_ref[:, off:off + width]).astype(ref.dtype)
        off += width
    gates = _split_dot(h, wg_ref[...])
    gates_ref[...] = gates
    gt = gates.T[:8, :]
    for c in range(TM_PROJ // M_CHUNK):
        gatest_ref[c] = gt[:, c * M_CHUNK:(c + 1) * M_CHUNK]


def _inproj(x, g, w_main, w_gate):
    n_s = SEQ // TM_PROJ
    row = lambda b, s: (b * n_s + s, 0)
    const = lambda b, s: (0, 0)
    out_shapes = (
        jax.ShapeDtypeStruct((TOKENS, A_WIDTH), BF16),
        jax.ShapeDtypeStruct((TOKENS, A_WIDTH), BF16),
        jax.ShapeDtypeStruct((TOKENS, A_WIDTH), BF16),
        jax.ShapeDtypeStruct((TOKENS, 2 * M_WIDTH), BF16),
        jax.ShapeDtypeStruct((TOKENS, M_WIDTH), BF16),
        jax.ShapeDtypeStruct((TOKENS, M_WIDTH), BF16),
        jax.ShapeDtypeStruct((SEQ, BATCH * S_WIDTH), BF16),
        jax.ShapeDtypeStruct((TOKENS, LANES), F32),
        jax.ShapeDtypeStruct((TOKENS // M_CHUNK, 8, M_CHUNK), F32),
    )
    out_specs = (
        pl.BlockSpec((TM_PROJ, A_WIDTH), row),
        pl.BlockSpec((TM_PROJ, A_WIDTH), row),
        pl.BlockSpec((TM_PROJ, A_WIDTH), row),
        pl.BlockSpec((TM_PROJ, 2 * M_WIDTH), row),
        pl.BlockSpec((TM_PROJ, M_WIDTH), row),
        pl.BlockSpec((TM_PROJ, M_WIDTH), row),
        pl.BlockSpec((TM_PROJ, S_WIDTH), lambda b, s: (s, b)),
        pl.BlockSpec((TM_PROJ, LANES), row),
        pl.BlockSpec((TM_PROJ // M_CHUNK, 8, M_CHUNK), lambda b, s: (b * n_s + s, 0, 0)),
    )
    return pl.pallas_call(
        _inproj_kernel,
        grid=(BATCH, n_s),
        in_specs=[
            pl.BlockSpec((TM_PROJ, D_MODEL), row),
            pl.BlockSpec((1, D_MODEL), const),
            pl.BlockSpec((D_MODEL, W_MAIN), const),
            pl.BlockSpec((D_MODEL, 2 * LANES), const),
        ],
        out_specs=out_specs,
        out_shape=out_shapes,
        compiler_params=pltpu.CompilerParams(
            dimension_semantics=("arbitrary", "arbitrary"), vmem_limit_bytes=VMEM_LIMIT),
        name="inproj",
    )(x, g, w_main, w_gate)


def _mlstm_kernel(mqk_ref, mv_ref, mo_ref, gates_ref, gatest_ref, cw_ref, cb_ref, brow_ref, bcol_ref, og_ref,
                  y_ref, qk_s, cols_s, wrow_s):
    L = M_CHUNK
    dh = M_HEAD_DIM
    r_i = lax.broadcasted_iota(jnp.int32, (L, L), 0)
    c_i = lax.broadcasted_iota(jnp.int32, (L, L), 1)
    causal = c_i <= r_i
    tri_u = (r_i <= c_i).astype(BF16)
    eye = (lax.broadcasted_iota(jnp.int32, (8, LANES), 0)
           == lax.broadcasted_iota(jnp.int32, (8, LANES), 1)).astype(BF16)
    w = cw_ref[...]
    cb = cb_ref[...]
    lane_q = lax.broadcasted_iota(jnp.int32, (1, 2 * M_WIDTH), 1) < M_WIDTH
    kscale = jnp.where(lane_q, 1.0, dh ** -0.5).astype(F32)
    for c in range(M_NCHUNK):
        halo = 16 if c > 0 else 0
        for strip in range(2 * M_WIDTH // LANES):
            cols = slice(strip * LANES, (strip + 1) * LANES)
            xt = mqk_ref[c * L - halo:(c + 1) * L, cols].astype(F32)
            rows = lax.broadcasted_iota(jnp.int32, xt.shape, 0)
            y = cb[:, cols] + w[CONV_K - 1:CONV_K, cols] * xt
            for k in range(1, CONV_K):
                sh = pltpu.roll(xt, k, 0)
                if c == 0:
                    sh = jnp.where(rows >= k, sh, 0.0)
                y = y + w[CONV_K - 1 - k:CONV_K - k, cols] * sh
            y = y[halo:, :]
            y = y * _sigmoid(y) * kscale[:, cols]
            qk_s[c * L:(c + 1) * L, cols] = y.astype(BF16)

    rows_all = M_NCHUNK * 8
    grow = gatest_ref[...].reshape(rows_all, L) + jnp.tile(bcol_ref[...], (M_NCHUNK, 1))
    cs_row = sum(_dot(term, tri_u) for term in _bf16_terms(_log_sigmoid(grow)))
    w_row = grow - pltpu.roll(cs_row, rows_all - M_HEADS, 0)
    lane_t = lax.broadcasted_iota(jnp.int32, (rows_all, L), 1)
    pm = w_row
    shift = 1
    while shift < L:
        pm = jnp.maximum(pm, jnp.where(lane_t >= shift, pltpu.roll(pm, shift, 1), -jnp.inf))
        shift *= 2
    wrow_s[...] = w_row.reshape(M_NCHUNK, 8, L)
    is_i_row = lax.broadcasted_iota(jnp.int32, (rows_all, L), 0) % 8 < M_HEADS
    terms = _bf16_terms(jnp.where(is_i_row, pm, cs_row))
    for c in range(M_NCHUNK):
        cols_s[c * L:(c + 1) * L, :] = sum(_dot_tn(term[c * 8:(c + 1) * 8, :], eye) for term in terms)

    pad_lane = lax.broadcasted_iota(jnp.int32, (L, LANES - dh), 1)
    ones_pad = jnp.where(pad_lane == 0, 1.0, 0.0).astype(BF16)
    og = og_ref[...]
    head_mean = jnp.where(lax.broadcasted_iota(jnp.int32, (M_WIDTH, M_WIDTH), 0) // dh
                          == lax.broadcasted_iota(jnp.int32, (M_WIDTH, M_WIDTH), 1) // dh,
                          1.0 / dh, 0.0).astype(BF16)

    def chunk(c, carry):
        r0 = pl.multiple_of(c * L, L)
        gcol = gates_ref[pl.ds(r0, L), :] + brow_ref[...]
        cs_col = pm_col = cols_s[pl.ds(r0, L), :]
        w_row = wrow_s[c]
        qk = qk_s[pl.ds(r0, L), :]
        vv = mv_ref[pl.ds(r0, L), :]
        hs = [None] * M_HEADS
        new_carry = [None] * M_HEADS

        def head(h):
            cst, m = carry[h]
            q = qk[:, h * dh:(h + 1) * dh]
            k = qk[:, M_WIDTH + h * dh:M_WIDTH + (h + 1) * dh]
            v_aug = jnp.concatenate([vv[:, h * dh:(h + 1) * dh], ones_pad], axis=1)
            bc = cs_col[:, M_HEADS + h:M_HEADS + h + 1]
            ic = gcol[:, h:h + 1]
            m_row = jnp.maximum(m, pm_col[:, h:h + 1])
            mt = bc + m_row
            inter = jnp.exp(m - m_row)
            yield
            sqk = _dot_nt(q, k) * jnp.exp(jnp.where(causal, w_row[h:h + 1, :] - m_row, -jnp.inf))
            yield
            nd = inter * _dot(q, cst.astype(BF16)) + _dot(sqk.astype(BF16), v_aug)
            yield
            num = nd[:, :dh]
            den = nd[:, dh:dh + 1]
            hs[h] = num / jnp.maximum(jnp.abs(den), jnp.exp(-mt))
            bl = bc[L - 1:L, :]
            gs = bl - bc + ic
            m_new = jnp.maximum(bl + m, jnp.max(gs, axis=0, keepdims=True))
            decay = jnp.exp(bl + m - m_new)
            wts = jnp.exp(gs - m_new)
            yield
            kw = (k.astype(F32) * wts).astype(BF16)
            new_carry[h] = (decay * cst + _dot_tn(kw, v_aug), m_new)

        stages = [head(h) for h in range(M_HEADS)]
        while stages:
            for stage in list(stages):
                try:
                    next(stage)
                except StopIteration:
                    stages.remove(stage)
        hcat = jnp.concatenate(hs, axis=1)
        hcat = hcat * lax.rsqrt(_dot((hcat * hcat).astype(BF16), head_mean) + EPS) * og
        o = mo_ref[pl.ds(r0, L), :].astype(F32)
        y_ref[pl.ds(r0, L), :] = (hcat * _sigmoid(o)).astype(y_ref.dtype)
        return tuple(new_carry)

    init = tuple((jnp.zeros((dh, LANES), F32), jnp.zeros((1, 1), F32)) for _ in range(M_HEADS))
    lax.fori_loop(0, M_NCHUNK, chunk, init, unroll=2)


def _mlstm(mqk, mv, mo, gates, gatest, conv_w, conv_b, bias_row, bias_col, out_g):
    per_b = lambda b: (b, 0)
    const = lambda b: (0, 0)
    return pl.pallas_call(
        _mlstm_kernel,
        grid=(BATCH,),
        in_specs=[
            pl.BlockSpec((SEQ, 2 * M_WIDTH), per_b),
            pl.BlockSpec((SEQ, M_WIDTH), per_b),
            pl.BlockSpec((SEQ, M_WIDTH), per_b),
            pl.BlockSpec((SEQ, LANES), per_b),
            pl.BlockSpec((M_NCHUNK, 8, M_CHUNK), lambda b: (b, 0, 0)),
            pl.BlockSpec((CONV_K, 2 * M_WIDTH), const),
            pl.BlockSpec((1, 2 * M_WIDTH), const),
            pl.BlockSpec((1, LANES), const),
            pl.BlockSpec((8, 1), const),
            pl.BlockSpec((1, M_WIDTH), const),
        ],
        out_specs=pl.BlockSpec((SEQ, M_WIDTH), per_b),
        out_shape=jax.ShapeDtypeStruct((TOKENS, M_WIDTH), BF16),
        scratch_shapes=[
            pltpu.VMEM((SEQ, 2 * M_WIDTH), BF16),
            pltpu.VMEM((SEQ, LANES), F32),
            pltpu.VMEM((M_NCHUNK, 8, M_CHUNK), F32),
        ],
        compiler_params=pltpu.CompilerParams(
            dimension_semantics=("arbitrary",), vmem_limit_bytes=VMEM_LIMIT),
        name="mlstm",
    )(mqk, mv, mo, gates, gatest, conv_w, conv_b, bias_row, bias_col, out_g)


def _qk_norm_rope(x, gain, cos, sin):
    lane = lax.broadcasted_iota(jnp.int32, x.shape, 1)
    first = lane < A_HEAD_DIM
    sq = x * x
    tot = jnp.sum(sq, axis=-1, keepdims=True)
    s0 = jnp.sum(jnp.where(first, sq, 0.0), axis=-1, keepdims=True)
    ms = jnp.where(first, s0, tot - s0) * (1.0 / A_HEAD_DIM)
    xn = x * lax.rsqrt(ms + EPS) * gain
    in_low = (lane % A_HEAD_DIM) < ROPE_HALF
    swapped = jnp.where(in_low, pltpu.roll(xn, LANES - ROPE_HALF, 1), pltpu.roll(xn, ROPE_HALF, 1))
    return xn * cos + swapped * sin


def _moba_kernel(aq_ref, ak_ref, av_ref, cos_ref, sin_ref, qg_ref, kg_ref, og_ref, y_ref, kr_s, s_s):
    blk = MOBA_BLOCK
    dh = A_HEAD_DIM
    km_first = lax.broadcasted_iota(jnp.int32, (MOBA_NB, HEAD_PAIR), 1) < dh
    first = lax.broadcasted_iota(jnp.int32, (blk, HEAD_PAIR), 1) < dh
    blk_id = lax.broadcasted_iota(jnp.int32, (MOBA_NB, blk), 0)
    eye = (lax.broadcasted_iota(jnp.int32, (MOBA_NB, LANES), 0)
           == lax.broadcasted_iota(jnp.int32, (MOBA_NB, LANES), 1)).astype(F32)
    q_pos = lax.broadcasted_iota(jnp.int32, (2 * blk, blk), 0) % blk
    k_pos = lax.broadcasted_iota(jnp.int32, (2 * blk, blk), 1)
    causal = k_pos <= q_pos
    og = og_ref[...]

    km_rows = []
    prepped = {}

    def key_prep(j):
        rows = slice(j * blk, (j + 1) * blk)
        kr = _qk_norm_rope(ak_ref[rows, :].astype(F32), kg_ref[...], cos_ref[rows, :], sin_ref[rows, :])
        kr_s[rows, :] = kr.astype(BF16)
        km_rows.append(jnp.mean(kr, axis=0, keepdims=True))
        yield

    def query_prep(i):
        rows = slice(i * blk, (i + 1) * blk)
        qr = _qk_norm_rope(aq_ref[rows, :].astype(F32), qg_ref[...], cos_ref[rows, :], sin_ref[rows, :])
        qs = qr * (dh ** -0.5 * math.log2(math.e))
        q2 = jnp.concatenate([jnp.where(first, qs, 0.0), jnp.where(first, 0.0, qs)], axis=0).astype(BF16)
        yield

        bias = None
        if i > MOBA_TOPK:
            km = jnp.concatenate(km_rows[:i] + [jnp.zeros((MOBA_NB - i, HEAD_PAIR), F32)], axis=0)
            km_heads = (jnp.where(km_first, km, 0.0), jnp.where(km_first, 0.0, km))
            past = blk_id < i
            cols = []
            for kmh in km_heads:
                gate = jnp.where(past, _dot_nt(kmh, qr, precision=HIGHEST), -jnp.inf)
                rank = jnp.zeros((MOBA_NB, blk), F32)
                for m_blk in range(i):
                    gm = gate[m_blk:m_blk + 1, :]
                    ahead = (gm > gate) | ((gm == gate) & (m_blk < blk_id))
                    rank = rank + jnp.where(ahead, 1.0, 0.0)
                sel = jnp.where(past & (rank < float(MOBA_TOPK)), 1.0, 0.0)
                cols.append(_dot_tn(sel, eye))
            bias = jnp.where(jnp.concatenate(cols, axis=0) > 0.5, 0.0, -jnp.inf)
        prepped[i] = (q2, bias)
        yield

    def scores(i, out):
        q2, bias = prepped.pop(i)
        m_run = None
        for j in range(i + 1):
            s = _dot_nt(q2, kr_s[j * blk:(j + 1) * blk, :])
            if j == i:
                s = jnp.where(causal, s, -jnp.inf)
            elif bias is not None:
                s = s + bias[:, j:j + 1]
            s_s[i % 2, :, j * blk:(j + 1) * blk] = s
            part = jnp.maximum(s[:, :LANES], s[:, LANES:])
            m_run = part if m_run is None else jnp.maximum(m_run, part)
            yield
        out.append(jnp.max(m_run, axis=1, keepdims=True))

    def attend(i, m_fin):
        rows = slice(i * blk, (i + 1) * blk)
        l_run = jnp.zeros((2 * blk, LANES), F32)
        acc = jnp.zeros((2 * blk, HEAD_PAIR), F32)
        m_lanes = jnp.broadcast_to(m_fin, (2 * blk, LANES))
        for j in range(i + 1):
            p_lo = jnp.exp2(s_s[i % 2, :, j * blk:j * blk + LANES] - m_lanes)
            p_hi = jnp.exp2(s_s[i % 2, :, j * blk + LANES:(j + 1) * blk] - m_lanes)
            l_run = l_run + (p_lo + p_hi)
            p = jnp.concatenate([p_lo, p_hi], axis=1)
            acc = acc + _dot(p.astype(BF16), av_ref[j * blk:(j + 1) * blk, :])
            yield
        o2 = acc / jnp.sum(l_run, axis=1, keepdims=True)
        o = jnp.where(first, o2[:blk, :], o2[blk:, :])
        sq = o * o
        tot = jnp.sum(sq, axis=-1, keepdims=True)
        s0 = jnp.sum(jnp.where(first, sq, 0.0), axis=-1, keepdims=True)
        ms = jnp.where(first, s0, tot - s0) * (1.0 / dh)
        y_ref[rows, :] = (o * lax.rsqrt(ms + EPS) * og).astype(y_ref.dtype)

    def interleave(*stages):
        stages = list(stages)
        while stages:
            for stage in list(stages):
                try:
                    next(stage)
                except StopIteration:
                    stages.remove(stage)

    m_cur = []
    interleave(key_prep(0), key_prep(1), query_prep(0), query_prep(1))
    interleave(scores(0, m_cur))
    for i in range(MOBA_NB):
        m_next = []
        stages = [attend(i, m_cur[0])]
        if i + 1 < MOBA_NB:
            stages.insert(0, scores(i + 1, m_next))
        if i + 2 < MOBA_NB:
            stages += [key_prep(i + 2), query_prep(i + 2)]
        interleave(*stages)
        m_cur = m_next


def _moba(aq, ak, av, cos_t, sin_t, q_gain, k_gain, out_gain):
    n_pairs = A_HEADS // 2
    const = lambda b, p: (0, 0)
    pair = lambda b, p: (b, p)
    return pl.pallas_call(
        _moba_kernel,
        grid=(BATCH, n_pairs),
        in_specs=[
            pl.BlockSpec((SEQ, HEAD_PAIR), pair),
            pl.BlockSpec((SEQ, HEAD_PAIR), pair),
            pl.BlockSpec((SEQ, HEAD_PAIR), pair),
            pl.BlockSpec((SEQ, HEAD_PAIR), const),
            pl.BlockSpec((SEQ, HEAD_PAIR), const),
            pl.BlockSpec((1, HEAD_PAIR), const),
            pl.BlockSpec((1, HEAD_PAIR), const),
            pl.BlockSpec((None, 1, HEAD_PAIR), lambda b, p: (p, 0, 0)),
        ],
        out_specs=pl.BlockSpec((SEQ, HEAD_PAIR), pair),
        out_shape=jax.ShapeDtypeStruct((TOKENS, A_WIDTH), BF16),
        scratch_shapes=[
            pltpu.VMEM((SEQ, HEAD_PAIR), BF16),
            pltpu.VMEM((2, 2 * MOBA_BLOCK, SEQ), F32),
        ],
        compiler_params=pltpu.CompilerParams(
            dimension_semantics=("arbitrary", "arbitrary"), vmem_limit_bytes=VMEM_LIMIT),
        name="moba",
    )(aq, ak, av, cos_t, sin_t, q_gain, k_gain, out_gain)


def _s5_kernel(u_ref, abar_ref, bre_ref, bim_ref, cre_ref, cim_ref, d_ref, gw_ref, gb_ref, gmat_ref, og_ref,
               y_ref, xr_s, xi_s, st_s, u_s, y_s):
    step = pl.program_id(0)
    rows = S5_STEPS * BATCH
    tile = 256
    n_tiles = rows // tile
    halves = S_WIDTH // LANES

    @pl.when(step == 0)
    def _init():
        st_s[...] = jnp.zeros_like(st_s)

    for b in range(BATCH):
        for hv in range(halves):
            lanes = slice(b * S_WIDTH + hv * LANES, b * S_WIDTH + (hv + 1) * LANES)
            u_s[hv, pl.ds(b, S5_STEPS, stride=BATCH), :] = u_ref[:, lanes].astype(F32)

    def u_rows(r):
        return jnp.concatenate([u_s[hv, r * tile:(r + 1) * tile, :] for hv in range(halves)], axis=1)

    for r in range(n_tiles):
        u = u_rows(r).astype(BF16)
        t0 = r * tile // BATCH
        xr_s[t0:t0 + tile // BATCH] = _dot(u, bre_ref[...]).reshape(tile // BATCH, BATCH, S_FLAT)
        xi_s[t0:t0 + tile // BATCH] = _dot(u, bim_ref[...]).reshape(tile // BATCH, BATCH, S_FLAT)

    ar = jnp.broadcast_to(abar_ref[0:1, :], (BATCH, S_FLAT))
    ai = jnp.broadcast_to(abar_ref[1:2, :], (BATCH, S_FLAT))

    def scan_step(t, carry):
        sr, si = carry
        nr = ar * sr - ai * si + xr_s[t]
        ni = ar * si + ai * sr + xi_s[t]
        xr_s[t] = nr
        xi_s[t] = ni
        return nr, ni

    sr, si = lax.fori_loop(0, S5_STEPS, scan_step, (st_s[0], st_s[1]), unroll=4)
    st_s[0] = sr
    st_s[1] = si

    for r in range(n_tiles):
        t0 = r * tile // BATCH
        xr = xr_s[t0:t0 + tile // BATCH].reshape(tile, S_FLAT).astype(BF16)
        xi = xi_s[t0:t0 + tile // BATCH].reshape(tile, S_FLAT).astype(BF16)
        y = _dot(xr, cre_ref[...]) - _dot(xi, cim_ref[...]) + d_ref[...] * u_rows(r)
        z = 0.5 * y * (1.0 + jnp.tanh(math.sqrt(2.0 / math.pi) * (y + 0.044715 * (y * y * y))))
        y2 = z * _sigmoid(_dot(z.astype(BF16), gw_ref[...]) + gb_ref[...])
        ms = _dot((y2 * y2).astype(BF16), gmat_ref[...])
        yn = y2 * lax.rsqrt(ms + EPS) * og_ref[...]
        for hv in range(halves):
            y_s[hv, r * tile:(r + 1) * tile, :] = yn[:, hv * LANES:(hv + 1) * LANES]

    for b in range(BATCH):
        for hv in range(halves):
            lanes = slice(b * S_WIDTH + hv * LANES, b * S_WIDTH + (hv + 1) * LANES)
            y_ref[:, lanes] = y_s[hv, pl.ds(b, S5_STEPS, stride=BATCH), :].astype(y_ref.dtype)


def _s5(u_sb, abar, bre, bim, cre, cim, d_row, glu_w, glu_b, gmat, out_g):
    rows = S5_STEPS * BATCH
    const = lambda s: (0, 0)
    return pl.pallas_call(
        _s5_kernel,
        grid=(SEQ // S5_STEPS,),
        in_specs=[
            pl.BlockSpec((S5_STEPS, BATCH * S_WIDTH), lambda s: (s, 0)),
            pl.BlockSpec((2, S_FLAT), const),
            pl.BlockSpec((S_WIDTH, S_FLAT), const),
            pl.BlockSpec((S_WIDTH, S_FLAT), const),
            pl.BlockSpec((S_FLAT, S_WIDTH), const),
            pl.BlockSpec((S_FLAT, S_WIDTH), const),
            pl.BlockSpec((1, S_WIDTH), const),
            pl.BlockSpec((S_WIDTH, S_WIDTH), const),
            pl.BlockSpec((1, S_WIDTH), const),
            pl.BlockSpec((S_WIDTH, S_WIDTH), const),
            pl.BlockSpec((1, S_WIDTH), const),
        ],
        out_specs=pl.BlockSpec((S5_STEPS, BATCH * S_WIDTH), lambda s: (s, 0)),
        out_shape=jax.ShapeDtypeStruct((SEQ, BATCH * S_WIDTH), BF16),
        scratch_shapes=[
            pltpu.VMEM((S5_STEPS, BATCH, S_FLAT), F32),
            pltpu.VMEM((S5_STEPS, BATCH, S_FLAT), F32),
            pltpu.VMEM((2, BATCH, S_FLAT), F32),
            pltpu.VMEM((S_WIDTH // LANES, rows, LANES), F32),
            pltpu.VMEM((S_WIDTH // LANES, rows, LANES), F32),
        ],
        compiler_params=pltpu.CompilerParams(
            dimension_semantics=("arbitrary",), vmem_limit_bytes=VMEM_LIMIT),
        name="s5",
    )(u_sb, abar, bre, bim, cre, cim, d_row, glu_w, glu_b, gmat, out_g)


def _outproj_kernel(x_ref, ym_ref, ya_ref, ys_ref, w_ref, g_ref, wr_ref, br_ref,
                    x1_ref, xn_ref, route_ref, rows_s):
    n_chunks = TM_PROJ // OUT_ROWS
    logit_chunks = [None] * n_chunks

    def row_chunk(c):
        r = slice(c * OUT_ROWS, (c + 1) * OUT_ROWS)
        x1 = x_ref[r, :]
        x1 = x1 + _dot(ym_ref[r, :], w_ref[0:M_WIDTH, :])
        x1 = x1 + _dot(ya_ref[r, :], w_ref[M_WIDTH:M_WIDTH + A_WIDTH, :])
        x1 = x1 + _dot(ys_ref[r, :], w_ref[M_WIDTH + A_WIDTH:, :])
        yield
        x1_ref[r, :] = x1
        xn = x1 * lax.rsqrt(jnp.mean(x1 * x1, axis=-1, keepdims=True) + EPS) * g_ref[...]
        yield
        for j in range(ROW_TILES):
            rows_s[pl.ds(c * OUT_ROWS * ROW_TILES + j, OUT_ROWS, stride=ROW_TILES), :] = xn[:, j * LANES:(j + 1) * LANES]
        logit_chunks[c] = _split_dot(xn, wr_ref[...]) + br_ref[...]

    stages = [row_chunk(c) for c in range(n_chunks)]
    while stages:
        for stage in list(stages):
            try:
                next(stage)
            except StopIteration:
                stages.remove(stage)
    xn_ref[...] = rows_s[...].astype(xn_ref.dtype)
    logits = jnp.concatenate(logit_chunks, axis=0)

    lane = lax.broadcasted_iota(jnp.int32, logits.shape, 1).astype(F32)
    is_grp = lane < N_GROUPS
    neg = -jnp.inf
    gl = jnp.where(is_grp, logits, neg)
    gmax = jnp.max(gl, axis=-1, keepdims=True)
    gsum = jnp.sum(jnp.where(is_grp, jnp.exp(logits - gmax), 0.0), axis=-1, keepdims=True)
    g_sel = jnp.min(jnp.where(gl == gmax, lane, float(LANES)), axis=-1, keepdims=True)
    lo = N_GROUPS + EXPERTS_PER_GROUP * g_sel
    el = jnp.where((lane >= lo) & (lane < lo + EXPERTS_PER_GROUP), logits, neg)
    t1 = jnp.max(el, axis=-1, keepdims=True)
    i1 = jnp.min(jnp.where(el == t1, lane, float(LANES)), axis=-1, keepdims=True)
    el2 = jnp.where(lane == i1, neg, el)
    t2 = jnp.max(el2, axis=-1, keepdims=True)
    i2 = jnp.min(jnp.where(el2 == t2, lane, float(LANES)), axis=-1, keepdims=True)
    e21 = jnp.exp(t2 - t1)
    w1 = 1.0 / ((1.0 + e21) * gsum)
    w2 = e21 / ((1.0 + e21) * gsum)
    route = jnp.where(lane == 0.0, i1 - N_GROUPS,
                      jnp.where(lane == 1.0, i2 - N_GROUPS,
                                jnp.where(lane == 2.0, w1, jnp.where(lane == 3.0, w2, 0.0))))
    route_ref[...] = route


def _outproj(x, y_m, y_a, y_s_sb, w_out, g2, w_router, b_router):
    n_s = SEQ // TM_PROJ
    row = lambda b, s: (b * n_s + s, 0)
    const = lambda b, s: (0, 0)
    return pl.pallas_call(
        _outproj_kernel,
        grid=(BATCH, n_s),
        in_specs=[
            pl.BlockSpec((TM_PROJ, D_MODEL), row),
            pl.BlockSpec((TM_PROJ, M_WIDTH), row),
            pl.BlockSpec((TM_PROJ, A_WIDTH), row),
            pl.BlockSpec((TM_PROJ, S_WIDTH), lambda b, s: (s, b)),
            pl.BlockSpec((D_MIX, D_MODEL), const),
            pl.BlockSpec((1, D_MODEL), const),
            pl.BlockSpec((D_MODEL, 2 * LANES), const),
            pl.BlockSpec((1, LANES), const),
        ],
        out_specs=(
            pl.BlockSpec((TM_PROJ, D_MODEL), row),
            pl.BlockSpec((TM_PROJ * ROW_TILES, LANES), row),
            pl.BlockSpec((TM_PROJ, LANES), row),
        ),
        out_shape=(
            jax.ShapeDtypeStruct((TOKENS, D_MODEL), F32),
            jax.ShapeDtypeStruct((TOKENS * ROW_TILES, LANES), BF16),
            jax.ShapeDtypeStruct((TOKENS, LANES), F32),
        ),
        scratch_shapes=[pltpu.VMEM((TM_PROJ * ROW_TILES, LANES), F32)],
        compiler_params=pltpu.CompilerParams(
            dimension_semantics=("arbitrary", "arbitrary"), vmem_limit_bytes=VMEM_LIMIT),
        name="outproj",
    )(x, y_m, y_a, y_s_sb, w_out, g2, w_router, b_router)


def _ffn_kernel(be_ref, src_ref, xs_ref, wg_ref, wu_ref, wd_ref, ys_ref, rows_s):
    del be_ref
    i = pl.program_id(0)

    @pl.when(src_ref[i] == i)
    def _expert_block():
        rows_s[...] = xs_ref[...].astype(F32)
        xb = jnp.concatenate([rows_s[pl.ds(j, FFN_BLOCK, stride=ROW_TILES), :] for j in range(ROW_TILES)],
                             axis=1).astype(BF16)
        g = _dot(xb, wg_ref[...])
        u = _dot(xb, wu_ref[...])
        hmid = (g * _sigmoid(g) * u).astype(BF16)
        y = _dot(hmid, wd_ref[...])
        for j in range(ROW_TILES):
            ys_ref[pl.ds(j, FFN_BLOCK, stride=ROW_TILES), :] = y[:, j * LANES:(j + 1) * LANES]

    @pl.when(src_ref[i] != i)
    def _unused_tail_block():
        ys_ref[...] = jnp.zeros_like(ys_ref)


def _expert_ffn(layer, blk_exp, blk_src, xs, w_gate, w_up, w_down):
    nblk = blk_exp.shape[0]
    block = (FFN_BLOCK * ROW_TILES, LANES)
    grid_spec = pltpu.PrefetchScalarGridSpec(
        num_scalar_prefetch=2,
        grid=(nblk,),
        in_specs=[
            pl.BlockSpec(block, lambda i, be, src: (src[i], 0)),
            pl.BlockSpec((None, None, D_MODEL, D_EXPERT), lambda i, be, src: (layer, be[i], 0, 0)),
            pl.BlockSpec((None, None, D_MODEL, D_EXPERT), lambda i, be, src: (layer, be[i], 0, 0)),
            pl.BlockSpec((None, None, D_EXPERT, D_MODEL), lambda i, be, src: (layer, be[i], 0, 0)),
        ],
        out_specs=pl.BlockSpec(block, lambda i, be, src: (i, 0)),
        scratch_shapes=[pltpu.VMEM(block, F32)],
    )
    return pl.pallas_call(
        _ffn_kernel,
        grid_spec=grid_spec,
        out_shape=jax.ShapeDtypeStruct((nblk * FFN_BLOCK * ROW_TILES, LANES), F32),
        compiler_params=pltpu.CompilerParams(
            dimension_semantics=("arbitrary",), vmem_limit_bytes=VMEM_LIMIT),
        name="expert_ffn",
    )(blk_exp, blk_src, xs, w_gate, w_up, w_down)


def _plan_kernel(e_ref, ps_ref, dest_ref, run_s, tri_s):
    i = pl.program_id(0)

    @pl.when(i == 0)
    def _init():
        run_s[...] = ps_ref[...]
        s_i = lax.broadcasted_iota(jnp.int32, (PLAN_TILE, PLAN_TILE), 0)
        t_i = lax.broadcasted_iota(jnp.int32, (PLAN_TILE, PLAN_TILE), 1)
        tri_s[...] = (s_i < t_i).astype(BF16)

    expert = lax.broadcasted_iota(jnp.int32, (N_EXPERTS, PLAN_TILE), 0)
    onehot = jnp.where(e_ref[...] == expert, 1.0, 0.0)
    before = _dot(onehot.astype(BF16), tri_s[...])
    dest = jnp.sum(onehot * (before + run_s[:, 0:1]), axis=0, keepdims=True)
    dest_ref[...] = dest.astype(jnp.int32)
    run_s[...] = run_s[...] + jnp.sum(onehot, axis=1, keepdims=True)


def _plan(e_rows, pstart_col):
    n_tiles = e_rows.shape[0]
    return pl.pallas_call(
        _plan_kernel,
        grid=(n_tiles,),
        in_specs=[
            pl.BlockSpec((None, 1, PLAN_TILE), lambda i: (i, 0, 0)),
            pl.BlockSpec((N_EXPERTS, LANES), lambda i: (0, 0)),
        ],
        out_specs=pl.BlockSpec((None, 1, PLAN_TILE), lambda i: (i, 0, 0)),
        out_shape=jax.ShapeDtypeStruct((n_tiles, 1, PLAN_TILE), jnp.int32),
        scratch_shapes=[pltpu.VMEM((N_EXPERTS, LANES), F32), pltpu.VMEM((PLAN_TILE, PLAN_TILE), BF16)],
        compiler_params=pltpu.CompilerParams(
            dimension_semantics=("arbitrary",), vmem_limit_bytes=VMEM_LIMIT),
        name="moe_plan",
    )(e_rows, pstart_col)


def _dispatch_kernel(fill_ref, d0_ref, d1_ref, xn_ref, xs_hbm, zero_s, sem, fill_sem):
    @pl.when(pl.program_id(0) == 0)
    def _zero_padding():
        zero_s[...] = jnp.zeros_like(zero_s)

        def fill(c):
            start = pl.multiple_of(fill_ref[0, c] * ROW_TILES, ROW_TILES)
            return pltpu.make_async_copy(zero_s, xs_hbm.at[pl.ds(start, FFN_BLOCK * ROW_TILES), :], fill_sem)

        for c in range(N_FILLS):
            pl.when(fill_ref[0, c] >= 0)(lambda c=c: fill(c).start())
        for c in range(N_FILLS):
            pl.when(fill_ref[0, c] >= 0)(lambda c=c: fill(c).wait())

    def row_copy(dref, t):
        src = xn_ref.at[pl.ds(pl.multiple_of(t * ROW_TILES, ROW_TILES), ROW_TILES), :]
        dst = xs_hbm.at[pl.ds(pl.multiple_of(dref[0, t] * ROW_TILES, ROW_TILES), ROW_TILES), :]
        return pltpu.make_async_copy(src, dst, sem)

    def start(t, carry):
        row_copy(d0_ref, t).start()
        row_copy(d1_ref, t).start(priority=1)
        return carry

    lax.fori_loop(0, DISPATCH_TOKENS, start, 0, unroll=8)
    for _ in range(EXPERT_TOPK):
        pltpu.make_async_copy(xn_ref, xs_hbm.at[pl.ds(0, DISPATCH_TOKENS * ROW_TILES), :], sem).wait()


def _dispatch(fill_rows, dest, xn, n_rows):
    n_tiles = TOKENS // DISPATCH_TOKENS
    dest4 = dest.reshape(EXPERT_TOPK, n_tiles, 1, DISPATCH_TOKENS)
    smem_tile = lambda k: pl.BlockSpec((None, None, 1, DISPATCH_TOKENS), lambda i: (k, i, 0, 0),
                                       memory_space=pltpu.SMEM)
    return pl.pallas_call(
        _dispatch_kernel,
        grid=(n_tiles,),
        in_specs=[
            pl.BlockSpec(memory_space=pltpu.SMEM),
            smem_tile(0),
            smem_tile(1),
            pl.BlockSpec((DISPATCH_TOKENS * ROW_TILES, LANES), lambda i: (i, 0)),
        ],
        out_specs=pl.BlockSpec(memory_space=pl.ANY),
        out_shape=jax.ShapeDtypeStruct((n_rows * ROW_TILES, LANES), BF16),
        scratch_shapes=[
            pltpu.VMEM((FFN_BLOCK * ROW_TILES, LANES), BF16),
            pltpu.SemaphoreType.DMA(()),
            pltpu.SemaphoreType.DMA(()),
        ],
        compiler_params=pltpu.CompilerParams(
            dimension_semantics=("arbitrary",), vmem_limit_bytes=VMEM_LIMIT),
        name="moe_dispatch",
    )(fill_rows, dest4, dest4, xn)


def _combine_kernel(d0_ref, d1_ref, d0_next_ref, d1_next_ref, x1_ref, route_ref, ys_hbm, out_ref, buf, sem):
    i = pl.program_id(0)
    n = pl.num_programs(0)

    def start_tile(drefs, slot):
        def body(t, carry):
            for k in range(EXPERT_TOPK):
                src = ys_hbm.at[pl.ds(pl.multiple_of(drefs[k][0, t] * ROW_TILES, ROW_TILES), ROW_TILES), :]
                dst = buf.at[slot, k, pl.ds(pl.multiple_of(t * ROW_TILES, ROW_TILES), ROW_TILES), :]
                pltpu.make_async_copy(src, dst, sem.at[slot]).start(priority=k)
            return carry
        lax.fori_loop(0, COMBINE_TOKENS, body, 0, unroll=8)

    slot = i % 2

    @pl.when(i == 0)
    def _first():
        start_tile((d0_ref, d1_ref), 0)

    @pl.when(i + 1 < n)
    def _prefetch():
        start_tile((d0_next_ref, d1_next_ref), 1 - slot)

    for k in range(EXPERT_TOPK):
        pltpu.make_async_copy(ys_hbm.at[pl.ds(0, COMBINE_TOKENS * ROW_TILES), :], buf.at[slot, k],
                              sem.at[slot]).wait()
    w = route_ref[...]
    w1 = w[:, 2:3]
    w2 = w[:, 3:4]
    for j in range(ROW_TILES):
        cols = slice(j * LANES, (j + 1) * LANES)
        chunk = pl.ds(j, COMBINE_TOKENS, stride=ROW_TILES)
        out_ref[:, cols] = x1_ref[:, cols] + w1 * buf[slot, 0, chunk, :] + w2 * buf[slot, 1, chunk, :]


def _combine(dest, x1, route, ys3):
    n_tiles = TOKENS // COMBINE_TOKENS
    dest4 = dest.reshape(EXPERT_TOPK, n_tiles, 1, COMBINE_TOKENS)
    smem_tile = lambda k, nxt: pl.BlockSpec(
        (None, None, 1, COMBINE_TOKENS), lambda i: (k, jnp.minimum(i + nxt, n_tiles - 1), 0, 0),
        memory_space=pltpu.SMEM)
    return pl.pallas_call(
        _combine_kernel,
        grid=(n_tiles,),
        in_specs=[
            smem_tile(0, 0), smem_tile(1, 0), smem_tile(0, 1), smem_tile(1, 1),
            pl.BlockSpec((COMBINE_TOKENS, D_MODEL), lambda i: (i, 0)),
            pl.BlockSpec((COMBINE_TOKENS, LANES), lambda i: (i, 0)),
            pl.BlockSpec(memory_space=pl.ANY),
        ],
        out_specs=pl.BlockSpec((COMBINE_TOKENS, D_MODEL), lambda i: (i, 0)),
        out_shape=jax.ShapeDtypeStruct((TOKENS, D_MODEL), F32),
        scratch_shapes=[
            pltpu.VMEM((2, EXPERT_TOPK, COMBINE_TOKENS * ROW_TILES, LANES), F32),
            pltpu.SemaphoreType.DMA((2,)),
        ],
        compiler_params=pltpu.CompilerParams(
            dimension_semantics=("arbitrary",), vmem_limit_bytes=VMEM_LIMIT),
        name="moe_combine",
    )(dest4, dest4, dest4, dest4, x1, route, ys3)


def _rope_tables():
    inv = ROPE_THETA ** (-np.arange(ROPE_HALF, dtype=np.float64) * 2.0 / ROPE_DIM)
    ang = np.arange(SEQ, dtype=np.float64)[:, None] * inv[None, :]
    cos = np.ones((SEQ, A_HEAD_DIM))
    sin = np.zeros((SEQ, A_HEAD_DIM))
    cos[:, :ROPE_HALF] = np.cos(ang)
    cos[:, ROPE_HALF:ROPE_DIM] = np.cos(ang)
    sin[:, :ROPE_HALF] = -np.sin(ang)
    sin[:, ROPE_HALF:ROPE_DIM] = np.sin(ang)
    return (jnp.asarray(np.tile(cos, (1, 2)), F32), jnp.asarray(np.tile(sin, (1, 2)), F32))


def _split_w_in(w_in):
    sizes = [M_WIDTH, M_WIDTH, M_WIDTH, M_WIDTH, M_HEADS, M_HEADS, A_WIDTH, A_WIDTH, A_WIDTH, S_WIDTH]
    offs = np.cumsum([0] + sizes)
    mq, mk, mv, mo, mi, mf, aq, ak, av, su = (w_in[:, offs[n]:offs[n + 1]] for n in range(len(sizes)))
    w_main = jnp.concatenate([aq, ak, av, mq, mk, mv, mo, su], axis=1).astype(BF16)
    gate = jnp.concatenate([mi, mf], axis=1)
    return w_main, _hi_lo(jnp.pad(gate, ((0, 0), (0, LANES - 2 * M_HEADS))))


def _s5_params(a_re, a_im, b_re, b_im, c_re, c_im, log_dt):
    dt = jnp.exp(log_dt)[:, None]
    mag = jnp.exp(a_re * dt)
    abar_re, abar_im = mag * jnp.cos(a_im * dt), mag * jnp.sin(a_im * dt)
    zr, zi = abar_re - 1.0, abar_im
    den = a_re * a_re + a_im * a_im
    fr, fi = (zr * a_re + zi * a_im) / den, (zi * a_re - zr * a_im) / den
    bbar_re = fr[..., None] * b_re - fi[..., None] * b_im
    bbar_im = fr[..., None] * b_im + fi[..., None] * b_re
    eye = jnp.eye(S_GROUPS, dtype=F32)
    dense_b = lambda t: jnp.einsum('gph,gk->ghkp', t, eye).reshape(S_WIDTH, S_FLAT).astype(BF16)
    dense_c = lambda t: jnp.einsum('ghp,gk->gpkh', t, eye).reshape(S_FLAT, S_WIDTH).astype(BF16)
    abar = jnp.stack([abar_re.reshape(S_FLAT), abar_im.reshape(S_FLAT)])
    return abar, dense_b(bbar_re), dense_b(bbar_im), dense_c(c_re), dense_c(c_im)


def _moe(layer, x1, xn, route, w_gate, w_up, w_down):
    n_assign = TOKENS * EXPERT_TOPK
    e_flat = route[:, :EXPERT_TOPK].astype(jnp.int32).T.reshape(n_assign)
    experts = jnp.arange(N_EXPERTS, dtype=jnp.int32)
    counts = jnp.sum((e_flat[:, None] == experts[None, :]).astype(jnp.int32), axis=0)
    padded = (counts + FFN_BLOCK - 1) // FFN_BLOCK * FFN_BLOCK
    pends = jnp.cumsum(padded)
    pstarts = pends - padded
    n_rows = (n_assign + N_EXPERTS * (FFN_BLOCK - 1) + FFN_BLOCK - 1) // FFN_BLOCK * FFN_BLOCK
    nblk = n_rows // FFN_BLOCK
    blk_start = jnp.arange(nblk, dtype=jnp.int32) * FFN_BLOCK
    blk_exp = jnp.minimum(jnp.sum((pends[None, :] <= blk_start[:, None]).astype(jnp.int32), axis=1),
                          N_EXPERTS - 1).astype(jnp.int32)
    pstart_col = jnp.broadcast_to(pstarts.astype(F32)[:, None], (N_EXPERTS, LANES))
    dest = _plan(e_flat.reshape(n_assign // PLAN_TILE, 1, PLAN_TILE), pstart_col)
    tail = pends[-1] + jnp.arange(N_EXPERTS, dtype=jnp.int32) * FFN_BLOCK
    fill_rows = jnp.concatenate([jnp.where(padded > 0, pends - FFN_BLOCK, -1), jnp.where(tail < n_rows, tail, -1)])
    fill_rows = fill_rows.astype(jnp.int32).reshape(1, N_FILLS)
    xs = _dispatch(fill_rows, dest, xn, n_rows)
    blk_src = jnp.minimum(jnp.arange(nblk, dtype=jnp.int32), pends[-1] // FFN_BLOCK - 1).astype(jnp.int32)
    ys = _expert_ffn(layer, blk_exp, blk_src, xs, w_gate, w_up, w_down)
    return _combine(dest, x1, route, ys)


def kernel(x, norm1_g, w_in, m_bias_i, m_bias_f, m_conv_w, m_conv_b, m_out_g, a_q_g, a_k_g, a_out_g, s_a_re, s_a_im, s_b_re, s_b_im, s_c_re, s_c_im, s_d, s_log_dt, s_glu_w, s_glu_b, s_out_g, w_out, norm2_g, r_group_w, r_group_b, r_expert_w, r_expert_b, e_w_gate, e_w_up, e_w_down):
    cos_t, sin_t = _rope_tables()
    gmat = jnp.asarray(np.kron(np.eye(S_GROUPS), np.full((S_GROUP_DIM, S_GROUP_DIM), 1.0 / S_GROUP_DIM)), BF16)
    xf = x.reshape(TOKENS, D_MODEL)
    for l in range(DEPTH):
        w_main, w_gate = _split_w_in(w_in[l])
        aq, ak, av, mqk, mv, mo, su, gates, gatest = _inproj(xf, norm1_g[l].reshape(1, D_MODEL), w_main, w_gate)

        bias = jnp.concatenate([m_bias_i[l], m_bias_f[l]])
        bias_row = jnp.pad(bias, (0, LANES - 2 * M_HEADS)).reshape(1, LANES)
        y_m = _mlstm(mqk, mv, mo, gates, gatest, m_conv_w[l], m_conv_b[l].reshape(1, 2 * M_WIDTH),
                     bias_row, bias.reshape(2 * M_HEADS, 1), m_out_g[l].reshape(1, M_WIDTH))

        y_a = _moba(aq, ak, av, cos_t, sin_t,
                    jnp.tile(a_q_g[l], 2).reshape(1, HEAD_PAIR), jnp.tile(a_k_g[l], 2).reshape(1, HEAD_PAIR),
                    a_out_g[l].reshape(A_HEADS // 2, 1, HEAD_PAIR))

        abar, bre, bim, cre, cim = _s5_params(s_a_re[l], s_a_im[l], s_b_re[l], s_b_im[l],
                                              s_c_re[l], s_c_im[l], s_log_dt[l])
        y_s = _s5(su, abar, bre, bim, cre, cim,
                  s_d[l].reshape(1, S_WIDTH), s_glu_w[l].astype(BF16), s_glu_b[l].reshape(1, S_WIDTH),
                  gmat, s_out_g[l].reshape(1, S_WIDTH))

        w_router = jnp.pad(jnp.concatenate([r_group_w[l], r_expert_w[l]], axis=1),
                           ((0, 0), (0, LANES - N_GROUPS - N_EXPERTS)))
        b_router = jnp.pad(jnp.concatenate([r_group_b[l], r_expert_b[l]]),
                           (0, LANES - N_GROUPS - N_EXPERTS)).reshape(1, LANES)
        x1, xn, route = _outproj(xf, y_m, y_a, y_s, w_out[l].astype(BF16),
                                 norm2_g[l].reshape(1, D_MODEL), _hi_lo(w_router), b_router)
        xf = _moe(l, x1, xn, route, e_w_gate, e_w_up, e_w_down)
    return xf.reshape(BATCH, SEQ, D_MODEL)
```
